```python
import math
import jax, jax.numpy as jnp
from jax import lax
import numpy as np

D_MODEL = 1024
BATCH = 8
SEQ = 2048
DEPTH = 2

N_MIXERS = 2
HEAD_DIM = 128
N_MAIN_HEADS = 12
N_MEM_HEADS = 4
N_MEM = 256
INNER = (N_MAIN_HEADS + N_MEM_HEADS) * HEAD_DIM
MAIN_W = N_MAIN_HEADS * HEAD_DIM
MEM_W = N_MEM_HEADS * HEAD_DIM
RET_KEY_DIM = HEAD_DIM // 2
BLOCK_Q = 128
RET_CHUNK = 128
ROPE_BASE = 10000.0
EPS = 1e-6
NEG = -1e30
FOX_IN = 3 * MAIN_W + N_MAIN_HEADS + MEM_W + INNER
RET_IN = 2 * N_MAIN_HEADS * RET_KEY_DIM + MAIN_W + MEM_W + INNER
N_FOX = len(range(0, DEPTH, N_MIXERS))
N_RET = len(range(1, DEPTH, N_MIXERS))

kernel_name = "fox_retention_interleaved_hybrid"


def rmsnorm(x, g):
    xf = x.astype(jnp.float32)
    y = xf * lax.rsqrt(jnp.mean(xf * xf, axis=-1, keepdims=True) + EPS)
    return (y * g.astype(jnp.float32)).astype(x.dtype)


def split_cols(t, sizes):
    idx = list(np.cumsum(sizes)[:-1])
    return jnp.split(t, idx, axis=-1)


def rotary(t, pos):
    half = t.shape[-1] // 2
    inv = 1.0 / (ROPE_BASE ** (jnp.arange(half, dtype=jnp.float32) / half))
    ang = pos[:, None] * inv[None, :]
    cos = jnp.cos(ang)[None, :, None, :].astype(t.dtype)
    sin = jnp.sin(ang)[None, :, None, :].astype(t.dtype)
    t1, t2 = t[..., :half], t[..., half:]
    return jnp.concatenate([t1 * cos - t2 * sin, t1 * sin + t2 * cos], axis=-1)


def fox_attention(q, k, v, f_logit, b_f):
    B, S, _ = q.shape
    H, d = N_MAIN_HEADS, HEAD_DIM
    q = q.reshape(B, S, H, d).transpose(0, 2, 1, 3)
    k = k.reshape(B, S, H, d).transpose(0, 2, 1, 3)
    v = v.reshape(B, S, H, d).transpose(0, 2, 1, 3)
    log_f = jax.nn.log_sigmoid(f_logit.astype(jnp.float32) + b_f.astype(jnp.float32))
    c = jnp.cumsum(log_f, axis=1).transpose(0, 2, 1)
    nb = S // BLOCK_Q
    qb = q.reshape(B, H, nb, BLOCK_Q, d).transpose(2, 0, 1, 3, 4)
    cb = c.reshape(B, H, nb, BLOCK_Q).transpose(2, 0, 1, 3)
    starts = jnp.arange(nb, dtype=jnp.int32) * BLOCK_Q
    key_pos = jnp.arange(S, dtype=jnp.int32)
    scale = 1.0 / math.sqrt(d)

    def block(args):
        qi, ci, st = args
        qpos = st + jnp.arange(BLOCK_Q, dtype=jnp.int32)
        s = jnp.einsum('bhqd,bhkd->bhqk', qi, k).astype(jnp.float32) * scale
        s = s + ci[..., None] - c[:, :, None, :]
        mask = key_pos[None, :] <= qpos[:, None]
        s = jnp.where(mask[None, None], s, NEG)
        p = jax.nn.softmax(s, axis=-1).astype(v.dtype)
        return jnp.einsum('bhqk,bhkd->bhqd', p, v)

    out = lax.map(block, (qb, cb, starts))
    return out.transpose(1, 0, 3, 2, 4).reshape(B, S, H * d)


def retention(q, k, v):
    B, S, _ = q.shape
    H, dk, dv, C = N_MAIN_HEADS, RET_KEY_DIM, HEAD_DIM, RET_CHUNK
    pos = jnp.arange(S, dtype=jnp.float32)
    qf = rotary(q.reshape(B, S, H, dk).astype(jnp.float32), pos)
    kf = rotary(k.reshape(B, S, H, dk).astype(jnp.float32), pos) * (dk ** -0.5)
    vf = v.reshape(B, S, H, dv).astype(jnp.float32)
    lg = jnp.log1p(-jnp.exp2(-5.0 - jnp.arange(H, dtype=jnp.float32)))
    n = jnp.arange(C, dtype=jnp.float32)
    diff = n[:, None] - n[None, :]
    d_inner = jnp.where(diff[None] >= 0, jnp.exp(lg[:, None, None] * jnp.maximum(diff, 0.0)[None]), 0.0)
    xi = jnp.exp(lg[:, None] * (n[None, :] + 1.0))
    zeta = jnp.exp(lg[:, None] * (C - 1.0 - n[None, :]))
    g_chunk = jnp.exp(lg * C)
    nc = S // C

    def to_chunks(t):
        return t.reshape(B, nc, C, H, t.shape[-1]).transpose(1, 0, 3, 2, 4)

    def step(R, xs):
        qc, kc, vc = xs
        inner = jnp.einsum('bhnd,bhmd->bhnm', qc, kc) * d_inner[None]
        o = jnp.einsum('bhnm,bhme->bhne', inner, vc)
        o = o + jnp.einsum('bhnd,bhde->bhne', qc, R) * xi[None, :, :, None]
        R = g_chunk[None, :, None, None] * R + jnp.einsum('bhmd,bhme->bhde', kc * zeta[None, :, :, None], vc)
        return R, o

    R0 = jnp.zeros((B, H, dk, dv), jnp.float32)
    _, out = lax.scan(step, R0, (to_chunks(qf), to_chunks(kf), to_chunks(vf)))
    out = out.transpose(1, 0, 3, 2, 4).reshape(B, S, H, dv)
    out = out * lax.rsqrt(jnp.mean(out * out, axis=-1, keepdims=True) + EPS)
    return out.reshape(B, S, H * dv).astype(q.dtype)


def mem_cross_attention(q_mem, mem_n, w_kv):
    B, S, _ = q_mem.shape
    kv = mem_n @ w_kv
    km, vm = split_cols(kv, [MEM_W, MEM_W])
    q = q_mem.reshape(B, S, N_MEM_HEADS, HEAD_DIM)
    km = km.reshape(B, -1, N_MEM_HEADS, HEAD_DIM)
    vm = vm.reshape(B, -1, N_MEM_HEADS, HEAD_DIM)
    s = jnp.einsum('bshd,bmhd->bhsm', q, km).astype(jnp.float32) / math.sqrt(HEAD_DIM)
    p = jax.nn.softmax(s, axis=-1).astype(vm.dtype)
    return jnp.einsum('bhsm,bmhd->bshd', p, vm).reshape(B, S, MEM_W)


def setup_inputs(seed: int = 0) -> dict:
    key = jax.random.key(seed)
    ks = jax.random.split(key, 10)
    f32 = jnp.float32
    x = jax.random.normal(ks[0], (BATCH, SEQ, D_MODEL), f32)
    mem = jax.random.normal(ks[1], (BATCH, N_MEM, D_MODEL), f32)
    norm_g = 1.0 + 0.02 * jax.random.normal(ks[2], (DEPTH, D_MODEL), f32)
    fox_w_in = jax.random.normal(ks[3], (N_FOX, D_MODEL, FOX_IN), f32) * D_MODEL ** -0.5
    fox_b_f = 3.0 + 0.5 * jax.random.normal(ks[4], (N_FOX, N_MAIN_HEADS), f32)
    ret_w_in = jax.random.normal(ks[5], (N_RET, D_MODEL, RET_IN), f32) * D_MODEL ** -0.5
    mem_norm_g = 1.0 + 0.02 * jax.random.normal(ks[6], (D_MODEL,), f32)
    w_mem_kv = jax.random.normal(ks[7], (DEPTH, D_MODEL, 2 * MEM_W), f32) * D_MODEL ** -0.5
    w_out = jax.random.normal(ks[8], (DEPTH, INNER, D_MODEL), f32) * INNER ** -0.5
    final_norm_g = 1.0 + 0.02 * jax.random.normal(ks[9], (D_MODEL,), f32)
    return {"x": x, "mem": mem, "norm_g": norm_g, "fox_w_in": fox_w_in, "fox_b_f": fox_b_f,
            "ret_w_in": ret_w_in, "mem_norm_g": mem_norm_g, "w_mem_kv": w_mem_kv,
            "w_out": w_out, "final_norm_g": final_norm_g}


def reference(x, mem, norm_g, fox_w_in, fox_b_f, ret_w_in, mem_norm_g, w_mem_kv, w_out, final_norm_g):
    mem_n = rmsnorm(mem, mem_norm_g)
    for i in range(DEPTH):
        h = rmsnorm(x, norm_g[i])
        j = i // N_MIXERS
        if i % N_MIXERS == 0:
            proj = h @ fox_w_in[j]
            q, k, v, f_logit, q_mem, z = split_cols(
                proj, [MAIN_W, MAIN_W, MAIN_W, N_MAIN_HEADS, MEM_W, INNER])
            main = fox_attention(q, k, v, f_logit, fox_b_f[j])
        else:
            proj = h @ ret_w_in[j]
            qk_w = N_MAIN_HEADS * RET_KEY_DIM
            q, k, v, q_mem, z = split_cols(proj, [qk_w, qk_w, MAIN_W, MEM_W, INNER])
            main = retention(q, k, v)
        memo = mem_cross_attention(q_mem, mem_n, w_mem_kv[i])
        o = jnp.concatenate([main, memo], axis=-1) * jax.nn.silu(z)
        x = x + o @ w_out[i]
    return rmsnorm(x, final_norm_g)
```

```python
import functools
import math

import jax
import jax.numpy as jnp
import numpy as np
from jax import lax
from jax.experimental import pallas as pl
from jax.experimental.pallas import tpu as pltpu

D_MODEL = 1024
BATCH = 8
SEQ = 2048
HEAD_DIM = 128
N_MAIN_HEADS = 12
N_MEM_HEADS = 4
N_MEM = 256
MAIN_W = N_MAIN_HEADS * HEAD_DIM
MEM_W = N_MEM_HEADS * HEAD_DIM
INNER = MAIN_W + MEM_W
RET_KEY_DIM = HEAD_DIM // 2
RET_QK_W = N_MAIN_HEADS * RET_KEY_DIM
RET_CHUNK = 128
ROPE_BASE = 10000.0
EPS = 1e-6
NEG = -1e30
LOG2E = 1.4426950408889634

LANES = 128
F_PAD = LANES
VMEM_LIMIT = 56 * 1024 * 1024

PROJ_TM = 512
PROJ_TN = 512
FOX_TQ = 256
FOX_TK = 256
OUT_TM = 512

f32 = jnp.float32
bf16 = jnp.bfloat16


def _silu(z):
    return z * (1.0 / (1.0 + jnp.exp(-z)))


def _rmsnorm_rows(x, g):
    ms = jnp.mean(x * x, axis=-1, keepdims=True)
    return (x * lax.rsqrt(ms + EPS)) * g


def _norm_proj_kernel(*refs, out_widths, rot_out):
    n_out = len(out_widths)
    x_ref, g_ref, w_ref = refs[:3]
    tab_refs = refs[3:len(refs) - n_out]
    out_refs = refs[len(refs) - n_out:]
    h = _rmsnorm_rows(x_ref[...], g_ref[...]).astype(bf16)
    if rot_out is not None:
        cos_ref, sin_ref = tab_refs
        lane = lax.broadcasted_iota(jnp.int32, (h.shape[0], LANES), 1)
        first_half = (lane % RET_KEY_DIM) < (RET_KEY_DIM // 2)
    col = 0
    for oi, (o_ref, width) in enumerate(zip(out_refs, out_widths)):
        for c0 in range(0, width, PROJ_TN):
            cw = min(PROJ_TN, width - c0)
            y = jnp.dot(h, w_ref[:, col + c0:col + c0 + cw], preferred_element_type=f32)
            if oi == rot_out:
                for s0 in range(0, cw, LANES):
                    ys = y[:, s0:s0 + LANES]
                    sw = jnp.where(first_half, pltpu.roll(ys, LANES - RET_KEY_DIM // 2, 1),
                                   pltpu.roll(ys, RET_KEY_DIM // 2, 1))
                    r = ys * cos_ref[:, c0 + s0:c0 + s0 + LANES] + sw * sin_ref[:, c0 + s0:c0 + s0 + LANES]
                    o_ref[:, c0 + s0:c0 + s0 + LANES] = r.astype(o_ref.dtype)
            else:
                o_ref[:, c0:c0 + cw] = y.astype(o_ref.dtype)
        col += width


def _norm_proj(x, g, w, out_widths, out_dtypes, *, rot_out=None, tables=None, name):
    t_rows, d = x.shape
    n = w.shape[1]
    assert sum(out_widths) == n and t_rows % PROJ_TM == 0
    in_specs = [
        pl.BlockSpec((PROJ_TM, d), lambda i: (i, 0)),
        pl.BlockSpec((1, d), lambda i: (0, 0)),
        pl.BlockSpec((d, n), lambda i: (0, 0), pipeline_mode=pl.Buffered(1)),
    ]
    args = [x, g.reshape(1, d), w]
    if rot_out is not None:
        tiles_per_seq = SEQ // PROJ_TM
        tw = out_widths[rot_out]
        for tab in tables:
            in_specs.append(pl.BlockSpec((PROJ_TM, tw), lambda i: (i % tiles_per_seq, 0)))
            args.append(tab)
    out_specs = [pl.BlockSpec((PROJ_TM, wd), lambda i: (i, 0)) for wd in out_widths]
    out_shape = [jax.ShapeDtypeStruct((t_rows, wd), dt) for wd, dt in zip(out_widths, out_dtypes)]
    return pl.pallas_call(
        functools.partial(_norm_proj_kernel, out_widths=tuple(out_widths), rot_out=rot_out),
        grid=(t_rows // PROJ_TM,),
        in_specs=in_specs,
        out_specs=out_specs,
        out_shape=out_shape,
        compiler_params=pltpu.CompilerParams(dimension_semantics=("arbitrary",),
                                             vmem_limit_bytes=VMEM_LIMIT),
        name=name,
    )(*args)


def _fox_gate_kernel(f_ref, bf_ref, o_ref):
    blk = LANES
    row = lax.broadcasted_iota(jnp.int32, (blk, blk), 0)
    coli = lax.broadcasted_iota(jnp.int32, (blk, blk), 1)
    tri = jnp.where(row >= coli, 1.0, 0.0).astype(bf16)
    carry = jnp.zeros((1, F_PAD), f32)
    for b0 in range(0, SEQ, blk):
        xg = f_ref[b0:b0 + blk, :] + bf_ref[...]
        ls = jnp.minimum(xg, 0.0) - jnp.log1p(jnp.exp(-jnp.abs(xg)))
        hi = ls.astype(bf16)
        r1 = ls - hi.astype(f32)
        mid = r1.astype(bf16)
        lo = (r1 - mid.astype(f32)).astype(bf16)
        cs = (jnp.dot(tri, hi, preferred_element_type=f32)
              + jnp.dot(tri, mid, preferred_element_type=f32)
              + jnp.dot(tri, lo, preferred_element_type=f32)) + carry
        carry = cs[blk - 1:blk, :]
        ct = (cs * (-LOG2E)).T
        o_ref[:, b0:b0 + blk] = ct[:o_ref.shape[0], :]


def _fox_gate(f, b_f):
    hp = 16
    bf_pad = jnp.zeros((1, F_PAD), f32).at[0, :N_MAIN_HEADS].set(b_f.astype(f32))
    return pl.pallas_call(
        _fox_gate_kernel,
        grid=(BATCH,),
        in_specs=[pl.BlockSpec((SEQ, F_PAD), lambda b: (b, 0)),
                  pl.BlockSpec((1, F_PAD), lambda b: (0, 0))],
        out_specs=pl.BlockSpec((None, hp, SEQ), lambda b: (b, 0, 0)),
        out_shape=jax.ShapeDtypeStruct((BATCH, hp, SEQ), f32),
        compiler_params=pltpu.CompilerParams(dimension_semantics=("arbitrary",)),
        name="fox_gate",
    )(f, bf_pad)


def _fox_attn_kernel(q_ref, k_ref, v_ref, nb_ref, z_ref, o_ref, m_ref, l_ref, acc_ref):
    qi = pl.program_id(2)
    q = q_ref[...]
    a = LOG2E / math.sqrt(HEAD_DIM)
    m_ref[...] = jnp.full(m_ref.shape, NEG, f32)
    l_ref[...] = jnp.zeros(l_ref.shape, f32)
    acc_ref[...] = jnp.zeros(acc_ref.shape, f32)

    def step(j, masked):
        rows = pl.ds(pl.multiple_of(j * FOX_TK, FOX_TK), FOX_TK)
        ks = k_ref[rows, :]
        vs = v_ref[rows, :]
        s = lax.dot_general(q, ks, (((1,), (1,)), ((), ())), preferred_element_type=f32)
        t = s * a + nb_ref[pl.ds(j, 1), :]
        if masked:
            r = lax.broadcasted_iota(jnp.int32, t.shape, 0)
            c = lax.broadcasted_iota(jnp.int32, t.shape, 1)
            t = jnp.where(c <= r, t, NEG)
        m_old = m_ref[...]
        m_new = jnp.maximum(m_old, jnp.max(t, axis=-1, keepdims=True))
        p = jnp.exp2(t - m_new)
        alpha = jnp.exp2(m_old - m_new)
        l_ref[...] = alpha * l_ref[...] + jnp.sum(p, axis=-1, keepdims=True)
        acc_ref[...] = alpha * acc_ref[...] + jnp.dot(p.astype(bf16), vs, preferred_element_type=f32)
        m_ref[...] = m_new

    def body(j, carry):
        step(j, False)
        return carry

    lax.fori_loop(0, qi, body, 0)
    step(qi, True)
    out = acc_ref[...] * (1.0 / l_ref[...])
    o_ref[...] = (out * _silu(z_ref[...].astype(f32))).astype(o_ref.dtype)


def _fox_attention(q, k, v, nb, z):
    nq = SEQ // FOX_TQ
    assert FOX_TQ == FOX_TK
    return pl.pallas_call(
        _fox_attn_kernel,
        grid=(BATCH, N_MAIN_HEADS, nq),
        in_specs=[
            pl.BlockSpec((FOX_TQ, HEAD_DIM), lambda b, h, i: (b * nq + i, h)),
            pl.BlockSpec((SEQ, HEAD_DIM), lambda b, h, i: (b, h)),
            pl.BlockSpec((SEQ, HEAD_DIM), lambda b, h, i: (b, h)),
            pl.BlockSpec((None, None, SEQ // FOX_TK, FOX_TK), lambda b, h, i: (b, h, 0, 0)),
            pl.BlockSpec((FOX_TQ, HEAD_DIM), lambda b, h, i: (b * nq + i, h)),
        ],
        out_specs=pl.BlockSpec((FOX_TQ, HEAD_DIM), lambda b, h, i: (b * nq + i, h)),
        out_shape=jax.ShapeDtypeStruct((BATCH * SEQ, MAIN_W), bf16),
        scratch_shapes=[pltpu.VMEM((FOX_TQ, 1), f32), pltpu.VMEM((FOX_TQ, 1), f32),
                        pltpu.VMEM((FOX_TQ, HEAD_DIM), f32)],
        compiler_params=pltpu.CompilerParams(
            dimension_semantics=("arbitrary", "arbitrary", "arbitrary")),
        name="fox_attention",
    )(q, k, v, nb, z)


def _retention_kernel(q_ref, k_ref, v_ref, z_ref, d_ref, xi_ref, zeta_ref, g_ref, o_ref, r_ref):
    c_len = RET_CHUNK
    r_ref[...] = jnp.zeros(r_ref.shape, f32)
    lane = lax.broadcasted_iota(jnp.int32, (c_len, LANES), 1)
    head_mask = (lane < RET_KEY_DIM, lane >= RET_KEY_DIM)

    def body(c, carry):
        rows = pl.ds(pl.multiple_of(c * c_len, c_len), c_len)
        q2 = q_ref[rows, :]
        k2 = k_ref[rows, :]
        v2 = v_ref[rows, :]
        r_state = r_ref[...]
        r_b = r_state.astype(bf16)
        for hd in range(2):
            cols = slice(hd * HEAD_DIM, (hd + 1) * HEAD_DIM)
            qa = jnp.where(head_mask[hd], q2, jnp.zeros_like(q2))
            s = lax.dot_general(qa, k2, (((1,), (1,)), ((), ())), preferred_element_type=f32)
            inner = (s * d_ref[hd]).astype(bf16)
            o = jnp.dot(inner, v2[:, cols], preferred_element_type=f32)
            o = o + jnp.dot(qa, r_b[:, cols], preferred_element_type=f32) * xi_ref[hd]
            ms = jnp.mean(o * o, axis=-1, keepdims=True)
            on = o * lax.rsqrt(ms + EPS)
            zg = _silu(z_ref[rows, cols].astype(f32))
            o_ref[rows, cols] = (on * zg).astype(o_ref.dtype)
        kz = (k2.astype(f32) * zeta_ref[...]).astype(bf16)
        upd = lax.dot_general(kz, v2, (((0,), (0,)), ((), ())), preferred_element_type=f32)
        r_ref[...] = r_state * g_ref[...] + upd
        return carry

    lax.fori_loop(0, SEQ // c_len, body, 0)


def _retention_tables():
    h, c_len = N_MAIN_HEADS, RET_CHUNK
    lg = jnp.log1p(-jnp.exp2(-5.0 - jnp.arange(h, dtype=f32)))
    n = jnp.arange(c_len, dtype=f32)
    diff = n[:, None] - n[None, :]
    d_inner = jnp.where(diff[None] >= 0,
                        jnp.exp(lg[:, None, None] * jnp.maximum(diff, 0.0)[None]), 0.0)
    xi = jnp.exp(lg[:, None] * (n[None, :] + 1.0))
    zeta = jnp.exp(lg[:, None] * (c_len - 1.0 - n[None, :]))
    g_chunk = jnp.exp(lg * c_len)
    xi_b = jnp.broadcast_to(xi[:, :, None], (h, c_len, HEAD_DIM))
    zeta2 = jnp.repeat(zeta.reshape(h // 2, 2, c_len).transpose(0, 2, 1), RET_KEY_DIM, axis=-1)
    g2 = jnp.broadcast_to(jnp.repeat(g_chunk.reshape(h // 2, 2), RET_KEY_DIM, axis=-1)[:, :, None],
                          (h // 2, 2 * RET_KEY_DIM, 2 * HEAD_DIM))
    return d_inner, xi_b, zeta2, g2


def _retention(qk, v, z):
    hp = N_MAIN_HEADS // 2
    d_inner, xi_b, zeta2, g2 = _retention_tables()
    c_len = RET_CHUNK
    return pl.pallas_call(
        _retention_kernel,
        grid=(BATCH, hp),
        in_specs=[
            pl.BlockSpec((SEQ, LANES), lambda b, p: (b, p)),
            pl.BlockSpec((SEQ, LANES), lambda b, p: (b, hp + p)),
            pl.BlockSpec((SEQ, 2 * HEAD_DIM), lambda b, p: (b, p)),
            pl.BlockSpec((SEQ, 2 * HEAD_DIM), lambda b, p: (b, p)),
            pl.BlockSpec((2, c_len, c_len), lambda b, p: (p, 0, 0)),
            pl.BlockSpec((2, c_len, HEAD_DIM), lambda b, p: (p, 0, 0)),
            pl.BlockSpec((None, c_len, LANES), lambda b, p: (p, 0, 0)),
            pl.BlockSpec((None, LANES, 2 * HEAD_DIM), lambda b, p: (p, 0, 0)),
        ],
        out_specs=pl.BlockSpec((SEQ, 2 * HEAD_DIM), lambda b, p: (b, p)),
        out_shape=jax.ShapeDtypeStruct((BATCH * SEQ, MAIN_W), bf16),
        scratch_shapes=[pltpu.VMEM((LANES, 2 * HEAD_DIM), f32)],
        compiler_params=pltpu.CompilerParams(dimension_semantics=("arbitrary", "arbitrary")),
        name="retention",
    )(qk, qk, v, z, d_inner, xi_b, zeta2, g2)


def _out_proj_kernel(om_ref, qm_ref, zm_ref, kv_ref, w_ref, x_ref, g_ref, o_ref, *, final_norm):
    scale = 1.0 / math.sqrt(HEAD_DIM)
    y = jnp.dot(om_ref[...], w_ref[:MAIN_W, :], preferred_element_type=f32)
    for hd in range(N_MEM_HEADS):
        cols = slice(hd * HEAD_DIM, (hd + 1) * HEAD_DIM)
        kh = kv_ref[:, cols]
        vh = kv_ref[:, MEM_W + hd * HEAD_DIM:MEM_W + (hd + 1) * HEAD_DIM]
        s = lax.dot_general(qm_ref[:, cols], kh, (((1,), (1,)), ((), ())),
                            preferred_element_type=f32) * scale
        e = jnp.exp(s - jnp.max(s, axis=-1, keepdims=True))
        p = e * (1.0 / jnp.sum(e, axis=-1, keepdims=True))
        memo = jnp.dot(p.astype(bf16), vh, preferred_element_type=f32)
        og = (memo * _silu(zm_ref[:, cols].astype(f32))).astype(bf16)
        y = y + jnp.dot(og, w_ref[MAIN_W + hd * HEAD_DIM:MAIN_W + (hd + 1) * HEAD_DIM, :],
                        preferred_element_type=f32)
    xn = x_ref[...] + y
    if final_norm:
        xn = _rmsnorm_rows(xn, g_ref[...])
    o_ref[...] = xn


def _out_proj(o_main, qm, z, kv, layer, w_out, x, g_final, *, final_norm):
    t_rows = x.shape[0]
    tiles_per_seq = SEQ // OUT_TM
    z_blk = MAIN_W // MEM_W
    return pl.pallas_call(
        functools.partial(_out_proj_kernel, final_norm=final_norm),
        grid=(t_rows // OUT_TM,),
        in_specs=[
            pl.BlockSpec((OUT_TM, MAIN_W), lambda i: (i, 0)),
            pl.BlockSpec((OUT_TM, MEM_W), lambda i: (i, 0)),
            pl.BlockSpec((OUT_TM, MEM_W), lambda i: (i, z_blk)),
            pl.BlockSpec((N_MEM, 2 * MEM_W), lambda i: (i // tiles_per_seq, layer)),
            pl.BlockSpec((INNER, D_MODEL), lambda i: (0, 0), pipeline_mode=pl.Buffered(1)),
            pl.BlockSpec((OUT_TM, D_MODEL), lambda i: (i, 0)),
            pl.BlockSpec((1, D_MODEL), lambda i: (0, 0)),
        ],
        out_specs=pl.BlockSpec((OUT_TM, D_MODEL), lambda i: (i, 0)),
        out_shape=jax.ShapeDtypeStruct((t_rows, D_MODEL), f32),
        compiler_params=pltpu.CompilerParams(dimension_semantics=("arbitrary",),
                                             vmem_limit_bytes=VMEM_LIMIT),
        name="out_proj_final" if final_norm else "out_proj",
    )(o_main, qm, z, kv, w_out, x, g_final.reshape(1, D_MODEL))


def _rotary_tables():
    half = RET_KEY_DIM // 2
    pos = jnp.arange(SEQ, dtype=f32)
    inv = 1.0 / (ROPE_BASE ** (jnp.arange(half, dtype=f32) / half))
    ang = pos[:, None] * inv[None, :]
    cos, sin = jnp.cos(ang), jnp.sin(ang)
    cos_h = jnp.concatenate([cos, cos], axis=-1)
    sin_h = jnp.concatenate([-sin, sin], axis=-1)
    k_scale = RET_KEY_DIM ** -0.5
    cos_t = jnp.concatenate([jnp.tile(cos_h, (1, N_MAIN_HEADS)),
                             jnp.tile(cos_h, (1, N_MAIN_HEADS)) * k_scale], axis=-1)
    sin_t = jnp.concatenate([jnp.tile(sin_h, (1, N_MAIN_HEADS)),
                             jnp.tile(sin_h, (1, N_MAIN_HEADS)) * k_scale], axis=-1)
    return cos_t, sin_t


def kernel(x, mem, norm_g, fox_w_in, fox_b_f, ret_w_in, mem_norm_g, w_mem_kv, w_out, final_norm_g):
    t_rows = BATCH * SEQ
    x2 = x.reshape(t_rows, D_MODEL)

    w_kv = jnp.concatenate([w_mem_kv[0], w_mem_kv[1]], axis=-1).astype(bf16)
    (kv,) = _norm_proj(mem.reshape(BATCH * N_MEM, D_MODEL), mem_norm_g, w_kv,
                       [2 * MEM_W * 2], [bf16], name="mem_kv_proj")

    w0 = fox_w_in[0]
    o_q, o_k, o_v, o_f, o_qm, o_z = np.cumsum([0, MAIN_W, MAIN_W, MAIN_W, N_MAIN_HEADS, MEM_W])
    w_f = jnp.pad(w0[:, o_f:o_f + N_MAIN_HEADS], ((0, 0), (0, F_PAD - N_MAIN_HEADS)))
    w0r = jnp.concatenate([w0[:, :o_f], w0[:, o_qm:], w_f], axis=-1).astype(bf16)
    q, k, v, qm, z, f = _norm_proj(
        x2, norm_g[0], w0r, [MAIN_W, MAIN_W, MAIN_W, MEM_W, INNER, F_PAD],
        [bf16, bf16, bf16, bf16, bf16, f32], name="fox_in_proj")
    nb = _fox_gate(f, fox_b_f[0])
    nb = nb.reshape(BATCH, 16, SEQ // FOX_TK, FOX_TK)
    o_main = _fox_attention(q, k, v, nb, z)
    x2 = _out_proj(o_main, qm, z, kv, 0, w_out[0].astype(bf16), x2, final_norm_g, final_norm=False)

    w1 = ret_w_in[0].astype(bf16)
    cos_t, sin_t = _rotary_tables()
    qk, v, qm, z = _norm_proj(
        x2, norm_g[1], w1, [2 * RET_QK_W, MAIN_W, MEM_W, INNER], [bf16] * 4,
        rot_out=0, tables=(cos_t, sin_t), name="ret_in_proj")
    o_main = _retention(qk, v, z)
    out = _out_proj(o_main, qm, z, kv, 1, w_out[1].astype(bf16), x2, final_norm_g, final_norm=True)
    return out.reshape(BATCH, SEQ, D_MODEL)
```

```python
import functools
import math

import jax
import jax.numpy as jnp
import numpy as np
from jax import lax
from jax.experimental import pallas as pl
from jax.experimental.pallas import tpu as pltpu

D_MODEL = 1024
BATCH = 8
SEQ = 2048
HEAD_DIM = 128
N_MAIN_HEADS = 12
N_MEM_HEADS = 4
N_MEM = 256
MAIN_W = N_MAIN_HEADS * HEAD_DIM
MEM_W = N_MEM_HEADS * HEAD_DIM
INNER = MAIN_W + MEM_W
RET_KEY_DIM = HEAD_DIM // 2
RET_QK_W = N_MAIN_HEADS * RET_KEY_DIM
RET_CHUNK = 128
ROPE_BASE = 10000.0
EPS = 1e-6
NEG = -1e30
LOG2E = 1.4426950408889634

LANES = 128
F_PAD = LANES
VMEM_LIMIT = 56 * 1024 * 1024

PROJ_TM = 512
PROJ_TN = 512
FOX_T = 256
FOX_HEADS_PER_STEP = 6
BIAS_TERMS = 3
OUT_TM = 512

f32 = jnp.float32
bf16 = jnp.bfloat16


def _silu(z):
    return z * (1.0 / (1.0 + jnp.exp(-z)))


def _rmsnorm_rows(x, g):
    ms = jnp.mean(x * x, axis=-1, keepdims=True)
    return (x * lax.rsqrt(ms + EPS)) * g


def _split_bf16(v):
    hi = v.astype(bf16)
    r1 = v - hi.astype(f32)
    mid = r1.astype(bf16)
    lo = (r1 - mid.astype(f32)).astype(bf16)
    return hi, mid, lo


def _norm_proj_kernel(*refs, out_widths, out_kinds):
    n_out = len(out_widths)
    x_ref, g_ref, w_ref = refs[:3]
    tab_refs = refs[3:len(refs) - n_out]
    out_refs = refs[len(refs) - n_out:]
    h = _rmsnorm_rows(x_ref[...], g_ref[...]).astype(bf16)
    tm = h.shape[0]
    if "rot" in out_kinds:
        cos_ref, sin_ref = tab_refs
        lane = lax.broadcasted_iota(jnp.int32, (tm, LANES), 1)
        first_half = (lane % RET_KEY_DIM) < (RET_KEY_DIM // 2)
    col = 0
    for o_ref, width, kind in zip(out_refs, out_widths, out_kinds):
        for c0 in range(0, width, PROJ_TN):
            cw = min(PROJ_TN, width - c0)
            y = jnp.dot(h, w_ref[:, col + c0:col + c0 + cw], preferred_element_type=f32)
            if kind == "rot":
                for s0 in range(0, cw, LANES):
                    ys = y[:, s0:s0 + LANES]
                    sw = jnp.where(first_half, pltpu.roll(ys, LANES - RET_KEY_DIM // 2, 1),
                                   pltpu.roll(ys, RET_KEY_DIM // 2, 1))
                    r = ys * cos_ref[:, c0 + s0:c0 + s0 + LANES] + sw * sin_ref[:, c0 + s0:c0 + s0 + LANES]
                    o_ref[:, c0 + s0:c0 + s0 + LANES] = r.astype(o_ref.dtype)
            elif kind == "kt":
                for r0 in range(tm // FOX_T):
                    for s0 in range(0, cw, LANES):
                        piece = y[r0 * FOX_T:(r0 + 1) * FOX_T, s0:s0 + LANES]
                        o_ref[r0, c0 + s0:c0 + s0 + LANES, :] = piece.T.astype(o_ref.dtype)
            else:
                o_ref[:, c0:c0 + cw] = y.astype(o_ref.dtype)
        col += width


def _norm_proj(x, g, w, out_widths, out_dtypes, out_kinds=None, *, tables=None, name):
    t_rows, d = x.shape
    n = w.shape[1]
    out_kinds = tuple(out_kinds or ["plain"] * len(out_widths))
    assert sum(out_widths) == n and t_rows % PROJ_TM == 0
    tiles_per_seq = SEQ // PROJ_TM
    in_specs = [
        pl.BlockSpec((PROJ_TM, d), lambda i: (i, 0)),
        pl.BlockSpec((1, d), lambda i: (0, 0)),
        pl.BlockSpec((d, n), lambda i: (0, 0), pipeline_mode=pl.Buffered(1)),
    ]
    args = [x, g.reshape(1, d), w]
    if "rot" in out_kinds:
        tw = out_widths[out_kinds.index("rot")]
        for tab in tables:
            in_specs.append(pl.BlockSpec((PROJ_TM, tw), lambda i: (i % tiles_per_seq, 0)))
            args.append(tab)
    out_specs, out_shape = [], []
    for wd, dt, kind in zip(out_widths, out_dtypes, out_kinds):
        if kind == "kt":
            kt_per_tile = PROJ_TM // FOX_T
            out_specs.append(pl.BlockSpec((None, kt_per_tile, wd, FOX_T),
                                          lambda i: (i // tiles_per_seq, i % tiles_per_seq, 0, 0)))
            out_shape.append(jax.ShapeDtypeStruct((t_rows // SEQ, SEQ // FOX_T, wd, FOX_T), dt))
        else:
            out_specs.append(pl.BlockSpec((PROJ_TM, wd), lambda i: (i, 0)))
            out_shape.append(jax.ShapeDtypeStruct((t_rows, wd), dt))
    return pl.pallas_call(
        functools.partial(_norm_proj_kernel, out_widths=tuple(out_widths), out_kinds=out_kinds),
        grid=(t_rows // PROJ_TM,),
        in_specs=in_specs,
        out_specs=out_specs,
        out_shape=out_shape,
        compiler_params=pltpu.CompilerParams(dimension_semantics=("arbitrary",),
                                             vmem_limit_bytes=VMEM_LIMIT),
        name=name,
    )(*args)


def _fox_gate_kernel(f_ref, bf_ref, o_ref):
    blk = LANES
    row = lax.broadcasted_iota(jnp.int32, (blk, blk), 0)
    coli = lax.broadcasted_iota(jnp.int32, (blk, blk), 1)
    tri = jnp.where(row >= coli, 1.0, 0.0).astype(bf16)
    spread = [jnp.where((coli == BIAS_TERMS * row + t) & (row < N_MAIN_HEADS), 1.0, 0.0).astype(bf16)
              for t in range(BIAS_TERMS)]
    carry = jnp.zeros((1, F_PAD), f32)
    for b0 in range(0, SEQ, blk):
        xg = f_ref[b0:b0 + blk, :] + bf_ref[...]
        ls = jnp.minimum(xg, 0.0) - jnp.log1p(jnp.exp(-jnp.abs(xg)))
        cs = sum(jnp.dot(tri, term, preferred_element_type=f32) for term in _split_bf16(ls)) + carry
        carry = cs[blk - 1:blk, :]
        bias = cs * (-math.sqrt(HEAD_DIM))
        nb = sum(jnp.dot(term, sp, preferred_element_type=f32)
                 for term, sp in zip(_split_bf16(bias), spread))
        o_ref[b0:b0 + blk, :] = nb.astype(o_ref.dtype)


def _fox_gate(f, b_f):
    bf_pad = jnp.zeros((1, F_PAD), f32).at[0, :N_MAIN_HEADS].set(b_f.astype(f32))
    return pl.pallas_call(
        _fox_gate_kernel,
        grid=(BATCH,),
        in_specs=[pl.BlockSpec((SEQ, F_PAD), lambda b: (b, 0)),
                  pl.BlockSpec((1, F_PAD), lambda b: (0, 0))],
        out_specs=pl.BlockSpec((SEQ, LANES), lambda b: (b, 0)),
        out_shape=jax.ShapeDtypeStruct((BATCH * SEQ, LANES), bf16),
        compiler_params=pltpu.CompilerParams(dimension_semantics=("arbitrary",)),
        name="fox_gate",
    )(f, bf_pad)


def _fox_attn_kernel(q_ref, k_ref, vt_ref, nb_ref, z_ref, o_ref, m_ref, l_ref, acc_ref):
    t = FOX_T
    head0 = pl.program_id(1) * FOX_HEADS_PER_STEP
    a = LOG2E / math.sqrt(HEAD_DIM)
    lane = lax.broadcasted_iota(jnp.int32, (t, LANES), 1)
    key_idx = lax.broadcasted_iota(jnp.int32, (t, t), 0)
    qry_idx = lax.broadcasted_iota(jnp.int32, (t, t), 1)
    causal = key_idx <= qry_idx

    def q_tile(qi, carry):
        qrows = pl.ds(pl.multiple_of(qi * t, t), t)
        q_aug = []
        for hd in range(FOX_HEADS_PER_STEP):
            lo = BIAS_TERMS * (head0 + hd)
            sel = jnp.where((lane >= lo) & (lane < lo + BIAS_TERMS), 1.0, 0.0).astype(bf16)
            q_aug.append(jnp.concatenate([q_ref[qrows, hd * HEAD_DIM:(hd + 1) * HEAD_DIM], sel], axis=1))
            m_ref[hd] = jnp.full((1, t), NEG, f32)
            l_ref[hd] = jnp.zeros((1, t), f32)
            acc_ref[hd] = jnp.zeros((HEAD_DIM, t), f32)

        def step(j, masked):
            krows = pl.ds(pl.multiple_of(j * t, t), t)
            nbj = nb_ref[krows, :]
            heads = range(FOX_HEADS_PER_STEP)
            col = [slice(hd * HEAD_DIM, (hd + 1) * HEAD_DIM) for hd in heads]
            u = [lax.dot_general(jnp.concatenate([k_ref[krows, col[hd]], nbj], axis=1), q_aug[hd],
                                 (((1,), (1,)), ((), ())), preferred_element_type=f32) * a
                 for hd in heads]
            p, alpha = [], []
            for hd in heads:
                uh = jnp.where(causal, u[hd], NEG) if masked else u[hd]
                m_old = m_ref[hd]
                m_new = jnp.maximum(m_old, jnp.max(uh, axis=0, keepdims=True))
                ph = jnp.exp2(uh - m_new)
                al = jnp.exp2(m_old - m_new)
                l_ref[hd] = al * l_ref[hd] + jnp.sum(ph, axis=0, keepdims=True)
                m_ref[hd] = m_new
                p.append(ph.astype(bf16))
                alpha.append(al)
            for hd in heads:
                acc_ref[hd] = alpha[hd] * acc_ref[hd] + jnp.dot(vt_ref[j, col[hd], :], p[hd],
                                                                preferred_element_type=f32)

        def body(j, c):
            step(j, False)
            return c

        lax.fori_loop(0, qi, body, 0)
        step(qi, True)
        for hd in range(FOX_HEADS_PER_STEP):
            cols = slice(hd * HEAD_DIM, (hd + 1) * HEAD_DIM)
            out_t = acc_ref[hd] * (1.0 / l_ref[hd])
            o_ref[qrows, cols] = (out_t.T * _silu(z_ref[qrows, cols].astype(f32))).astype(o_ref.dtype)
        return carry

    lax.fori_loop(0, SEQ // t, q_tile, 0)


def _fox_attention(q, k, vt, nb, z):
    hw = FOX_HEADS_PER_STEP * HEAD_DIM
    return pl.pallas_call(
        _fox_attn_kernel,
        grid=(BATCH, N_MAIN_HEADS // FOX_HEADS_PER_STEP),
        in_specs=[
            pl.BlockSpec((SEQ, hw), lambda b, p: (b, p)),
            pl.BlockSpec((SEQ, hw), lambda b, p: (b, p)),
            pl.BlockSpec((None, SEQ // FOX_T, hw, FOX_T), lambda b, p: (b, 0, p, 0)),
            pl.BlockSpec((SEQ, LANES), lambda b, p: (b, 0)),
            pl.BlockSpec((SEQ, hw), lambda b, p: (b, p)),
        ],
        out_specs=pl.BlockSpec((SEQ, hw), lambda b, p: (b, p)),
        out_shape=jax.ShapeDtypeStruct((BATCH * SEQ, MAIN_W), bf16),
        scratch_shapes=[pltpu.VMEM((FOX_HEADS_PER_STEP, 1, FOX_T), f32),
                        pltpu.VMEM((FOX_HEADS_PER_STEP, 1, FOX_T), f32),
                        pltpu.VMEM((FOX_HEADS_PER_STEP, HEAD_DIM, FOX_T), f32)],
        compiler_params=pltpu.CompilerParams(dimension_semantics=("arbitrary", "arbitrary"),
                                             vmem_limit_bytes=VMEM_LIMIT),
        name="fox_attention",
    )(q, k, vt, nb, z)


def _retention_kernel(q_ref, k_ref, v_ref, z_ref, d_ref, xi_ref, zeta_ref, g_ref, o_ref, r_ref):
    c_len = RET_CHUNK
    r_ref[...] = jnp.zeros(r_ref.shape, f32)
    lane = lax.broadcasted_iota(jnp.int32, (c_len, LANES), 1)
    head_mask = (lane < RET_KEY_DIM, lane >= RET_KEY_DIM)

    def body(c, carry):
        rows = pl.ds(pl.multiple_of(c * c_len, c_len), c_len)
        q2 = q_ref[rows, :]
        k2 = k_ref[rows, :]
        v2 = v_ref[rows, :]
        r_state = r_ref[...]
        r_b = r_state.astype(bf16)
        for hd in range(2):
            cols = slice(hd * HEAD_DIM, (hd + 1) * HEAD_DIM)
            qa = jnp.where(head_mask[hd], q2, jnp.zeros_like(q2))
            s = lax.dot_general(qa, k2, (((1,), (1,)), ((), ())), preferred_element_type=f32)
            inner = (s * d_ref[hd]).astype(bf16)
            o = jnp.dot(inner, v2[:, cols], preferred_element_type=f32)
            o = o + jnp.dot(qa, r_b[:, cols], preferred_element_type=f32) * xi_ref[hd]
            ms = jnp.mean(o * o, axis=-1, keepdims=True)
            on = o * lax.rsqrt(ms + EPS)
            zg = _silu(z_ref[rows, cols].astype(f32))
            o_ref[rows, cols] = (on * zg).astype(o_ref.dtype)
        kz = (k2.astype(f32) * zeta_ref[...]).astype(bf16)
        upd = lax.dot_general(kz, v2, (((0,), (0,)), ((), ())), preferred_element_type=f32)
        r_ref[...] = r_state * g_ref[...] + upd
        return carry

    lax.fori_loop(0, SEQ // c_len, body, 0)


def _retention_tables():
    h, c_len = N_MAIN_HEADS, RET_CHUNK
    lg = jnp.log1p(-jnp.exp2(-5.0 - jnp.arange(h, dtype=f32)))
    n = jnp.arange(c_len, dtype=f32)
    diff = n[:, None] - n[None, :]
    d_inner = jnp.where(diff[None] >= 0,
                        jnp.exp(lg[:, None, None] * jnp.maximum(diff, 0.0)[None]), 0.0)
    xi = jnp.exp(lg[:, None] * (n[None, :] + 1.0))
    zeta = jnp.exp(lg[:, None] * (c_len - 1.0 - n[None, :]))
    g_chunk = jnp.exp(lg * c_len)
    xi_b = jnp.broadcast_to(xi[:, :, None], (h, c_len, HEAD_DIM))
    zeta2 = jnp.repeat(zeta.reshape(h // 2, 2, c_len).transpose(0, 2, 1), RET_KEY_DIM, axis=-1)
    g2 = jnp.broadcast_to(jnp.repeat(g_chunk.reshape(h // 2, 2), RET_KEY_DIM, axis=-1)[:, :, None],
                          (h // 2, 2 * RET_KEY_DIM, 2 * HEAD_DIM))
    return d_inner, xi_b, zeta2, g2


def _retention(qk, v, z):
    hp = N_MAIN_HEADS // 2
    d_inner, xi_b, zeta2, g2 = _retention_tables()
    c_len = RET_CHUNK
    return pl.pallas_call(
        _retention_kernel,
        grid=(BATCH, hp),
        in_specs=[
            pl.BlockSpec((SEQ, LANES), lambda b, p: (b, p)),
            pl.BlockSpec((SEQ, LANES), lambda b, p: (b, hp + p)),
            pl.BlockSpec((SEQ, 2 * HEAD_DIM), lambda b, p: (b, p)),
            pl.BlockSpec((SEQ, 2 * HEAD_DIM), lambda b, p: (b, p)),
            pl.BlockSpec((2, c_len, c_len), lambda b, p: (p, 0, 0)),
            pl.BlockSpec((2, c_len, HEAD_DIM), lambda b, p: (p, 0, 0)),
            pl.BlockSpec((None, c_len, LANES), lambda b, p: (p, 0, 0)),
            pl.BlockSpec((None, LANES, 2 * HEAD_DIM), lambda b, p: (p, 0, 0)),
        ],
        out_specs=pl.BlockSpec((SEQ, 2 * HEAD_DIM), lambda b, p: (b, p)),
        out_shape=jax.ShapeDtypeStruct((BATCH * SEQ, MAIN_W), bf16),
        scratch_shapes=[pltpu.VMEM((LANES, 2 * HEAD_DIM), f32)],
        compiler_params=pltpu.CompilerParams(dimension_semantics=("arbitrary", "arbitrary")),
        name="retention",
    )(qk, qk, v, z, d_inner, xi_b, zeta2, g2)


def _out_proj_kernel(om_ref, qm_ref, zm_ref, kv_ref, w_ref, x_ref, g_ref, o_ref, *, final_norm):
    scale = 1.0 / math.sqrt(HEAD_DIM)
    y = jnp.dot(om_ref[...], w_ref[:MAIN_W, :], preferred_element_type=f32)
    for hd in range(N_MEM_HEADS):
        cols = slice(hd * HEAD_DIM, (hd + 1) * HEAD_DIM)
        kh = kv_ref[:, cols]
        vh = kv_ref[:, MEM_W + hd * HEAD_DIM:MEM_W + (hd + 1) * HEAD_DIM]
        s = lax.dot_general(qm_ref[:, cols], kh, (((1,), (1,)), ((), ())),
                            preferred_element_type=f32) * scale
        e = jnp.exp(s - jnp.max(s, axis=-1, keepdims=True))
        p = e * (1.0 / jnp.sum(e, axis=-1, keepdims=True))
        memo = jnp.dot(p.astype(bf16), vh, preferred_element_type=f32)
        og = (memo * _silu(zm_ref[:, cols].astype(f32))).astype(bf16)
        y = y + jnp.dot(og, w_ref[MAIN_W + hd * HEAD_DIM:MAIN_W + (hd + 1) * HEAD_DIM, :],
                        preferred_element_type=f32)
    xn = x_ref[...] + y
    if final_norm:
        xn = _rmsnorm_rows(xn, g_ref[...])
    o_ref[...] = xn


def _out_proj(o_main, qm, z, kv, layer, w_out, x, g_final, *, final_norm):
    t_rows = x.shape[0]
    tiles_per_seq = SEQ // OUT_TM
    z_blk = MAIN_W // MEM_W
    return pl.pallas_call(
        functools.partial(_out_proj_kernel, final_norm=final_norm),
        grid=(t_rows // OUT_TM,),
        in_specs=[
            pl.BlockSpec((OUT_TM, MAIN_W), lambda i: (i, 0)),
            pl.BlockSpec((OUT_TM, MEM_W), lambda i: (i, 0)),
            pl.BlockSpec((OUT_TM, MEM_W), lambda i: (i, z_blk)),
            pl.BlockSpec((N_MEM, 2 * MEM_W), lambda i: (i // tiles_per_seq, layer)),
            pl.BlockSpec((INNER, D_MODEL), lambda i: (0, 0), pipeline_mode=pl.Buffered(1)),
            pl.BlockSpec((OUT_TM, D_MODEL), lambda i: (i, 0)),
            pl.BlockSpec((1, D_MODEL), lambda i: (0, 0)),
        ],
        out_specs=pl.BlockSpec((OUT_TM, D_MODEL), lambda i: (i, 0)),
        out_shape=jax.ShapeDtypeStruct((t_rows, D_MODEL), f32),
        compiler_params=pltpu.CompilerParams(dimension_semantics=("arbitrary",),
                                             vmem_limit_bytes=VMEM_LIMIT),
        name="out_proj_final" if final_norm else "out_proj",
    )(o_main, qm, z, kv, w_out, x, g_final.reshape(1, D_MODEL))


def _rotary_tables():
    half = RET_KEY_DIM // 2
    pos = jnp.arange(SEQ, dtype=f32)
    inv = 1.0 / (ROPE_BASE ** (jnp.arange(half, dtype=f32) / half))
    ang = pos[:, None] * inv[None, :]
    cos, sin = jnp.cos(ang), jnp.sin(ang)
    cos_h = jnp.concatenate([cos, cos], axis=-1)
    sin_h = jnp.concatenate([-sin, sin], axis=-1)
    k_scale = RET_KEY_DIM ** -0.5
    cos_t = jnp.concatenate([jnp.tile(cos_h, (1, N_MAIN_HEADS)),
                             jnp.tile(cos_h, (1, N_MAIN_HEADS)) * k_scale], axis=-1)
    sin_t = jnp.concatenate([jnp.tile(sin_h, (1, N_MAIN_HEADS)),
                             jnp.tile(sin_h, (1, N_MAIN_HEADS)) * k_scale], axis=-1)
    return cos_t, sin_t


def kernel(x, mem, norm_g, fox_w_in, fox_b_f, ret_w_in, mem_norm_g, w_mem_kv, w_out, final_norm_g):
    t_rows = BATCH * SEQ
    x2 = x.reshape(t_rows, D_MODEL)

    w_kv = jnp.concatenate([w_mem_kv[0], w_mem_kv[1]], axis=-1).astype(bf16)
    (kv,) = _norm_proj(mem.reshape(BATCH * N_MEM, D_MODEL), mem_norm_g, w_kv,
                       [2 * MEM_W * 2], [bf16], name="mem_kv_proj")

    w0 = fox_w_in[0]
    o_q, o_k, o_v, o_f, o_qm, o_z = np.cumsum([0, MAIN_W, MAIN_W, MAIN_W, N_MAIN_HEADS, MEM_W])
    w_f = jnp.pad(w0[:, o_f:o_f + N_MAIN_HEADS], ((0, 0), (0, F_PAD - N_MAIN_HEADS)))
    w0r = jnp.concatenate([w0[:, :o_f], w0[:, o_qm:], w_f], axis=-1).astype(bf16)
    q, k, vt, qm, z, f = _norm_proj(
        x2, norm_g[0], w0r, [MAIN_W, MAIN_W, MAIN_W, MEM_W, INNER, F_PAD],
        [bf16, bf16, bf16, bf16, bf16, f32], ["plain", "plain", "kt", "plain", "plain", "plain"],
        name="fox_in_proj")
    nb = _fox_gate(f, fox_b_f[0])
    o_main = _fox_attention(q, k, vt, nb, z)
    x2 = _out_proj(o_main, qm, z, kv, 0, w_out[0].astype(bf16), x2, final_norm_g, final_norm=False)

    w1 = ret_w_in[0].astype(bf16)
    qk, v, qm, z = _norm_proj(
        x2, norm_g[1], w1, [2 * RET_QK_W, MAIN_W, MEM_W, INNER], [bf16] * 4,
        ["rot", "plain", "plain", "plain"], tables=_rotary_tables(), name="ret_in_proj")
    o_main = _retention(qk, v, z)
    out = _out_proj(o_main, qm, z, kv, 1, w_out[1].astype(bf16), x2, final_norm_g, final_norm=True)
    return out.reshape(BATCH, SEQ, D_MODEL)
```

```python
import functools
import math

import jax
import jax.numpy as jnp
import numpy as np
from jax import lax
from jax.experimental import pallas as pl
from jax.experimental.pallas import tpu as pltpu

D_MODEL = 1024
BATCH = 8
SEQ = 2048
HEAD_DIM = 128
N_MAIN_HEADS = 12
N_MEM_HEADS = 4
N_MEM = 256
MAIN_W = N_MAIN_HEADS * HEAD_DIM
MEM_W = N_MEM_HEADS * HEAD_DIM
INNER = MAIN_W + MEM_W
RET_KEY_DIM = HEAD_DIM // 2
RET_QK_W = N_MAIN_HEADS * RET_KEY_DIM
RET_CHUNK = 128
ROPE_BASE = 10000.0
EPS = 1e-6
NEG = -1e30
LOG2E = 1.4426950408889634

LANES = 128
F_PAD = LANES
VMEM_LIMIT = 56 * 1024 * 1024

PROJ_TM = 512
PROJ_TN = 512
FOX_T = 256
FOX_HEADS_PER_STEP = 6
BIAS_TERMS = 3
FOX_SUM_ROWS = 16
FOX_Q_SCALE = LOG2E / math.sqrt(HEAD_DIM)
OUT_TM = 512

f32 = jnp.float32
bf16 = jnp.bfloat16


def _silu(z):
    return z * (1.0 / (1.0 + jnp.exp(-z)))


def _rmsnorm_rows(x, g):
    ms = jnp.mean(x * x, axis=-1, keepdims=True)
    return (x * lax.rsqrt(ms + EPS)) * g


def _split_bf16(v):
    hi = v.astype(bf16)
    r1 = v - hi.astype(f32)
    mid = r1.astype(bf16)
    lo = (r1 - mid.astype(f32)).astype(bf16)
    return hi, mid, lo


def _norm_proj_kernel(*refs, out_widths, out_kinds):
    n_out = len(out_widths)
    x_ref, g_ref, w_ref = refs[:3]
    tab_refs = refs[3:len(refs) - n_out]
    out_refs = refs[len(refs) - n_out:]
    h = _rmsnorm_rows(x_ref[...], g_ref[...]).astype(bf16)
    tm = h.shape[0]
    if "rot" in out_kinds:
        cos_ref, sin_ref = tab_refs
        lane = lax.broadcasted_iota(jnp.int32, (tm, LANES), 1)
        first_half = (lane % RET_KEY_DIM) < (RET_KEY_DIM // 2)
    col = 0
    for o_ref, width, kind in zip(out_refs, out_widths, out_kinds):
        for c0 in range(0, width, PROJ_TN):
            cw = min(PROJ_TN, width - c0)
            y = jnp.dot(h, w_ref[:, col + c0:col + c0 + cw], preferred_element_type=f32)
            if kind == "rot":
                for s0 in range(0, cw, LANES):
                    ys = y[:, s0:s0 + LANES]
                    sw = jnp.where(first_half, pltpu.roll(ys, LANES - RET_KEY_DIM // 2, 1),
                                   pltpu.roll(ys, RET_KEY_DIM // 2, 1))
                    r = ys * cos_ref[:, c0 + s0:c0 + s0 + LANES] + sw * sin_ref[:, c0 + s0:c0 + s0 + LANES]
                    o_ref[:, c0 + s0:c0 + s0 + LANES] = r.astype(o_ref.dtype)
            elif kind == "kt":
                for r0 in range(tm // FOX_T):
                    for s0 in range(0, cw, LANES):
                        piece = y[r0 * FOX_T:(r0 + 1) * FOX_T, s0:s0 + LANES]
                        o_ref[r0, c0 + s0:c0 + s0 + LANES, :] = piece.T.astype(o_ref.dtype)
            elif kind == "fox_q":
                o_ref[:, c0:c0 + cw] = (y * FOX_Q_SCALE).astype(o_ref.dtype)
            else:
                o_ref[:, c0:c0 + cw] = y.astype(o_ref.dtype)
        col += width


def _norm_proj(x, g, w, out_widths, out_dtypes, out_kinds=None, *, tables=None, name):
    t_rows, d = x.shape
    n = w.shape[1]
    out_kinds = tuple(out_kinds or ["plain"] * len(out_widths))
    assert sum(out_widths) == n and t_rows % PROJ_TM == 0
    tiles_per_seq = SEQ // PROJ_TM
    in_specs = [
        pl.BlockSpec((PROJ_TM, d), lambda i: (i, 0)),
        pl.BlockSpec((1, d), lambda i: (0, 0)),
        pl.BlockSpec((d, n), lambda i: (0, 0), pipeline_mode=pl.Buffered(1)),
    ]
    args = [x, g.reshape(1, d), w]
    if "rot" in out_kinds:
        tw = out_widths[out_kinds.index("rot")]
        for tab in tables:
            in_specs.append(pl.BlockSpec((PROJ_TM, tw), lambda i: (i % tiles_per_seq, 0)))
            args.append(tab)
    out_specs, out_shape = [], []
    for wd, dt, kind in zip(out_widths, out_dtypes, out_kinds):
        if kind == "kt":
            kt_per_tile = PROJ_TM // FOX_T
            out_specs.append(pl.BlockSpec((None, kt_per_tile, wd, FOX_T),
                                          lambda i: (i // tiles_per_seq, i % tiles_per_seq, 0, 0)))
            out_shape.append(jax.ShapeDtypeStruct((t_rows // SEQ, SEQ // FOX_T, wd, FOX_T), dt))
        else:
            out_specs.append(pl.BlockSpec((PROJ_TM, wd), lambda i: (i, 0)))
            out_shape.append(jax.ShapeDtypeStruct((t_rows, wd), dt))
    return pl.pallas_call(
        functools.partial(_norm_proj_kernel, out_widths=tuple(out_widths), out_kinds=out_kinds),
        grid=(t_rows // PROJ_TM,),
        in_specs=in_specs,
        out_specs=out_specs,
        out_shape=out_shape,
        compiler_params=pltpu.CompilerParams(dimension_semantics=("arbitrary",),
                                             vmem_limit_bytes=VMEM_LIMIT),
        name=name,
    )(*args)


def _fox_gate_kernel(f_ref, bf_ref, o_ref):
    blk = LANES
    row = lax.broadcasted_iota(jnp.int32, (blk, blk), 0)
    coli = lax.broadcasted_iota(jnp.int32, (blk, blk), 1)
    tri = jnp.where(row >= coli, 1.0, 0.0).astype(bf16)
    spread = [jnp.where((coli == BIAS_TERMS * row + t) & (row < N_MAIN_HEADS), 1.0, 0.0).astype(bf16)
              for t in range(BIAS_TERMS)]
    carry = jnp.zeros((1, F_PAD), f32)
    for b0 in range(0, SEQ, blk):
        xg = f_ref[b0:b0 + blk, :] + bf_ref[...]
        ls = jnp.minimum(xg, 0.0) - jnp.log1p(jnp.exp(-jnp.abs(xg)))
        cs = sum(jnp.dot(tri, term, preferred_element_type=f32) for term in _split_bf16(ls)) + carry
        carry = cs[blk - 1:blk, :]
        bias = cs * (-LOG2E)
        nb = sum(jnp.dot(term, sp, preferred_element_type=f32)
                 for term, sp in zip(_split_bf16(bias), spread))
        o_ref[b0:b0 + blk, :] = nb.astype(o_ref.dtype)


def _fox_gate(f, b_f):
    bf_pad = jnp.zeros((1, F_PAD), f32).at[0, :N_MAIN_HEADS].set(b_f.astype(f32))
    return pl.pallas_call(
        _fox_gate_kernel,
        grid=(BATCH,),
        in_specs=[pl.BlockSpec((SEQ, F_PAD), lambda b: (b, 0)),
                  pl.BlockSpec((1, F_PAD), lambda b: (0, 0))],
        out_specs=pl.BlockSpec((SEQ, LANES), lambda b: (b, 0)),
        out_shape=jax.ShapeDtypeStruct((BATCH * SEQ, LANES), bf16),
        compiler_params=pltpu.CompilerParams(dimension_semantics=("arbitrary",)),
        name="fox_gate",
    )(f, bf_pad)


def _fox_attn_kernel(q_ref, k_ref, vt_ref, nb_ref, z_ref, o_ref, m_ref, acc_ref):
    t = FOX_T
    head0 = pl.program_id(1) * FOX_HEADS_PER_STEP
    heads = range(FOX_HEADS_PER_STEP)
    col = [slice(hd * HEAD_DIM, (hd + 1) * HEAD_DIM) for hd in heads]
    lane = lax.broadcasted_iota(jnp.int32, (t, LANES), 1)
    key_idx = lax.broadcasted_iota(jnp.int32, (t, t), 0)
    qry_idx = lax.broadcasted_iota(jnp.int32, (t, t), 1)
    causal = key_idx <= qry_idx
    ones_rows = jnp.ones((FOX_SUM_ROWS, t), bf16)

    def q_tile(qi, carry):
        qrows = pl.ds(pl.multiple_of(qi * t, t), t)
        q_aug = []
        for hd in heads:
            lo = BIAS_TERMS * (head0 + hd)
            sel = jnp.where((lane >= lo) & (lane < lo + BIAS_TERMS), 1.0, 0.0).astype(bf16)
            q_aug.append(jnp.concatenate([q_ref[qrows, col[hd]], sel], axis=1))
            m_ref[hd] = jnp.full((1, t), NEG, f32)
            acc_ref[hd] = jnp.zeros((HEAD_DIM + FOX_SUM_ROWS, t), f32)

        def step(j, masked):
            krows = pl.ds(pl.multiple_of(j * t, t), t)
            nbj = nb_ref[krows, :]
            u = [lax.dot_general(jnp.concatenate([k_ref[krows, col[hd]], nbj], axis=1), q_aug[hd],
                                 (((1,), (1,)), ((), ())), preferred_element_type=f32)
                 for hd in heads]
            p, alpha = [], []
            for hd in heads:
                uh = jnp.where(causal, u[hd], NEG) if masked else u[hd]
                m_old = m_ref[hd]
                m_new = jnp.maximum(m_old, jnp.max(uh, axis=0, keepdims=True))
                p.append(jnp.exp2(uh - m_new).astype(bf16))
                alpha.append(jnp.exp2(m_old - m_new))
                m_ref[hd] = m_new
            for hd in heads:
                vt_aug = jnp.concatenate([vt_ref[j, col[hd], :], ones_rows], axis=0)
                acc_ref[hd] = alpha[hd] * acc_ref[hd] + jnp.dot(vt_aug, p[hd], preferred_element_type=f32)

        def body(j, c):
            step(j, False)
            return c

        lax.fori_loop(0, qi, body, 0)
        step(qi, True)
        for hd in heads:
            acc = acc_ref[hd]
            out_t = acc[:HEAD_DIM, :] * (1.0 / acc[HEAD_DIM:HEAD_DIM + 1, :])
            o_ref[qrows, col[hd]] = (out_t.T * _silu(z_ref[qrows, col[hd]].astype(f32))).astype(o_ref.dtype)
        return carry

    lax.fori_loop(0, SEQ // t, q_tile, 0)


def _fox_attention(q, k, vt, nb, z):
    hw = FOX_HEADS_PER_STEP * HEAD_DIM
    return pl.pallas_call(
        _fox_attn_kernel,
        grid=(BATCH, N_MAIN_HEADS // FOX_HEADS_PER_STEP),
        in_specs=[
            pl.BlockSpec((SEQ, hw), lambda b, p: (b, p)),
            pl.BlockSpec((SEQ, hw), lambda b, p: (b, p)),
            pl.BlockSpec((None, SEQ // FOX_T, hw, FOX_T), lambda b, p: (b, 0, p, 0)),
            pl.BlockSpec((SEQ, LANES), lambda b, p: (b, 0)),
            pl.BlockSpec((SEQ, hw), lambda b, p: (b, p)),
        ],
        out_specs=pl.BlockSpec((SEQ, hw), lambda b, p: (b, p)),
        out_shape=jax.ShapeDtypeStruct((BATCH * SEQ, MAIN_W), bf16),
        scratch_shapes=[pltpu.VMEM((FOX_HEADS_PER_STEP, 1, FOX_T), f32),
                        pltpu.VMEM((FOX_HEADS_PER_STEP, HEAD_DIM + FOX_SUM_ROWS, FOX_T), f32)],
        compiler_params=pltpu.CompilerParams(dimension_semantics=("arbitrary", "arbitrary"),
                                             vmem_limit_bytes=VMEM_LIMIT),
        name="fox_attention",
    )(q, k, vt, nb, z)


def _retention_kernel(q_ref, k_ref, v_ref, z_ref, d_ref, xi_ref, zeta_ref, g_ref, o_ref, r_ref):
    c_len = RET_CHUNK
    r_ref[...] = jnp.zeros(r_ref.shape, f32)
    lane = lax.broadcasted_iota(jnp.int32, (c_len, LANES), 1)
    head_mask = (lane < RET_KEY_DIM, lane >= RET_KEY_DIM)

    def body(c, carry):
        rows = pl.ds(pl.multiple_of(c * c_len, c_len), c_len)
        q2 = q_ref[rows, :]
        k2 = k_ref[rows, :]
        v2 = v_ref[rows, :]
        r_state = r_ref[...]
        r_b = r_state.astype(bf16)
        for hd in range(2):
            cols = slice(hd * HEAD_DIM, (hd + 1) * HEAD_DIM)
            qa = jnp.where(head_mask[hd], q2, jnp.zeros_like(q2))
            s = lax.dot_general(qa, k2, (((1,), (1,)), ((), ())), preferred_element_type=f32)
            inner = (s * d_ref[hd]).astype(bf16)
            o = jnp.dot(inner, v2[:, cols], preferred_element_type=f32)
            o = o + jnp.dot(qa, r_b[:, cols], preferred_element_type=f32) * xi_ref[hd]
            ms = jnp.mean(o * o, axis=-1, keepdims=True)
            on = o * lax.rsqrt(ms + EPS)
            zg = _silu(z_ref[rows, cols].astype(f32))
            o_ref[rows, cols] = (on * zg).astype(o_ref.dtype)
        kz = (k2.astype(f32) * zeta_ref[...]).astype(bf16)
        upd = lax.dot_general(kz, v2, (((0,), (0,)), ((), ())), preferred_element_type=f32)
        r_ref[...] = r_state * g_ref[...] + upd
        return carry

    lax.fori_loop(0, SEQ // c_len, body, 0)


def _retention_tables():
    h, c_len = N_MAIN_HEADS, RET_CHUNK
    lg = jnp.log1p(-jnp.exp2(-5.0 - jnp.arange(h, dtype=f32)))
    n = jnp.arange(c_len, dtype=f32)
    diff = n[:, None] - n[None, :]
    d_inner = jnp.where(diff[None] >= 0,
                        jnp.exp(lg[:, None, None] * jnp.maximum(diff, 0.0)[None]), 0.0)
    xi = jnp.exp(lg[:, None] * (n[None, :] + 1.0))
    zeta = jnp.exp(lg[:, None] * (c_len - 1.0 - n[None, :]))
    g_chunk = jnp.exp(lg * c_len)
    xi_b = jnp.broadcast_to(xi[:, :, None], (h, c_len, HEAD_DIM))
    zeta2 = jnp.repeat(zeta.reshape(h // 2, 2, c_len).transpose(0, 2, 1), RET_KEY_DIM, axis=-1)
    g2 = jnp.broadcast_to(jnp.repeat(g_chunk.reshape(h // 2, 2), RET_KEY_DIM, axis=-1)[:, :, None],
                          (h // 2, 2 * RET_KEY_DIM, 2 * HEAD_DIM))
    return d_inner, xi_b, zeta2, g2


def _retention(qk, v, z):
    hp = N_MAIN_HEADS // 2
    d_inner, xi_b, zeta2, g2 = _retention_tables()
    c_len = RET_CHUNK
    return pl.pallas_call(
        _retention_kernel,
        grid=(BATCH, hp),
        in_specs=[
            pl.BlockSpec((SEQ, LANES), lambda b, p: (b, p)),
            pl.BlockSpec((SEQ, LANES), lambda b, p: (b, hp + p)),
            pl.BlockSpec((SEQ, 2 * HEAD_DIM), lambda b, p: (b, p)),
            pl.BlockSpec((SEQ, 2 * HEAD_DIM), lambda b, p: (b, p)),
            pl.BlockSpec((2, c_len, c_len), lambda b, p: (p, 0, 0)),
            pl.BlockSpec((2, c_len, HEAD_DIM), lambda b, p: (p, 0, 0)),
            pl.BlockSpec((None, c_len, LANES), lambda b, p: (p, 0, 0)),
            pl.BlockSpec((None, LANES, 2 * HEAD_DIM), lambda b, p: (p, 0, 0)),
        ],
        out_specs=pl.BlockSpec((SEQ, 2 * HEAD_DIM), lambda b, p: (b, p)),
        out_shape=jax.ShapeDtypeStruct((BATCH * SEQ, MAIN_W), bf16),
        scratch_shapes=[pltpu.VMEM((LANES, 2 * HEAD_DIM), f32)],
        compiler_params=pltpu.CompilerParams(dimension_semantics=("arbitrary", "arbitrary")),
        name="retention",
    )(qk, qk, v, z, d_inner, xi_b, zeta2, g2)


def _out_proj_kernel(om_ref, qm_ref, zm_ref, kv_ref, w_ref, x_ref, g_ref, o_ref, *, final_norm):
    scale = 1.0 / math.sqrt(HEAD_DIM)
    y = jnp.dot(om_ref[...], w_ref[:MAIN_W, :], preferred_element_type=f32)
    for hd in range(N_MEM_HEADS):
        cols = slice(hd * HEAD_DIM, (hd + 1) * HEAD_DIM)
        kh = kv_ref[:, cols]
        vh = kv_ref[:, MEM_W + hd * HEAD_DIM:MEM_W + (hd + 1) * HEAD_DIM]
        s = lax.dot_general(qm_ref[:, cols], kh, (((1,), (1,)), ((), ())),
                            preferred_element_type=f32) * scale
        e = jnp.exp(s - jnp.max(s, axis=-1, keepdims=True))
        p = e * (1.0 / jnp.sum(e, axis=-1, keepdims=True))
        memo = jnp.dot(p.astype(bf16), vh, preferred_element_type=f32)
        og = (memo * _silu(zm_ref[:, cols].astype(f32))).astype(bf16)
        y = y + jnp.dot(og, w_ref[MAIN_W + hd * HEAD_DIM:MAIN_W + (hd + 1) * HEAD_DIM, :],
                        preferred_element_type=f32)
    xn = x_ref[...] + y
    if final_norm:
        xn = _rmsnorm_rows(xn, g_ref[...])
    o_ref[...] = xn


def _out_proj(o_main, qm, z, kv, layer, w_out, x, g_final, *, final_norm):
    t_rows = x.shape[0]
    tiles_per_seq = SEQ // OUT_TM
    z_blk = MAIN_W // MEM_W
    return pl.pallas_call(
        functools.partial(_out_proj_kernel, final_norm=final_norm),
        grid=(t_rows // OUT_TM,),
        in_specs=[
            pl.BlockSpec((OUT_TM, MAIN_W), lambda i: (i, 0)),
            pl.BlockSpec((OUT_TM, MEM_W), lambda i: (i, 0)),
            pl.BlockSpec((OUT_TM, MEM_W), lambda i: (i, z_blk)),
            pl.BlockSpec((N_MEM, 2 * MEM_W), lambda i: (i // tiles_per_seq, layer)),
            pl.BlockSpec((INNER, D_MODEL), lambda i: (0, 0), pipeline_mode=pl.Buffered(1)),
            pl.BlockSpec((OUT_TM, D_MODEL), lambda i: (i, 0)),
            pl.BlockSpec((1, D_MODEL), lambda i: (0, 0)),
        ],
        out_specs=pl.BlockSpec((OUT_TM, D_MODEL), lambda i: (i, 0)),
        out_shape=jax.ShapeDtypeStruct((t_rows, D_MODEL), f32),
        compiler_params=pltpu.CompilerParams(dimension_semantics=("arbitrary",),
                                             vmem_limit_bytes=VMEM_LIMIT),
        name="out_proj_final" if final_norm else "out_proj",
    )(o_main, qm, z, kv, w_out, x, g_final.reshape(1, D_MODEL))


def _rotary_tables():
    half = RET_KEY_DIM // 2
    pos = jnp.arange(SEQ, dtype=f32)
    inv = 1.0 / (ROPE_BASE ** (jnp.arange(half, dtype=f32) / half))
    ang = pos[:, None] * inv[None, :]
    cos, sin = jnp.cos(ang), jnp.sin(ang)
    cos_h = jnp.concatenate([cos, cos], axis=-1)
    sin_h = jnp.concatenate([-sin, sin], axis=-1)
    k_scale = RET_KEY_DIM ** -0.5
    cos_t = jnp.concatenate([jnp.tile(cos_h, (1, N_MAIN_HEADS)),
                             jnp.tile(cos_h, (1, N_MAIN_HEADS)) * k_scale], axis=-1)
    sin_t = jnp.concatenate([jnp.tile(sin_h, (1, N_MAIN_HEADS)),
                             jnp.tile(sin_h, (1, N_MAIN_HEADS)) * k_scale], axis=-1)
    return cos_t, sin_t


def kernel(x, mem, norm_g, fox_w_in, fox_b_f, ret_w_in, mem_norm_g, w_mem_kv, w_out, final_norm_g):
    t_rows = BATCH * SEQ
    x2 = x.reshape(t_rows, D_MODEL)

    w_kv = jnp.concatenate([w_mem_kv[0], w_mem_kv[1]], axis=-1).astype(bf16)
    (kv,) = _norm_proj(mem.reshape(BATCH * N_MEM, D_MODEL), mem_norm_g, w_kv,
                       [2 * MEM_W * 2], [bf16], name="mem_kv_proj")

    w0 = fox_w_in[0]
    o_q, o_k, o_v, o_f, o_qm, o_z = np.cumsum([0, MAIN_W, MAIN_W, MAIN_W, N_MAIN_HEADS, MEM_W])
    w_f = jnp.pad(w0[:, o_f:o_f + N_MAIN_HEADS], ((0, 0), (0, F_PAD - N_MAIN_HEADS)))
    w0r = jnp.concatenate([w0[:, :o_f], w0[:, o_qm:], w_f], axis=-1).astype(bf16)
    q, k, vt, qm, z, f = _norm_proj(
        x2, norm_g[0], w0r, [MAIN_W, MAIN_W, MAIN_W, MEM_W, INNER, F_PAD],
        [bf16, bf16, bf16, bf16, bf16, f32], ["fox_q", "plain", "kt", "plain", "plain", "plain"],
        name="fox_in_proj")
    nb = _fox_gate(f, fox_b_f[0])
    o_main = _fox_attention(q, k, vt, nb, z)
    x2 = _out_proj(o_main, qm, z, kv, 0, w_out[0].astype(bf16), x2, final_norm_g, final_norm=False)

    w1 = ret_w_in[0].astype(bf16)
    qk, v, qm, z = _norm_proj(
        x2, norm_g[1], w1, [2 * RET_QK_W, MAIN_W, MEM_W, INNER], [bf16] * 4,
        ["rot", "plain", "plain", "plain"], tables=_rotary_tables(), name="ret_in_proj")
    o_main = _retention(qk, v, z)
    out = _out_proj(o_main, qm, z, kv, 1, w_out[1].astype(bf16), x2, final_norm_g, final_norm=True)
    return out.reshape(BATCH, SEQ, D_MODEL)
```

```python
import functools
import math

import jax
import jax.numpy as jnp
import numpy as np
from jax import lax
from jax.experimental import pallas as pl
from jax.experimental.pallas import tpu as pltpu

D_MODEL = 1024
BATCH = 8
SEQ = 2048
HEAD_DIM = 128
N_MAIN_HEADS = 12
N_MEM_HEADS = 4
N_MEM = 256
MAIN_W = N_MAIN_HEADS * HEAD_DIM
MEM_W = N_MEM_HEADS * HEAD_DIM
INNER = MAIN_W + MEM_W
RET_KEY_DIM = HEAD_DIM // 2
RET_QK_W = N_MAIN_HEADS * RET_KEY_DIM
RET_CHUNK = 128
ROPE_BASE = 10000.0
EPS = 1e-6
NEG = -1e30
LOG2E = 1.4426950408889634

LANES = 128
F_PAD = LANES
VMEM_LIMIT = 56 * 1024 * 1024

PROJ_TM = 512
PROJ_TN = 512
FOX_T = 256
FOX_HEADS_PER_STEP = 6
BIAS_TERMS = 3
FOX_SUM_ROWS = 16
FOX_Q_SCALE = LOG2E / math.sqrt(HEAD_DIM)
RET_PAIRS_PER_STEP = 3
RET_CHUNKS_PER_ITER = 2
OUT_TM = 512

f32 = jnp.float32
bf16 = jnp.bfloat16


def _silu(z):
    return z * (1.0 / (1.0 + jnp.exp(-z)))


def _rmsnorm_rows(x, g):
    ms = jnp.mean(x * x, axis=-1, keepdims=True)
    return (x * lax.rsqrt(ms + EPS)) * g


def _split_bf16(v):
    hi = v.astype(bf16)
    r1 = v - hi.astype(f32)
    mid = r1.astype(bf16)
    lo = (r1 - mid.astype(f32)).astype(bf16)
    return hi, mid, lo


def _norm_proj_kernel(*refs, out_widths, out_kinds):
    n_out = len(out_widths)
    x_ref, g_ref, w_ref = refs[:3]
    tab_refs = refs[3:len(refs) - n_out]
    out_refs = refs[len(refs) - n_out:]
    h = _rmsnorm_rows(x_ref[...], g_ref[...]).astype(bf16)
    tm = h.shape[0]
    if "rot" in out_kinds:
        cos_ref, sin_ref = tab_refs
        lane = lax.broadcasted_iota(jnp.int32, (tm, LANES), 1)
        first_half = (lane % RET_KEY_DIM) < (RET_KEY_DIM // 2)
    col = 0
    for o_ref, width, kind in zip(out_refs, out_widths, out_kinds):
        for c0 in range(0, width, PROJ_TN):
            cw = min(PROJ_TN, width - c0)
            y = jnp.dot(h, w_ref[:, col + c0:col + c0 + cw], preferred_element_type=f32)
            if kind == "rot":
                for s0 in range(0, cw, LANES):
                    ys = y[:, s0:s0 + LANES]
                    sw = jnp.where(first_half, pltpu.roll(ys, LANES - RET_KEY_DIM // 2, 1),
                                   pltpu.roll(ys, RET_KEY_DIM // 2, 1))
                    r = ys * cos_ref[:, c0 + s0:c0 + s0 + LANES] + sw * sin_ref[:, c0 + s0:c0 + s0 + LANES]
                    o_ref[:, c0 + s0:c0 + s0 + LANES] = r.astype(o_ref.dtype)
            elif kind == "kt":
                for r0 in range(tm // FOX_T):
                    for s0 in range(0, cw, LANES):
                        piece = y[r0 * FOX_T:(r0 + 1) * FOX_T, s0:s0 + LANES]
                        o_ref[r0, c0 + s0:c0 + s0 + LANES, :] = piece.T.astype(o_ref.dtype)
            elif kind == "fox_q":
                o_ref[:, c0:c0 + cw] = (y * FOX_Q_SCALE).astype(o_ref.dtype)
            else:
                o_ref[:, c0:c0 + cw] = y.astype(o_ref.dtype)
        col += width


def _norm_proj(x, g, w, out_widths, out_dtypes, out_kinds=None, *, tables=None, name):
    t_rows, d = x.shape
    n = w.shape[1]
    out_kinds = tuple(out_kinds or ["plain"] * len(out_widths))
    assert sum(out_widths) == n and t_rows % PROJ_TM == 0
    tiles_per_seq = SEQ // PROJ_TM
    in_specs = [
        pl.BlockSpec((PROJ_TM, d), lambda i: (i, 0)),
        pl.BlockSpec((1, d), lambda i: (0, 0)),
        pl.BlockSpec((d, n), lambda i: (0, 0), pipeline_mode=pl.Buffered(1)),
    ]
    args = [x, g.reshape(1, d), w]
    if "rot" in out_kinds:
        tw = out_widths[out_kinds.index("rot")]
        for tab in tables:
            in_specs.append(pl.BlockSpec((PROJ_TM, tw), lambda i: (i % tiles_per_seq, 0)))
            args.append(tab)
    out_specs, out_shape = [], []
    for wd, dt, kind in zip(out_widths, out_dtypes, out_kinds):
        if kind == "kt":
            kt_per_tile = PROJ_TM // FOX_T
            out_specs.append(pl.BlockSpec((None, kt_per_tile, wd, FOX_T),
                                          lambda i: (i // tiles_per_seq, i % tiles_per_seq, 0, 0)))
            out_shape.append(jax.ShapeDtypeStruct((t_rows // SEQ, SEQ // FOX_T, wd, FOX_T), dt))
        else:
            out_specs.append(pl.BlockSpec((PROJ_TM, wd), lambda i: (i, 0)))
            out_shape.append(jax.ShapeDtypeStruct((t_rows, wd), dt))
    return pl.pallas_call(
        functools.partial(_norm_proj_kernel, out_widths=tuple(out_widths), out_kinds=out_kinds),
        grid=(t_rows // PROJ_TM,),
        in_specs=in_specs,
        out_specs=out_specs,
        out_shape=out_shape,
        compiler_params=pltpu.CompilerParams(dimension_semantics=("arbitrary",),
                                             vmem_limit_bytes=VMEM_LIMIT),
        name=name,
    )(*args)


def _fox_gate_kernel(f_ref, bf_ref, o_ref):
    blk = LANES
    row = lax.broadcasted_iota(jnp.int32, (blk, blk), 0)
    coli = lax.broadcasted_iota(jnp.int32, (blk, blk), 1)
    tri = jnp.where(row >= coli, 1.0, 0.0).astype(bf16)
    spread = [jnp.where((coli == BIAS_TERMS * row + t) & (row < N_MAIN_HEADS), 1.0, 0.0).astype(bf16)
              for t in range(BIAS_TERMS)]
    carry = jnp.zeros((1, F_PAD), f32)
    for b0 in range(0, SEQ, blk):
        xg = f_ref[b0:b0 + blk, :] + bf_ref[...]
        ls = jnp.minimum(xg, 0.0) - jnp.log1p(jnp.exp(-jnp.abs(xg)))
        cs = sum(jnp.dot(tri, term, preferred_element_type=f32) for term in _split_bf16(ls)) + carry
        carry = cs[blk - 1:blk, :]
        bias = cs * (-LOG2E)
        nb = sum(jnp.dot(term, sp, preferred_element_type=f32)
                 for term, sp in zip(_split_bf16(bias), spread))
        o_ref[b0:b0 + blk, :] = nb.astype(o_ref.dtype)


def _fox_gate(f, b_f):
    bf_pad = jnp.zeros((1, F_PAD), f32).at[0, :N_MAIN_HEADS].set(b_f.astype(f32))
    return pl.pallas_call(
        _fox_gate_kernel,
        grid=(BATCH,),
        in_specs=[pl.BlockSpec((SEQ, F_PAD), lambda b: (b, 0)),
                  pl.BlockSpec((1, F_PAD), lambda b: (0, 0))],
        out_specs=pl.BlockSpec((SEQ, LANES), lambda b: (b, 0)),
        out_shape=jax.ShapeDtypeStruct((BATCH * SEQ, LANES), bf16),
        compiler_params=pltpu.CompilerParams(dimension_semantics=("arbitrary",)),
        name="fox_gate",
    )(f, bf_pad)


def _fox_attn_kernel(q_ref, k_ref, vt_ref, nb_ref, z_ref, o_ref, m_ref, acc_ref):
    t = FOX_T
    head0 = pl.program_id(1) * FOX_HEADS_PER_STEP
    heads = range(FOX_HEADS_PER_STEP)
    col = [slice(hd * HEAD_DIM, (hd + 1) * HEAD_DIM) for hd in heads]
    lane = lax.broadcasted_iota(jnp.int32, (t, LANES), 1)
    key_idx = lax.broadcasted_iota(jnp.int32, (t, t), 0)
    qry_idx = lax.broadcasted_iota(jnp.int32, (t, t), 1)
    causal = key_idx <= qry_idx
    ones_rows = jnp.ones((FOX_SUM_ROWS, t), bf16)

    def q_tile(qi, carry):
        qrows = pl.ds(pl.multiple_of(qi * t, t), t)
        q_aug = []
        for hd in heads:
            lo = BIAS_TERMS * (head0 + hd)
            sel = jnp.where((lane >= lo) & (lane < lo + BIAS_TERMS), 1.0, 0.0).astype(bf16)
            q_aug.append(jnp.concatenate([q_ref[qrows, col[hd]], sel], axis=1))
            m_ref[hd] = jnp.full((1, t), NEG, f32)
            acc_ref[hd] = jnp.zeros((HEAD_DIM + FOX_SUM_ROWS, t), f32)

        def step(j, masked):
            krows = pl.ds(pl.multiple_of(j * t, t), t)
            nbj = nb_ref[krows, :]
            u = [lax.dot_general(jnp.concatenate([k_ref[krows, col[hd]], nbj], axis=1), q_aug[hd],
                                 (((1,), (1,)), ((), ())), preferred_element_type=f32)
                 for hd in heads]
            p, alpha = [], []
            for hd in heads:
                uh = jnp.where(causal, u[hd], NEG) if masked else u[hd]
                m_old = m_ref[hd]
                m_new = jnp.maximum(m_old, jnp.max(uh, axis=0, keepdims=True))
                p.append(jnp.exp2(uh - m_new).astype(bf16))
                alpha.append(jnp.exp2(m_old - m_new))
                m_ref[hd] = m_new
            for hd in heads:
                vt_aug = jnp.concatenate([vt_ref[j, col[hd], :], ones_rows], axis=0)
                acc_ref[hd] = alpha[hd] * acc_ref[hd] + jnp.dot(vt_aug, p[hd], preferred_element_type=f32)

        def body(j, c):
            step(j, False)
            return c

        lax.fori_loop(0, qi, body, 0)
        step(qi, True)
        for hd in heads:
            acc = acc_ref[hd]
            out_t = acc[:HEAD_DIM, :] * (1.0 / acc[HEAD_DIM:HEAD_DIM + 1, :])
            o_ref[qrows, col[hd]] = (out_t.T * _silu(z_ref[qrows, col[hd]].astype(f32))).astype(o_ref.dtype)
        return carry

    lax.fori_loop(0, SEQ // t, q_tile, 0)


def _fox_attention(q, k, vt, nb, z):
    hw = FOX_HEADS_PER_STEP * HEAD_DIM
    return pl.pallas_call(
        _fox_attn_kernel,
        grid=(BATCH, N_MAIN_HEADS // FOX_HEADS_PER_STEP),
        in_specs=[
            pl.BlockSpec((SEQ, hw), lambda b, p: (b, p)),
            pl.BlockSpec((SEQ, hw), lambda b, p: (b, p)),
            pl.BlockSpec((None, SEQ // FOX_T, hw, FOX_T), lambda b, p: (b, 0, p, 0)),
            pl.BlockSpec((SEQ, LANES), lambda b, p: (b, 0)),
            pl.BlockSpec((SEQ, hw), lambda b, p: (b, p)),
        ],
        out_specs=pl.BlockSpec((SEQ, hw), lambda b, p: (b, p)),
        out_shape=jax.ShapeDtypeStruct((BATCH * SEQ, MAIN_W), bf16),
        scratch_shapes=[pltpu.VMEM((FOX_HEADS_PER_STEP, 1, FOX_T), f32),
                        pltpu.VMEM((FOX_HEADS_PER_STEP, HEAD_DIM + FOX_SUM_ROWS, FOX_T), f32)],
        compiler_params=pltpu.CompilerParams(dimension_semantics=("arbitrary", "arbitrary"),
                                             vmem_limit_bytes=VMEM_LIMIT),
        name="fox_attention",
    )(q, k, vt, nb, z)


def _retention_kernel(q_ref, k_ref, v_ref, z_ref, d_ref, xi_ref, zeta_ref, g_ref, o_ref, r_ref):
    c_len = RET_CHUNK
    r_ref[...] = jnp.zeros(r_ref.shape, f32)
    lane = lax.broadcasted_iota(jnp.int32, (c_len, LANES), 1)
    head_mask = (lane < RET_KEY_DIM, lane >= RET_KEY_DIM)
    pairs = range(RET_PAIRS_PER_STEP)
    chunks = range(RET_CHUNKS_PER_ITER)

    def body(it, carry):
        rows = [pl.ds(pl.multiple_of((it * RET_CHUNKS_PER_ITER + ci) * c_len, c_len), c_len) for ci in chunks]
        qa, s, upd = {}, {}, {}
        for ci in chunks:
            for pr in pairs:
                q2 = q_ref[rows[ci], pr * LANES:(pr + 1) * LANES]
                k2 = k_ref[rows[ci], pr * LANES:(pr + 1) * LANES]
                v2 = v_ref[rows[ci], pr * 2 * HEAD_DIM:(pr + 1) * 2 * HEAD_DIM]
                qa[ci, pr] = jnp.concatenate([jnp.where(head_mask[hd], q2, jnp.zeros_like(q2))
                                              for hd in range(2)], axis=0)
                s[ci, pr] = lax.dot_general(qa[ci, pr], k2, (((1,), (1,)), ((), ())),
                                            preferred_element_type=f32)
                kz = (k2.astype(f32) * zeta_ref[pr]).astype(bf16)
                upd[ci, pr] = lax.dot_general(kz, v2, (((0,), (0,)), ((), ())), preferred_element_type=f32)
        r_b = {}
        for pr in pairs:
            r_state = r_ref[pr]
            for ci in chunks:
                r_b[ci, pr] = r_state.astype(bf16)
                r_state = r_state * g_ref[pr] + upd[ci, pr]
            r_ref[pr] = r_state
        o = {}
        for ci in chunks:
            for pr in pairs:
                for hd in range(2):
                    hrows = slice(hd * c_len, (hd + 1) * c_len)
                    cols = slice((2 * pr + hd) * HEAD_DIM, (2 * pr + hd + 1) * HEAD_DIM)
                    inner = (s[ci, pr][hrows, :] * d_ref[2 * pr + hd]).astype(bf16)
                    intra = jnp.dot(inner, v_ref[rows[ci], cols], preferred_element_type=f32)
                    cross = jnp.dot(qa[ci, pr][hrows, :], r_b[ci, pr][:, hd * HEAD_DIM:(hd + 1) * HEAD_DIM],
                                    preferred_element_type=f32)
                    o[ci, pr, hd] = intra + cross * xi_ref[2 * pr + hd]
        for ci in chunks:
            for pr in pairs:
                for hd in range(2):
                    cols = slice((2 * pr + hd) * HEAD_DIM, (2 * pr + hd + 1) * HEAD_DIM)
                    oh = o[ci, pr, hd]
                    ms = jnp.mean(oh * oh, axis=-1, keepdims=True)
                    on = oh * lax.rsqrt(ms + EPS)
                    zg = _silu(z_ref[rows[ci], cols].astype(f32))
                    o_ref[rows[ci], cols] = (on * zg).astype(o_ref.dtype)
        return carry

    lax.fori_loop(0, SEQ // (c_len * RET_CHUNKS_PER_ITER), body, 0)


def _retention_tables():
    h, c_len = N_MAIN_HEADS, RET_CHUNK
    lg = jnp.log1p(-jnp.exp2(-5.0 - jnp.arange(h, dtype=f32)))
    n = jnp.arange(c_len, dtype=f32)
    diff = n[:, None] - n[None, :]
    d_inner = jnp.where(diff[None] >= 0,
                        jnp.exp(lg[:, None, None] * jnp.maximum(diff, 0.0)[None]), 0.0)
    xi = jnp.exp(lg[:, None] * (n[None, :] + 1.0))
    zeta = jnp.exp(lg[:, None] * (c_len - 1.0 - n[None, :]))
    g_chunk = jnp.exp(lg * c_len)
    xi_b = jnp.broadcast_to(xi[:, :, None], (h, c_len, HEAD_DIM))
    zeta2 = jnp.repeat(zeta.reshape(h // 2, 2, c_len).transpose(0, 2, 1), RET_KEY_DIM, axis=-1)
    g2 = jnp.broadcast_to(jnp.repeat(g_chunk.reshape(h // 2, 2), RET_KEY_DIM, axis=-1)[:, :, None],
                          (h // 2, 2 * RET_KEY_DIM, 2 * HEAD_DIM))
    return d_inner, xi_b, zeta2, g2


def _retention(qk, v, z):
    n_pairs = N_MAIN_HEADS // 2
    pp = RET_PAIRS_PER_STEP
    steps = n_pairs // pp
    d_inner, xi_b, zeta2, g2 = _retention_tables()
    c_len = RET_CHUNK
    return pl.pallas_call(
        _retention_kernel,
        grid=(BATCH, steps),
        in_specs=[
            pl.BlockSpec((SEQ, pp * LANES), lambda b, p: (b, p)),
            pl.BlockSpec((SEQ, pp * LANES), lambda b, p: (b, steps + p)),
            pl.BlockSpec((SEQ, pp * 2 * HEAD_DIM), lambda b, p: (b, p)),
            pl.BlockSpec((SEQ, pp * 2 * HEAD_DIM), lambda b, p: (b, p)),
            pl.BlockSpec((2 * pp, c_len, c_len), lambda b, p: (p, 0, 0)),
            pl.BlockSpec((2 * pp, c_len, HEAD_DIM), lambda b, p: (p, 0, 0)),
            pl.BlockSpec((pp, c_len, LANES), lambda b, p: (p, 0, 0)),
            pl.BlockSpec((pp, LANES, 2 * HEAD_DIM), lambda b, p: (p, 0, 0)),
        ],
        out_specs=pl.BlockSpec((SEQ, pp * 2 * HEAD_DIM), lambda b, p: (b, p)),
        out_shape=jax.ShapeDtypeStruct((BATCH * SEQ, MAIN_W), bf16),
        scratch_shapes=[pltpu.VMEM((pp, LANES, 2 * HEAD_DIM), f32)],
        compiler_params=pltpu.CompilerParams(dimension_semantics=("arbitrary", "arbitrary"),
                                             vmem_limit_bytes=VMEM_LIMIT),
        name="retention",
    )(qk, qk, v, z, d_inner, xi_b, zeta2, g2)


def _out_proj_kernel(om_ref, qm_ref, zm_ref, kv_ref, w_ref, x_ref, g_ref, o_ref, *, final_norm):
    scale = 1.0 / math.sqrt(HEAD_DIM)
    y = jnp.dot(om_ref[...], w_ref[:MAIN_W, :], preferred_element_type=f32)
    for hd in range(N_MEM_HEADS):
        cols = slice(hd * HEAD_DIM, (hd + 1) * HEAD_DIM)
        kh = kv_ref[:, cols]
        vh = kv_ref[:, MEM_W + hd * HEAD_DIM:MEM_W + (hd + 1) * HEAD_DIM]
        s = lax.dot_general(qm_ref[:, cols], kh, (((1,), (1,)), ((), ())),
                            preferred_element_type=f32) * scale
        e = jnp.exp(s - jnp.max(s, axis=-1, keepdims=True))
        p = e * (1.0 / jnp.sum(e, axis=-1, keepdims=True))
        memo = jnp.dot(p.astype(bf16), vh, preferred_element_type=f32)
        og = (memo * _silu(zm_ref[:, cols].astype(f32))).astype(bf16)
        y = y + jnp.dot(og, w_ref[MAIN_W + hd * HEAD_DIM:MAIN_W + (hd + 1) * HEAD_DIM, :],
                        preferred_element_type=f32)
    xn = x_ref[...] + y
    if final_norm:
        xn = _rmsnorm_rows(xn, g_ref[...])
    o_ref[...] = xn


def _out_proj(o_main, qm, z, kv, layer, w_out, x, g_final, *, final_norm):
    t_rows = x.shape[0]
    tiles_per_seq = SEQ // OUT_TM
    z_blk = MAIN_W // MEM_W
    return pl.pallas_call(
        functools.partial(_out_proj_kernel, final_norm=final_norm),
        grid=(t_rows // OUT_TM,),
        in_specs=[
            pl.BlockSpec((OUT_TM, MAIN_W), lambda i: (i, 0)),
            pl.BlockSpec((OUT_TM, MEM_W), lambda i: (i, 0)),
            pl.BlockSpec((OUT_TM, MEM_W), lambda i: (i, z_blk)),
            pl.BlockSpec((N_MEM, 2 * MEM_W), lambda i: (i // tiles_per_seq, layer)),
            pl.BlockSpec((INNER, D_MODEL), lambda i: (0, 0), pipeline_mode=pl.Buffered(1)),
            pl.BlockSpec((OUT_TM, D_MODEL), lambda i: (i, 0)),
            pl.BlockSpec((1, D_MODEL), lambda i: (0, 0)),
        ],
        out_specs=pl.BlockSpec((OUT_TM, D_MODEL), lambda i: (i, 0)),
        out_shape=jax.ShapeDtypeStruct((t_rows, D_MODEL), f32),
        compiler_params=pltpu.CompilerParams(dimension_semantics=("arbitrary",),
                                             vmem_limit_bytes=VMEM_LIMIT),
        name="out_proj_final" if final_norm else "out_proj",
    )(o_main, qm, z, kv, w_out, x, g_final.reshape(1, D_MODEL))


def _rotary_tables():
    half = RET_KEY_DIM // 2
    pos = jnp.arange(SEQ, dtype=f32)
    inv = 1.0 / (ROPE_BASE ** (jnp.arange(half, dtype=f32) / half))
    ang = pos[:, None] * inv[None, :]
    cos, sin = jnp.cos(ang), jnp.sin(ang)
    cos_h = jnp.concatenate([cos, cos], axis=-1)
    sin_h = jnp.concatenate([-sin, sin], axis=-1)
    k_scale = RET_KEY_DIM ** -0.5
    cos_t = jnp.concatenate([jnp.tile(cos_h, (1, N_MAIN_HEADS)),
                             jnp.tile(cos_h, (1, N_MAIN_HEADS)) * k_scale], axis=-1)
    sin_t = jnp.concatenate([jnp.tile(sin_h, (1, N_MAIN_HEADS)),
                             jnp.tile(sin_h, (1, N_MAIN_HEADS)) * k_scale], axis=-1)
    return cos_t, sin_t


def kernel(x, mem, norm_g, fox_w_in, fox_b_f, ret_w_in, mem_norm_g, w_mem_kv, w_out, final_norm_g):
    t_rows = BATCH * SEQ
    x2 = x.reshape(t_rows, D_MODEL)

    w_kv = jnp.concatenate([w_mem_kv[0], w_mem_kv[1]], axis=-1).astype(bf16)
    (kv,) = _norm_proj(mem.reshape(BATCH * N_MEM, D_MODEL), mem_norm_g, w_kv,
                       [2 * MEM_W * 2], [bf16], name="mem_kv_proj")

    w0 = fox_w_in[0]
    o_q, o_k, o_v, o_f, o_qm, o_z = np.cumsum([0, MAIN_W, MAIN_W, MAIN_W, N_MAIN_HEADS, MEM_W])
    w_f = jnp.pad(w0[:, o_f:o_f + N_MAIN_HEADS], ((0, 0), (0, F_PAD - N_MAIN_HEADS)))
    w0r = jnp.concatenate([w0[:, :o_f], w0[:, o_qm:], w_f], axis=-1).astype(bf16)
    q, k, vt, qm, z, f = _norm_proj(
        x2, norm_g[0], w0r, [MAIN_W, MAIN_W, MAIN_W, MEM_W, INNER, F_PAD],
        [bf16, bf16, bf16, bf16, bf16, f32], ["fox_q", "plain", "kt", "plain", "plain", "plain"],
        name="fox_in_proj")
    nb = _fox_gate(f, fox_b_f[0])
    o_main = _fox_attention(q, k, vt, nb, z)
    x2 = _out_proj(o_main, qm, z, kv, 0, w_out[0].astype(bf16), x2, final_norm_g, final_norm=False)

    w1 = ret_w_in[0].astype(bf16)
    qk, v, qm, z = _norm_proj(
        x2, norm_g[1], w1, [2 * RET_QK_W, MAIN_W, MEM_W, INNER], [bf16] * 4,
        ["rot", "plain", "plain", "plain"], tables=_rotary_tables(), name="ret_in_proj")
    o_main = _retention(qk, v, z)
    out = _out_proj(o_main, qm, z, kv, 1, w_out[1].astype(bf16), x2, final_norm_g, final_norm=True)
    return out.reshape(BATCH, SEQ, D_MODEL)
```

```python
import functools
import math

import jax
import jax.numpy as jnp
import numpy as np
from jax import lax
from jax.experimental import pallas as pl
from jax.experimental.pallas import tpu as pltpu

D_MODEL = 1024
BATCH = 8
SEQ = 2048
HEAD_DIM = 128
N_MAIN_HEADS = 12
N_MEM_HEADS = 4
N_MEM = 256
MAIN_W = N_MAIN_HEADS * HEAD_DIM
MEM_W = N_MEM_HEADS * HEAD_DIM
INNER = MAIN_W + MEM_W
RET_KEY_DIM = HEAD_DIM // 2
RET_QK_W = N_MAIN_HEADS * RET_KEY_DIM
RET_CHUNK = 128
ROPE_BASE = 10000.0
EPS = 1e-6
NEG = -1e30
LOG2E = 1.4426950408889634

LANES = 128
F_PAD = LANES
VMEM_LIMIT = 56 * 1024 * 1024

PROJ_TM = 512
PROJ_TN = 512
FOX_T = 256
FOX_HEADS_PER_STEP = 6
BIAS_TERMS = 3
FOX_SUM_ROWS = 16
FOX_Q_SCALE = LOG2E / math.sqrt(HEAD_DIM)
RET_PAIRS_PER_STEP = 3
RET_CHUNKS_PER_ITER = 2
OUT_TM = 512

f32 = jnp.float32
bf16 = jnp.bfloat16


def _silu(z):
    return z * (1.0 / (1.0 + jnp.exp(-z)))


def _rmsnorm_rows(x, g):
    ms = jnp.mean(x * x, axis=-1, keepdims=True)
    return (x * lax.rsqrt(ms + EPS)) * g


def _split_bf16(v):
    hi = v.astype(bf16)
    r1 = v - hi.astype(f32)
    mid = r1.astype(bf16)
    lo = (r1 - mid.astype(f32)).astype(bf16)
    return hi, mid, lo


def _norm_proj_kernel(*refs, n_weights, w_transposed, out_widths, out_kinds):
    n_out = len(out_widths)
    x_ref, g_ref = refs[:2]
    w_refs = refs[2:2 + n_weights]
    tab_refs = refs[2 + n_weights:len(refs) - n_out]
    out_refs = refs[len(refs) - n_out:]
    h = _rmsnorm_rows(x_ref[...], g_ref[...]).astype(bf16)
    tm = h.shape[0]
    if "rot" in out_kinds:
        cos_ref, sin_ref = tab_refs
        lane = lax.broadcasted_iota(jnp.int32, (tm, LANES), 1)
        first_half = (lane % RET_KEY_DIM) < (RET_KEY_DIM // 2)
    out_axis = 0 if w_transposed else 1
    wi, col = 0, 0
    for o_ref, width, kind in zip(out_refs, out_widths, out_kinds):
        if col == w_refs[wi].shape[out_axis]:
            wi, col = wi + 1, 0
        w_ref = w_refs[wi]
        for c0 in range(0, width, PROJ_TN):
            cw = min(PROJ_TN, width - c0)
            if w_transposed:
                y = lax.dot_general(h, w_ref[col + c0:col + c0 + cw, :], (((1,), (1,)), ((), ())),
                                    preferred_element_type=f32)
            else:
                y = jnp.dot(h, w_ref[:, col + c0:col + c0 + cw], preferred_element_type=f32)
            if kind == "rot":
                for s0 in range(0, cw, LANES):
                    ys = y[:, s0:s0 + LANES]
                    sw = jnp.where(first_half, pltpu.roll(ys, LANES - RET_KEY_DIM // 2, 1),
                                   pltpu.roll(ys, RET_KEY_DIM // 2, 1))
                    r = ys * cos_ref[:, c0 + s0:c0 + s0 + LANES] + sw * sin_ref[:, c0 + s0:c0 + s0 + LANES]
                    o_ref[:, c0 + s0:c0 + s0 + LANES] = r.astype(o_ref.dtype)
            elif kind == "kt":
                for r0 in range(tm // FOX_T):
                    for s0 in range(0, cw, LANES):
                        piece = y[r0 * FOX_T:(r0 + 1) * FOX_T, s0:s0 + LANES]
                        o_ref[r0, c0 + s0:c0 + s0 + LANES, :] = piece.T.astype(o_ref.dtype)
            elif kind == "fox_q":
                o_ref[:, c0:c0 + cw] = (y * FOX_Q_SCALE).astype(o_ref.dtype)
            else:
                o_ref[:, c0:c0 + cw] = y.astype(o_ref.dtype)
        col += width


def _norm_proj(x, g, weights, out_widths, out_dtypes, out_kinds=None, *, w_transposed=False, tables=None,
               name):
    t_rows, d = x.shape
    out_kinds = tuple(out_kinds or ["plain"] * len(out_widths))
    out_axis = 0 if w_transposed else 1
    assert sum(out_widths) == sum(w.shape[out_axis] for w in weights) and t_rows % PROJ_TM == 0
    tiles_per_seq = SEQ // PROJ_TM
    in_specs = [
        pl.BlockSpec((PROJ_TM, d), lambda i: (i, 0)),
        pl.BlockSpec((1, d), lambda i: (0, 0)),
    ]
    in_specs += [pl.BlockSpec(w.shape, lambda i: (0, 0), pipeline_mode=pl.Buffered(1)) for w in weights]
    args = [x, g.reshape(1, d), *weights]
    if "rot" in out_kinds:
        tw = out_widths[out_kinds.index("rot")]
        for tab in tables:
            in_specs.append(pl.BlockSpec((PROJ_TM, tw), lambda i: (i % tiles_per_seq, 0)))
            args.append(tab)
    out_specs, out_shape = [], []
    for wd, dt, kind in zip(out_widths, out_dtypes, out_kinds):
        if kind == "kt":
            kt_per_tile = PROJ_TM // FOX_T
            out_specs.append(pl.BlockSpec((None, kt_per_tile, wd, FOX_T),
                                          lambda i: (i // tiles_per_seq, i % tiles_per_seq, 0, 0)))
            out_shape.append(jax.ShapeDtypeStruct((t_rows // SEQ, SEQ // FOX_T, wd, FOX_T), dt))
        else:
            out_specs.append(pl.BlockSpec((PROJ_TM, wd), lambda i: (i, 0)))
            out_shape.append(jax.ShapeDtypeStruct((t_rows, wd), dt))
    return pl.pallas_call(
        functools.partial(_norm_proj_kernel, n_weights=len(weights), w_transposed=w_transposed,
                          out_widths=tuple(out_widths), out_kinds=out_kinds),
        grid=(t_rows // PROJ_TM,),
        in_specs=in_specs,
        out_specs=out_specs,
        out_shape=out_shape,
        compiler_params=pltpu.CompilerParams(dimension_semantics=("arbitrary",),
                                             vmem_limit_bytes=VMEM_LIMIT),
        name=name,
    )(*args)


def _fox_gate_kernel(f_ref, bf_ref, o_ref):
    blk = LANES
    row = lax.broadcasted_iota(jnp.int32, (blk, blk), 0)
    coli = lax.broadcasted_iota(jnp.int32, (blk, blk), 1)
    tri = jnp.where(row >= coli, 1.0, 0.0).astype(bf16)
    spread = [jnp.where((coli == BIAS_TERMS * row + t) & (row < N_MAIN_HEADS), 1.0, 0.0).astype(bf16)
              for t in range(BIAS_TERMS)]
    carry = jnp.zeros((1, F_PAD), f32)
    for b0 in range(0, SEQ, blk):
        xg = f_ref[b0:b0 + blk, :] + bf_ref[...]
        ls = jnp.minimum(xg, 0.0) - jnp.log1p(jnp.exp(-jnp.abs(xg)))
        cs = sum(jnp.dot(tri, term, preferred_element_type=f32) for term in _split_bf16(ls)) + carry
        carry = cs[blk - 1:blk, :]
        bias = cs * (-LOG2E)
        nb = sum(jnp.dot(term, sp, preferred_element_type=f32)
                 for term, sp in zip(_split_bf16(bias), spread))
        o_ref[b0:b0 + blk, :] = nb.astype(o_ref.dtype)


def _fox_gate(f, b_f):
    bf_pad = jnp.zeros((1, F_PAD), f32).at[0, :N_MAIN_HEADS].set(b_f.astype(f32))
    return pl.pallas_call(
        _fox_gate_kernel,
        grid=(BATCH,),
        in_specs=[pl.BlockSpec((SEQ, F_PAD), lambda b: (b, 0)),
                  pl.BlockSpec((1, F_PAD), lambda b: (0, 0))],
        out_specs=pl.BlockSpec((SEQ, LANES), lambda b: (b, 0)),
        out_shape=jax.ShapeDtypeStruct((BATCH * SEQ, LANES), bf16),
        compiler_params=pltpu.CompilerParams(dimension_semantics=("arbitrary",)),
        name="fox_gate",
    )(f, bf_pad)


def _fox_attn_kernel(q_ref, k_ref, vt_ref, nb_ref, z_ref, o_ref, m_ref, acc_ref):
    t = FOX_T
    head0 = pl.program_id(1) * FOX_HEADS_PER_STEP
    heads = range(FOX_HEADS_PER_STEP)
    col = [slice(hd * HEAD_DIM, (hd + 1) * HEAD_DIM) for hd in heads]
    lane = lax.broadcasted_iota(jnp.int32, (t, LANES), 1)
    key_idx = lax.broadcasted_iota(jnp.int32, (t, t), 0)
    qry_idx = lax.broadcasted_iota(jnp.int32, (t, t), 1)
    causal = key_idx <= qry_idx
    ones_rows = jnp.ones((FOX_SUM_ROWS, t), bf16)

    def q_tile(qi, carry):
        qrows = pl.ds(pl.multiple_of(qi * t, t), t)
        q_aug = []
        for hd in heads:
            lo = BIAS_TERMS * (head0 + hd)
            sel = jnp.where((lane >= lo) & (lane < lo + BIAS_TERMS), 1.0, 0.0).astype(bf16)
            q_aug.append(jnp.concatenate([q_ref[qrows, col[hd]], sel], axis=1))
            m_ref[hd] = jnp.full((1, t), NEG, f32)
            acc_ref[hd] = jnp.zeros((HEAD_DIM + FOX_SUM_ROWS, t), f32)

        def step(j, masked):
            krows = pl.ds(pl.multiple_of(j * t, t), t)
            nbj = nb_ref[krows, :]
            u = [lax.dot_general(jnp.concatenate([k_ref[krows, col[hd]], nbj], axis=1), q_aug[hd],
                                 (((1,), (1,)), ((), ())), preferred_element_type=f32)
                 for hd in heads]
            p, alpha = [], []
            for hd in heads:
                uh = jnp.where(causal, u[hd], NEG) if masked else u[hd]
                m_old = m_ref[hd]
                m_new = jnp.maximum(m_old, jnp.max(uh, axis=0, keepdims=True))
                p.append(jnp.exp2(uh - m_new).astype(bf16))
                alpha.append(jnp.exp2(m_old - m_new))
                m_ref[hd] = m_new
            for hd in heads:
                vt_aug = jnp.concatenate([vt_ref[j, col[hd], :], ones_rows], axis=0)
                acc_ref[hd] = alpha[hd] * acc_ref[hd] + jnp.dot(vt_aug, p[hd], preferred_element_type=f32)

        def body(j, c):
            step(j, False)
            return c

        lax.fori_loop(0, qi, body, 0)
        step(qi, True)
        for hd in heads:
            acc = acc_ref[hd]
            out_t = acc[:HEAD_DIM, :] * (1.0 / acc[HEAD_DIM:HEAD_DIM + 1, :])
            o_ref[qrows, col[hd]] = (out_t.T * _silu(z_ref[qrows, col[hd]].astype(f32))).astype(o_ref.dtype)
        return carry

    lax.fori_loop(0, SEQ // t, q_tile, 0)


def _fox_attention(q, k, vt, nb, z):
    hw = FOX_HEADS_PER_STEP * HEAD_DIM
    return pl.pallas_call(
        _fox_attn_kernel,
        grid=(BATCH, N_MAIN_HEADS // FOX_HEADS_PER_STEP),
        in_specs=[
            pl.BlockSpec((SEQ, hw), lambda b, p: (b, p)),
            pl.BlockSpec((SEQ, hw), lambda b, p: (b, p)),
            pl.BlockSpec((None, SEQ // FOX_T, hw, FOX_T), lambda b, p: (b, 0, p, 0)),
            pl.BlockSpec((SEQ, LANES), lambda b, p: (b, 0)),
            pl.BlockSpec((SEQ, hw), lambda b, p: (b, p)),
        ],
        out_specs=pl.BlockSpec((SEQ, hw), lambda b, p: (b, p)),
        out_shape=jax.ShapeDtypeStruct((BATCH * SEQ, MAIN_W), bf16),
        scratch_shapes=[pltpu.VMEM((FOX_HEADS_PER_STEP, 1, FOX_T), f32),
                        pltpu.VMEM((FOX_HEADS_PER_STEP, HEAD_DIM + FOX_SUM_ROWS, FOX_T), f32)],
        compiler_params=pltpu.CompilerParams(dimension_semantics=("arbitrary", "arbitrary"),
                                             vmem_limit_bytes=VMEM_LIMIT),
        name="fox_attention",
    )(q, k, vt, nb, z)


def _retention_kernel(q_ref, k_ref, v_ref, z_ref, d_ref, xi_ref, zeta_ref, g_ref, o_ref, r_ref):
    c_len = RET_CHUNK
    r_ref[...] = jnp.zeros(r_ref.shape, f32)
    lane = lax.broadcasted_iota(jnp.int32, (c_len, LANES), 1)
    head_mask = (lane < RET_KEY_DIM, lane >= RET_KEY_DIM)
    pairs = range(RET_PAIRS_PER_STEP)
    chunks = range(RET_CHUNKS_PER_ITER)

    def body(it, carry):
        rows = [pl.ds(pl.multiple_of((it * RET_CHUNKS_PER_ITER + ci) * c_len, c_len), c_len) for ci in chunks]
        qa, s, upd = {}, {}, {}
        for ci in chunks:
            for pr in pairs:
                q2 = q_ref[rows[ci], pr * LANES:(pr + 1) * LANES]
                k2 = k_ref[rows[ci], pr * LANES:(pr + 1) * LANES]
                v2 = v_ref[rows[ci], pr * 2 * HEAD_DIM:(pr + 1) * 2 * HEAD_DIM]
                qa[ci, pr] = jnp.concatenate([jnp.where(head_mask[hd], q2, jnp.zeros_like(q2))
                                              for hd in range(2)], axis=0)
                s[ci, pr] = lax.dot_general(qa[ci, pr], k2, (((1,), (1,)), ((), ())),
                                            preferred_element_type=f32)
                kz = (k2.astype(f32) * zeta_ref[pr]).astype(bf16)
                upd[ci, pr] = lax.dot_general(kz, v2, (((0,), (0,)), ((), ())), preferred_element_type=f32)
        r_b = {}
        for pr in pairs:
            r_state = r_ref[pr]
            for ci in chunks:
                r_b[ci, pr] = r_state.astype(bf16)
                r_state = r_state * g_ref[pr] + upd[ci, pr]
            r_ref[pr] = r_state
        o = {}
        for ci in chunks:
            for pr in pairs:
                for hd in range(2):
                    hrows = slice(hd * c_len, (hd + 1) * c_len)
                    cols = slice((2 * pr + hd) * HEAD_DIM, (2 * pr + hd + 1) * HEAD_DIM)
                    inner = (s[ci, pr][hrows, :] * d_ref[2 * pr + hd]).astype(bf16)
                    intra = jnp.dot(inner, v_ref[rows[ci], cols], preferred_element_type=f32)
                    cross = jnp.dot(qa[ci, pr][hrows, :], r_b[ci, pr][:, hd * HEAD_DIM:(hd + 1) * HEAD_DIM],
                                    preferred_element_type=f32)
                    o[ci, pr, hd] = intra + cross * xi_ref[2 * pr + hd]
        for ci in chunks:
            for pr in pairs:
                for hd in range(2):
                    cols = slice((2 * pr + hd) * HEAD_DIM, (2 * pr + hd + 1) * HEAD_DIM)
                    oh = o[ci, pr, hd]
                    ms = jnp.mean(oh * oh, axis=-1, keepdims=True)
                    on = oh * lax.rsqrt(ms + EPS)
                    zg = _silu(z_ref[rows[ci], cols].astype(f32))
                    o_ref[rows[ci], cols] = (on * zg).astype(o_ref.dtype)
        return carry

    lax.fori_loop(0, SEQ // (c_len * RET_CHUNKS_PER_ITER), body, 0)


def _retention_tables():
    h, c_len = N_MAIN_HEADS, RET_CHUNK
    lg = jnp.log1p(-jnp.exp2(-5.0 - jnp.arange(h, dtype=f32)))
    n = jnp.arange(c_len, dtype=f32)
    diff = n[:, None] - n[None, :]
    d_inner = jnp.where(diff[None] >= 0,
                        jnp.exp(lg[:, None, None] * jnp.maximum(diff, 0.0)[None]), 0.0)
    xi = jnp.exp(lg[:, None] * (n[None, :] + 1.0))
    zeta = jnp.exp(lg[:, None] * (c_len - 1.0 - n[None, :]))
    g_chunk = jnp.exp(lg * c_len)
    xi_b = jnp.broadcast_to(xi[:, :, None], (h, c_len, HEAD_DIM))
    zeta2 = jnp.repeat(zeta.reshape(h // 2, 2, c_len).transpose(0, 2, 1), RET_KEY_DIM, axis=-1)
    g2 = jnp.broadcast_to(jnp.repeat(g_chunk.reshape(h // 2, 2), RET_KEY_DIM, axis=-1)[:, :, None],
                          (h // 2, 2 * RET_KEY_DIM, 2 * HEAD_DIM))
    return d_inner, xi_b, zeta2, g2


def _retention(qk, v, z):
    n_pairs = N_MAIN_HEADS // 2
    pp = RET_PAIRS_PER_STEP
    steps = n_pairs // pp
    d_inner, xi_b, zeta2, g2 = _retention_tables()
    c_len = RET_CHUNK
    return pl.pallas_call(
        _retention_kernel,
        grid=(BATCH, steps),
        in_specs=[
            pl.BlockSpec((SEQ, pp * LANES), lambda b, p: (b, p)),
            pl.BlockSpec((SEQ, pp * LANES), lambda b, p: (b, steps + p)),
            pl.BlockSpec((SEQ, pp * 2 * HEAD_DIM), lambda b, p: (b, p)),
            pl.BlockSpec((SEQ, pp * 2 * HEAD_DIM), lambda b, p: (b, p)),
            pl.BlockSpec((2 * pp, c_len, c_len), lambda b, p: (p, 0, 0)),
            pl.BlockSpec((2 * pp, c_len, HEAD_DIM), lambda b, p: (p, 0, 0)),
            pl.BlockSpec((pp, c_len, LANES), lambda b, p: (p, 0, 0)),
            pl.BlockSpec((pp, LANES, 2 * HEAD_DIM), lambda b, p: (p, 0, 0)),
        ],
        out_specs=pl.BlockSpec((SEQ, pp * 2 * HEAD_DIM), lambda b, p: (b, p)),
        out_shape=jax.ShapeDtypeStruct((BATCH * SEQ, MAIN_W), bf16),
        scratch_shapes=[pltpu.VMEM((pp, LANES, 2 * HEAD_DIM), f32)],
        compiler_params=pltpu.CompilerParams(dimension_semantics=("arbitrary", "arbitrary"),
                                             vmem_limit_bytes=VMEM_LIMIT),
        name="retention",
    )(qk, qk, v, z, d_inner, xi_b, zeta2, g2)


def _out_proj_kernel(om_ref, qm_ref, zm_ref, kv_ref, w_ref, x_ref, g_ref, o_ref, *, final_norm):
    scale = 1.0 / math.sqrt(HEAD_DIM)
    heads = range(N_MEM_HEADS)
    col = [slice(hd * HEAD_DIM, (hd + 1) * HEAD_DIM) for hd in heads]
    half = D_MODEL // 2
    s = [lax.dot_general(qm_ref[:, col[hd]], kv_ref[:, col[hd]], (((1,), (1,)), ((), ())),
                         preferred_element_type=f32) * scale for hd in heads]
    y_lo = jnp.dot(om_ref[...], w_ref[:MAIN_W, :half], preferred_element_type=f32)
    p = []
    for hd in heads:
        e = jnp.exp(s[hd] - jnp.max(s[hd], axis=-1, keepdims=True))
        p.append((e * (1.0 / jnp.sum(e, axis=-1, keepdims=True))).astype(bf16))
    memo = [jnp.dot(p[hd], kv_ref[:, MEM_W + hd * HEAD_DIM:MEM_W + (hd + 1) * HEAD_DIM],
                    preferred_element_type=f32) for hd in heads]
    y_hi = jnp.dot(om_ref[...], w_ref[:MAIN_W, half:], preferred_element_type=f32)
    og = jnp.concatenate([(memo[hd] * _silu(zm_ref[:, col[hd]].astype(f32))).astype(bf16) for hd in heads],
                         axis=1)
    y = jnp.concatenate([y_lo, y_hi], axis=1) + jnp.dot(og, w_ref[MAIN_W:, :], preferred_element_type=f32)
    xn = x_ref[...] + y
    if final_norm:
        xn = _rmsnorm_rows(xn, g_ref[...])
    o_ref[...] = xn


def _out_proj(o_main, qm, z, kv, w_out, x, g_final, *, final_norm):
    t_rows = x.shape[0]
    tiles_per_seq = SEQ // OUT_TM
    z_blk = MAIN_W // MEM_W
    return pl.pallas_call(
        functools.partial(_out_proj_kernel, final_norm=final_norm),
        grid=(t_rows // OUT_TM,),
        in_specs=[
            pl.BlockSpec((OUT_TM, MAIN_W), lambda i: (i, 0)),
            pl.BlockSpec((OUT_TM, MEM_W), lambda i: (i, 0)),
            pl.BlockSpec((OUT_TM, MEM_W), lambda i: (i, z_blk)),
            pl.BlockSpec((N_MEM, 2 * MEM_W), lambda i: (i // tiles_per_seq, 0)),
            pl.BlockSpec((INNER, D_MODEL), lambda i: (0, 0), pipeline_mode=pl.Buffered(1)),
            pl.BlockSpec((OUT_TM, D_MODEL), lambda i: (i, 0)),
            pl.BlockSpec((1, D_MODEL), lambda i: (0, 0)),
        ],
        out_specs=pl.BlockSpec((OUT_TM, D_MODEL), lambda i: (i, 0)),
        out_shape=jax.ShapeDtypeStruct((t_rows, D_MODEL), f32),
        compiler_params=pltpu.CompilerParams(dimension_semantics=("arbitrary",),
                                             vmem_limit_bytes=VMEM_LIMIT),
        name="out_proj_final" if final_norm else "out_proj",
    )(o_main, qm, z, kv, w_out, x, g_final.reshape(1, D_MODEL))


def _rotary_tables():
    half = RET_KEY_DIM // 2
    pos = jnp.arange(SEQ, dtype=f32)
    inv = 1.0 / (ROPE_BASE ** (jnp.arange(half, dtype=f32) / half))
    ang = pos[:, None] * inv[None, :]
    cos, sin = jnp.cos(ang), jnp.sin(ang)
    cos_h = jnp.concatenate([cos, cos], axis=-1)
    sin_h = jnp.concatenate([-sin, sin], axis=-1)
    k_scale = RET_KEY_DIM ** -0.5
    cos_t = jnp.concatenate([jnp.tile(cos_h, (1, N_MAIN_HEADS)),
                             jnp.tile(cos_h, (1, N_MAIN_HEADS)) * k_scale], axis=-1)
    sin_t = jnp.concatenate([jnp.tile(sin_h, (1, N_MAIN_HEADS)),
                             jnp.tile(sin_h, (1, N_MAIN_HEADS)) * k_scale], axis=-1)
    return cos_t, sin_t


def kernel(x, mem, norm_g, fox_w_in, fox_b_f, ret_w_in, mem_norm_g, w_mem_kv, w_out, final_norm_g):
    t_rows = BATCH * SEQ
    x2 = x.reshape(t_rows, D_MODEL)

    kv0, kv1 = _norm_proj(mem.reshape(BATCH * N_MEM, D_MODEL), mem_norm_g,
                          [w_mem_kv[0].astype(bf16), w_mem_kv[1].astype(bf16)],
                          [2 * MEM_W, 2 * MEM_W], [bf16, bf16], name="mem_kv_proj")

    o_f = 3 * MAIN_W
    o_qm = o_f + N_MAIN_HEADS
    w0t = jnp.swapaxes(fox_w_in[0], 0, 1).astype(bf16)
    w_qkv = w0t[:o_f]
    w_qmz = w0t[o_qm:]
    w_f = jnp.pad(w0t[o_f:o_qm], ((0, F_PAD - N_MAIN_HEADS), (0, 0)))
    q, k, vt, qm, z, f = _norm_proj(
        x2, norm_g[0], [w_qkv, w_qmz, w_f], [MAIN_W, MAIN_W, MAIN_W, MEM_W, INNER, F_PAD],
        [bf16, bf16, bf16, bf16, bf16, f32], ["fox_q", "plain", "kt", "plain", "plain", "plain"],
        w_transposed=True, name="fox_in_proj")
    nb = _fox_gate(f, fox_b_f[0])
    o_main = _fox_attention(q, k, vt, nb, z)
    x2 = _out_proj(o_main, qm, z, kv0, w_out[0].astype(bf16), x2, final_norm_g, final_norm=False)

    w1 = ret_w_in[0].astype(bf16)
    qk, v, qm, z = _norm_proj(
        x2, norm_g[1], [w1], [2 * RET_QK_W, MAIN_W, MEM_W, INNER], [bf16] * 4,
        ["rot", "plain", "plain", "plain"], tables=_rotary_tables(), name="ret_in_proj")
    o_main = _retention(qk, v, z)
    out = _out_proj(o_main, qm, z, kv1, w_out[1].astype(bf16), x2, final_norm_g, final_norm=True)
    return out.reshape(BATCH, SEQ, D_MODEL)
```

```python
import functools
import math

import jax
import jax.numpy as jnp
import numpy as np
from jax import lax
from jax.experimental import pallas as pl
from jax.experimental.pallas import tpu as pltpu

D_MODEL = 1024
BATCH = 8
SEQ = 2048
HEAD_DIM = 128
N_MAIN_HEADS = 12
N_MEM_HEADS = 4
N_MEM = 256
MAIN_W = N_MAIN_HEADS * HEAD_DIM
MEM_W = N_MEM_HEADS * HEAD_DIM
INNER = MAIN_W + MEM_W
RET_KEY_DIM = HEAD_DIM // 2
RET_QK_W = N_MAIN_HEADS * RET_KEY_DIM
RET_CHUNK = 128
ROPE_BASE = 10000.0
EPS = 1e-6
NEG = -1e30
LOG2E = 1.4426950408889634

LANES = 128
F_PAD = LANES
VMEM_LIMIT = 56 * 1024 * 1024

PROJ_TM = 512
PROJ_TN = 512
FOX_T = 256
FOX_HEADS_PER_STEP = 6
BIAS_TERMS = 3
FOX_SUM_ROWS = 16
FOX_Q_SCALE = LOG2E / math.sqrt(HEAD_DIM)
RET_PAIRS_PER_STEP = 3
RET_CHUNKS_PER_ITER = 2
OUT_TM = 512

f32 = jnp.float32
bf16 = jnp.bfloat16


def _silu(z):
    return z * (1.0 / (1.0 + jnp.exp(-z)))


def _rmsnorm_rows(x, g):
    ms = jnp.mean(x * x, axis=-1, keepdims=True)
    return (x * lax.rsqrt(ms + EPS)) * g


def _split_bf16(v):
    hi = v.astype(bf16)
    r1 = v - hi.astype(f32)
    mid = r1.astype(bf16)
    lo = (r1 - mid.astype(f32)).astype(bf16)
    return hi, mid, lo


def _norm_proj_kernel(*refs, n_weights, w_transposed, out_widths, out_kinds):
    n_out = len(out_widths)
    x_ref, g_ref = refs[:2]
    w_refs = refs[2:2 + n_weights]
    tab_refs = refs[2 + n_weights:len(refs) - n_out]
    out_refs = refs[len(refs) - n_out:]
    h = _rmsnorm_rows(x_ref[...], g_ref[...]).astype(bf16)
    tm = h.shape[0]
    if "rot" in out_kinds:
        cos_ref, sin_ref = tab_refs
        lane = lax.broadcasted_iota(jnp.int32, (tm, LANES), 1)
        first_half = (lane % RET_KEY_DIM) < (RET_KEY_DIM // 2)
    out_axis = 0 if w_transposed else 1
    wi, col = 0, 0
    for o_ref, width, kind in zip(out_refs, out_widths, out_kinds):
        if col == w_refs[wi].shape[out_axis]:
            wi, col = wi + 1, 0
        w_ref = w_refs[wi]
        for c0 in range(0, width, PROJ_TN):
            cw = min(PROJ_TN, width - c0)
            if w_transposed:
                y = lax.dot_general(h, w_ref[col + c0:col + c0 + cw, :], (((1,), (1,)), ((), ())),
                                    preferred_element_type=f32)
            else:
                y = jnp.dot(h, w_ref[:, col + c0:col + c0 + cw], preferred_element_type=f32)
            if kind == "rot":
                for s0 in range(0, cw, LANES):
                    ys = y[:, s0:s0 + LANES]
                    sw = jnp.where(first_half, pltpu.roll(ys, LANES - RET_KEY_DIM // 2, 1),
                                   pltpu.roll(ys, RET_KEY_DIM // 2, 1))
                    r = ys * cos_ref[:, c0 + s0:c0 + s0 + LANES] + sw * sin_ref[:, c0 + s0:c0 + s0 + LANES]
                    o_ref[:, c0 + s0:c0 + s0 + LANES] = r.astype(o_ref.dtype)
            elif kind == "kt":
                for r0 in range(tm // FOX_T):
                    for s0 in range(0, cw, LANES):
                        piece = y[r0 * FOX_T:(r0 + 1) * FOX_T, s0:s0 + LANES]
                        o_ref[r0, c0 + s0:c0 + s0 + LANES, :] = piece.T.astype(o_ref.dtype)
            elif kind == "fox_q":
                o_ref[:, c0:c0 + cw] = (y * FOX_Q_SCALE).astype(o_ref.dtype)
            else:
                o_ref[:, c0:c0 + cw] = y.astype(o_ref.dtype)
        col += width


def _norm_proj(x, g, weights, out_widths, out_dtypes, out_kinds=None, *, w_transposed=False, tables=None,
               name):
    t_rows, d = x.shape
    out_kinds = tuple(out_kinds or ["plain"] * len(out_widths))
    out_axis = 0 if w_transposed else 1
    assert sum(out_widths) == sum(w.shape[out_axis] for w in weights) and t_rows % PROJ_TM == 0
    tiles_per_seq = SEQ // PROJ_TM
    in_specs = [
        pl.BlockSpec((PROJ_TM, d), lambda i: (i, 0)),
        pl.BlockSpec((1, d), lambda i: (0, 0)),
    ]
    in_specs += [pl.BlockSpec(w.shape, lambda i: (0, 0), pipeline_mode=pl.Buffered(1)) for w in weights]
    args = [x, g.reshape(1, d), *weights]
    if "rot" in out_kinds:
        tw = out_widths[out_kinds.index("rot")]
        for tab in tables:
            in_specs.append(pl.BlockSpec((PROJ_TM, tw), lambda i: (i % tiles_per_seq, 0)))
            args.append(tab)
    out_specs, out_shape = [], []
    for wd, dt, kind in zip(out_widths, out_dtypes, out_kinds):
        if kind == "kt":
            kt_per_tile = PROJ_TM // FOX_T
            out_specs.append(pl.BlockSpec((None, kt_per_tile, wd, FOX_T),
                                          lambda i: (i // tiles_per_seq, i % tiles_per_seq, 0, 0)))
            out_shape.append(jax.ShapeDtypeStruct((t_rows // SEQ, SEQ // FOX_T, wd, FOX_T), dt))
        else:
            out_specs.append(pl.BlockSpec((PROJ_TM, wd), lambda i: (i, 0)))
            out_shape.append(jax.ShapeDtypeStruct((t_rows, wd), dt))
    return pl.pallas_call(
        functools.partial(_norm_proj_kernel, n_weights=len(weights), w_transposed=w_transposed,
                          out_widths=tuple(out_widths), out_kinds=out_kinds),
        grid=(t_rows // PROJ_TM,),
        in_specs=in_specs,
        out_specs=out_specs,
        out_shape=out_shape,
        compiler_params=pltpu.CompilerParams(dimension_semantics=("arbitrary",),
                                             vmem_limit_bytes=VMEM_LIMIT),
        name=name,
    )(*args)


def _fox_gate_kernel(f_ref, bf_ref, o_ref):
    blk = LANES
    row = lax.broadcasted_iota(jnp.int32, (blk, blk), 0)
    coli = lax.broadcasted_iota(jnp.int32, (blk, blk), 1)
    tri = jnp.where(row >= coli, 1.0, 0.0).astype(bf16)
    spread = [jnp.where((coli == BIAS_TERMS * row + t) & (row < N_MAIN_HEADS), 1.0, 0.0).astype(bf16)
              for t in range(BIAS_TERMS)]
    carry = jnp.zeros((1, F_PAD), f32)
    for b0 in range(0, SEQ, blk):
        xg = f_ref[b0:b0 + blk, :] + bf_ref[...]
        ls = jnp.minimum(xg, 0.0) - jnp.log1p(jnp.exp(-jnp.abs(xg)))
        cs = sum(jnp.dot(tri, term, preferred_element_type=f32) for term in _split_bf16(ls)) + carry
        carry = cs[blk - 1:blk, :]
        bias = cs * (-LOG2E)
        nb = sum(jnp.dot(term, sp, preferred_element_type=f32)
                 for term, sp in zip(_split_bf16(bias), spread))
        o_ref[b0:b0 + blk, :] = nb.astype(o_ref.dtype)


def _fox_gate(f, b_f):
    bf_pad = jnp.zeros((1, F_PAD), f32).at[0, :N_MAIN_HEADS].set(b_f.astype(f32))
    return pl.pallas_call(
        _fox_gate_kernel,
        grid=(BATCH,),
        in_specs=[pl.BlockSpec((SEQ, F_PAD), lambda b: (b, 0)),
                  pl.BlockSpec((1, F_PAD), lambda b: (0, 0))],
        out_specs=pl.BlockSpec((SEQ, LANES), lambda b: (b, 0)),
        out_shape=jax.ShapeDtypeStruct((BATCH * SEQ, LANES), bf16),
        compiler_params=pltpu.CompilerParams(dimension_semantics=("arbitrary",)),
        name="fox_gate",
    )(f, bf_pad)


def _fox_attn_kernel(q_ref, k_ref, vt_ref, nb_ref, z_ref, o_ref, m_ref, acc_ref, u_ref, p_ref, a_ref):
    t = FOX_T
    nt = SEQ // t
    head0 = pl.program_id(1) * FOX_HEADS_PER_STEP
    heads = range(FOX_HEADS_PER_STEP)
    col = [slice(hd * HEAD_DIM, (hd + 1) * HEAD_DIM) for hd in heads]
    lane = lax.broadcasted_iota(jnp.int32, (t, LANES), 1)
    sel = []
    for hd in heads:
        lo = BIAS_TERMS * (head0 + hd)
        sel.append(jnp.where((lane >= lo) & (lane < lo + BIAS_TERMS), 1.0, 0.0).astype(bf16))
    ones_rows = jnp.ones((FOX_SUM_ROWS, t), bf16)

    def tile_rows(i):
        return pl.ds(pl.multiple_of(i * t, t), t)

    def scores(qt, kt):
        nbj = nb_ref[tile_rows(kt), :]
        out = []
        for hd in heads:
            k_aug = jnp.concatenate([k_ref[tile_rows(kt), col[hd]], nbj], axis=1)
            q_aug = jnp.concatenate([q_ref[tile_rows(qt), col[hd]], sel[hd]], axis=1)
            out.append(lax.dot_general(k_aug, q_aug, (((1,), (1,)), ((), ())),
                                       preferred_element_type=f32))
        return out

    def softmax_update(qt, u, masked):
        if masked:
            causal = (lax.broadcasted_iota(jnp.int32, (t, t), 0) <= lax.broadcasted_iota(jnp.int32, (t, t), 1))
        p, alpha = [], []
        for hd in heads:
            uh = jnp.where(causal, u[hd], NEG) if masked else u[hd]
            m_old = m_ref[qt, hd]
            m_new = jnp.maximum(m_old, jnp.max(uh, axis=0, keepdims=True))
            p.append(jnp.exp2(uh - m_new).astype(bf16))
            alpha.append(jnp.exp2(m_old - m_new))
            m_ref[qt, hd] = m_new
        return p, alpha

    def pv(kt, p):
        return [jnp.dot(jnp.concatenate([vt_ref[kt, col[hd], :], ones_rows], axis=0), p[hd],
                        preferred_element_type=f32) for hd in heads]

    def accumulate(qt, alpha, pv_vals):
        for hd in heads:
            acc_ref[qt, hd] = alpha[hd] * acc_ref[qt, hd] + pv_vals[hd]

    def finalize(qt):
        for hd in heads:
            acc = acc_ref[qt, hd]
            out_t = acc[:HEAD_DIM, :] * (1.0 / acc[HEAD_DIM:HEAD_DIM + 1, :])
            gate = _silu(z_ref[tile_rows(qt), col[hd]].astype(f32))
            o_ref[tile_rows(qt), col[hd]] = (out_t.T * gate).astype(o_ref.dtype)

    def next_item(qt, kt):
        wrap = kt + 1 >= qt
        return jnp.minimum(jnp.where(wrap, qt + 1, qt), nt - 1), jnp.where(wrap, 0, kt + 1)

    def load_list(ref):
        return [ref[hd] for hd in heads]

    def store_list(ref, vals):
        for hd in heads:
            ref[hd] = vals[hd]

    m_ref[...] = jnp.full(m_ref.shape, NEG, f32)
    acc_ref[...] = jnp.zeros(acc_ref.shape, f32)
    p_ref[...] = jnp.zeros(p_ref.shape, bf16)
    a_ref[...] = jnp.ones(a_ref.shape, f32)

    n_items = nt * (nt - 1) // 2
    assert n_items % 2 == 0
    store_list(u_ref, scores(1, 0))

    def pass1(_, carry):
        qa, ka, qpb, kpb = carry
        u_a = load_list(u_ref)
        pv_prev = pv(kpb, load_list(p_ref))
        qb, kb = next_item(qa, ka)
        u_b = scores(qb, kb)
        qn, kn = next_item(qb, kb)
        store_list(u_ref, scores(qn, kn))
        p_a, al_a = softmax_update(qa, u_a, False)
        accumulate(qpb, load_list(a_ref), pv_prev)
        pv_a = pv(ka, p_a)
        p_b, al_b = softmax_update(qb, u_b, False)
        accumulate(qa, al_a, pv_a)
        store_list(p_ref, p_b)
        store_list(a_ref, al_b)
        return qn, kn, qb, kb

    one, zero = jnp.int32(1), jnp.int32(0)
    _, _, qpb, kpb = lax.fori_loop(0, n_items // 2, pass1, (one, zero, one, zero))
    accumulate(qpb, load_list(a_ref), pv(kpb, load_list(p_ref)))

    assert nt % 2 == 0
    store_list(u_ref, scores(0, 0))

    def pass2(i, carry):
        ta = 2 * i
        tb = ta + 1
        u_b = scores(tb, tb)
        p_a, al_a = softmax_update(ta, load_list(u_ref), True)
        pv_a = pv(ta, p_a)
        tn = jnp.minimum(ta + 2, nt - 1)
        u_n = scores(tn, tn)
        p_b, al_b = softmax_update(tb, u_b, True)
        accumulate(ta, al_a, pv_a)
        pv_b = pv(tb, p_b)
        finalize(ta)
        store_list(u_ref, u_n)
        accumulate(tb, al_b, pv_b)
        finalize(tb)
        return carry

    lax.fori_loop(0, nt // 2, pass2, 0)


def _fox_attention(q, k, vt, nb, z):
    hw = FOX_HEADS_PER_STEP * HEAD_DIM
    return pl.pallas_call(
        _fox_attn_kernel,
        grid=(BATCH, N_MAIN_HEADS // FOX_HEADS_PER_STEP),
        in_specs=[
            pl.BlockSpec((SEQ, hw), lambda b, p: (b, p)),
            pl.BlockSpec((SEQ, hw), lambda b, p: (b, p)),
            pl.BlockSpec((None, SEQ // FOX_T, hw, FOX_T), lambda b, p: (b, 0, p, 0)),
            pl.BlockSpec((SEQ, LANES), lambda b, p: (b, 0)),
            pl.BlockSpec((SEQ, hw), lambda b, p: (b, p)),
        ],
        out_specs=pl.BlockSpec((SEQ, hw), lambda b, p: (b, p)),
        out_shape=jax.ShapeDtypeStruct((BATCH * SEQ, MAIN_W), bf16),
        scratch_shapes=[
            pltpu.VMEM((SEQ // FOX_T, FOX_HEADS_PER_STEP, 1, FOX_T), f32),
            pltpu.VMEM((SEQ // FOX_T, FOX_HEADS_PER_STEP, HEAD_DIM + FOX_SUM_ROWS, FOX_T), f32),
            pltpu.VMEM((FOX_HEADS_PER_STEP, FOX_T, FOX_T), f32),
            pltpu.VMEM((FOX_HEADS_PER_STEP, FOX_T, FOX_T), bf16),
            pltpu.VMEM((FOX_HEADS_PER_STEP, 1, FOX_T), f32),
        ],
        compiler_params=pltpu.CompilerParams(dimension_semantics=("arbitrary", "arbitrary"),
                                             vmem_limit_bytes=VMEM_LIMIT),
        name="fox_attention",
    )(q, k, vt, nb, z)


def _retention_kernel(q_ref, k_ref, v_ref, z_ref, d_ref, xi_ref, zeta_ref, g_ref, o_ref, r_ref):
    c_len = RET_CHUNK
    r_ref[...] = jnp.zeros(r_ref.shape, f32)
    lane = lax.broadcasted_iota(jnp.int32, (c_len, LANES), 1)
    head_mask = (lane < RET_KEY_DIM, lane >= RET_KEY_DIM)
    pairs = range(RET_PAIRS_PER_STEP)
    chunks = range(RET_CHUNKS_PER_ITER)

    def body(it, carry):
        rows = [pl.ds(pl.multiple_of((it * RET_CHUNKS_PER_ITER + ci) * c_len, c_len), c_len) for ci in chunks]
        qa, s, upd = {}, {}, {}
        for ci in chunks:
            for pr in pairs:
                q2 = q_ref[rows[ci], pr * LANES:(pr + 1) * LANES]
                k2 = k_ref[rows[ci], pr * LANES:(pr + 1) * LANES]
                v2 = v_ref[rows[ci], pr * 2 * HEAD_DIM:(pr + 1) * 2 * HEAD_DIM]
                qa[ci, pr] = jnp.concatenate([jnp.where(head_mask[hd], q2, jnp.zeros_like(q2))
                                              for hd in range(2)], axis=0)
                s[ci, pr] = lax.dot_general(qa[ci, pr], k2, (((1,), (1,)), ((), ())),
                                            preferred_element_type=f32)
                kz = (k2.astype(f32) * zeta_ref[pr]).astype(bf16)
                upd[ci, pr] = lax.dot_general(kz, v2, (((0,), (0,)), ((), ())), preferred_element_type=f32)
        r_b = {}
        for pr in pairs:
            r_state = r_ref[pr]
            for ci in chunks:
                r_b[ci, pr] = r_state.astype(bf16)
                r_state = r_state * g_ref[pr] + upd[ci, pr]
            r_ref[pr] = r_state
        o = {}
        for ci in chunks:
            for pr in pairs:
                for hd in range(2):
                    hrows = slice(hd * c_len, (hd + 1) * c_len)
                    cols = slice((2 * pr + hd) * HEAD_DIM, (2 * pr + hd + 1) * HEAD_DIM)
                    inner = (s[ci, pr][hrows, :] * d_ref[2 * pr + hd]).astype(bf16)
                    intra = jnp.dot(inner, v_ref[rows[ci], cols], preferred_element_type=f32)
                    cross = jnp.dot(qa[ci, pr][hrows, :], r_b[ci, pr][:, hd * HEAD_DIM:(hd + 1) * HEAD_DIM],
                                    preferred_element_type=f32)
                    o[ci, pr, hd] = intra + cross * xi_ref[2 * pr + hd]
        for ci in chunks:
            for pr in pairs:
                for hd in range(2):
                    cols = slice((2 * pr + hd) * HEAD_DIM, (2 * pr + hd + 1) * HEAD_DIM)
                    oh = o[ci, pr, hd]
                    ms = jnp.mean(oh * oh, axis=-1, keepdims=True)
                    on = oh * lax.rsqrt(ms + EPS)
                    zg = _silu(z_ref[rows[ci], cols].astype(f32))
                    o_ref[rows[ci], cols] = (on * zg).astype(o_ref.dtype)
        return carry

    lax.fori_loop(0, SEQ // (c_len * RET_CHUNKS_PER_ITER), body, 0)


def _retention_tables():
    h, c_len = N_MAIN_HEADS, RET_CHUNK
    lg = jnp.log1p(-jnp.exp2(-5.0 - jnp.arange(h, dtype=f32)))
    n = jnp.arange(c_len, dtype=f32)
    diff = n[:, None] - n[None, :]
    d_inner = jnp.where(diff[None] >= 0,
                        jnp.exp(lg[:, None, None] * jnp.maximum(diff, 0.0)[None]), 0.0)
    xi = jnp.exp(lg[:, None] * (n[None, :] + 1.0))
    zeta = jnp.exp(lg[:, None] * (c_len - 1.0 - n[None, :]))
    g_chunk = jnp.exp(lg * c_len)
    xi_b = jnp.broadcast_to(xi[:, :, None], (h, c_len, HEAD_DIM))
    zeta2 = jnp.repeat(zeta.reshape(h // 2, 2, c_len).transpose(0, 2, 1), RET_KEY_DIM, axis=-1)
    g2 = jnp.broadcast_to(jnp.repeat(g_chunk.reshape(h // 2, 2), RET_KEY_DIM, axis=-1)[:, :, None],
                          (h // 2, 2 * RET_KEY_DIM, 2 * HEAD_DIM))
    return d_inner, xi_b, zeta2, g2


def _retention(qk, v, z):
    n_pairs = N_MAIN_HEADS // 2
    pp = RET_PAIRS_PER_STEP
    steps = n_pairs // pp
    d_inner, xi_b, zeta2, g2 = _retention_tables()
    c_len = RET_CHUNK
    return pl.pallas_call(
        _retention_kernel,
        grid=(BATCH, steps),
        in_specs=[
            pl.BlockSpec((SEQ, pp * LANES), lambda b, p: (b, p)),
            pl.BlockSpec((SEQ, pp * LANES), lambda b, p: (b, steps + p)),
            pl.BlockSpec((SEQ, pp * 2 * HEAD_DIM), lambda b, p: (b, p)),
            pl.BlockSpec((SEQ, pp * 2 * HEAD_DIM), lambda b, p: (b, p)),
            pl.BlockSpec((2 * pp, c_len, c_len), lambda b, p: (p, 0, 0)),
            pl.BlockSpec((2 * pp, c_len, HEAD_DIM), lambda b, p: (p, 0, 0)),
            pl.BlockSpec((pp, c_len, LANES), lambda b, p: (p, 0, 0)),
            pl.BlockSpec((pp, LANES, 2 * HEAD_DIM), lambda b, p: (p, 0, 0)),
        ],
        out_specs=pl.BlockSpec((SEQ, pp * 2 * HEAD_DIM), lambda b, p: (b, p)),
        out_shape=jax.ShapeDtypeStruct((BATCH * SEQ, MAIN_W), bf16),
        scratch_shapes=[pltpu.VMEM((pp, LANES, 2 * HEAD_DIM), f32)],
        compiler_params=pltpu.CompilerParams(dimension_semantics=("arbitrary", "arbitrary"),
                                             vmem_limit_bytes=VMEM_LIMIT),
        name="retention",
    )(qk, qk, v, z, d_inner, xi_b, zeta2, g2)


def _out_proj_kernel(om_ref, qm_ref, zm_ref, kv_ref, w_ref, x_ref, g_ref, o_ref, *, final_norm):
    scale = 1.0 / math.sqrt(HEAD_DIM)
    heads = range(N_MEM_HEADS)
    col = [slice(hd * HEAD_DIM, (hd + 1) * HEAD_DIM) for hd in heads]
    half = D_MODEL // 2
    s = [lax.dot_general(qm_ref[:, col[hd]], kv_ref[:, col[hd]], (((1,), (1,)), ((), ())),
                         preferred_element_type=f32) * scale for hd in heads]
    y_lo = jnp.dot(om_ref[...], w_ref[:MAIN_W, :half], preferred_element_type=f32)
    p = []
    for hd in heads:
        e = jnp.exp(s[hd] - jnp.max(s[hd], axis=-1, keepdims=True))
        p.append((e * (1.0 / jnp.sum(e, axis=-1, keepdims=True))).astype(bf16))
    memo = [jnp.dot(p[hd], kv_ref[:, MEM_W + hd * HEAD_DIM:MEM_W + (hd + 1) * HEAD_DIM],
                    preferred_element_type=f32) for hd in heads]
    y_hi = jnp.dot(om_ref[...], w_ref[:MAIN_W, half:], preferred_element_type=f32)
    og = jnp.concatenate([(memo[hd] * _silu(zm_ref[:, col[hd]].astype(f32))).astype(bf16) for hd in heads],
                         axis=1)
    y = jnp.concatenate([y_lo, y_hi], axis=1) + jnp.dot(og, w_ref[MAIN_W:, :], preferred_element_type=f32)
    xn = x_ref[...] + y
    if final_norm:
        xn = _rmsnorm_rows(xn, g_ref[...])
    o_ref[...] = xn


def _out_proj(o_main, qm, z, kv, w_out, x, g_final, *, final_norm):
    t_rows = x.shape[0]
    tiles_per_seq = SEQ // OUT_TM
    z_blk = MAIN_W // MEM_W
    return pl.pallas_call(
        functools.partial(_out_proj_kernel, final_norm=final_norm),
        grid=(t_rows // OUT_TM,),
        in_specs=[
            pl.BlockSpec((OUT_TM, MAIN_W), lambda i: (i, 0)),
            pl.BlockSpec((OUT_TM, MEM_W), lambda i: (i, 0)),
            pl.BlockSpec((OUT_TM, MEM_W), lambda i: (i, z_blk)),
            pl.BlockSpec((N_MEM, 2 * MEM_W), lambda i: (i // tiles_per_seq, 0)),
            pl.BlockSpec((INNER, D_MODEL), lambda i: (0, 0), pipeline_mode=pl.Buffered(1)),
            pl.BlockSpec((OUT_TM, D_MODEL), lambda i: (i, 0)),
            pl.BlockSpec((1, D_MODEL), lambda i: (0, 0)),
        ],
        out_specs=pl.BlockSpec((OUT_TM, D_MODEL), lambda i: (i, 0)),
        out_shape=jax.ShapeDtypeStruct((t_rows, D_MODEL), f32),
        compiler_params=pltpu.CompilerParams(dimension_semantics=("arbitrary",),
                                             vmem_limit_bytes=VMEM_LIMIT),
        name="out_proj_final" if final_norm else "out_proj",
    )(o_main, qm, z, kv, w_out, x, g_final.reshape(1, D_MODEL))


def _rotary_tables():
    half = RET_KEY_DIM // 2
    pos = jnp.arange(SEQ, dtype=f32)
    inv = 1.0 / (ROPE_BASE ** (jnp.arange(half, dtype=f32) / half))
    ang = pos[:, None] * inv[None, :]
    cos, sin = jnp.cos(ang), jnp.sin(ang)
    cos_h = jnp.concatenate([cos, cos], axis=-1)
    sin_h = jnp.concatenate([-sin, sin], axis=-1)
    k_scale = RET_KEY_DIM ** -0.5
    cos_t = jnp.concatenate([jnp.tile(cos_h, (1, N_MAIN_HEADS)),
                             jnp.tile(cos_h, (1, N_MAIN_HEADS)) * k_scale], axis=-1)
    sin_t = jnp.concatenate([jnp.tile(sin_h, (1, N_MAIN_HEADS)),
                             jnp.tile(sin_h, (1, N_MAIN_HEADS)) * k_scale], axis=-1)
    return cos_t, sin_t


def kernel(x, mem, norm_g, fox_w_in, fox_b_f, ret_w_in, mem_norm_g, w_mem_kv, w_out, final_norm_g):
    t_rows = BATCH * SEQ
    x2 = x.reshape(t_rows, D_MODEL)

    kv0, kv1 = _norm_proj(mem.reshape(BATCH * N_MEM, D_MODEL), mem_norm_g,
                          [w_mem_kv[0].astype(bf16), w_mem_kv[1].astype(bf16)],
                          [2 * MEM_W, 2 * MEM_W], [bf16, bf16], name="mem_kv_proj")

    o_f = 3 * MAIN_W
    o_qm = o_f + N_MAIN_HEADS
    w0t = jnp.swapaxes(fox_w_in[0], 0, 1).astype(bf16)
    w_qkv = w0t[:o_f]
    w_qmz = w0t[o_qm:]
    w_f = jnp.pad(w0t[o_f:o_qm], ((0, F_PAD - N_MAIN_HEADS), (0, 0)))
    q, k, vt, qm, z, f = _norm_proj(
        x2, norm_g[0], [w_qkv, w_qmz, w_f], [MAIN_W, MAIN_W, MAIN_W, MEM_W, INNER, F_PAD],
        [bf16, bf16, bf16, bf16, bf16, f32], ["fox_q", "plain", "kt", "plain", "plain", "plain"],
        w_transposed=True, name="fox_in_proj")
    nb = _fox_gate(f, fox_b_f[0])
    o_main = _fox_attention(q, k, vt, nb, z)
    x2 = _out_proj(o_main, qm, z, kv0, w_out[0].astype(bf16), x2, final_norm_g, final_norm=False)

    w1 = ret_w_in[0].astype(bf16)
    qk, v, qm, z = _norm_proj(
        x2, norm_g[1], [w1], [2 * RET_QK_W, MAIN_W, MEM_W, INNER], [bf16] * 4,
        ["rot", "plain", "plain", "plain"], tables=_rotary_tables(), name="ret_in_proj")
    o_main = _retention(qk, v, z)
    out = _out_proj(o_main, qm, z, kv1, w_out[1].astype(bf16), x2, final_norm_g, final_norm=True)
    return out.reshape(BATCH, SEQ, D_MODEL)
```

```python
import functools
import math

import jax
import jax.numpy as jnp
import numpy as np
from jax import lax
from jax.experimental import pallas as pl
from jax.experimental.pallas import tpu as pltpu

D_MODEL = 1024
BATCH = 8
SEQ = 2048
HEAD_DIM = 128
N_MAIN_HEADS = 12
N_MEM_HEADS = 4
N_MEM = 256
MAIN_W = N_MAIN_HEADS * HEAD_DIM
MEM_W = N_MEM_HEADS * HEAD_DIM
INNER = MAIN_W + MEM_W
RET_KEY_DIM = HEAD_DIM // 2
RET_QK_W = N_MAIN_HEADS * RET_KEY_DIM
RET_CHUNK = 128
ROPE_BASE = 10000.0
EPS = 1e-6
NEG = -1e30
LOG2E = 1.4426950408889634

LANES = 128
F_PAD = LANES
VMEM_LIMIT = 56 * 1024 * 1024

PROJ_TM = 512
PROJ_TN = 512
FOX_T = 256
FOX_HEADS_PER_STEP = 6
BIAS_TERMS = 3
FOX_SUM_ROWS = 16
FOX_Q_SCALE = LOG2E / math.sqrt(HEAD_DIM)
RET_PAIRS_PER_STEP = 3
RET_CHUNKS_PER_ITER = 2
OUT_TM = 512

f32 = jnp.float32
bf16 = jnp.bfloat16


def _silu(z):
    return z * (1.0 / (1.0 + jnp.exp(-z)))


def _rmsnorm_rows(x, g):
    ms = jnp.mean(x * x, axis=-1, keepdims=True)
    return (x * lax.rsqrt(ms + EPS)) * g


def _split_bf16(v):
    hi = v.astype(bf16)
    r1 = v - hi.astype(f32)
    mid = r1.astype(bf16)
    lo = (r1 - mid.astype(f32)).astype(bf16)
    return hi, mid, lo


def _norm_proj_kernel(*refs, n_weights, w_transposed, out_widths, out_kinds):
    n_out = len(out_widths)
    x_ref, g_ref = refs[:2]
    w_refs = refs[2:2 + n_weights]
    tab_refs = refs[2 + n_weights:len(refs) - n_out]
    out_refs = refs[len(refs) - n_out:]
    h = _rmsnorm_rows(x_ref[...], g_ref[...]).astype(bf16)
    tm = h.shape[0]
    if "rot" in out_kinds:
        lane = lax.broadcasted_iota(jnp.int32, (tm, LANES), 1)
        first_half = (lane % RET_KEY_DIM) < (RET_KEY_DIM // 2)
    out_axis = 0 if w_transposed else 1
    wi, col = 0, 0
    for o_ref, width, kind in zip(out_refs, out_widths, out_kinds):
        if col == w_refs[wi].shape[out_axis]:
            wi, col = wi + 1, 0
        w_ref = w_refs[wi]
        for c0 in range(0, width, PROJ_TN):
            cw = min(PROJ_TN, width - c0)
            if w_transposed:
                y = lax.dot_general(h, w_ref[col + c0:col + c0 + cw, :], (((1,), (1,)), ((), ())),
                                    preferred_element_type=f32)
            else:
                y = jnp.dot(h, w_ref[:, col + c0:col + c0 + cw], preferred_element_type=f32)
            if kind == "rot":
                for s0 in range(0, cw, LANES):
                    ys = y[:, s0:s0 + LANES]
                    sw = jnp.where(first_half, pltpu.roll(ys, LANES - RET_KEY_DIM // 2, 1),
                                   pltpu.roll(ys, RET_KEY_DIM // 2, 1))
                    cos_ref, sin_ref = tab_refs[:2] if c0 + s0 < RET_QK_W else tab_refs[2:]
                    r = ys * cos_ref[...] + sw * sin_ref[...]
                    o_ref[:, c0 + s0:c0 + s0 + LANES] = r.astype(o_ref.dtype)
            elif kind == "kt":
                for r0 in range(tm // FOX_T):
                    for s0 in range(0, cw, LANES):
                        piece = y[r0 * FOX_T:(r0 + 1) * FOX_T, s0:s0 + LANES]
                        o_ref[r0, c0 + s0:c0 + s0 + LANES, :] = piece.T.astype(o_ref.dtype)
            elif kind == "fox_q":
                o_ref[:, c0:c0 + cw] = (y * FOX_Q_SCALE).astype(o_ref.dtype)
            else:
                o_ref[:, c0:c0 + cw] = y.astype(o_ref.dtype)
        col += width


def _norm_proj(x, g, weights, out_widths, out_dtypes, out_kinds=None, *, w_transposed=False, tables=None,
               name):
    t_rows, d = x.shape
    out_kinds = tuple(out_kinds or ["plain"] * len(out_widths))
    out_axis = 0 if w_transposed else 1
    assert sum(out_widths) == sum(w.shape[out_axis] for w in weights) and t_rows % PROJ_TM == 0
    tiles_per_seq = SEQ // PROJ_TM
    in_specs = [
        pl.BlockSpec((PROJ_TM, d), lambda i: (i, 0)),
        pl.BlockSpec((1, d), lambda i: (0, 0)),
    ]
    in_specs += [pl.BlockSpec(w.shape, lambda i: (0, 0), pipeline_mode=pl.Buffered(1)) for w in weights]
    args = [x, g.reshape(1, d), *weights]
    if "rot" in out_kinds:
        for tab in tables:
            in_specs.append(pl.BlockSpec((PROJ_TM, LANES), lambda i: (i % tiles_per_seq, 0)))
            args.append(tab)
    out_specs, out_shape = [], []
    for wd, dt, kind in zip(out_widths, out_dtypes, out_kinds):
        if kind == "kt":
            kt_per_tile = PROJ_TM // FOX_T
            out_specs.append(pl.BlockSpec((None, kt_per_tile, wd, FOX_T),
                                          lambda i: (i // tiles_per_seq, i % tiles_per_seq, 0, 0)))
            out_shape.append(jax.ShapeDtypeStruct((t_rows // SEQ, SEQ // FOX_T, wd, FOX_T), dt))
        else:
            out_specs.append(pl.BlockSpec((PROJ_TM, wd), lambda i: (i, 0)))
            out_shape.append(jax.ShapeDtypeStruct((t_rows, wd), dt))
    return pl.pallas_call(
        functools.partial(_norm_proj_kernel, n_weights=len(weights), w_transposed=w_transposed,
                          out_widths=tuple(out_widths), out_kinds=out_kinds),
        grid=(t_rows // PROJ_TM,),
        in_specs=in_specs,
        out_specs=out_specs,
        out_shape=out_shape,
        compiler_params=pltpu.CompilerParams(dimension_semantics=("arbitrary",),
                                             vmem_limit_bytes=VMEM_LIMIT),
        name=name,
    )(*args)


def _fox_gate_kernel(f_ref, bf_ref, o_ref):
    blk = LANES
    row = lax.broadcasted_iota(jnp.int32, (blk, blk), 0)
    coli = lax.broadcasted_iota(jnp.int32, (blk, blk), 1)
    tri = jnp.where(row >= coli, 1.0, 0.0).astype(bf16)
    spread = [jnp.where((coli == BIAS_TERMS * row + t) & (row < N_MAIN_HEADS), 1.0, 0.0).astype(bf16)
              for t in range(BIAS_TERMS)]
    carry = jnp.zeros((1, F_PAD), f32)
    for b0 in range(0, SEQ, blk):
        xg = f_ref[b0:b0 + blk, :] + bf_ref[...]
        ls = jnp.minimum(xg, 0.0) - jnp.log1p(jnp.exp(-jnp.abs(xg)))
        cs = sum(jnp.dot(tri, term, preferred_element_type=f32) for term in _split_bf16(ls)) + carry
        carry = cs[blk - 1:blk, :]
        bias = cs * (-LOG2E)
        nb = sum(jnp.dot(term, sp, preferred_element_type=f32)
                 for term, sp in zip(_split_bf16(bias), spread))
        o_ref[b0:b0 + blk, :] = nb.astype(o_ref.dtype)


def _fox_gate(f, b_f):
    bf_pad = jnp.zeros((1, F_PAD), f32).at[0, :N_MAIN_HEADS].set(b_f.astype(f32))
    return pl.pallas_call(
        _fox_gate_kernel,
        grid=(BATCH,),
        in_specs=[pl.BlockSpec((SEQ, F_PAD), lambda b: (b, 0)),
                  pl.BlockSpec((1, F_PAD), lambda b: (0, 0))],
        out_specs=pl.BlockSpec((SEQ, LANES), lambda b: (b, 0)),
        out_shape=jax.ShapeDtypeStruct((BATCH * SEQ, LANES), bf16),
        compiler_params=pltpu.CompilerParams(dimension_semantics=("arbitrary",)),
        name="fox_gate",
    )(f, bf_pad)


def _fox_attn_kernel(q_ref, k_ref, vt_ref, nb_ref, z_ref, o_ref, m_ref, acc_ref, u_ref, p_ref, a_ref):
    t = FOX_T
    nt = SEQ // t
    head0 = pl.program_id(1) * FOX_HEADS_PER_STEP
    heads = range(FOX_HEADS_PER_STEP)
    col = [slice(hd * HEAD_DIM, (hd + 1) * HEAD_DIM) for hd in heads]
    lane = lax.broadcasted_iota(jnp.int32, (t, LANES), 1)
    sel = []
    for hd in heads:
        lo = BIAS_TERMS * (head0 + hd)
        sel.append(jnp.where((lane >= lo) & (lane < lo + BIAS_TERMS), 1.0, 0.0).astype(bf16))
    ones_rows = jnp.ones((FOX_SUM_ROWS, t), bf16)

    def tile_rows(i):
        return pl.ds(pl.multiple_of(i * t, t), t)

    def scores(qt, kt):
        nbj = nb_ref[tile_rows(kt), :]
        out = []
        for hd in heads:
            k_aug = jnp.concatenate([k_ref[tile_rows(kt), col[hd]], nbj], axis=1)
            q_aug = jnp.concatenate([q_ref[tile_rows(qt), col[hd]], sel[hd]], axis=1)
            out.append(lax.dot_general(k_aug, q_aug, (((1,), (1,)), ((), ())),
                                       preferred_element_type=f32))
        return out

    def softmax_update(qt, u, masked):
        if masked:
            causal = (lax.broadcasted_iota(jnp.int32, (t, t), 0) <= lax.broadcasted_iota(jnp.int32, (t, t), 1))
        p, alpha = [], []
        for hd in heads:
            uh = jnp.where(causal, u[hd], NEG) if masked else u[hd]
            m_old = m_ref[qt, hd]
            m_new = jnp.maximum(m_old, jnp.max(uh, axis=0, keepdims=True))
            p.append(jnp.exp2(uh - m_new).astype(bf16))
            alpha.append(jnp.exp2(m_old - m_new))
            m_ref[qt, hd] = m_new
        return p, alpha

    def pv(kt, p):
        return [jnp.dot(jnp.concatenate([vt_ref[kt, col[hd], :], ones_rows], axis=0), p[hd],
                        preferred_element_type=f32) for hd in heads]

    def accumulate(qt, alpha, pv_vals):
        for hd in heads:
            acc_ref[qt, hd] = alpha[hd] * acc_ref[qt, hd] + pv_vals[hd]

    def finalize(qt):
        for hd in heads:
            acc = acc_ref[qt, hd]
            out_t = acc[:HEAD_DIM, :] * (1.0 / acc[HEAD_DIM:HEAD_DIM + 1, :])
            gate = _silu(z_ref[tile_rows(qt), col[hd]].astype(f32))
            o_ref[tile_rows(qt), col[hd]] = (out_t.T * gate).astype(o_ref.dtype)

    def next_item(qt, kt):
        wrap = kt + 1 >= qt
        return jnp.minimum(jnp.where(wrap, qt + 1, qt), nt - 1), jnp.where(wrap, 0, kt + 1)

    def load_list(ref):
        return [ref[hd] for hd in heads]

    def store_list(ref, vals):
        for hd in heads:
            ref[hd] = vals[hd]

    m_ref[...] = jnp.full(m_ref.shape, NEG, f32)
    acc_ref[...] = jnp.zeros(acc_ref.shape, f32)
    p_ref[...] = jnp.zeros(p_ref.shape, bf16)
    a_ref[...] = jnp.ones(a_ref.shape, f32)

    n_items = nt * (nt - 1) // 2
    assert n_items % 2 == 0
    store_list(u_ref, scores(1, 0))

    def pass1(_, carry):
        qa, ka, qpb, kpb = carry
        u_a = load_list(u_ref)
        pv_prev = pv(kpb, load_list(p_ref))
        qb, kb = next_item(qa, ka)
        u_b = scores(qb, kb)
        qn, kn = next_item(qb, kb)
        store_list(u_ref, scores(qn, kn))
        p_a, al_a = softmax_update(qa, u_a, False)
        accumulate(qpb, load_list(a_ref), pv_prev)
        pv_a = pv(ka, p_a)
        p_b, al_b = softmax_update(qb, u_b, False)
        accumulate(qa, al_a, pv_a)
        store_list(p_ref, p_b)
        store_list(a_ref, al_b)
        return qn, kn, qb, kb

    one, zero = jnp.int32(1), jnp.int32(0)
    _, _, qpb, kpb = lax.fori_loop(0, n_items // 2, pass1, (one, zero, one, zero))
    accumulate(qpb, load_list(a_ref), pv(kpb, load_list(p_ref)))

    assert nt % 2 == 0
    store_list(u_ref, scores(0, 0))

    def pass2(i, carry):
        ta = 2 * i
        tb = ta + 1
        u_b = scores(tb, tb)
        p_a, al_a = softmax_update(ta, load_list(u_ref), True)
        pv_a = pv(ta, p_a)
        tn = jnp.minimum(ta + 2, nt - 1)
        u_n = scores(tn, tn)
        p_b, al_b = softmax_update(tb, u_b, True)
        accumulate(ta, al_a, pv_a)
        pv_b = pv(tb, p_b)
        finalize(ta)
        store_list(u_ref, u_n)
        accumulate(tb, al_b, pv_b)
        finalize(tb)
        return carry

    lax.fori_loop(0, nt // 2, pass2, 0)


def _fox_attention(q, k, vt, nb, z):
    hw = FOX_HEADS_PER_STEP * HEAD_DIM
    return pl.pallas_call(
        _fox_attn_kernel,
        grid=(BATCH, N_MAIN_HEADS // FOX_HEADS_PER_STEP),
        in_specs=[
            pl.BlockSpec((SEQ, hw), lambda b, p: (b, p)),
            pl.BlockSpec((SEQ, hw), lambda b, p: (b, p)),
            pl.BlockSpec((None, SEQ // FOX_T, hw, FOX_T), lambda b, p: (b, 0, p, 0)),
            pl.BlockSpec((SEQ, LANES), lambda b, p: (b, 0)),
            pl.BlockSpec((SEQ, hw), lambda b, p: (b, p)),
        ],
        out_specs=pl.BlockSpec((SEQ, hw), lambda b, p: (b, p)),
        out_shape=jax.ShapeDtypeStruct((BATCH * SEQ, MAIN_W), bf16),
        scratch_shapes=[
            pltpu.VMEM((SEQ // FOX_T, FOX_HEADS_PER_STEP, 1, FOX_T), f32),
            pltpu.VMEM((SEQ // FOX_T, FOX_HEADS_PER_STEP, HEAD_DIM + FOX_SUM_ROWS, FOX_T), f32),
            pltpu.VMEM((FOX_HEADS_PER_STEP, FOX_T, FOX_T), f32),
            pltpu.VMEM((FOX_HEADS_PER_STEP, FOX_T, FOX_T), bf16),
            pltpu.VMEM((FOX_HEADS_PER_STEP, 1, FOX_T), f32),
        ],
        compiler_params=pltpu.CompilerParams(dimension_semantics=("arbitrary", "arbitrary"),
                                             vmem_limit_bytes=VMEM_LIMIT),
        name="fox_attention",
    )(q, k, vt, nb, z)


def _retention_kernel(q_ref, k_ref, v_ref, z_ref, d_ref, xi_ref, zeta_ref, g_ref, o_ref, r_ref):
    c_len = RET_CHUNK
    r_ref[...] = jnp.zeros(r_ref.shape, f32)
    lane = lax.broadcasted_iota(jnp.int32, (c_len, LANES), 1)
    head_mask = (lane < RET_KEY_DIM, lane >= RET_KEY_DIM)
    pairs = range(RET_PAIRS_PER_STEP)
    chunks = range(RET_CHUNKS_PER_ITER)

    def body(it, carry):
        rows = [pl.ds(pl.multiple_of((it * RET_CHUNKS_PER_ITER + ci) * c_len, c_len), c_len) for ci in chunks]
        qa, s, upd = {}, {}, {}
        for ci in chunks:
            for pr in pairs:
                q2 = q_ref[rows[ci], pr * LANES:(pr + 1) * LANES]
                k2 = k_ref[rows[ci], pr * LANES:(pr + 1) * LANES]
                v2 = v_ref[rows[ci], pr * 2 * HEAD_DIM:(pr + 1) * 2 * HEAD_DIM]
                qa[ci, pr] = jnp.concatenate([jnp.where(head_mask[hd], q2, jnp.zeros_like(q2))
                                              for hd in range(2)], axis=0)
                s[ci, pr] = lax.dot_general(qa[ci, pr], k2, (((1,), (1,)), ((), ())),
                                            preferred_element_type=f32)
                kz = (k2.astype(f32) * zeta_ref[pr]).astype(bf16)
                upd[ci, pr] = lax.dot_general(kz, v2, (((0,), (0,)), ((), ())), preferred_element_type=f32)
        r_b = {}
        for pr in pairs:
            r_state = r_ref[pr]
            for ci in chunks:
                r_b[ci, pr] = r_state.astype(bf16)
                r_state = r_state * g_ref[pr] + upd[ci, pr]
            r_ref[pr] = r_state
        o = {}
        for ci in chunks:
            for pr in pairs:
                for hd in range(2):
                    hrows = slice(hd * c_len, (hd + 1) * c_len)
                    cols = slice((2 * pr + hd) * HEAD_DIM, (2 * pr + hd + 1) * HEAD_DIM)
                    inner = (s[ci, pr][hrows, :] * d_ref[2 * pr + hd]).astype(bf16)
                    intra = jnp.dot(inner, v_ref[rows[ci], cols], preferred_element_type=f32)
                    cross = jnp.dot(qa[ci, pr][hrows, :], r_b[ci, pr][:, hd * HEAD_DIM:(hd + 1) * HEAD_DIM],
                                    preferred_element_type=f32)
                    o[ci, pr, hd] = intra + cross * xi_ref[2 * pr + hd]
        for ci in chunks:
            for pr in pairs:
                for hd in range(2):
                    cols = slice((2 * pr + hd) * HEAD_DIM, (2 * pr + hd + 1) * HEAD_DIM)
                    oh = o[ci, pr, hd]
                    ms = jnp.mean(oh * oh, axis=-1, keepdims=True)
                    on = oh * lax.rsqrt(ms + EPS)
                    zg = _silu(z_ref[rows[ci], cols].astype(f32))
                    o_ref[rows[ci], cols] = (on * zg).astype(o_ref.dtype)
        return carry

    lax.fori_loop(0, SEQ // (c_len * RET_CHUNKS_PER_ITER), body, 0)


def _retention_tables():
    h, c_len = N_MAIN_HEADS, RET_CHUNK
    lg = np.log1p(-np.exp2(-5.0 - np.arange(h, dtype=np.float64)))
    n = np.arange(c_len, dtype=np.float64)
    diff = n[:, None] - n[None, :]
    d_inner = np.where(diff[None] >= 0, np.exp(lg[:, None, None] * np.maximum(diff, 0.0)[None]), 0.0)
    xi = np.exp(lg[:, None] * (n[None, :] + 1.0))
    zeta = np.exp(lg[:, None] * (c_len - 1.0 - n[None, :]))
    g_chunk = np.exp(lg * c_len)
    xi_b = np.broadcast_to(xi[:, :, None], (h, c_len, HEAD_DIM))
    zeta2 = np.repeat(zeta.reshape(h // 2, 2, c_len).transpose(0, 2, 1), RET_KEY_DIM, axis=-1)
    g2 = np.broadcast_to(np.repeat(g_chunk.reshape(h // 2, 2), RET_KEY_DIM, axis=-1)[:, :, None],
                         (h // 2, 2 * RET_KEY_DIM, 2 * HEAD_DIM))
    return tuple(jnp.asarray(np.ascontiguousarray(t), dtype=f32) for t in (d_inner, xi_b, zeta2, g2))


def _retention(qk, v, z):
    n_pairs = N_MAIN_HEADS // 2
    pp = RET_PAIRS_PER_STEP
    steps = n_pairs // pp
    d_inner, xi_b, zeta2, g2 = _retention_tables()
    c_len = RET_CHUNK
    return pl.pallas_call(
        _retention_kernel,
        grid=(BATCH, steps),
        in_specs=[
            pl.BlockSpec((SEQ, pp * LANES), lambda b, p: (b, p)),
            pl.BlockSpec((SEQ, pp * LANES), lambda b, p: (b, steps + p)),
            pl.BlockSpec((SEQ, pp * 2 * HEAD_DIM), lambda b, p: (b, p)),
            pl.BlockSpec((SEQ, pp * 2 * HEAD_DIM), lambda b, p: (b, p)),
            pl.BlockSpec((2 * pp, c_len, c_len), lambda b, p: (p, 0, 0)),
            pl.BlockSpec((2 * pp, c_len, HEAD_DIM), lambda b, p: (p, 0, 0)),
            pl.BlockSpec((pp, c_len, LANES), lambda b, p: (p, 0, 0)),
            pl.BlockSpec((pp, LANES, 2 * HEAD_DIM), lambda b, p: (p, 0, 0)),
        ],
        out_specs=pl.BlockSpec((SEQ, pp * 2 * HEAD_DIM), lambda b, p: (b, p)),
        out_shape=jax.ShapeDtypeStruct((BATCH * SEQ, MAIN_W), bf16),
        scratch_shapes=[pltpu.VMEM((pp, LANES, 2 * HEAD_DIM), f32)],
        compiler_params=pltpu.CompilerParams(dimension_semantics=("arbitrary", "arbitrary"),
                                             vmem_limit_bytes=VMEM_LIMIT),
        name="retention",
    )(qk, qk, v, z, d_inner, xi_b, zeta2, g2)


def _out_proj_kernel(om_ref, qm_ref, zm_ref, kv_ref, w_ref, x_ref, g_ref, o_ref, *, final_norm):
    scale = 1.0 / math.sqrt(HEAD_DIM)
    heads = range(N_MEM_HEADS)
    col = [slice(hd * HEAD_DIM, (hd + 1) * HEAD_DIM) for hd in heads]
    half = D_MODEL // 2
    s = [lax.dot_general(qm_ref[:, col[hd]], kv_ref[:, col[hd]], (((1,), (1,)), ((), ())),
                         preferred_element_type=f32) * scale for hd in heads]
    y_lo = jnp.dot(om_ref[...], w_ref[:MAIN_W, :half], preferred_element_type=f32)
    p = []
    for hd in heads:
        e = jnp.exp(s[hd] - jnp.max(s[hd], axis=-1, keepdims=True))
        p.append((e * (1.0 / jnp.sum(e, axis=-1, keepdims=True))).astype(bf16))
    memo = [jnp.dot(p[hd], kv_ref[:, MEM_W + hd * HEAD_DIM:MEM_W + (hd + 1) * HEAD_DIM],
                    preferred_element_type=f32) for hd in heads]
    y_hi = jnp.dot(om_ref[...], w_ref[:MAIN_W, half:], preferred_element_type=f32)
    og = jnp.concatenate([(memo[hd] * _silu(zm_ref[:, col[hd]].astype(f32))).astype(bf16) for hd in heads],
                         axis=1)
    y = jnp.concatenate([y_lo, y_hi], axis=1) + jnp.dot(og, w_ref[MAIN_W:, :], preferred_element_type=f32)
    xn = x_ref[...] + y
    if final_norm:
        xn = _rmsnorm_rows(xn, g_ref[...])
    o_ref[...] = xn


def _out_proj(o_main, qm, z, kv, w_out, x, g_final, *, final_norm):
    t_rows = x.shape[0]
    tiles_per_seq = SEQ // OUT_TM
    z_blk = MAIN_W // MEM_W
    return pl.pallas_call(
        functools.partial(_out_proj_kernel, final_norm=final_norm),
        grid=(t_rows // OUT_TM,),
        in_specs=[
            pl.BlockSpec((OUT_TM, MAIN_W), lambda i: (i, 0)),
            pl.BlockSpec((OUT_TM, MEM_W), lambda i: (i, 0)),
            pl.BlockSpec((OUT_TM, MEM_W), lambda i: (i, z_blk)),
            pl.BlockSpec((N_MEM, 2 * MEM_W), lambda i: (i // tiles_per_seq, 0)),
            pl.BlockSpec((INNER, D_MODEL), lambda i: (0, 0), pipeline_mode=pl.Buffered(1)),
            pl.BlockSpec((OUT_TM, D_MODEL), lambda i: (i, 0)),
            pl.BlockSpec((1, D_MODEL), lambda i: (0, 0)),
        ],
        out_specs=pl.BlockSpec((OUT_TM, D_MODEL), lambda i: (i, 0)),
        out_shape=jax.ShapeDtypeStruct((t_rows, D_MODEL), f32),
        compiler_params=pltpu.CompilerParams(dimension_semantics=("arbitrary",),
                                             vmem_limit_bytes=VMEM_LIMIT),
        name="out_proj_final" if final_norm else "out_proj",
    )(o_main, qm, z, kv, w_out, x, g_final.reshape(1, D_MODEL))


def _rotary_tables():
    half = RET_KEY_DIM // 2
    pos = np.arange(SEQ, dtype=np.float64)
    inv = 1.0 / (ROPE_BASE ** (np.arange(half, dtype=np.float64) / half))
    ang = pos[:, None] * inv[None, :]
    cos, sin = np.cos(ang), np.sin(ang)
    reps = LANES // RET_KEY_DIM
    cos_t = np.tile(np.concatenate([cos, cos], axis=-1), (1, reps))
    sin_t = np.tile(np.concatenate([-sin, sin], axis=-1), (1, reps))
    k_scale = RET_KEY_DIM ** -0.5
    return tuple(jnp.asarray(t, dtype=f32) for t in (cos_t, sin_t, cos_t * k_scale, sin_t * k_scale))


def kernel(x, mem, norm_g, fox_w_in, fox_b_f, ret_w_in, mem_norm_g, w_mem_kv, w_out, final_norm_g):
    t_rows = BATCH * SEQ
    x2 = x.reshape(t_rows, D_MODEL)

    kv0, kv1 = _norm_proj(mem.reshape(BATCH * N_MEM, D_MODEL), mem_norm_g,
                          [w_mem_kv[0].astype(bf16), w_mem_kv[1].astype(bf16)],
                          [2 * MEM_W, 2 * MEM_W], [bf16, bf16], name="mem_kv_proj")

    o_f = 3 * MAIN_W
    o_qm = o_f + N_MAIN_HEADS
    w0t = jnp.swapaxes(fox_w_in[0], 0, 1).astype(bf16)
    w_qkv = w0t[:o_f]
    w_qmz = w0t[o_qm:]
    w_f = jnp.pad(w0t[o_f:o_qm], ((0, F_PAD - N_MAIN_HEADS), (0, 0)))
    q, k, vt, qm, z, f = _norm_proj(
        x2, norm_g[0], [w_qkv, w_qmz, w_f], [MAIN_W, MAIN_W, MAIN_W, MEM_W, INNER, F_PAD],
        [bf16, bf16, bf16, bf16, bf16, f32], ["fox_q", "plain", "kt", "plain", "plain", "plain"],
        w_transposed=True, name="fox_in_proj")
    nb = _fox_gate(f, fox_b_f[0])
    o_main = _fox_attention(q, k, vt, nb, z)
    x2 = _out_proj(o_main, qm, z, kv0, w_out[0].astype(bf16), x2, final_norm_g, final_norm=False)

    w1 = ret_w_in[0].astype(bf16)
    qk, v, qm, z = _norm_proj(
        x2, norm_g[1], [w1], [2 * RET_QK_W, MAIN_W, MEM_W, INNER], [bf16] * 4,
        ["rot", "plain", "plain", "plain"], tables=_rotary_tables(), name="ret_in_proj")
    o_main = _retention(qk, v, z)
    out = _out_proj(o_main, qm, z, kv1, w_out[1].astype(bf16), x2, final_norm_g, final_norm=True)
    return out.reshape(BATCH, SEQ, D_MODEL)
```

```python
import functools
import math

import jax
import jax.numpy as jnp
import numpy as np
from jax import lax
from jax.experimental import pallas as pl
from jax.experimental.pallas import tpu as pltpu

D_MODEL = 1024
BATCH = 8
SEQ = 2048
HEAD_DIM = 128
N_MAIN_HEADS = 12
N_MEM_HEADS = 4
N_MEM = 256
MAIN_W = N_MAIN_HEADS * HEAD_DIM
MEM_W = N_MEM_HEADS * HEAD_DIM
INNER = MAIN_W + MEM_W
RET_KEY_DIM = HEAD_DIM // 2
RET_QK_W = N_MAIN_HEADS * RET_KEY_DIM
RET_CHUNK = 128
ROPE_BASE = 10000.0
EPS = 1e-6
NEG = -1e30
LOG2E = 1.4426950408889634

LANES = 128
F_PAD = LANES
VMEM_LIMIT = 56 * 1024 * 1024

PROJ_TM = 512
PROJ_TN = 512
FOX_T = 256
FOX_HEADS_PER_STEP = 6
BIAS_TERMS = 3
FOX_SUM_ROWS = 16
FOX_Q_SCALE = LOG2E / math.sqrt(HEAD_DIM)
RET_PAIRS_PER_STEP = 3
RET_CHUNKS_PER_ITER = 2
OUT_TM = 512

f32 = jnp.float32
bf16 = jnp.bfloat16


def _silu(z):
    h = 0.5 * z
    return h + h * jnp.tanh(h)


def _rmsnorm_rows(x, g):
    ms = jnp.mean(x * x, axis=-1, keepdims=True)
    return (x * lax.rsqrt(ms + EPS)) * g


def _split_bf16(v):
    hi = v.astype(bf16)
    r1 = v - hi.astype(f32)
    mid = r1.astype(bf16)
    lo = (r1 - mid.astype(f32)).astype(bf16)
    return hi, mid, lo


def _norm_proj_kernel(*refs, n_weights, w_transposed, out_widths, out_kinds):
    n_out = len(out_widths)
    x_ref, g_ref = refs[:2]
    w_refs = refs[2:2 + n_weights]
    tab_refs = refs[2 + n_weights:len(refs) - n_out]
    out_refs = refs[len(refs) - n_out:]
    h = _rmsnorm_rows(x_ref[...], g_ref[...]).astype(bf16)
    tm = h.shape[0]
    if "rot" in out_kinds:
        lane = lax.broadcasted_iota(jnp.int32, (tm, LANES), 1)
        first_half = (lane % RET_KEY_DIM) < (RET_KEY_DIM // 2)
    out_axis = 0 if w_transposed else 1
    wi, col = 0, 0
    for o_ref, width, kind in zip(out_refs, out_widths, out_kinds):
        if col == w_refs[wi].shape[out_axis]:
            wi, col = wi + 1, 0
        w_ref = w_refs[wi]
        for c0 in range(0, width, PROJ_TN):
            cw = min(PROJ_TN, width - c0)
            if w_transposed:
                y = lax.dot_general(h, w_ref[col + c0:col + c0 + cw, :], (((1,), (1,)), ((), ())),
                                    preferred_element_type=f32)
            else:
                y = jnp.dot(h, w_ref[:, col + c0:col + c0 + cw], preferred_element_type=f32)
            if kind == "rot":
                for s0 in range(0, cw, LANES):
                    ys = y[:, s0:s0 + LANES]
                    sw = jnp.where(first_half, pltpu.roll(ys, LANES - RET_KEY_DIM // 2, 1),
                                   pltpu.roll(ys, RET_KEY_DIM // 2, 1))
                    cos_ref, sin_ref = tab_refs[:2] if c0 + s0 < RET_QK_W else tab_refs[2:]
                    r = ys * cos_ref[...] + sw * sin_ref[...]
                    o_ref[:, c0 + s0:c0 + s0 + LANES] = r.astype(o_ref.dtype)
            elif kind == "kt":
                for r0 in range(tm // FOX_T):
                    for s0 in range(0, cw, LANES):
                        piece = y[r0 * FOX_T:(r0 + 1) * FOX_T, s0:s0 + LANES]
                        o_ref[r0, c0 + s0:c0 + s0 + LANES, :] = piece.T.astype(o_ref.dtype)
            elif kind == "fox_q":
                o_ref[:, c0:c0 + cw] = (y * FOX_Q_SCALE).astype(o_ref.dtype)
            else:
                o_ref[:, c0:c0 + cw] = y.astype(o_ref.dtype)
        col += width


def _norm_proj(x, g, weights, out_widths, out_dtypes, out_kinds=None, *, w_transposed=False, tables=None,
               name):
    t_rows, d = x.shape
    out_kinds = tuple(out_kinds or ["plain"] * len(out_widths))
    w_arrays = [w[0] if isinstance(w, tuple) else w for w in weights]
    w_blocks = [(w[1], d) if isinstance(w, tuple) else w.shape for w in weights]
    out_axis = 0 if w_transposed else 1
    assert sum(out_widths) == sum(blk[out_axis] for blk in w_blocks) and t_rows % PROJ_TM == 0
    tiles_per_seq = SEQ // PROJ_TM
    in_specs = [
        pl.BlockSpec((PROJ_TM, d), lambda i: (i, 0)),
        pl.BlockSpec((1, d), lambda i: (0, 0)),
    ]
    in_specs += [pl.BlockSpec(blk, lambda i: (0, 0), pipeline_mode=pl.Buffered(1)) for blk in w_blocks]
    args = [x, g.reshape(1, d), *w_arrays]
    if "rot" in out_kinds:
        for tab in tables:
            in_specs.append(pl.BlockSpec((PROJ_TM, LANES), lambda i: (i % tiles_per_seq, 0)))
            args.append(tab)
    out_specs, out_shape = [], []
    for wd, dt, kind in zip(out_widths, out_dtypes, out_kinds):
        if kind == "kt":
            kt_per_tile = PROJ_TM // FOX_T
            out_specs.append(pl.BlockSpec((None, kt_per_tile, wd, FOX_T),
                                          lambda i: (i // tiles_per_seq, i % tiles_per_seq, 0, 0)))
            out_shape.append(jax.ShapeDtypeStruct((t_rows // SEQ, SEQ // FOX_T, wd, FOX_T), dt))
        else:
            out_specs.append(pl.BlockSpec((PROJ_TM, wd), lambda i: (i, 0)))
            out_shape.append(jax.ShapeDtypeStruct((t_rows, wd), dt))
    return pl.pallas_call(
        functools.partial(_norm_proj_kernel, n_weights=len(weights), w_transposed=w_transposed,
                          out_widths=tuple(out_widths), out_kinds=out_kinds),
        grid=(t_rows // PROJ_TM,),
        in_specs=in_specs,
        out_specs=out_specs,
        out_shape=out_shape,
        compiler_params=pltpu.CompilerParams(dimension_semantics=("arbitrary",),
                                             vmem_limit_bytes=VMEM_LIMIT),
        name=name,
    )(*args)


def _fox_gate_kernel(f_ref, bf_ref, o_ref):
    blk = LANES
    row = lax.broadcasted_iota(jnp.int32, (blk, blk), 0)
    coli = lax.broadcasted_iota(jnp.int32, (blk, blk), 1)
    tri = jnp.where(row >= coli, 1.0, 0.0).astype(bf16)
    spread = [jnp.where((coli == BIAS_TERMS * row + t) & (row < N_MAIN_HEADS), 1.0, 0.0).astype(bf16)
              for t in range(BIAS_TERMS)]
    carry = jnp.zeros((1, F_PAD), f32)
    for b0 in range(0, SEQ, blk):
        xg = f_ref[b0:b0 + blk, :] + bf_ref[...]
        ls = jnp.minimum(xg, 0.0) - jnp.log1p(jnp.exp(-jnp.abs(xg)))
        cs = sum(jnp.dot(tri, term, preferred_element_type=f32) for term in _split_bf16(ls)) + carry
        carry = cs[blk - 1:blk, :]
        bias = cs * (-LOG2E)
        nb = sum(jnp.dot(term, sp, preferred_element_type=f32)
                 for term, sp in zip(_split_bf16(bias), spread))
        o_ref[b0:b0 + blk, :] = nb.astype(o_ref.dtype)


def _fox_gate(f, b_f):
    bf_pad = jnp.zeros((1, F_PAD), f32).at[0, :N_MAIN_HEADS].set(b_f.astype(f32))
    return pl.pallas_call(
        _fox_gate_kernel,
        grid=(BATCH,),
        in_specs=[pl.BlockSpec((SEQ, F_PAD), lambda b: (b, 0)),
                  pl.BlockSpec((1, F_PAD), lambda b: (0, 0))],
        out_specs=pl.BlockSpec((SEQ, LANES), lambda b: (b, 0)),
        out_shape=jax.ShapeDtypeStruct((BATCH * SEQ, LANES), bf16),
        compiler_params=pltpu.CompilerParams(dimension_semantics=("arbitrary",)),
        name="fox_gate",
    )(f, bf_pad)


def _fox_attn_kernel(q_ref, k_ref, vt_ref, nb_ref, z_ref, o_ref, m_ref, acc_ref, u_ref, p_ref, a_ref):
    t = FOX_T
    nt = SEQ // t
    head0 = pl.program_id(1) * FOX_HEADS_PER_STEP
    heads = range(FOX_HEADS_PER_STEP)
    col = [slice(hd * HEAD_DIM, (hd + 1) * HEAD_DIM) for hd in heads]
    lane = lax.broadcasted_iota(jnp.int32, (t, LANES), 1)
    sel = []
    for hd in heads:
        lo = BIAS_TERMS * (head0 + hd)
        sel.append(jnp.where((lane >= lo) & (lane < lo + BIAS_TERMS), 1.0, 0.0).astype(bf16))
    ones_rows = jnp.ones((FOX_SUM_ROWS, t), bf16)

    def tile_rows(i):
        return pl.ds(pl.multiple_of(i * t, t), t)

    def scores(qt, kt):
        nbj = nb_ref[tile_rows(kt), :]
        out = []
        for hd in heads:
            k_aug = jnp.concatenate([k_ref[tile_rows(kt), col[hd]], nbj], axis=1)
            q_aug = jnp.concatenate([q_ref[tile_rows(qt), col[hd]], sel[hd]], axis=1)
            out.append(lax.dot_general(k_aug, q_aug, (((1,), (1,)), ((), ())),
                                       preferred_element_type=f32))
        return out

    def softmax_update(qt, u, masked):
        if masked:
            causal = (lax.broadcasted_iota(jnp.int32, (t, t), 0) <= lax.broadcasted_iota(jnp.int32, (t, t), 1))
        p, alpha = [], []
        for hd in heads:
            uh = jnp.where(causal, u[hd], NEG) if masked else u[hd]
            m_old = m_ref[qt, hd]
            m_new = jnp.maximum(m_old, jnp.max(uh, axis=0, keepdims=True))
            p.append(jnp.exp2(uh - m_new).astype(bf16))
            alpha.append(jnp.exp2(m_old - m_new))
            m_ref[qt, hd] = m_new
        return p, alpha

    def pv(kt, p):
        return [jnp.dot(jnp.concatenate([vt_ref[kt, col[hd], :], ones_rows], axis=0), p[hd],
                        preferred_element_type=f32) for hd in heads]

    def accumulate(qt, alpha, pv_vals):
        for hd in heads:
            acc_ref[qt, hd] = alpha[hd] * acc_ref[qt, hd] + pv_vals[hd]

    def finalize(qt):
        for hd in heads:
            acc = acc_ref[qt, hd]
            out_t = acc[:HEAD_DIM, :] * (1.0 / acc[HEAD_DIM:HEAD_DIM + 1, :])
            gate = _silu(z_ref[tile_rows(qt), col[hd]].astype(f32))
            o_ref[tile_rows(qt), col[hd]] = (out_t.T * gate).astype(o_ref.dtype)

    def next_item(qt, kt):
        wrap = kt + 1 >= qt
        return jnp.minimum(jnp.where(wrap, qt + 1, qt), nt - 1), jnp.where(wrap, 0, kt + 1)

    def load_list(ref):
        return [ref[hd] for hd in heads]

    def store_list(ref, vals):
        for hd in heads:
            ref[hd] = vals[hd]

    m_ref[...] = jnp.full(m_ref.shape, NEG, f32)
    acc_ref[...] = jnp.zeros(acc_ref.shape, f32)
    p_ref[...] = jnp.zeros(p_ref.shape, bf16)
    a_ref[...] = jnp.ones(a_ref.shape, f32)

    n_items = nt * (nt - 1) // 2
    assert n_items % 2 == 0
    store_list(u_ref, scores(1, 0))

    def pass1(_, carry):
        qa, ka, qpb, kpb = carry
        u_a = load_list(u_ref)
        pv_prev = pv(kpb, load_list(p_ref))
        qb, kb = next_item(qa, ka)
        u_b = scores(qb, kb)
        qn, kn = next_item(qb, kb)
        store_list(u_ref, scores(qn, kn))
        p_a, al_a = softmax_update(qa, u_a, False)
        accumulate(qpb, load_list(a_ref), pv_prev)
        pv_a = pv(ka, p_a)
        p_b, al_b = softmax_update(qb, u_b, False)
        accumulate(qa, al_a, pv_a)
        store_list(p_ref, p_b)
        store_list(a_ref, al_b)
        return qn, kn, qb, kb

    one, zero = jnp.int32(1), jnp.int32(0)
    _, _, qpb, kpb = lax.fori_loop(0, n_items // 2, pass1, (one, zero, one, zero))
    accumulate(qpb, load_list(a_ref), pv(kpb, load_list(p_ref)))

    assert nt % 2 == 0
    store_list(u_ref, scores(0, 0))

    def pass2(i, carry):
        ta = 2 * i
        tb = ta + 1
        u_b = scores(tb, tb)
        p_a, al_a = softmax_update(ta, load_list(u_ref), True)
        pv_a = pv(ta, p_a)
        tn = jnp.minimum(ta + 2, nt - 1)
        u_n = scores(tn, tn)
        p_b, al_b = softmax_update(tb, u_b, True)
        accumulate(ta, al_a, pv_a)
        pv_b = pv(tb, p_b)
        finalize(ta)
        store_list(u_ref, u_n)
        accumulate(tb, al_b, pv_b)
        finalize(tb)
        return carry

    lax.fori_loop(0, nt // 2, pass2, 0)


def _fox_attention(q, k, vt, nb, z):
    hw = FOX_HEADS_PER_STEP * HEAD_DIM
    return pl.pallas_call(
        _fox_attn_kernel,
        grid=(BATCH, N_MAIN_HEADS // FOX_HEADS_PER_STEP),
        in_specs=[
            pl.BlockSpec((SEQ, hw), lambda b, p: (b, p)),
            pl.BlockSpec((SEQ, hw), lambda b, p: (b, p)),
            pl.BlockSpec((None, SEQ // FOX_T, hw, FOX_T), lambda b, p: (b, 0, p, 0)),
            pl.BlockSpec((SEQ, LANES), lambda b, p: (b, 0)),
            pl.BlockSpec((SEQ, hw), lambda b, p: (b, p)),
        ],
        out_specs=pl.BlockSpec((SEQ, hw), lambda b, p: (b, p)),
        out_shape=jax.ShapeDtypeStruct((BATCH * SEQ, MAIN_W), bf16),
        scratch_shapes=[
            pltpu.VMEM((SEQ // FOX_T, FOX_HEADS_PER_STEP, 1, FOX_T), f32),
            pltpu.VMEM((SEQ // FOX_T, FOX_HEADS_PER_STEP, HEAD_DIM + FOX_SUM_ROWS, FOX_T), f32),
            pltpu.VMEM((FOX_HEADS_PER_STEP, FOX_T, FOX_T), f32),
            pltpu.VMEM((FOX_HEADS_PER_STEP, FOX_T, FOX_T), bf16),
            pltpu.VMEM((FOX_HEADS_PER_STEP, 1, FOX_T), f32),
        ],
        compiler_params=pltpu.CompilerParams(dimension_semantics=("arbitrary", "arbitrary"),
                                             vmem_limit_bytes=VMEM_LIMIT),
        name="fox_attention",
    )(q, k, vt, nb, z)


def _retention_kernel(q_ref, k_ref, v_ref, z_ref, d_ref, xi_ref, zeta_ref, g_ref, o_ref, r_ref):
    c_len = RET_CHUNK
    r_ref[...] = jnp.zeros(r_ref.shape, f32)
    lane = lax.broadcasted_iota(jnp.int32, (c_len, LANES), 1)
    head_mask = (lane < RET_KEY_DIM, lane >= RET_KEY_DIM)
    pairs = range(RET_PAIRS_PER_STEP)
    chunks = range(RET_CHUNKS_PER_ITER)

    def body(it, carry):
        rows = [pl.ds(pl.multiple_of((it * RET_CHUNKS_PER_ITER + ci) * c_len, c_len), c_len) for ci in chunks]
        qa, s, upd = {}, {}, {}
        for ci in chunks:
            for pr in pairs:
                q2 = q_ref[rows[ci], pr * LANES:(pr + 1) * LANES]
                k2 = k_ref[rows[ci], pr * LANES:(pr + 1) * LANES]
                v2 = v_ref[rows[ci], pr * 2 * HEAD_DIM:(pr + 1) * 2 * HEAD_DIM]
                qa[ci, pr] = jnp.concatenate([jnp.where(head_mask[hd], q2, jnp.zeros_like(q2))
                                              for hd in range(2)], axis=0)
                s[ci, pr] = lax.dot_general(qa[ci, pr], k2, (((1,), (1,)), ((), ())),
                                            preferred_element_type=f32)
                kz = (k2.astype(f32) * zeta_ref[pr]).astype(bf16)
                upd[ci, pr] = lax.dot_general(kz, v2, (((0,), (0,)), ((), ())), preferred_element_type=f32)
        r_b = {}
        for pr in pairs:
            r_state = r_ref[pr]
            for ci in chunks:
                r_b[ci, pr] = r_state.astype(bf16)
                r_state = r_state * g_ref[pr] + upd[ci, pr]
            r_ref[pr] = r_state
        o = {}
        for ci in chunks:
            for pr in pairs:
                for hd in range(2):
                    hrows = slice(hd * c_len, (hd + 1) * c_len)
                    cols = slice((2 * pr + hd) * HEAD_DIM, (2 * pr + hd + 1) * HEAD_DIM)
                    inner = (s[ci, pr][hrows, :] * d_ref[2 * pr + hd]).astype(bf16)
                    intra = jnp.dot(inner, v_ref[rows[ci], cols], preferred_element_type=f32)
                    cross = jnp.dot(qa[ci, pr][hrows, :], r_b[ci, pr][:, hd * HEAD_DIM:(hd + 1) * HEAD_DIM],
                                    preferred_element_type=f32)
                    o[ci, pr, hd] = intra + cross * xi_ref[2 * pr + hd]
        for ci in chunks:
            for pr in pairs:
                for hd in range(2):
                    cols = slice((2 * pr + hd) * HEAD_DIM, (2 * pr + hd + 1) * HEAD_DIM)
                    oh = o[ci, pr, hd]
                    ms = jnp.mean(oh * oh, axis=-1, keepdims=True)
                    on = oh * lax.rsqrt(ms + EPS)
                    zg = _silu(z_ref[rows[ci], cols].astype(f32))
                    o_ref[rows[ci], cols] = (on * zg).astype(o_ref.dtype)
        return carry

    lax.fori_loop(0, SEQ // (c_len * RET_CHUNKS_PER_ITER), body, 0)


def _retention_tables():
    h, c_len = N_MAIN_HEADS, RET_CHUNK
    lg = np.log1p(-np.exp2(-5.0 - np.arange(h, dtype=np.float64)))
    n = np.arange(c_len, dtype=np.float64)
    diff = n[:, None] - n[None, :]
    d_inner = np.where(diff[None] >= 0, np.exp(lg[:, None, None] * np.maximum(diff, 0.0)[None]), 0.0)
    xi = np.exp(lg[:, None] * (n[None, :] + 1.0))
    zeta = np.exp(lg[:, None] * (c_len - 1.0 - n[None, :]))
    g_chunk = np.exp(lg * c_len)
    xi_b = np.broadcast_to(xi[:, :, None], (h, c_len, HEAD_DIM))
    zeta2 = np.repeat(zeta.reshape(h // 2, 2, c_len).transpose(0, 2, 1), RET_KEY_DIM, axis=-1)
    g2 = np.broadcast_to(np.repeat(g_chunk.reshape(h // 2, 2), RET_KEY_DIM, axis=-1)[:, :, None],
                         (h // 2, 2 * RET_KEY_DIM, 2 * HEAD_DIM))
    return tuple(jnp.asarray(np.ascontiguousarray(t), dtype=f32) for t in (d_inner, xi_b, zeta2, g2))


def _retention(qk, v, z):
    n_pairs = N_MAIN_HEADS // 2
    pp = RET_PAIRS_PER_STEP
    steps = n_pairs // pp
    d_inner, xi_b, zeta2, g2 = _retention_tables()
    c_len = RET_CHUNK
    return pl.pallas_call(
        _retention_kernel,
        grid=(BATCH, steps),
        in_specs=[
            pl.BlockSpec((SEQ, pp * LANES), lambda b, p: (b, p)),
            pl.BlockSpec((SEQ, pp * LANES), lambda b, p: (b, steps + p)),
            pl.BlockSpec((SEQ, pp * 2 * HEAD_DIM), lambda b, p: (b, p)),
            pl.BlockSpec((SEQ, pp * 2 * HEAD_DIM), lambda b, p: (b, p)),
            pl.BlockSpec((2 * pp, c_len, c_len), lambda b, p: (p, 0, 0)),
            pl.BlockSpec((2 * pp, c_len, HEAD_DIM), lambda b, p: (p, 0, 0)),
            pl.BlockSpec((pp, c_len, LANES), lambda b, p: (p, 0, 0)),
            pl.BlockSpec((pp, LANES, 2 * HEAD_DIM), lambda b, p: (p, 0, 0)),
        ],
        out_specs=pl.BlockSpec((SEQ, pp * 2 * HEAD_DIM), lambda b, p: (b, p)),
        out_shape=jax.ShapeDtypeStruct((BATCH * SEQ, MAIN_W), bf16),
        scratch_shapes=[pltpu.VMEM((pp, LANES, 2 * HEAD_DIM), f32)],
        compiler_params=pltpu.CompilerParams(dimension_semantics=("arbitrary", "arbitrary"),
                                             vmem_limit_bytes=VMEM_LIMIT),
        name="retention",
    )(qk, qk, v, z, d_inner, xi_b, zeta2, g2)


def _out_proj_kernel(om_ref, qm_ref, zm_ref, kv_ref, w_ref, x_ref, g_ref, o_ref, *, final_norm):
    scale = 1.0 / math.sqrt(HEAD_DIM)
    heads = range(N_MEM_HEADS)
    col = [slice(hd * HEAD_DIM, (hd + 1) * HEAD_DIM) for hd in heads]
    half = D_MODEL // 2
    s = [lax.dot_general(qm_ref[:, col[hd]], kv_ref[:, col[hd]], (((1,), (1,)), ((), ())),
                         preferred_element_type=f32) * scale for hd in heads]
    y_lo = jnp.dot(om_ref[...], w_ref[:MAIN_W, :half], preferred_element_type=f32)
    p = []
    for hd in heads:
        e = jnp.exp(s[hd] - jnp.max(s[hd], axis=-1, keepdims=True))
        p.append((e * (1.0 / jnp.sum(e, axis=-1, keepdims=True))).astype(bf16))
    memo = [jnp.dot(p[hd], kv_ref[:, MEM_W + hd * HEAD_DIM:MEM_W + (hd + 1) * HEAD_DIM],
                    preferred_element_type=f32) for hd in heads]
    y_hi = jnp.dot(om_ref[...], w_ref[:MAIN_W, half:], preferred_element_type=f32)
    og = jnp.concatenate([(memo[hd] * _silu(zm_ref[:, col[hd]].astype(f32))).astype(bf16) for hd in heads],
                         axis=1)
    y = jnp.concatenate([y_lo, y_hi], axis=1) + jnp.dot(og, w_ref[MAIN_W:, :], preferred_element_type=f32)
    xn = x_ref[...] + y
    if final_norm:
        xn = _rmsnorm_rows(xn, g_ref[...])
    o_ref[...] = xn


def _out_proj(o_main, qm, z, kv, w_out, x, g_final, *, final_norm):
    t_rows = x.shape[0]
    tiles_per_seq = SEQ // OUT_TM
    z_blk = MAIN_W // MEM_W
    return pl.pallas_call(
        functools.partial(_out_proj_kernel, final_norm=final_norm),
        grid=(t_rows // OUT_TM,),
        in_specs=[
            pl.BlockSpec((OUT_TM, MAIN_W), lambda i: (i, 0)),
            pl.BlockSpec((OUT_TM, MEM_W), lambda i: (i, 0)),
            pl.BlockSpec((OUT_TM, MEM_W), lambda i: (i, z_blk)),
            pl.BlockSpec((N_MEM, 2 * MEM_W), lambda i: (i // tiles_per_seq, 0)),
            pl.BlockSpec((INNER, D_MODEL), lambda i: (0, 0), pipeline_mode=pl.Buffered(1)),
            pl.BlockSpec((OUT_TM, D_MODEL), lambda i: (i, 0)),
            pl.BlockSpec((1, D_MODEL), lambda i: (0, 0)),
        ],
        out_specs=pl.BlockSpec((OUT_TM, D_MODEL), lambda i: (i, 0)),
        out_shape=jax.ShapeDtypeStruct((t_rows, D_MODEL), f32),
        compiler_params=pltpu.CompilerParams(dimension_semantics=("arbitrary",),
                                             vmem_limit_bytes=VMEM_LIMIT),
        name="out_proj_final" if final_norm else "out_proj",
    )(o_main, qm, z, kv, w_out, x, g_final.reshape(1, D_MODEL))


def _rotary_tables():
    half = RET_KEY_DIM // 2
    pos = np.arange(SEQ, dtype=np.float64)
    inv = 1.0 / (ROPE_BASE ** (np.arange(half, dtype=np.float64) / half))
    ang = pos[:, None] * inv[None, :]
    cos, sin = np.cos(ang), np.sin(ang)
    reps = LANES // RET_KEY_DIM
    cos_t = np.tile(np.concatenate([cos, cos], axis=-1), (1, reps))
    sin_t = np.tile(np.concatenate([-sin, sin], axis=-1), (1, reps))
    k_scale = RET_KEY_DIM ** -0.5
    return tuple(jnp.asarray(t, dtype=f32) for t in (cos_t, sin_t, cos_t * k_scale, sin_t * k_scale))


def kernel(x, mem, norm_g, fox_w_in, fox_b_f, ret_w_in, mem_norm_g, w_mem_kv, w_out, final_norm_g):
    t_rows = BATCH * SEQ
    x2 = x.reshape(t_rows, D_MODEL)

    kv0, kv1 = _norm_proj(mem.reshape(BATCH * N_MEM, D_MODEL), mem_norm_g,
                          [w_mem_kv[0].astype(bf16), w_mem_kv[1].astype(bf16)],
                          [2 * MEM_W, 2 * MEM_W], [bf16, bf16], name="mem_kv_proj")

    o_f = 3 * MAIN_W
    o_qm = o_f + N_MAIN_HEADS
    w0t = jnp.swapaxes(fox_w_in[0], 0, 1).astype(bf16)
    w_qmz = w0t[o_qm:]
    w_f = jnp.pad(w0t[o_f:o_qm], ((0, F_PAD - N_MAIN_HEADS), (0, 0)))
    q, k, vt, qm, z, f = _norm_proj(
        x2, norm_g[0], [(w0t, o_f), w_qmz, w_f], [MAIN_W, MAIN_W, MAIN_W, MEM_W, INNER, F_PAD],
        [bf16, bf16, bf16, bf16, bf16, f32], ["fox_q", "plain", "kt", "plain", "plain", "plain"],
        w_transposed=True, name="fox_in_proj")
    nb = _fox_gate(f, fox_b_f[0])
    o_main = _fox_attention(q, k, vt, nb, z)
    x2 = _out_proj(o_main, qm, z, kv0, w_out[0].astype(bf16), x2, final_norm_g, final_norm=False)

    w1 = ret_w_in[0].astype(bf16)
    qk, v, qm, z = _norm_proj(
        x2, norm_g[1], [w1], [2 * RET_QK_W, MAIN_W, MEM_W, INNER], [bf16] * 4,
        ["rot", "plain", "plain", "plain"], tables=_rotary_tables(), name="ret_in_proj")
    o_main = _retention(qk, v, z)
    out = _out_proj(o_main, qm, z, kv1, w_out[1].astype(bf16), x2, final_norm_g, final_norm=True)
    return out.reshape(BATCH, SEQ, D_MODEL)
```

```python
import functools
import math

import jax
import jax.numpy as jnp
import numpy as np
from jax import lax
from jax.experimental import pallas as pl
from jax.experimental.pallas import tpu as pltpu

D_MODEL = 1024
BATCH = 8
SEQ = 2048
HEAD_DIM = 128
N_MAIN_HEADS = 12
N_MEM_HEADS = 4
N_MEM = 256
MAIN_W = N_MAIN_HEADS * HEAD_DIM
MEM_W = N_MEM_HEADS * HEAD_DIM
INNER = MAIN_W + MEM_W
RET_KEY_DIM = HEAD_DIM // 2
RET_QK_W = N_MAIN_HEADS * RET_KEY_DIM
RET_CHUNK = 128
ROPE_BASE = 10000.0
EPS = 1e-6
NEG = -1e30
LOG2E = 1.4426950408889634

LANES = 128
F_PAD = LANES
VMEM_LIMIT = 56 * 1024 * 1024

PROJ_TM = 512
PROJ_TN = 512
FOX_T = 256
FOX_HEADS_PER_STEP = 6
BIAS_TERMS = 3
FOX_SUM_ROWS = 16
FOX_Q_SCALE = LOG2E / math.sqrt(HEAD_DIM)
RET_PAIRS_PER_STEP = 3
RET_CHUNKS_PER_ITER = 2
OUT_TM = 512

f32 = jnp.float32
bf16 = jnp.bfloat16


def _silu(z):
    h = 0.5 * z
    return h + h * jnp.tanh(h)


def _rmsnorm_rows(x, g):
    ms = jnp.mean(x * x, axis=-1, keepdims=True)
    return (x * lax.rsqrt(ms + EPS)) * g


def _split_bf16(v):
    hi = v.astype(bf16)
    r1 = v - hi.astype(f32)
    mid = r1.astype(bf16)
    lo = (r1 - mid.astype(f32)).astype(bf16)
    return hi, mid, lo


def _norm_proj_kernel(*refs, n_weights, w_transposed, out_widths, out_kinds):
    n_out = len(out_widths)
    x_ref, g_ref = refs[:2]
    w_refs = refs[2:2 + n_weights]
    tab_refs = refs[2 + n_weights:len(refs) - n_out]
    out_refs = refs[len(refs) - n_out:]
    h = _rmsnorm_rows(x_ref[...], g_ref[...]).astype(bf16)
    tm = h.shape[0]
    if "rot" in out_kinds:
        lane = lax.broadcasted_iota(jnp.int32, (tm, LANES), 1)
        first_half = (lane % RET_KEY_DIM) < (RET_KEY_DIM // 2)
    out_axis = 0 if w_transposed else 1
    wi, col = 0, 0
    for o_ref, width, kind in zip(out_refs, out_widths, out_kinds):
        if col == w_refs[wi].shape[out_axis]:
            wi, col = wi + 1, 0
        w_ref = w_refs[wi]
        for c0 in range(0, width, PROJ_TN):
            cw = min(PROJ_TN, width - c0)
            if w_transposed:
                y = lax.dot_general(h, w_ref[col + c0:col + c0 + cw, :], (((1,), (1,)), ((), ())),
                                    preferred_element_type=f32)
            else:
                y = jnp.dot(h, w_ref[:, col + c0:col + c0 + cw], preferred_element_type=f32)
            if kind == "rot":
                for s0 in range(0, cw, LANES):
                    ys = y[:, s0:s0 + LANES]
                    sw = jnp.where(first_half, pltpu.roll(ys, LANES - RET_KEY_DIM // 2, 1),
                                   pltpu.roll(ys, RET_KEY_DIM // 2, 1))
                    cos_ref, sin_ref = tab_refs[:2] if c0 + s0 < RET_QK_W else tab_refs[2:]
                    r = ys * cos_ref[...] + sw * sin_ref[...]
                    o_ref[:, c0 + s0:c0 + s0 + LANES] = r.astype(o_ref.dtype)
            elif kind == "kt":
                for r0 in range(tm // FOX_T):
                    for s0 in range(0, cw, LANES):
                        piece = y[r0 * FOX_T:(r0 + 1) * FOX_T, s0:s0 + LANES]
                        o_ref[r0, c0 + s0:c0 + s0 + LANES, :] = piece.T.astype(o_ref.dtype)
            elif kind == "fox_q":
                o_ref[:, c0:c0 + cw] = (y * FOX_Q_SCALE).astype(o_ref.dtype)
            else:
                o_ref[:, c0:c0 + cw] = y.astype(o_ref.dtype)
        col += width


def _norm_proj(x, g, weights, out_widths, out_dtypes, out_kinds=None, *, w_transposed=False, tables=None,
               name):
    t_rows, d = x.shape
    out_kinds = tuple(out_kinds or ["plain"] * len(out_widths))
    w_arrays = [w[0] if isinstance(w, tuple) else w for w in weights]
    w_blocks = [(w[1], d) if isinstance(w, tuple) else w.shape for w in weights]
    out_axis = 0 if w_transposed else 1
    assert sum(out_widths) == sum(blk[out_axis] for blk in w_blocks) and t_rows % PROJ_TM == 0
    tiles_per_seq = SEQ // PROJ_TM
    in_specs = [
        pl.BlockSpec((PROJ_TM, d), lambda i: (i, 0)),
        pl.BlockSpec((1, d), lambda i: (0, 0)),
    ]
    in_specs += [pl.BlockSpec(blk, lambda i: (0, 0), pipeline_mode=pl.Buffered(1)) for blk in w_blocks]
    args = [x, g.reshape(1, d), *w_arrays]
    if "rot" in out_kinds:
        for tab in tables:
            in_specs.append(pl.BlockSpec((PROJ_TM, LANES), lambda i: (i % tiles_per_seq, 0)))
            args.append(tab)
    out_specs, out_shape = [], []
    for wd, dt, kind in zip(out_widths, out_dtypes, out_kinds):
        if kind == "kt":
            kt_per_tile = PROJ_TM // FOX_T
            out_specs.append(pl.BlockSpec((None, kt_per_tile, wd, FOX_T),
                                          lambda i: (i // tiles_per_seq, i % tiles_per_seq, 0, 0)))
            out_shape.append(jax.ShapeDtypeStruct((t_rows // SEQ, SEQ // FOX_T, wd, FOX_T), dt))
        else:
            out_specs.append(pl.BlockSpec((PROJ_TM, wd), lambda i: (i, 0)))
            out_shape.append(jax.ShapeDtypeStruct((t_rows, wd), dt))
    return pl.pallas_call(
        functools.partial(_norm_proj_kernel, n_weights=len(weights), w_transposed=w_transposed,
                          out_widths=tuple(out_widths), out_kinds=out_kinds),
        grid=(t_rows // PROJ_TM,),
        in_specs=in_specs,
        out_specs=out_specs,
        out_shape=out_shape,
        compiler_params=pltpu.CompilerParams(dimension_semantics=("arbitrary",),
                                             vmem_limit_bytes=VMEM_LIMIT),
        name=name,
    )(*args)


def _fox_gate_kernel(f_ref, bf_ref, o_ref):
    blk = LANES
    row = lax.broadcasted_iota(jnp.int32, (blk, blk), 0)
    coli = lax.broadcasted_iota(jnp.int32, (blk, blk), 1)
    tri = jnp.where(row >= coli, 1.0, 0.0).astype(bf16)
    spread = [jnp.where((coli == BIAS_TERMS * row + t) & (row < N_MAIN_HEADS), 1.0, 0.0).astype(bf16)
              for t in range(BIAS_TERMS)]
    carry = jnp.zeros((1, F_PAD), f32)
    for b0 in range(0, SEQ, blk):
        xg = f_ref[b0:b0 + blk, :] + bf_ref[...]
        ls = jnp.minimum(xg, 0.0) - jnp.log1p(jnp.exp(-jnp.abs(xg)))
        cs = sum(jnp.dot(tri, term, preferred_element_type=f32) for term in _split_bf16(ls)) + carry
        carry = cs[blk - 1:blk, :]
        bias = cs * (-LOG2E)
        nb = sum(jnp.dot(term, sp, preferred_element_type=f32)
                 for term, sp in zip(_split_bf16(bias), spread))
        o_ref[b0:b0 + blk, :] = nb.astype(o_ref.dtype)


def _fox_gate(f, b_f):
    bf_pad = jnp.zeros((1, F_PAD), f32).at[0, :N_MAIN_HEADS].set(b_f.astype(f32))
    return pl.pallas_call(
        _fox_gate_kernel,
        grid=(BATCH,),
        in_specs=[pl.BlockSpec((SEQ, F_PAD), lambda b: (b, 0)),
                  pl.BlockSpec((1, F_PAD), lambda b: (0, 0))],
        out_specs=pl.BlockSpec((SEQ, LANES), lambda b: (b, 0)),
        out_shape=jax.ShapeDtypeStruct((BATCH * SEQ, LANES), bf16),
        compiler_params=pltpu.CompilerParams(dimension_semantics=("arbitrary",)),
        name="fox_gate",
    )(f, bf_pad)


def _fox_attn_kernel(q_ref, k_ref, vt_ref, nb_ref, z_ref, o_ref, m_ref, acc_ref, u_ref, p_ref, a_ref):
    t = FOX_T
    nt = SEQ // t
    head0 = pl.program_id(1) * FOX_HEADS_PER_STEP
    heads = range(FOX_HEADS_PER_STEP)
    col = [slice(hd * HEAD_DIM, (hd + 1) * HEAD_DIM) for hd in heads]
    lane = lax.broadcasted_iota(jnp.int32, (t, LANES), 1)
    sel = []
    for hd in heads:
        lo = BIAS_TERMS * (head0 + hd)
        sel.append(jnp.where((lane >= lo) & (lane < lo + BIAS_TERMS), 1.0, 0.0).astype(bf16))
    ones_rows = jnp.ones((FOX_SUM_ROWS, t), bf16)

    def tile_rows(i):
        return pl.ds(pl.multiple_of(i * t, t), t)

    def score(qt, kt, hd):
        k_aug = jnp.concatenate([k_ref[tile_rows(kt), col[hd]], nb_ref[tile_rows(kt), :]], axis=1)
        q_aug = jnp.concatenate([q_ref[tile_rows(qt), col[hd]], sel[hd]], axis=1)
        return lax.dot_general(k_aug, q_aug, (((1,), (1,)), ((), ())),
                               preferred_element_type=f32)

    def scores(qt, kt):
        return [score(qt, kt, hd) for hd in heads]

    def softmax_update(qt, u, masked):
        if masked:
            causal = (lax.broadcasted_iota(jnp.int32, (t, t), 0) <= lax.broadcasted_iota(jnp.int32, (t, t), 1))
        p, alpha = [], []
        for hd in heads:
            uh = jnp.where(causal, u[hd], NEG) if masked else u[hd]
            m_old = m_ref[qt, hd]
            m_new = jnp.maximum(m_old, jnp.max(uh, axis=0, keepdims=True))
            p.append(jnp.exp2(uh - m_new).astype(bf16))
            alpha.append(jnp.exp2(m_old - m_new))
            m_ref[qt, hd] = m_new
        return p, alpha

    def pv1(kt, p_hd, hd):
        return jnp.dot(jnp.concatenate([vt_ref[kt, col[hd], :], ones_rows], axis=0), p_hd,
                       preferred_element_type=f32)

    def pv(kt, p):
        return [pv1(kt, p[hd], hd) for hd in heads]

    def accumulate(qt, alpha, pv_vals):
        for hd in heads:
            acc_ref[qt, hd] = alpha[hd] * acc_ref[qt, hd] + pv_vals[hd]

    def finalize(qt):
        for hd in heads:
            acc = acc_ref[qt, hd]
            out_t = acc[:HEAD_DIM, :] * (1.0 / acc[HEAD_DIM:HEAD_DIM + 1, :])
            gate = _silu(z_ref[tile_rows(qt), col[hd]].astype(f32))
            o_ref[tile_rows(qt), col[hd]] = (out_t.T * gate).astype(o_ref.dtype)

    def next_item(qt, kt):
        wrap = kt + 1 >= qt
        return jnp.minimum(jnp.where(wrap, qt + 1, qt), nt - 1), jnp.where(wrap, 0, kt + 1)

    def load_list(ref):
        return [ref[hd] for hd in heads]

    def store_list(ref, vals):
        for hd in heads:
            ref[hd] = vals[hd]

    m_ref[...] = jnp.full(m_ref.shape, NEG, f32)
    acc_ref[...] = jnp.zeros(acc_ref.shape, f32)
    p_ref[...] = jnp.zeros(p_ref.shape, bf16)
    a_ref[...] = jnp.ones(a_ref.shape, f32)

    n_items = nt * (nt - 1) // 2
    assert n_items % 2 == 0
    store_list(u_ref, scores(1, 0))

    def pass1(_, carry):
        qa, ka, qpb, kpb = carry
        u_a = load_list(u_ref)
        pv_prev = pv(kpb, load_list(p_ref))
        qb, kb = next_item(qa, ka)
        u_b = scores(qb, kb)
        qn, kn = next_item(qb, kb)
        p_a, al_a = softmax_update(qa, u_a, False)
        accumulate(qpb, load_list(a_ref), pv_prev)
        pv_a = []
        for hd in heads:
            u_ref[hd] = score(qn, kn, hd)
            pv_a.append(pv1(ka, p_a[hd], hd))
        p_b, al_b = softmax_update(qb, u_b, False)
        accumulate(qa, al_a, pv_a)
        store_list(p_ref, p_b)
        store_list(a_ref, al_b)
        return qn, kn, qb, kb

    one, zero = jnp.int32(1), jnp.int32(0)
    _, _, qpb, kpb = lax.fori_loop(0, n_items // 2, pass1, (one, zero, one, zero))
    accumulate(qpb, load_list(a_ref), pv(kpb, load_list(p_ref)))

    assert nt % 2 == 0
    store_list(u_ref, scores(0, 0))

    def pass2(i, carry):
        ta = 2 * i
        tb = ta + 1
        u_b = scores(tb, tb)
        p_a, al_a = softmax_update(ta, load_list(u_ref), True)
        pv_a = pv(ta, p_a)
        tn = jnp.minimum(ta + 2, nt - 1)
        u_n = scores(tn, tn)
        p_b, al_b = softmax_update(tb, u_b, True)
        accumulate(ta, al_a, pv_a)
        pv_b = pv(tb, p_b)
        finalize(ta)
        store_list(u_ref, u_n)
        accumulate(tb, al_b, pv_b)
        finalize(tb)
        return carry

    lax.fori_loop(0, nt // 2, pass2, 0)


def _fox_attention(q, k, vt, nb, z):
    hw = FOX_HEADS_PER_STEP * HEAD_DIM
    return pl.pallas_call(
        _fox_attn_kernel,
        grid=(BATCH, N_MAIN_HEADS // FOX_HEADS_PER_STEP),
        in_specs=[
            pl.BlockSpec((SEQ, hw), lambda b, p: (b, p)),
            pl.BlockSpec((SEQ, hw), lambda b, p: (b, p)),
            pl.BlockSpec((None, SEQ // FOX_T, hw, FOX_T), lambda b, p: (b, 0, p, 0)),
            pl.BlockSpec((SEQ, LANES), lambda b, p: (b, 0)),
            pl.BlockSpec((SEQ, hw), lambda b, p: (b, p)),
        ],
        out_specs=pl.BlockSpec((SEQ, hw), lambda b, p: (b, p)),
        out_shape=jax.ShapeDtypeStruct((BATCH * SEQ, MAIN_W), bf16),
        scratch_shapes=[
            pltpu.VMEM((SEQ // FOX_T, FOX_HEADS_PER_STEP, 1, FOX_T), f32),
            pltpu.VMEM((SEQ // FOX_T, FOX_HEADS_PER_STEP, HEAD_DIM + FOX_SUM_ROWS, FOX_T), f32),
            pltpu.VMEM((FOX_HEADS_PER_STEP, FOX_T, FOX_T), f32),
            pltpu.VMEM((FOX_HEADS_PER_STEP, FOX_T, FOX_T), bf16),
            pltpu.VMEM((FOX_HEADS_PER_STEP, 1, FOX_T), f32),
        ],
        compiler_params=pltpu.CompilerParams(dimension_semantics=("arbitrary", "arbitrary"),
                                             vmem_limit_bytes=VMEM_LIMIT),
        name="fox_attention",
    )(q, k, vt, nb, z)


def _retention_kernel(q_ref, k_ref, v_ref, z_ref, d_ref, xi_ref, zeta_ref, g_ref, o_ref, r_ref):
    c_len = RET_CHUNK
    r_ref[...] = jnp.zeros(r_ref.shape, f32)
    lane = lax.broadcasted_iota(jnp.int32, (c_len, LANES), 1)
    head_mask = (lane < RET_KEY_DIM, lane >= RET_KEY_DIM)
    pairs = range(RET_PAIRS_PER_STEP)
    chunks = range(RET_CHUNKS_PER_ITER)

    def body(it, carry):
        rows = [pl.ds(pl.multiple_of((it * RET_CHUNKS_PER_ITER + ci) * c_len, c_len), c_len) for ci in chunks]
        qa, s, upd = {}, {}, {}
        for ci in chunks:
            for pr in pairs:
                q2 = q_ref[rows[ci], pr * LANES:(pr + 1) * LANES]
                k2 = k_ref[rows[ci], pr * LANES:(pr + 1) * LANES]
                v2 = v_ref[rows[ci], pr * 2 * HEAD_DIM:(pr + 1) * 2 * HEAD_DIM]
                qa[ci, pr] = jnp.concatenate([jnp.where(head_mask[hd], q2, jnp.zeros_like(q2))
                                              for hd in range(2)], axis=0)
                s[ci, pr] = lax.dot_general(qa[ci, pr], k2, (((1,), (1,)), ((), ())),
                                            preferred_element_type=f32)
                kz = (k2.astype(f32) * zeta_ref[pr]).astype(bf16)
                upd[ci, pr] = lax.dot_general(kz, v2, (((0,), (0,)), ((), ())), preferred_element_type=f32)
        r_b = {}
        for pr in pairs:
            r_state = r_ref[pr]
            for ci in chunks:
                r_b[ci, pr] = r_state.astype(bf16)
                r_state = r_state * g_ref[pr] + upd[ci, pr]
            r_ref[pr] = r_state
        o = {}
        for ci in chunks:
            for pr in pairs:
                for hd in range(2):
                    hrows = slice(hd * c_len, (hd + 1) * c_len)
                    cols = slice((2 * pr + hd) * HEAD_DIM, (2 * pr + hd + 1) * HEAD_DIM)
                    inner = (s[ci, pr][hrows, :] * d_ref[2 * pr + hd]).astype(bf16)
                    intra = jnp.dot(inner, v_ref[rows[ci], cols], preferred_element_type=f32)
                    cross = jnp.dot(qa[ci, pr][hrows, :], r_b[ci, pr][:, hd * HEAD_DIM:(hd + 1) * HEAD_DIM],
                                    preferred_element_type=f32)
                    o[ci, pr, hd] = intra + cross * xi_ref[2 * pr + hd]
        for ci in chunks:
            for pr in pairs:
                for hd in range(2):
                    cols = slice((2 * pr + hd) * HEAD_DIM, (2 * pr + hd + 1) * HEAD_DIM)
                    oh = o[ci, pr, hd]
                    ms = jnp.mean(oh * oh, axis=-1, keepdims=True)
                    on = oh * lax.rsqrt(ms + EPS)
                    zg = _silu(z_ref[rows[ci], cols].astype(f32))
                    o_ref[rows[ci], cols] = (on * zg).astype(o_ref.dtype)
        return carry

    lax.fori_loop(0, SEQ // (c_len * RET_CHUNKS_PER_ITER), body, 0)


def _retention_tables():
    h, c_len = N_MAIN_HEADS, RET_CHUNK
    lg = np.log1p(-np.exp2(-5.0 - np.arange(h, dtype=np.float64)))
    n = np.arange(c_len, dtype=np.float64)
    diff = n[:, None] - n[None, :]
    d_inner = np.where(diff[None] >= 0, np.exp(lg[:, None, None] * np.maximum(diff, 0.0)[None]), 0.0)
    xi = np.exp(lg[:, None] * (n[None, :] + 1.0))
    zeta = np.exp(lg[:, None] * (c_len - 1.0 - n[None, :]))
    g_chunk = np.exp(lg * c_len)
    xi_b = np.broadcast_to(xi[:, :, None], (h, c_len, HEAD_DIM))
    zeta2 = np.repeat(zeta.reshape(h // 2, 2, c_len).transpose(0, 2, 1), RET_KEY_DIM, axis=-1)
    g2 = np.broadcast_to(np.repeat(g_chunk.reshape(h // 2, 2), RET_KEY_DIM, axis=-1)[:, :, None],
                         (h // 2, 2 * RET_KEY_DIM, 2 * HEAD_DIM))
    return tuple(jnp.asarray(np.ascontiguousarray(t), dtype=f32) for t in (d_inner, xi_b, zeta2, g2))


def _retention(qk, v, z):
    n_pairs = N_MAIN_HEADS // 2
    pp = RET_PAIRS_PER_STEP
    steps = n_pairs // pp
    d_inner, xi_b, zeta2, g2 = _retention_tables()
    c_len = RET_CHUNK
    return pl.pallas_call(
        _retention_kernel,
        grid=(BATCH, steps),
        in_specs=[
            pl.BlockSpec((SEQ, pp * LANES), lambda b, p: (b, p)),
            pl.BlockSpec((SEQ, pp * LANES), lambda b, p: (b, steps + p)),
            pl.BlockSpec((SEQ, pp * 2 * HEAD_DIM), lambda b, p: (b, p)),
            pl.BlockSpec((SEQ, pp * 2 * HEAD_DIM), lambda b, p: (b, p)),
            pl.BlockSpec((2 * pp, c_len, c_len), lambda b, p: (p, 0, 0)),
            pl.BlockSpec((2 * pp, c_len, HEAD_DIM), lambda b, p: (p, 0, 0)),
            pl.BlockSpec((pp, c_len, LANES), lambda b, p: (p, 0, 0)),
            pl.BlockSpec((pp, LANES, 2 * HEAD_DIM), lambda b, p: (p, 0, 0)),
        ],
        out_specs=pl.BlockSpec((SEQ, pp * 2 * HEAD_DIM), lambda b, p: (b, p)),
        out_shape=jax.ShapeDtypeStruct((BATCH * SEQ, MAIN_W), bf16),
        scratch_shapes=[pltpu.VMEM((pp, LANES, 2 * HEAD_DIM), f32)],
        compiler_params=pltpu.CompilerParams(dimension_semantics=("arbitrary", "arbitrary"),
                                             vmem_limit_bytes=VMEM_LIMIT),
        name="retention",
    )(qk, qk, v, z, d_inner, xi_b, zeta2, g2)


def _out_proj_kernel(om_ref, qm_ref, zm_ref, kv_ref, w_ref, x_ref, g_ref, o_ref, *, final_norm):
    scale = 1.0 / math.sqrt(HEAD_DIM)
    heads = range(N_MEM_HEADS)
    col = [slice(hd * HEAD_DIM, (hd + 1) * HEAD_DIM) for hd in heads]
    half = D_MODEL // 2
    s = [lax.dot_general(qm_ref[:, col[hd]], kv_ref[:, col[hd]], (((1,), (1,)), ((), ())),
                         preferred_element_type=f32) * scale for hd in heads]
    y_lo = jnp.dot(om_ref[...], w_ref[:MAIN_W, :half], preferred_element_type=f32)
    p = []
    for hd in heads:
        e = jnp.exp(s[hd] - jnp.max(s[hd], axis=-1, keepdims=True))
        p.append((e * (1.0 / jnp.sum(e, axis=-1, keepdims=True))).astype(bf16))
    memo = [jnp.dot(p[hd], kv_ref[:, MEM_W + hd * HEAD_DIM:MEM_W + (hd + 1) * HEAD_DIM],
                    preferred_element_type=f32) for hd in heads]
    y_hi = jnp.dot(om_ref[...], w_ref[:MAIN_W, half:], preferred_element_type=f32)
    og = jnp.concatenate([(memo[hd] * _silu(zm_ref[:, col[hd]].astype(f32))).astype(bf16) for hd in heads],
                         axis=1)
    y = jnp.concatenate([y_lo, y_hi], axis=1) + jnp.dot(og, w_ref[MAIN_W:, :], preferred_element_type=f32)
    xn = x_ref[...] + y
    if final_norm:
        xn = _rmsnorm_rows(xn, g_ref[...])
    o_ref[...] = xn


def _out_proj(o_main, qm, z, kv, w_out, x, g_final, *, final_norm):
    t_rows = x.shape[0]
    tiles_per_seq = SEQ // OUT_TM
    z_blk = MAIN_W // MEM_W
    return pl.pallas_call(
        functools.partial(_out_proj_kernel, final_norm=final_norm),
        grid=(t_rows // OUT_TM,),
        in_specs=[
            pl.BlockSpec((OUT_TM, MAIN_W), lambda i: (i, 0)),
            pl.BlockSpec((OUT_TM, MEM_W), lambda i: (i, 0)),
            pl.BlockSpec((OUT_TM, MEM_W), lambda i: (i, z_blk)),
            pl.BlockSpec((N_MEM, 2 * MEM_W), lambda i: (i // tiles_per_seq, 0)),
            pl.BlockSpec((INNER, D_MODEL), lambda i: (0, 0), pipeline_mode=pl.Buffered(1)),
            pl.BlockSpec((OUT_TM, D_MODEL), lambda i: (i, 0)),
            pl.BlockSpec((1, D_MODEL), lambda i: (0, 0)),
        ],
        out_specs=pl.BlockSpec((OUT_TM, D_MODEL), lambda i: (i, 0)),
        out_shape=jax.ShapeDtypeStruct((t_rows, D_MODEL), f32),
        compiler_params=pltpu.CompilerParams(dimension_semantics=("arbitrary",),
                                             vmem_limit_bytes=VMEM_LIMIT),
        name="out_proj_final" if final_norm else "out_proj",
    )(o_main, qm, z, kv, w_out, x, g_final.reshape(1, D_MODEL))


def _rotary_tables():
    half = RET_KEY_DIM // 2
    pos = np.arange(SEQ, dtype=np.float64)
    inv = 1.0 / (ROPE_BASE ** (np.arange(half, dtype=np.float64) / half))
    ang = pos[:, None] * inv[None, :]
    cos, sin = np.cos(ang), np.sin(ang)
    reps = LANES // RET_KEY_DIM
    cos_t = np.tile(np.concatenate([cos, cos], axis=-1), (1, reps))
    sin_t = np.tile(np.concatenate([-sin, sin], axis=-1), (1, reps))
    k_scale = RET_KEY_DIM ** -0.5
    return tuple(jnp.asarray(t, dtype=f32) for t in (cos_t, sin_t, cos_t * k_scale, sin_t * k_scale))


def kernel(x, mem, norm_g, fox_w_in, fox_b_f, ret_w_in, mem_norm_g, w_mem_kv, w_out, final_norm_g):
    t_rows = BATCH * SEQ
    x2 = x.reshape(t_rows, D_MODEL)

    kv0, kv1 = _norm_proj(mem.reshape(BATCH * N_MEM, D_MODEL), mem_norm_g,
                          [w_mem_kv[0].astype(bf16), w_mem_kv[1].astype(bf16)],
                          [2 * MEM_W, 2 * MEM_W], [bf16, bf16], name="mem_kv_proj")

    o_f = 3 * MAIN_W
    o_qm = o_f + N_MAIN_HEADS
    w0t = jnp.swapaxes(fox_w_in[0], 0, 1).astype(bf16)
    w_qmz = w0t[o_qm:]
    w_f = jnp.pad(w0t[o_f:o_qm], ((0, F_PAD - N_MAIN_HEADS), (0, 0)))
    q, k, vt, qm, z, f = _norm_proj(
        x2, norm_g[0], [(w0t, o_f), w_qmz, w_f], [MAIN_W, MAIN_W, MAIN_W, MEM_W, INNER, F_PAD],
        [bf16, bf16, bf16, bf16, bf16, f32], ["fox_q", "plain", "kt", "plain", "plain", "plain"],
        w_transposed=True, name="fox_in_proj")
    nb = _fox_gate(f, fox_b_f[0])
    o_main = _fox_attention(q, k, vt, nb, z)
    x2 = _out_proj(o_main, qm, z, kv0, w_out[0].astype(bf16), x2, final_norm_g, final_norm=False)

    w1 = ret_w_in[0].astype(bf16)
    qk, v, qm, z = _norm_proj(
        x2, norm_g[1], [w1], [2 * RET_QK_W, MAIN_W, MEM_W, INNER], [bf16] * 4,
        ["rot", "plain", "plain", "plain"], tables=_rotary_tables(), name="ret_in_proj")
    o_main = _retention(qk, v, z)
    out = _out_proj(o_main, qm, z, kv1, w_out[1].astype(bf16), x2, final_norm_g, final_norm=True)
    return out.reshape(BATCH, SEQ, D_MODEL)
```

```python
import functools
import math

import jax
import jax.numpy as jnp
import numpy as np
from jax import lax
from jax.experimental import pallas as pl
from jax.experimental.pallas import tpu as pltpu

D_MODEL = 1024
BATCH = 8
SEQ = 2048
HEAD_DIM = 128
N_MAIN_HEADS = 12
N_MEM_HEADS = 4
N_MEM = 256
MAIN_W = N_MAIN_HEADS * HEAD_DIM
MEM_W = N_MEM_HEADS * HEAD_DIM
INNER = MAIN_W + MEM_W
RET_KEY_DIM = HEAD_DIM // 2
RET_QK_W = N_MAIN_HEADS * RET_KEY_DIM
RET_CHUNK = 128
ROPE_BASE = 10000.0
EPS = 1e-6
NEG = -1e30
LOG2E = 1.4426950408889634

LANES = 128
F_PAD = LANES
VMEM_LIMIT = 56 * 1024 * 1024

PROJ_TM = 512
PROJ_TN = 512
FOX_T = 256
FOX_HEADS_PER_STEP = 6
BIAS_TERMS = 3
FOX_SUM_ROWS = 16
FOX_Q_SCALE = LOG2E / math.sqrt(HEAD_DIM)
FOX_BOUND_MARGIN = 1.02
FOX_MIN_DENOM = 2.0 ** -90
GATE_HEAD_ROWS = 16
RET_PAIRS_PER_STEP = 3
RET_CHUNKS_PER_ITER = 2
OUT_TM = 512

f32 = jnp.float32
bf16 = jnp.bfloat16


def _silu(z):
    h = 0.5 * z
    return h + h * jnp.tanh(h)


def _rmsnorm_rows(x, g):
    ms = jnp.mean(x * x, axis=-1, keepdims=True)
    return (x * lax.rsqrt(ms + EPS)) * g


def _split_bf16(v):
    hi = v.astype(bf16)
    r1 = v - hi.astype(f32)
    mid = r1.astype(bf16)
    lo = (r1 - mid.astype(f32)).astype(bf16)
    return hi, mid, lo


def _norm_proj_kernel(*refs, n_weights, w_transposed, out_widths, out_kinds):
    track_norms = "fox_q" in out_kinds
    n_out = len(out_widths) + (1 if track_norms else 0)
    x_ref, g_ref = refs[:2]
    w_refs = refs[2:2 + n_weights]
    tab_refs = refs[2 + n_weights:len(refs) - n_out]
    out_refs = refs[len(refs) - n_out:]
    h = _rmsnorm_rows(x_ref[...], g_ref[...]).astype(bf16)
    tm = h.shape[0]
    if "rot" in out_kinds:
        lane = lax.broadcasted_iota(jnp.int32, (tm, LANES), 1)
        first_half = (lane % RET_KEY_DIM) < (RET_KEY_DIM // 2)
    if track_norms:
        norm_sub = lax.broadcasted_iota(jnp.int32, (8, LANES), 0)
        norm_lane = lax.broadcasted_iota(jnp.int32, (8, LANES), 1)
        norms = jnp.zeros((8, LANES), f32)

        def with_norms(norms, y, row, head0):
            for s0 in range(0, y.shape[1], HEAD_DIM):
                ys = y[:, s0:s0 + HEAD_DIM]
                worst = jnp.max(jnp.sum(ys * ys, axis=1, keepdims=True), axis=0, keepdims=True)
                norms = jnp.where((norm_sub == row) & (norm_lane == head0 + s0 // HEAD_DIM), worst, norms)
            return norms
    out_axis = 0 if w_transposed else 1
    wi, col = 0, 0
    for o_ref, width, kind in zip(out_refs, out_widths, out_kinds):
        if col == w_refs[wi].shape[out_axis]:
            wi, col = wi + 1, 0
        w_ref = w_refs[wi]
        for c0 in range(0, width, PROJ_TN):
            cw = min(PROJ_TN, width - c0)
            if w_transposed:
                y = lax.dot_general(h, w_ref[col + c0:col + c0 + cw, :], (((1,), (1,)), ((), ())),
                                    preferred_element_type=f32)
            else:
                y = jnp.dot(h, w_ref[:, col + c0:col + c0 + cw], preferred_element_type=f32)
            if kind == "rot":
                for s0 in range(0, cw, LANES):
                    ys = y[:, s0:s0 + LANES]
                    sw = jnp.where(first_half, pltpu.roll(ys, LANES - RET_KEY_DIM // 2, 1),
                                   pltpu.roll(ys, RET_KEY_DIM // 2, 1))
                    cos_ref, sin_ref = tab_refs[:2] if c0 + s0 < RET_QK_W else tab_refs[2:]
                    r = ys * cos_ref[...] + sw * sin_ref[...]
                    o_ref[:, c0 + s0:c0 + s0 + LANES] = r.astype(o_ref.dtype)
            elif kind == "kt":
                for r0 in range(tm // FOX_T):
                    for s0 in range(0, cw, LANES):
                        piece = y[r0 * FOX_T:(r0 + 1) * FOX_T, s0:s0 + LANES]
                        o_ref[r0, c0 + s0:c0 + s0 + LANES, :] = piece.T.astype(o_ref.dtype)
            elif kind == "fox_q":
                y = y * FOX_Q_SCALE
                norms = with_norms(norms, y, 0, c0 // HEAD_DIM)
                o_ref[:, c0:c0 + cw] = y.astype(o_ref.dtype)
            elif kind == "fox_k":
                norms = with_norms(norms, y, 1, c0 // HEAD_DIM)
                o_ref[:, c0:c0 + cw] = y.astype(o_ref.dtype)
            else:
                o_ref[:, c0:c0 + cw] = y.astype(o_ref.dtype)
        col += width
    if track_norms:
        out_refs[-1][...] = norms


def _norm_proj(x, g, weights, out_widths, out_dtypes, out_kinds=None, *, w_transposed=False, tables=None,
               name):
    t_rows, d = x.shape
    out_kinds = tuple(out_kinds or ["plain"] * len(out_widths))
    w_arrays = [w[0] if isinstance(w, tuple) else w for w in weights]
    w_blocks = [(w[1], d) if isinstance(w, tuple) else w.shape for w in weights]
    out_axis = 0 if w_transposed else 1
    assert sum(out_widths) == sum(blk[out_axis] for blk in w_blocks) and t_rows % PROJ_TM == 0
    tiles_per_seq = SEQ // PROJ_TM
    in_specs = [
        pl.BlockSpec((PROJ_TM, d), lambda i: (i, 0)),
        pl.BlockSpec((1, d), lambda i: (0, 0)),
    ]
    in_specs += [pl.BlockSpec(blk, lambda i: (0, 0), pipeline_mode=pl.Buffered(1)) for blk in w_blocks]
    args = [x, g.reshape(1, d), *w_arrays]
    if "rot" in out_kinds:
        for tab in tables:
            in_specs.append(pl.BlockSpec((PROJ_TM, LANES), lambda i: (i % tiles_per_seq, 0)))
            args.append(tab)
    out_specs, out_shape = [], []
    for wd, dt, kind in zip(out_widths, out_dtypes, out_kinds):
        if kind == "kt":
            kt_per_tile = PROJ_TM // FOX_T
            out_specs.append(pl.BlockSpec((None, kt_per_tile, wd, FOX_T),
                                          lambda i: (i // tiles_per_seq, i % tiles_per_seq, 0, 0)))
            out_shape.append(jax.ShapeDtypeStruct((t_rows // SEQ, SEQ // FOX_T, wd, FOX_T), dt))
        else:
            out_specs.append(pl.BlockSpec((PROJ_TM, wd), lambda i: (i, 0)))
            out_shape.append(jax.ShapeDtypeStruct((t_rows, wd), dt))
    if "fox_q" in out_kinds:
        out_specs.append(pl.BlockSpec((None, 8, LANES), lambda i: (i, 0, 0)))
        out_shape.append(jax.ShapeDtypeStruct((t_rows // PROJ_TM, 8, LANES), f32))
    return pl.pallas_call(
        functools.partial(_norm_proj_kernel, n_weights=len(weights), w_transposed=w_transposed,
                          out_widths=tuple(out_widths), out_kinds=out_kinds),
        grid=(t_rows // PROJ_TM,),
        in_specs=in_specs,
        out_specs=out_specs,
        out_shape=out_shape,
        compiler_params=pltpu.CompilerParams(dimension_semantics=("arbitrary",),
                                             vmem_limit_bytes=VMEM_LIMIT),
        name=name,
    )(*args)


def _fox_gate_kernel(f_ref, bf_ref, o_ref, ot_ref):
    blk = LANES
    row = lax.broadcasted_iota(jnp.int32, (blk, blk), 0)
    coli = lax.broadcasted_iota(jnp.int32, (blk, blk), 1)
    tri = jnp.where(row >= coli, 1.0, 0.0).astype(bf16)
    spread = [jnp.where((coli == BIAS_TERMS * row + t) & (row < N_MAIN_HEADS), 1.0, 0.0).astype(bf16)
              for t in range(BIAS_TERMS)]
    carry = jnp.zeros((1, F_PAD), f32)
    for b0 in range(0, SEQ, blk):
        xg = f_ref[b0:b0 + blk, :] + bf_ref[...]
        ls = jnp.minimum(xg, 0.0) - jnp.log1p(jnp.exp(-jnp.abs(xg)))
        cs = sum(jnp.dot(tri, term, preferred_element_type=f32) for term in _split_bf16(ls)) + carry
        carry = cs[blk - 1:blk, :]
        bias = cs * (-LOG2E)
        nb = sum(jnp.dot(term, sp, preferred_element_type=f32)
                 for term, sp in zip(_split_bf16(bias), spread))
        o_ref[b0:b0 + blk, :] = nb.astype(o_ref.dtype)
        ot_ref[:, b0:b0 + blk] = bias.T[:ot_ref.shape[0], :]


def _fox_gate(f, b_f):
    bf_pad = jnp.zeros((1, F_PAD), f32).at[0, :N_MAIN_HEADS].set(b_f.astype(f32))
    return pl.pallas_call(
        _fox_gate_kernel,
        grid=(BATCH,),
        in_specs=[pl.BlockSpec((SEQ, F_PAD), lambda b: (b, 0)),
                  pl.BlockSpec((1, F_PAD), lambda b: (0, 0))],
        out_specs=[pl.BlockSpec((SEQ, LANES), lambda b: (b, 0)),
                   pl.BlockSpec((None, GATE_HEAD_ROWS, SEQ), lambda b: (b, 0, 0))],
        out_shape=[jax.ShapeDtypeStruct((BATCH * SEQ, LANES), bf16),
                   jax.ShapeDtypeStruct((BATCH, GATE_HEAD_ROWS, SEQ), f32)],
        compiler_params=pltpu.CompilerParams(dimension_semantics=("arbitrary",)),
        name="fox_gate",
    )(f, bf_pad)


def _fox_attn_kernel(q_ref, k_ref, vt_ref, nb_ref, z_ref, nbq_ref, norm_ref, o_ref,
                     m_ref, acc_ref, u_ref, p_ref, a_ref):
    t = FOX_T
    nt = SEQ // t
    head0 = pl.program_id(1) * FOX_HEADS_PER_STEP
    heads = range(FOX_HEADS_PER_STEP)
    col = [slice(hd * HEAD_DIM, (hd + 1) * HEAD_DIM) for hd in heads]
    lane = lax.broadcasted_iota(jnp.int32, (t, LANES), 1)
    sel = []
    for hd in heads:
        lo = BIAS_TERMS * (head0 + hd)
        sel.append(jnp.where((lane >= lo) & (lane < lo + BIAS_TERMS), 1.0, 0.0).astype(bf16))
    ones_rows = jnp.ones((FOX_SUM_ROWS, t), bf16)

    def tile_rows(i):
        return pl.ds(pl.multiple_of(i * t, t), t)

    def score(qt, kt, hd):
        k_aug = jnp.concatenate([k_ref[tile_rows(kt), col[hd]], nb_ref[tile_rows(kt), :]], axis=1)
        q_aug = jnp.concatenate([q_ref[tile_rows(qt), col[hd]], sel[hd]], axis=1)
        return lax.dot_general(k_aug, q_aug, (((1,), (1,)), ((), ())),
                               preferred_element_type=f32)

    def scores(qt, kt):
        return [score(qt, kt, hd) for hd in heads]

    def pv1(kt, p_hd, hd):
        return jnp.dot(jnp.concatenate([vt_ref[kt, col[hd], :], ones_rows], axis=0), p_hd,
                       preferred_element_type=f32)

    def pv(kt, p):
        return [pv1(kt, p[hd], hd) for hd in heads]

    def finalize(qt):
        for hd in heads:
            acc = acc_ref[qt, hd]
            out_t = acc[:HEAD_DIM, :] * (1.0 / acc[HEAD_DIM:HEAD_DIM + 1, :])
            gate = _silu(z_ref[tile_rows(qt), col[hd]].astype(f32))
            o_ref[tile_rows(qt), col[hd]] = (out_t.T * gate).astype(o_ref.dtype)

    def next_item(qt, kt):
        wrap = kt + 1 >= qt
        return jnp.minimum(jnp.where(wrap, qt + 1, qt), nt - 1), jnp.where(wrap, 0, kt + 1)

    def load_list(ref):
        return [ref[hd] for hd in heads]

    def store_list(ref, vals):
        for hd in heads:
            ref[hd] = vals[hd]

    def run(online):
        def numerators(qt, u, masked):
            if masked:
                causal = (lax.broadcasted_iota(jnp.int32, (t, t), 0)
                          <= lax.broadcasted_iota(jnp.int32, (t, t), 1))
            p, alpha = [], []
            for hd in heads:
                uh = jnp.where(causal, u[hd], NEG) if masked else u[hd]
                m_old = m_ref[qt, hd]
                if online:
                    m_new = jnp.maximum(m_old, jnp.max(uh, axis=0, keepdims=True))
                    alpha.append(jnp.exp2(m_old - m_new))
                    m_ref[qt, hd] = m_new
                else:
                    m_new = m_old
                p.append(jnp.exp2(uh - m_new).astype(bf16))
            return p, alpha

        def accumulate(qt, alpha, pv_vals):
            for hd in heads:
                prev = alpha[hd] * acc_ref[qt, hd] if online else acc_ref[qt, hd]
                acc_ref[qt, hd] = prev + pv_vals[hd]

        acc_ref[...] = jnp.zeros(acc_ref.shape, f32)
        p_ref[...] = jnp.zeros(p_ref.shape, bf16)
        if online:
            m_ref[...] = jnp.full(m_ref.shape, NEG, f32)
            a_ref[...] = jnp.ones(a_ref.shape, f32)

        n_items = nt * (nt - 1) // 2
        assert n_items % 2 == 0
        store_list(u_ref, scores(1, 0))

        def pass1(_, carry):
            qa, ka, qpb, kpb = carry
            u_a = load_list(u_ref)
            pv_prev = pv(kpb, load_list(p_ref))
            qb, kb = next_item(qa, ka)
            u_b = scores(qb, kb)
            qn, kn = next_item(qb, kb)
            p_a, al_a = numerators(qa, u_a, False)
            accumulate(qpb, load_list(a_ref) if online else None, pv_prev)
            pv_a = []
            for hd in heads:
                u_ref[hd] = score(qn, kn, hd)
                pv_a.append(pv1(ka, p_a[hd], hd))
            p_b, al_b = numerators(qb, u_b, False)
            accumulate(qa, al_a, pv_a)
            store_list(p_ref, p_b)
            if online:
                store_list(a_ref, al_b)
            return qn, kn, qb, kb

        one, zero = jnp.int32(1), jnp.int32(0)
        _, _, qpb, kpb = lax.fori_loop(0, n_items // 2, pass1, (one, zero, one, zero))
        accumulate(qpb, load_list(a_ref) if online else None, pv(kpb, load_list(p_ref)))

        assert nt % 2 == 0
        store_list(u_ref, scores(0, 0))

        def pass2(i, carry):
            ta = 2 * i
            tb = ta + 1
            u_b = scores(tb, tb)
            p_a, al_a = numerators(ta, load_list(u_ref), True)
            pv_a = pv(ta, p_a)
            tn = jnp.minimum(ta + 2, nt - 1)
            u_n = scores(tn, tn)
            p_b, al_b = numerators(tb, u_b, True)
            accumulate(ta, al_a, pv_a)
            pv_b = pv(tb, p_b)
            finalize(ta)
            store_list(u_ref, u_n)
            accumulate(tb, al_b, pv_b)
            finalize(tb)
            return carry

        lax.fori_loop(0, nt // 2, pass2, 0)

    norms = jnp.max(norm_ref[...], axis=0)
    hlane = lax.broadcasted_iota(jnp.int32, (1, LANES), 1)
    for hd in heads:
        pick = hlane == head0 + hd
        q2 = jnp.max(jnp.where(pick, norms[0:1, :], 0.0), axis=1, keepdims=True)
        k2 = jnp.max(jnp.where(pick, norms[1:2, :], 0.0), axis=1, keepdims=True)
        qk_bound = jnp.sqrt(q2 * k2) * FOX_BOUND_MARGIN + 1.0
        for qt in range(nt):
            m_ref[qt, hd] = nbq_ref[hd, qt:qt + 1, :] + qk_bound

    run(online=False)

    denom_min = jnp.min(acc_ref[:, :, HEAD_DIM:HEAD_DIM + 1, :])

    @pl.when(jnp.logical_not(denom_min >= FOX_MIN_DENOM))
    def _():
        run(online=True)


def _fox_attention(q, k, vt, nb, z, nbq, norms):
    hw = FOX_HEADS_PER_STEP * HEAD_DIM
    return pl.pallas_call(
        _fox_attn_kernel,
        grid=(BATCH, N_MAIN_HEADS // FOX_HEADS_PER_STEP),
        in_specs=[
            pl.BlockSpec((SEQ, hw), lambda b, p: (b, p)),
            pl.BlockSpec((SEQ, hw), lambda b, p: (b, p)),
            pl.BlockSpec((None, SEQ // FOX_T, hw, FOX_T), lambda b, p: (b, 0, p, 0)),
            pl.BlockSpec((SEQ, LANES), lambda b, p: (b, 0)),
            pl.BlockSpec((SEQ, hw), lambda b, p: (b, p)),
            pl.BlockSpec((None, FOX_HEADS_PER_STEP, SEQ // FOX_T, FOX_T), lambda b, p: (b, p, 0, 0)),
            pl.BlockSpec((SEQ // PROJ_TM, 8, LANES), lambda b, p: (b, 0, 0)),
        ],
        out_specs=pl.BlockSpec((SEQ, hw), lambda b, p: (b, p)),
        out_shape=jax.ShapeDtypeStruct((BATCH * SEQ, MAIN_W), bf16),
        scratch_shapes=[
            pltpu.VMEM((SEQ // FOX_T, FOX_HEADS_PER_STEP, 1, FOX_T), f32),
            pltpu.VMEM((SEQ // FOX_T, FOX_HEADS_PER_STEP, HEAD_DIM + FOX_SUM_ROWS, FOX_T), f32),
            pltpu.VMEM((FOX_HEADS_PER_STEP, FOX_T, FOX_T), f32),
            pltpu.VMEM((FOX_HEADS_PER_STEP, FOX_T, FOX_T), bf16),
            pltpu.VMEM((FOX_HEADS_PER_STEP, 1, FOX_T), f32),
        ],
        compiler_params=pltpu.CompilerParams(dimension_semantics=("arbitrary", "arbitrary"),
                                             vmem_limit_bytes=VMEM_LIMIT),
        name="fox_attention",
    )(q, k, vt, nb, z, nbq, norms)


def _retention_kernel(q_ref, k_ref, v_ref, z_ref, d_ref, xi_ref, zeta_ref, g_ref, o_ref, r_ref):
    c_len = RET_CHUNK
    r_ref[...] = jnp.zeros(r_ref.shape, f32)
    lane = lax.broadcasted_iota(jnp.int32, (c_len, LANES), 1)
    head_mask = (lane < RET_KEY_DIM, lane >= RET_KEY_DIM)
    pairs = range(RET_PAIRS_PER_STEP)
    chunks = range(RET_CHUNKS_PER_ITER)

    def body(it, carry):
        rows = [pl.ds(pl.multiple_of((it * RET_CHUNKS_PER_ITER + ci) * c_len, c_len), c_len) for ci in chunks]
        qa, s, upd = {}, {}, {}
        for ci in chunks:
            for pr in pairs:
                q2 = q_ref[rows[ci], pr * LANES:(pr + 1) * LANES]
                k2 = k_ref[rows[ci], pr * LANES:(pr + 1) * LANES]
                v2 = v_ref[rows[ci], pr * 2 * HEAD_DIM:(pr + 1) * 2 * HEAD_DIM]
                qa[ci, pr] = jnp.concatenate([jnp.where(head_mask[hd], q2, jnp.zeros_like(q2))
                                              for hd in range(2)], axis=0)
                s[ci, pr] = lax.dot_general(qa[ci, pr], k2, (((1,), (1,)), ((), ())),
                                            preferred_element_type=f32)
                kz = (k2.astype(f32) * zeta_ref[pr]).astype(bf16)
                upd[ci, pr] = lax.dot_general(kz, v2, (((0,), (0,)), ((), ())), preferred_element_type=f32)
        r_b = {}
        for pr in pairs:
            r_state = r_ref[pr]
            for ci in chunks:
                r_b[ci, pr] = r_state.astype(bf16)
                r_state = r_state * g_ref[pr] + upd[ci, pr]
            r_ref[pr] = r_state
        o = {}
        for ci in chunks:
            for pr in pairs:
                for hd in range(2):
                    hrows = slice(hd * c_len, (hd + 1) * c_len)
                    cols = slice((2 * pr + hd) * HEAD_DIM, (2 * pr + hd + 1) * HEAD_DIM)
                    inner = (s[ci, pr][hrows, :] * d_ref[2 * pr + hd]).astype(bf16)
                    intra = jnp.dot(inner, v_ref[rows[ci], cols], preferred_element_type=f32)
                    cross = jnp.dot(qa[ci, pr][hrows, :], r_b[ci, pr][:, hd * HEAD_DIM:(hd + 1) * HEAD_DIM],
                                    preferred_element_type=f32)
                    o[ci, pr, hd] = intra + cross * xi_ref[2 * pr + hd]
        for ci in chunks:
            for pr in pairs:
                for hd in range(2):
                    cols = slice((2 * pr + hd) * HEAD_DIM, (2 * pr + hd + 1) * HEAD_DIM)
                    oh = o[ci, pr, hd]
                    ms = jnp.mean(oh * oh, axis=-1, keepdims=True)
                    on = oh * lax.rsqrt(ms + EPS)
                    zg = _silu(z_ref[rows[ci], cols].astype(f32))
                    o_ref[rows[ci], cols] = (on * zg).astype(o_ref.dtype)
        return carry

    lax.fori_loop(0, SEQ // (c_len * RET_CHUNKS_PER_ITER), body, 0)


def _retention_tables():
    h, c_len = N_MAIN_HEADS, RET_CHUNK
    lg = np.log1p(-np.exp2(-5.0 - np.arange(h, dtype=np.float64)))
    n = np.arange(c_len, dtype=np.float64)
    diff = n[:, None] - n[None, :]
    d_inner = np.where(diff[None] >= 0, np.exp(lg[:, None, None] * np.maximum(diff, 0.0)[None]), 0.0)
    xi = np.exp(lg[:, None] * (n[None, :] + 1.0))
    zeta = np.exp(lg[:, None] * (c_len - 1.0 - n[None, :]))
    g_chunk = np.exp(lg * c_len)
    xi_b = np.broadcast_to(xi[:, :, None], (h, c_len, HEAD_DIM))
    zeta2 = np.repeat(zeta.reshape(h // 2, 2, c_len).transpose(0, 2, 1), RET_KEY_DIM, axis=-1)
    g2 = np.broadcast_to(np.repeat(g_chunk.reshape(h // 2, 2), RET_KEY_DIM, axis=-1)[:, :, None],
                         (h // 2, 2 * RET_KEY_DIM, 2 * HEAD_DIM))
    return tuple(jnp.asarray(np.ascontiguousarray(t), dtype=f32) for t in (d_inner, xi_b, zeta2, g2))


def _retention(qk, v, z):
    n_pairs = N_MAIN_HEADS // 2
    pp = RET_PAIRS_PER_STEP
    steps = n_pairs // pp
    d_inner, xi_b, zeta2, g2 = _retention_tables()
    c_len = RET_CHUNK
    return pl.pallas_call(
        _retention_kernel,
        grid=(BATCH, steps),
        in_specs=[
            pl.BlockSpec((SEQ, pp * LANES), lambda b, p: (b, p)),
            pl.BlockSpec((SEQ, pp * LANES), lambda b, p: (b, steps + p)),
            pl.BlockSpec((SEQ, pp * 2 * HEAD_DIM), lambda b, p: (b, p)),
            pl.BlockSpec((SEQ, pp * 2 * HEAD_DIM), lambda b, p: (b, p)),
            pl.BlockSpec((2 * pp, c_len, c_len), lambda b, p: (p, 0, 0)),
            pl.BlockSpec((2 * pp, c_len, HEAD_DIM), lambda b, p: (p, 0, 0)),
            pl.BlockSpec((pp, c_len, LANES), lambda b, p: (p, 0, 0)),
            pl.BlockSpec((pp, LANES, 2 * HEAD_DIM), lambda b, p: (p, 0, 0)),
        ],
        out_specs=pl.BlockSpec((SEQ, pp * 2 * HEAD_DIM), lambda b, p: (b, p)),
        out_shape=jax.ShapeDtypeStruct((BATCH * SEQ, MAIN_W), bf16),
        scratch_shapes=[pltpu.VMEM((pp, LANES, 2 * HEAD_DIM), f32)],
        compiler_params=pltpu.CompilerParams(dimension_semantics=("arbitrary", "arbitrary"),
                                             vmem_limit_bytes=VMEM_LIMIT),
        name="retention",
    )(qk, qk, v, z, d_inner, xi_b, zeta2, g2)


def _out_proj_kernel(om_ref, qm_ref, zm_ref, kv_ref, w_ref, x_ref, g_ref, o_ref, *, final_norm):
    scale = 1.0 / math.sqrt(HEAD_DIM)
    heads = range(N_MEM_HEADS)
    col = [slice(hd * HEAD_DIM, (hd + 1) * HEAD_DIM) for hd in heads]
    half = D_MODEL // 2
    s = [lax.dot_general(qm_ref[:, col[hd]], kv_ref[:, col[hd]], (((1,), (1,)), ((), ())),
                         preferred_element_type=f32) * scale for hd in heads]
    y_lo = jnp.dot(om_ref[...], w_ref[:MAIN_W, :half], preferred_element_type=f32)
    p = []
    for hd in heads:
        e = jnp.exp(s[hd] - jnp.max(s[hd], axis=-1, keepdims=True))
        p.append((e * (1.0 / jnp.sum(e, axis=-1, keepdims=True))).astype(bf16))
    memo = [jnp.dot(p[hd], kv_ref[:, MEM_W + hd * HEAD_DIM:MEM_W + (hd + 1) * HEAD_DIM],
                    preferred_element_type=f32) for hd in heads]
    y_hi = jnp.dot(om_ref[...], w_ref[:MAIN_W, half:], preferred_element_type=f32)
    og = jnp.concatenate([(memo[hd] * _silu(zm_ref[:, col[hd]].astype(f32))).astype(bf16) for hd in heads],
                         axis=1)
    y = jnp.concatenate([y_lo, y_hi], axis=1) + jnp.dot(og, w_ref[MAIN_W:, :], preferred_element_type=f32)
    xn = x_ref[...] + y
    if final_norm:
        xn = _rmsnorm_rows(xn, g_ref[...])
    o_ref[...] = xn


def _out_proj(o_main, qm, z, kv, w_out, x, g_final, *, final_norm):
    t_rows = x.shape[0]
    tiles_per_seq = SEQ // OUT_TM
    z_blk = MAIN_W // MEM_W
    return pl.pallas_call(
        functools.partial(_out_proj_kernel, final_norm=final_norm),
        grid=(t_rows // OUT_TM,),
        in_specs=[
            pl.BlockSpec((OUT_TM, MAIN_W), lambda i: (i, 0)),
            pl.BlockSpec((OUT_TM, MEM_W), lambda i: (i, 0)),
            pl.BlockSpec((OUT_TM, MEM_W), lambda i: (i, z_blk)),
            pl.BlockSpec((N_MEM, 2 * MEM_W), lambda i: (i // tiles_per_seq, 0)),
            pl.BlockSpec((INNER, D_MODEL), lambda i: (0, 0), pipeline_mode=pl.Buffered(1)),
            pl.BlockSpec((OUT_TM, D_MODEL), lambda i: (i, 0)),
            pl.BlockSpec((1, D_MODEL), lambda i: (0, 0)),
        ],
        out_specs=pl.BlockSpec((OUT_TM, D_MODEL), lambda i: (i, 0)),
        out_shape=jax.ShapeDtypeStruct((t_rows, D_MODEL), f32),
        compiler_params=pltpu.CompilerParams(dimension_semantics=("arbitrary",),
                                             vmem_limit_bytes=VMEM_LIMIT),
        name="out_proj_final" if final_norm else "out_proj",
    )(o_main, qm, z, kv, w_out, x, g_final.reshape(1, D_MODEL))


def _rotary_tables():
    half = RET_KEY_DIM // 2
    pos = np.arange(SEQ, dtype=np.float64)
    inv = 1.0 / (ROPE_BASE ** (np.arange(half, dtype=np.float64) / half))
    ang = pos[:, None] * inv[None, :]
    cos, sin = np.cos(ang), np.sin(ang)
    reps = LANES // RET_KEY_DIM
    cos_t = np.tile(np.concatenate([cos, cos], axis=-1), (1, reps))
    sin_t = np.tile(np.concatenate([-sin, sin], axis=-1), (1, reps))
    k_scale = RET_KEY_DIM ** -0.5
    return tuple(jnp.asarray(t, dtype=f32) for t in (cos_t, sin_t, cos_t * k_scale, sin_t * k_scale))


def kernel(x, mem, norm_g, fox_w_in, fox_b_f, ret_w_in, mem_norm_g, w_mem_kv, w_out, final_norm_g):
    t_rows = BATCH * SEQ
    x2 = x.reshape(t_rows, D_MODEL)

    kv0, kv1 = _norm_proj(mem.reshape(BATCH * N_MEM, D_MODEL), mem_norm_g,
                          [w_mem_kv[0].astype(bf16), w_mem_kv[1].astype(bf16)],
                          [2 * MEM_W, 2 * MEM_W], [bf16, bf16], name="mem_kv_proj")

    o_f = 3 * MAIN_W
    o_qm = o_f + N_MAIN_HEADS
    w0t = jnp.swapaxes(fox_w_in[0], 0, 1).astype(bf16)
    w_qmz = w0t[o_qm:]
    w_f = jnp.pad(w0t[o_f:o_qm], ((0, F_PAD - N_MAIN_HEADS), (0, 0)))
    q, k, vt, qm, z, f, norms = _norm_proj(
        x2, norm_g[0], [(w0t, o_f), w_qmz, w_f], [MAIN_W, MAIN_W, MAIN_W, MEM_W, INNER, F_PAD],
        [bf16, bf16, bf16, bf16, bf16, f32], ["fox_q", "fox_k", "kt", "plain", "plain", "plain"],
        w_transposed=True, name="fox_in_proj")
    nb, nbq = _fox_gate(f, fox_b_f[0])
    nbq = nbq.reshape(BATCH, GATE_HEAD_ROWS, SEQ // FOX_T, FOX_T)
    o_main = _fox_attention(q, k, vt, nb, z, nbq, norms)
    x2 = _out_proj(o_main, qm, z, kv0, w_out[0].astype(bf16), x2, final_norm_g, final_norm=False)

    w1 = ret_w_in[0].astype(bf16)
    qk, v, qm, z = _norm_proj(
        x2, norm_g[1], [w1], [2 * RET_QK_W, MAIN_W, MEM_W, INNER], [bf16] * 4,
        ["rot", "plain", "plain", "plain"], tables=_rotary_tables(), name="ret_in_proj")
    o_main = _retention(qk, v, z)
    out = _out_proj(o_main, qm, z, kv1, w_out[1].astype(bf16), x2, final_norm_g, final_norm=True)
    return out.reshape(BATCH, SEQ, D_MODEL)
```

```python
import functools
import math

import jax
import jax.numpy as jnp
import numpy as np
from jax import lax
from jax.experimental import pallas as pl
from jax.experimental.pallas import tpu as pltpu

D_MODEL = 1024
BATCH = 8
SEQ = 2048
HEAD_DIM = 128
N_MAIN_HEADS = 12
N_MEM_HEADS = 4
N_MEM = 256
MAIN_W = N_MAIN_HEADS * HEAD_DIM
MEM_W = N_MEM_HEADS * HEAD_DIM
INNER = MAIN_W + MEM_W
RET_KEY_DIM = HEAD_DIM // 2
RET_QK_W = N_MAIN_HEADS * RET_KEY_DIM
RET_CHUNK = 128
ROPE_BASE = 10000.0
EPS = 1e-6
NEG = -1e30
LOG2E = 1.4426950408889634

LANES = 128
F_PAD = LANES
VMEM_LIMIT = 56 * 1024 * 1024

PROJ_TM = 512
PROJ_TN = 512
FOX_T = 256
FOX_HEADS_PER_STEP = 6
BIAS_TERMS = 3
FOX_SUM_ROWS = 16
FOX_Q_SCALE = LOG2E / math.sqrt(HEAD_DIM)
FOX_BOUND_MARGIN = 1.02
FOX_MIN_DENOM = 2.0 ** -90
GATE_HEAD_ROWS = 16
RET_PAIRS_PER_STEP = 3
RET_CHUNKS_PER_ITER = 2
OUT_TM = 512

f32 = jnp.float32
bf16 = jnp.bfloat16


def _silu(z):
    h = 0.5 * z
    return h + h * jnp.tanh(h)


def _rmsnorm_rows(x, g):
    ms = jnp.mean(x * x, axis=-1, keepdims=True)
    return (x * lax.rsqrt(ms + EPS)) * g


def _split_bf16(v):
    hi = v.astype(bf16)
    r1 = v - hi.astype(f32)
    mid = r1.astype(bf16)
    lo = (r1 - mid.astype(f32)).astype(bf16)
    return hi, mid, lo


def _norm_proj_kernel(*refs, n_weights, w_transposed, out_widths, out_kinds):
    track_norms = "fox_q" in out_kinds
    n_out = len(out_widths) + (1 if track_norms else 0)
    x_ref, g_ref = refs[:2]
    w_refs = refs[2:2 + n_weights]
    tab_refs = refs[2 + n_weights:len(refs) - n_out]
    out_refs = refs[len(refs) - n_out:]
    h = _rmsnorm_rows(x_ref[...], g_ref[...]).astype(bf16)
    tm = h.shape[0]
    if "rot" in out_kinds:
        lane = lax.broadcasted_iota(jnp.int32, (tm, LANES), 1)
        first_half = (lane % RET_KEY_DIM) < (RET_KEY_DIM // 2)
    if track_norms:
        norm_sub = lax.broadcasted_iota(jnp.int32, (8, LANES), 0)
        norm_lane = lax.broadcasted_iota(jnp.int32, (8, LANES), 1)
        norms = jnp.zeros((8, LANES), f32)

        def with_norms(norms, y, row, head0):
            for s0 in range(0, y.shape[1], HEAD_DIM):
                ys = y[:, s0:s0 + HEAD_DIM]
                worst = jnp.max(jnp.sum(ys * ys, axis=1, keepdims=True), axis=0, keepdims=True)
                norms = jnp.where((norm_sub == row) & (norm_lane == head0 + s0 // HEAD_DIM), worst, norms)
            return norms
    out_axis = 0 if w_transposed else 1
    wi, col = 0, 0
    for o_ref, width, kind in zip(out_refs, out_widths, out_kinds):
        if col == w_refs[wi].shape[out_axis]:
            wi, col = wi + 1, 0
        w_ref = w_refs[wi]
        for c0 in range(0, width, PROJ_TN):
            cw = min(PROJ_TN, width - c0)
            if w_transposed:
                y = lax.dot_general(h, w_ref[col + c0:col + c0 + cw, :], (((1,), (1,)), ((), ())),
                                    preferred_element_type=f32)
            else:
                y = jnp.dot(h, w_ref[:, col + c0:col + c0 + cw], preferred_element_type=f32)
            if kind == "rot":
                for s0 in range(0, cw, LANES):
                    ys = y[:, s0:s0 + LANES]
                    sw = jnp.where(first_half, pltpu.roll(ys, LANES - RET_KEY_DIM // 2, 1),
                                   pltpu.roll(ys, RET_KEY_DIM // 2, 1))
                    cos_ref, sin_ref = tab_refs[:2] if c0 + s0 < RET_QK_W else tab_refs[2:]
                    r = ys * cos_ref[...] + sw * sin_ref[...]
                    o_ref[:, c0 + s0:c0 + s0 + LANES] = r.astype(o_ref.dtype)
            elif kind in ("kt", "fox_q"):
                if kind == "fox_q":
                    y = y * FOX_Q_SCALE
                    norms = with_norms(norms, y, 0, c0 // HEAD_DIM)
                for r0 in range(tm // FOX_T):
                    for s0 in range(0, cw, LANES):
                        piece = y[r0 * FOX_T:(r0 + 1) * FOX_T, s0:s0 + LANES]
                        o_ref[r0, c0 + s0:c0 + s0 + LANES, :] = piece.T.astype(o_ref.dtype)
            elif kind == "fox_k":
                norms = with_norms(norms, y, 1, c0 // HEAD_DIM)
                o_ref[:, c0:c0 + cw] = y.astype(o_ref.dtype)
            else:
                o_ref[:, c0:c0 + cw] = y.astype(o_ref.dtype)
        col += width
    if track_norms:
        out_refs[-1][...] = norms


def _norm_proj(x, g, weights, out_widths, out_dtypes, out_kinds=None, *, w_transposed=False, tables=None,
               name):
    t_rows, d = x.shape
    out_kinds = tuple(out_kinds or ["plain"] * len(out_widths))
    w_arrays = [w[0] if isinstance(w, tuple) else w for w in weights]
    w_blocks = [(w[1], d) if isinstance(w, tuple) else w.shape for w in weights]
    out_axis = 0 if w_transposed else 1
    assert sum(out_widths) == sum(blk[out_axis] for blk in w_blocks) and t_rows % PROJ_TM == 0
    tiles_per_seq = SEQ // PROJ_TM
    in_specs = [
        pl.BlockSpec((PROJ_TM, d), lambda i: (i, 0)),
        pl.BlockSpec((1, d), lambda i: (0, 0)),
    ]
    in_specs += [pl.BlockSpec(blk, lambda i: (0, 0), pipeline_mode=pl.Buffered(1)) for blk in w_blocks]
    args = [x, g.reshape(1, d), *w_arrays]
    if "rot" in out_kinds:
        for tab in tables:
            in_specs.append(pl.BlockSpec((PROJ_TM, LANES), lambda i: (i % tiles_per_seq, 0)))
            args.append(tab)
    out_specs, out_shape = [], []
    for wd, dt, kind in zip(out_widths, out_dtypes, out_kinds):
        if kind in ("kt", "fox_q"):
            kt_per_tile = PROJ_TM // FOX_T
            out_specs.append(pl.BlockSpec((None, kt_per_tile, wd, FOX_T),
                                          lambda i: (i // tiles_per_seq, i % tiles_per_seq, 0, 0)))
            out_shape.append(jax.ShapeDtypeStruct((t_rows // SEQ, SEQ // FOX_T, wd, FOX_T), dt))
        else:
            out_specs.append(pl.BlockSpec((PROJ_TM, wd), lambda i: (i, 0)))
            out_shape.append(jax.ShapeDtypeStruct((t_rows, wd), dt))
    if "fox_q" in out_kinds:
        out_specs.append(pl.BlockSpec((None, 8, LANES), lambda i: (i, 0, 0)))
        out_shape.append(jax.ShapeDtypeStruct((t_rows // PROJ_TM, 8, LANES), f32))
    return pl.pallas_call(
        functools.partial(_norm_proj_kernel, n_weights=len(weights), w_transposed=w_transposed,
                          out_widths=tuple(out_widths), out_kinds=out_kinds),
        grid=(t_rows // PROJ_TM,),
        in_specs=in_specs,
        out_specs=out_specs,
        out_shape=out_shape,
        compiler_params=pltpu.CompilerParams(dimension_semantics=("arbitrary",),
                                             vmem_limit_bytes=VMEM_LIMIT),
        name=name,
    )(*args)


def _fox_gate_kernel(f_ref, bf_ref, o_ref, ot_ref):
    blk = LANES
    row = lax.broadcasted_iota(jnp.int32, (blk, blk), 0)
    coli = lax.broadcasted_iota(jnp.int32, (blk, blk), 1)
    tri = jnp.where(row >= coli, 1.0, 0.0).astype(bf16)
    spread = [jnp.where((coli == BIAS_TERMS * row + t) & (row < N_MAIN_HEADS), 1.0, 0.0).astype(bf16)
              for t in range(BIAS_TERMS)]
    carry = jnp.zeros((1, F_PAD), f32)
    for b0 in range(0, SEQ, blk):
        xg = f_ref[b0:b0 + blk, :] + bf_ref[...]
        ls = jnp.minimum(xg, 0.0) - jnp.log1p(jnp.exp(-jnp.abs(xg)))
        cs = sum(jnp.dot(tri, term, preferred_element_type=f32) for term in _split_bf16(ls)) + carry
        carry = cs[blk - 1:blk, :]
        bias = cs * (-LOG2E)
        nb = sum(jnp.dot(term, sp, preferred_element_type=f32)
                 for term, sp in zip(_split_bf16(bias), spread))
        o_ref[b0:b0 + blk, :] = nb.astype(o_ref.dtype)
        ot_ref[:, b0:b0 + blk] = bias.T[:ot_ref.shape[0], :]


def _fox_gate(f, b_f):
    bf_pad = jnp.zeros((1, F_PAD), f32).at[0, :N_MAIN_HEADS].set(b_f.astype(f32))
    return pl.pallas_call(
        _fox_gate_kernel,
        grid=(BATCH,),
        in_specs=[pl.BlockSpec((SEQ, F_PAD), lambda b: (b, 0)),
                  pl.BlockSpec((1, F_PAD), lambda b: (0, 0))],
        out_specs=[pl.BlockSpec((SEQ, LANES), lambda b: (b, 0)),
                   pl.BlockSpec((None, GATE_HEAD_ROWS, SEQ), lambda b: (b, 0, 0))],
        out_shape=[jax.ShapeDtypeStruct((BATCH * SEQ, LANES), bf16),
                   jax.ShapeDtypeStruct((BATCH, GATE_HEAD_ROWS, SEQ), f32)],
        compiler_params=pltpu.CompilerParams(dimension_semantics=("arbitrary",)),
        name="fox_gate",
    )(f, bf_pad)


def _fox_attn_kernel(qt_ref, k_ref, vt_ref, nb_ref, z_ref, nbq_ref, norm_ref, o_ref,
                     m_ref, acc_ref, u_ref, p_ref, a_ref):
    t = FOX_T
    nt = SEQ // t
    head0 = pl.program_id(1) * FOX_HEADS_PER_STEP
    heads = range(FOX_HEADS_PER_STEP)
    col = [slice(hd * HEAD_DIM, (hd + 1) * HEAD_DIM) for hd in heads]
    sub = lax.broadcasted_iota(jnp.int32, (LANES, t), 0)
    sel = []
    for hd in heads:
        lo = BIAS_TERMS * (head0 + hd)
        sel.append(jnp.where((sub >= lo) & (sub < lo + BIAS_TERMS), 1.0, 0.0).astype(bf16))
    ones_rows = jnp.ones((FOX_SUM_ROWS, t), bf16)

    def tile_rows(i):
        return pl.ds(pl.multiple_of(i * t, t), t)

    def score(qt, kt, hd):
        k_aug = jnp.concatenate([k_ref[tile_rows(kt), col[hd]], nb_ref[tile_rows(kt), :]], axis=1)
        q_aug_t = jnp.concatenate([qt_ref[qt, col[hd], :], sel[hd]], axis=0)
        return jnp.dot(k_aug, q_aug_t, preferred_element_type=f32)

    def scores(qt, kt):
        return [score(qt, kt, hd) for hd in heads]

    def pv1(kt, p_hd, hd):
        return jnp.dot(jnp.concatenate([vt_ref[kt, col[hd], :], ones_rows], axis=0), p_hd,
                       preferred_element_type=f32)

    def pv(kt, p):
        return [pv1(kt, p[hd], hd) for hd in heads]

    def finalize(qt):
        for hd in heads:
            acc = acc_ref[qt, hd]
            out_t = acc[:HEAD_DIM, :] * (1.0 / acc[HEAD_DIM:HEAD_DIM + 1, :])
            gate = _silu(z_ref[tile_rows(qt), col[hd]].astype(f32))
            o_ref[tile_rows(qt), col[hd]] = (out_t.T * gate).astype(o_ref.dtype)

    def next_item(qt, kt):
        wrap = kt + 1 >= qt
        return jnp.minimum(jnp.where(wrap, qt + 1, qt), nt - 1), jnp.where(wrap, 0, kt + 1)

    def load_list(ref):
        return [ref[hd] for hd in heads]

    def store_list(ref, vals):
        for hd in heads:
            ref[hd] = vals[hd]

    def run(online):
        def numerators(qt, u, masked):
            if masked:
                causal = (lax.broadcasted_iota(jnp.int32, (t, t), 0)
                          <= lax.broadcasted_iota(jnp.int32, (t, t), 1))
            p, alpha = [], []
            for hd in heads:
                uh = jnp.where(causal, u[hd], NEG) if masked else u[hd]
                m_old = m_ref[qt, hd]
                if online:
                    m_new = jnp.maximum(m_old, jnp.max(uh, axis=0, keepdims=True))
                    alpha.append(jnp.exp2(m_old - m_new))
                    m_ref[qt, hd] = m_new
                else:
                    m_new = m_old
                p.append(jnp.exp2(uh - m_new).astype(bf16))
            return p, alpha

        def accumulate(qt, alpha, pv_vals):
            for hd in heads:
                prev = alpha[hd] * acc_ref[qt, hd] if online else acc_ref[qt, hd]
                acc_ref[qt, hd] = prev + pv_vals[hd]

        acc_ref[...] = jnp.zeros(acc_ref.shape, f32)
        p_ref[...] = jnp.zeros(p_ref.shape, bf16)
        if online:
            m_ref[...] = jnp.full(m_ref.shape, NEG, f32)
            a_ref[...] = jnp.ones(a_ref.shape, f32)

        n_items = nt * (nt - 1) // 2
        assert n_items % 2 == 0
        store_list(u_ref, scores(1, 0))

        def pass1(_, carry):
            qa, ka, qpb, kpb = carry
            u_a = load_list(u_ref)
            pv_prev = pv(kpb, load_list(p_ref))
            qb, kb = next_item(qa, ka)
            u_b = scores(qb, kb)
            qn, kn = next_item(qb, kb)
            p_a, al_a = numerators(qa, u_a, False)
            accumulate(qpb, load_list(a_ref) if online else None, pv_prev)
            pv_a = []
            for hd in heads:
                u_ref[hd] = score(qn, kn, hd)
                pv_a.append(pv1(ka, p_a[hd], hd))
            p_b, al_b = numerators(qb, u_b, False)
            accumulate(qa, al_a, pv_a)
            store_list(p_ref, p_b)
            if online:
                store_list(a_ref, al_b)
            return qn, kn, qb, kb

        one, zero = jnp.int32(1), jnp.int32(0)
        _, _, qpb, kpb = lax.fori_loop(0, n_items // 2, pass1, (one, zero, one, zero))
        accumulate(qpb, load_list(a_ref) if online else None, pv(kpb, load_list(p_ref)))

        assert nt % 2 == 0
        store_list(u_ref, scores(0, 0))

        def pass2(i, carry):
            ta = 2 * i
            tb = ta + 1
            u_b = scores(tb, tb)
            p_a, al_a = numerators(ta, load_list(u_ref), True)
            pv_a = pv(ta, p_a)
            tn = jnp.minimum(ta + 2, nt - 1)
            u_n = scores(tn, tn)
            p_b, al_b = numerators(tb, u_b, True)
            accumulate(ta, al_a, pv_a)
            pv_b = pv(tb, p_b)
            finalize(ta)
            store_list(u_ref, u_n)
            accumulate(tb, al_b, pv_b)
            finalize(tb)
            return carry

        lax.fori_loop(0, nt // 2, pass2, 0)

    norms = jnp.max(norm_ref[...], axis=0)
    hlane = lax.broadcasted_iota(jnp.int32, (1, LANES), 1)
    for hd in heads:
        pick = hlane == head0 + hd
        q2 = jnp.max(jnp.where(pick, norms[0:1, :], 0.0), axis=1, keepdims=True)
        k2 = jnp.max(jnp.where(pick, norms[1:2, :], 0.0), axis=1, keepdims=True)
        qk_bound = jnp.sqrt(q2 * k2) * FOX_BOUND_MARGIN + 1.0
        for qt in range(nt):
            m_ref[qt, hd] = nbq_ref[hd, qt:qt + 1, :] + qk_bound

    run(online=False)

    denom_min = jnp.min(acc_ref[:, :, HEAD_DIM:HEAD_DIM + 1, :])

    @pl.when(jnp.logical_not(denom_min >= FOX_MIN_DENOM))
    def _():
        run(online=True)


def _fox_attention(q, k, vt, nb, z, nbq, norms):
    hw = FOX_HEADS_PER_STEP * HEAD_DIM
    return pl.pallas_call(
        _fox_attn_kernel,
        grid=(BATCH, N_MAIN_HEADS // FOX_HEADS_PER_STEP),
        in_specs=[
            pl.BlockSpec((None, SEQ // FOX_T, hw, FOX_T), lambda b, p: (b, 0, p, 0)),
            pl.BlockSpec((SEQ, hw), lambda b, p: (b, p)),
            pl.BlockSpec((None, SEQ // FOX_T, hw, FOX_T), lambda b, p: (b, 0, p, 0)),
            pl.BlockSpec((SEQ, LANES), lambda b, p: (b, 0)),
            pl.BlockSpec((SEQ, hw), lambda b, p: (b, p)),
            pl.BlockSpec((None, FOX_HEADS_PER_STEP, SEQ // FOX_T, FOX_T), lambda b, p: (b, p, 0, 0)),
            pl.BlockSpec((SEQ // PROJ_TM, 8, LANES), lambda b, p: (b, 0, 0)),
        ],
        out_specs=pl.BlockSpec((SEQ, hw), lambda b, p: (b, p)),
        out_shape=jax.ShapeDtypeStruct((BATCH * SEQ, MAIN_W), bf16),
        scratch_shapes=[
            pltpu.VMEM((SEQ // FOX_T, FOX_HEADS_PER_STEP, 1, FOX_T), f32),
            pltpu.VMEM((SEQ // FOX_T, FOX_HEADS_PER_STEP, HEAD_DIM + FOX_SUM_ROWS, FOX_T), f32),
            pltpu.VMEM((FOX_HEADS_PER_STEP, FOX_T, FOX_T), f32),
            pltpu.VMEM((FOX_HEADS_PER_STEP, FOX_T, FOX_T), bf16),
            pltpu.VMEM((FOX_HEADS_PER_STEP, 1, FOX_T), f32),
        ],
        compiler_params=pltpu.CompilerParams(dimension_semantics=("arbitrary", "arbitrary"),
                                             vmem_limit_bytes=VMEM_LIMIT),
        name="fox_attention",
    )(q, k, vt, nb, z, nbq, norms)


def _retention_kernel(q_ref, k_ref, v_ref, z_ref, d_ref, xi_ref, zeta_ref, g_ref, o_ref, r_ref):
    c_len = RET_CHUNK
    r_ref[...] = jnp.zeros(r_ref.shape, f32)
    lane = lax.broadcasted_iota(jnp.int32, (c_len, LANES), 1)
    head_mask = (lane < RET_KEY_DIM, lane >= RET_KEY_DIM)
    pairs = range(RET_PAIRS_PER_STEP)
    chunks = range(RET_CHUNKS_PER_ITER)

    def body(it, carry):
        rows = [pl.ds(pl.multiple_of((it * RET_CHUNKS_PER_ITER + ci) * c_len, c_len), c_len) for ci in chunks]
        qa, s, upd = {}, {}, {}
        for ci in chunks:
            for pr in pairs:
                q2 = q_ref[rows[ci], pr * LANES:(pr + 1) * LANES]
                k2 = k_ref[rows[ci], pr * LANES:(pr + 1) * LANES]
                v2 = v_ref[rows[ci], pr * 2 * HEAD_DIM:(pr + 1) * 2 * HEAD_DIM]
                qa[ci, pr] = jnp.concatenate([jnp.where(head_mask[hd], q2, jnp.zeros_like(q2))
                                              for hd in range(2)], axis=0)
                s[ci, pr] = lax.dot_general(qa[ci, pr], k2, (((1,), (1,)), ((), ())),
                                            preferred_element_type=f32)
                kz = (k2.astype(f32) * zeta_ref[pr]).astype(bf16)
                upd[ci, pr] = lax.dot_general(kz, v2, (((0,), (0,)), ((), ())), preferred_element_type=f32)
        r_b = {}
        for pr in pairs:
            r_state = r_ref[pr]
            for ci in chunks:
                r_b[ci, pr] = r_state.astype(bf16)
                r_state = r_state * g_ref[pr] + upd[ci, pr]
            r_ref[pr] = r_state
        o = {}
        for ci in chunks:
            for pr in pairs:
                for hd in range(2):
                    hrows = slice(hd * c_len, (hd + 1) * c_len)
                    cols = slice((2 * pr + hd) * HEAD_DIM, (2 * pr + hd + 1) * HEAD_DIM)
                    inner = (s[ci, pr][hrows, :] * d_ref[2 * pr + hd]).astype(bf16)
                    intra = jnp.dot(inner, v_ref[rows[ci], cols], preferred_element_type=f32)
                    cross = jnp.dot(qa[ci, pr][hrows, :], r_b[ci, pr][:, hd * HEAD_DIM:(hd + 1) * HEAD_DIM],
                                    preferred_element_type=f32)
                    o[ci, pr, hd] = intra + cross * xi_ref[2 * pr + hd]
        for ci in chunks:
            for pr in pairs:
                for hd in range(2):
                    cols = slice((2 * pr + hd) * HEAD_DIM, (2 * pr + hd + 1) * HEAD_DIM)
                    oh = o[ci, pr, hd]
                    ms = jnp.mean(oh * oh, axis=-1, keepdims=True)
                    on = oh * lax.rsqrt(ms + EPS)
                    zg = _silu(z_ref[rows[ci], cols].astype(f32))
                    o_ref[rows[ci], cols] = (on * zg).astype(o_ref.dtype)
        return carry

    lax.fori_loop(0, SEQ // (c_len * RET_CHUNKS_PER_ITER), body, 0)


def _retention_tables():
    h, c_len = N_MAIN_HEADS, RET_CHUNK
    lg = np.log1p(-np.exp2(-5.0 - np.arange(h, dtype=np.float64)))
    n = np.arange(c_len, dtype=np.float64)
    diff = n[:, None] - n[None, :]
    d_inner = np.where(diff[None] >= 0, np.exp(lg[:, None, None] * np.maximum(diff, 0.0)[None]), 0.0)
    xi = np.exp(lg[:, None] * (n[None, :] + 1.0))
    zeta = np.exp(lg[:, None] * (c_len - 1.0 - n[None, :]))
    g_chunk = np.exp(lg * c_len)
    xi_b = np.broadcast_to(xi[:, :, None], (h, c_len, HEAD_DIM))
    zeta2 = np.repeat(zeta.reshape(h // 2, 2, c_len).transpose(0, 2, 1), RET_KEY_DIM, axis=-1)
    g2 = np.broadcast_to(np.repeat(g_chunk.reshape(h // 2, 2), RET_KEY_DIM, axis=-1)[:, :, None],
                         (h // 2, 2 * RET_KEY_DIM, 2 * HEAD_DIM))
    return tuple(jnp.asarray(np.ascontiguousarray(t), dtype=f32) for t in (d_inner, xi_b, zeta2, g2))


def _retention(qk, v, z):
    n_pairs = N_MAIN_HEADS // 2
    pp = RET_PAIRS_PER_STEP
    steps = n_pairs // pp
    d_inner, xi_b, zeta2, g2 = _retention_tables()
    c_len = RET_CHUNK
    return pl.pallas_call(
        _retention_kernel,
        grid=(BATCH, steps),
        in_specs=[
            pl.BlockSpec((SEQ, pp * LANES), lambda b, p: (b, p)),
            pl.BlockSpec((SEQ, pp * LANES), lambda b, p: (b, steps + p)),
            pl.BlockSpec((SEQ, pp * 2 * HEAD_DIM), lambda b, p: (b, p)),
            pl.BlockSpec((SEQ, pp * 2 * HEAD_DIM), lambda b, p: (b, p)),
            pl.BlockSpec((2 * pp, c_len, c_len), lambda b, p: (p, 0, 0)),
            pl.BlockSpec((2 * pp, c_len, HEAD_DIM), lambda b, p: (p, 0, 0)),
            pl.BlockSpec((pp, c_len, LANES), lambda b, p: (p, 0, 0)),
            pl.BlockSpec((pp, LANES, 2 * HEAD_DIM), lambda b, p: (p, 0, 0)),
        ],
        out_specs=pl.BlockSpec((SEQ, pp * 2 * HEAD_DIM), lambda b, p: (b, p)),
        out_shape=jax.ShapeDtypeStruct((BATCH * SEQ, MAIN_W), bf16),
        scratch_shapes=[pltpu.VMEM((pp, LANES, 2 * HEAD_DIM), f32)],
        compiler_params=pltpu.CompilerParams(dimension_semantics=("arbitrary", "arbitrary"),
                                             vmem_limit_bytes=VMEM_LIMIT),
        name="retention",
    )(qk, qk, v, z, d_inner, xi_b, zeta2, g2)


def _out_proj_kernel(om_ref, qm_ref, zm_ref, kv_ref, w_ref, x_ref, g_ref, o_ref, *, final_norm):
    scale = 1.0 / math.sqrt(HEAD_DIM)
    heads = range(N_MEM_HEADS)
    col = [slice(hd * HEAD_DIM, (hd + 1) * HEAD_DIM) for hd in heads]
    half = D_MODEL // 2
    s = [lax.dot_general(qm_ref[:, col[hd]], kv_ref[:, col[hd]], (((1,), (1,)), ((), ())),
                         preferred_element_type=f32) * scale for hd in heads]
    y_lo = jnp.dot(om_ref[...], w_ref[:MAIN_W, :half], preferred_element_type=f32)
    p = []
    for hd in heads:
        e = jnp.exp(s[hd] - jnp.max(s[hd], axis=-1, keepdims=True))
        p.append((e * (1.0 / jnp.sum(e, axis=-1, keepdims=True))).astype(bf16))
    memo = [jnp.dot(p[hd], kv_ref[:, MEM_W + hd * HEAD_DIM:MEM_W + (hd + 1) * HEAD_DIM],
                    preferred_element_type=f32) for hd in heads]
    y_hi = jnp.dot(om_ref[...], w_ref[:MAIN_W, half:], preferred_element_type=f32)
    og = jnp.concatenate([(memo[hd] * _silu(zm_ref[:, col[hd]].astype(f32))).astype(bf16) for hd in heads],
                         axis=1)
    y = jnp.concatenate([y_lo, y_hi], axis=1) + jnp.dot(og, w_ref[MAIN_W:, :], preferred_element_type=f32)
    xn = x_ref[...] + y
    if final_norm:
        xn = _rmsnorm_rows(xn, g_ref[...])
    o_ref[...] = xn


def _out_proj(o_main, qm, z, kv, w_out, x, g_final, *, final_norm):
    t_rows = x.shape[0]
    tiles_per_seq = SEQ // OUT_TM
    z_blk = MAIN_W // MEM_W
    return pl.pallas_call(
        functools.partial(_out_proj_kernel, final_norm=final_norm),
        grid=(t_rows // OUT_TM,),
        in_specs=[
            pl.BlockSpec((OUT_TM, MAIN_W), lambda i: (i, 0)),
            pl.BlockSpec((OUT_TM, MEM_W), lambda i: (i, 0)),
            pl.BlockSpec((OUT_TM, MEM_W), lambda i: (i, z_blk)),
            pl.BlockSpec((N_MEM, 2 * MEM_W), lambda i: (i // tiles_per_seq, 0)),
            pl.BlockSpec((INNER, D_MODEL), lambda i: (0, 0), pipeline_mode=pl.Buffered(1)),
            pl.BlockSpec((OUT_TM, D_MODEL), lambda i: (i, 0)),
            pl.BlockSpec((1, D_MODEL), lambda i: (0, 0)),
        ],
        out_specs=pl.BlockSpec((OUT_TM, D_MODEL), lambda i: (i, 0)),
        out_shape=jax.ShapeDtypeStruct((t_rows, D_MODEL), f32),
        compiler_params=pltpu.CompilerParams(dimension_semantics=("arbitrary",),
                                             vmem_limit_bytes=VMEM_LIMIT),
        name="out_proj_final" if final_norm else "out_proj",
    )(o_main, qm, z, kv, w_out, x, g_final.reshape(1, D_MODEL))


def _rotary_tables():
    half = RET_KEY_DIM // 2
    pos = np.arange(SEQ, dtype=np.float64)
    inv = 1.0 / (ROPE_BASE ** (np.arange(half, dtype=np.float64) / half))
    ang = pos[:, None] * inv[None, :]
    cos, sin = np.cos(ang), np.sin(ang)
    reps = LANES // RET_KEY_DIM
    cos_t = np.tile(np.concatenate([cos, cos], axis=-1), (1, reps))
    sin_t = np.tile(np.concatenate([-sin, sin], axis=-1), (1, reps))
    k_scale = RET_KEY_DIM ** -0.5
    return tuple(jnp.asarray(t, dtype=f32) for t in (cos_t, sin_t, cos_t * k_scale, sin_t * k_scale))


def kernel(x, mem, norm_g, fox_w_in, fox_b_f, ret_w_in, mem_norm_g, w_mem_kv, w_out, final_norm_g):
    t_rows = BATCH * SEQ
    x2 = x.reshape(t_rows, D_MODEL)

    kv0, kv1 = _norm_proj(mem.reshape(BATCH * N_MEM, D_MODEL), mem_norm_g,
                          [w_mem_kv[0].astype(bf16), w_mem_kv[1].astype(bf16)],
                          [2 * MEM_W, 2 * MEM_W], [bf16, bf16], name="mem_kv_proj")

    o_f = 3 * MAIN_W
    o_qm = o_f + N_MAIN_HEADS
    w0t = jnp.swapaxes(fox_w_in[0], 0, 1).astype(bf16)
    w_qmz = w0t[o_qm:]
    w_f = jnp.pad(w0t[o_f:o_qm], ((0, F_PAD - N_MAIN_HEADS), (0, 0)))
    q, k, vt, qm, z, f, norms = _norm_proj(
        x2, norm_g[0], [(w0t, o_f), w_qmz, w_f], [MAIN_W, MAIN_W, MAIN_W, MEM_W, INNER, F_PAD],
        [bf16, bf16, bf16, bf16, bf16, f32], ["fox_q", "fox_k", "kt", "plain", "plain", "plain"],
        w_transposed=True, name="fox_in_proj")
    nb, nbq = _fox_gate(f, fox_b_f[0])
    nbq = nbq.reshape(BATCH, GATE_HEAD_ROWS, SEQ // FOX_T, FOX_T)
    o_main = _fox_attention(q, k, vt, nb, z, nbq, norms)
    x2 = _out_proj(o_main, qm, z, kv0, w_out[0].astype(bf16), x2, final_norm_g, final_norm=False)

    w1 = ret_w_in[0].astype(bf16)
    qk, v, qm, z = _norm_proj(
        x2, norm_g[1], [w1], [2 * RET_QK_W, MAIN_W, MEM_W, INNER], [bf16] * 4,
        ["rot", "plain", "plain", "plain"], tables=_rotary_tables(), name="ret_in_proj")
    o_main = _retention(qk, v, z)
    out = _out_proj(o_main, qm, z, kv1, w_out[1].astype(bf16), x2, final_norm_g, final_norm=True)
    return out.reshape(BATCH, SEQ, D_MODEL)
```

```python
import functools
import math

import jax
import jax.numpy as jnp
import numpy as np
from jax import lax
from jax.experimental import pallas as pl
from jax.experimental.pallas import tpu as pltpu

D_MODEL = 1024
BATCH = 8
SEQ = 2048
HEAD_DIM = 128
N_MAIN_HEADS = 12
N_MEM_HEADS = 4
N_MEM = 256
MAIN_W = N_MAIN_HEADS * HEAD_DIM
MEM_W = N_MEM_HEADS * HEAD_DIM
INNER = MAIN_W + MEM_W
RET_KEY_DIM = HEAD_DIM // 2
RET_QK_W = N_MAIN_HEADS * RET_KEY_DIM
RET_CHUNK = 128
ROPE_BASE = 10000.0
EPS = 1e-6
NEG = -1e30
LOG2E = 1.4426950408889634

LANES = 128
F_PAD = LANES
VMEM_LIMIT = 56 * 1024 * 1024

PROJ_TM = 512
PROJ_TN = 512
FOX_T = 256
FOX_HEADS_PER_STEP = 6
FOX_ITEMS_PER_ITER = 4
FOX_ITEMS_PER_ITER_ONLINE = 2
BIAS_TERMS = 3
FOX_SUM_ROWS = 16
FOX_Q_SCALE = LOG2E / math.sqrt(HEAD_DIM)
FOX_BOUND_MARGIN = 1.02
FOX_MIN_DENOM = 2.0 ** -90
GATE_HEAD_ROWS = 16
RET_PAIRS_PER_STEP = 3
RET_CHUNKS_PER_ITER = 2
OUT_TM = 512

f32 = jnp.float32
bf16 = jnp.bfloat16


def _silu(z):
    h = 0.5 * z
    return h + h * jnp.tanh(h)


def _rmsnorm_rows(x, g):
    ms = jnp.mean(x * x, axis=-1, keepdims=True)
    return (x * lax.rsqrt(ms + EPS)) * g


def _split_bf16(v):
    hi = v.astype(bf16)
    r1 = v - hi.astype(f32)
    mid = r1.astype(bf16)
    lo = (r1 - mid.astype(f32)).astype(bf16)
    return hi, mid, lo


def _norm_proj_kernel(*refs, n_weights, w_transposed, out_widths, out_kinds):
    track_norms = "fox_q" in out_kinds
    n_out = len(out_widths) + (1 if track_norms else 0)
    x_ref, g_ref = refs[:2]
    w_refs = refs[2:2 + n_weights]
    tab_refs = refs[2 + n_weights:len(refs) - n_out]
    out_refs = refs[len(refs) - n_out:]
    h = _rmsnorm_rows(x_ref[...], g_ref[...]).astype(bf16)
    tm = h.shape[0]
    if "rot" in out_kinds:
        lane = lax.broadcasted_iota(jnp.int32, (tm, LANES), 1)
        first_half = (lane % RET_KEY_DIM) < (RET_KEY_DIM // 2)
    if track_norms:
        norm_sub = lax.broadcasted_iota(jnp.int32, (8, LANES), 0)
        norm_lane = lax.broadcasted_iota(jnp.int32, (8, LANES), 1)
        norms = jnp.zeros((8, LANES), f32)

        def with_norms(norms, y, row, head0):
            for s0 in range(0, y.shape[1], HEAD_DIM):
                ys = y[:, s0:s0 + HEAD_DIM]
                worst = jnp.max(jnp.sum(ys * ys, axis=1, keepdims=True), axis=0, keepdims=True)
                norms = jnp.where((norm_sub == row) & (norm_lane == head0 + s0 // HEAD_DIM), worst, norms)
            return norms
    out_axis = 0 if w_transposed else 1
    wi, col = 0, 0
    for o_ref, width, kind in zip(out_refs, out_widths, out_kinds):
        if col == w_refs[wi].shape[out_axis]:
            wi, col = wi + 1, 0
        w_ref = w_refs[wi]
        for c0 in range(0, width, PROJ_TN):
            cw = min(PROJ_TN, width - c0)
            if w_transposed:
                y = lax.dot_general(h, w_ref[col + c0:col + c0 + cw, :], (((1,), (1,)), ((), ())),
                                    preferred_element_type=f32)
            else:
                y = jnp.dot(h, w_ref[:, col + c0:col + c0 + cw], preferred_element_type=f32)
            if kind == "rot":
                for s0 in range(0, cw, LANES):
                    ys = y[:, s0:s0 + LANES]
                    sw = jnp.where(first_half, pltpu.roll(ys, LANES - RET_KEY_DIM // 2, 1),
                                   pltpu.roll(ys, RET_KEY_DIM // 2, 1))
                    cos_ref, sin_ref = tab_refs[:2] if c0 + s0 < RET_QK_W else tab_refs[2:]
                    r = ys * cos_ref[...] + sw * sin_ref[...]
                    o_ref[:, c0 + s0:c0 + s0 + LANES] = r.astype(o_ref.dtype)
            elif kind in ("kt", "fox_q"):
                if kind == "fox_q":
                    y = y * FOX_Q_SCALE
                    norms = with_norms(norms, y, 0, c0 // HEAD_DIM)
                for r0 in range(tm // FOX_T):
                    for s0 in range(0, cw, LANES):
                        piece = y[r0 * FOX_T:(r0 + 1) * FOX_T, s0:s0 + LANES]
                        o_ref[r0, c0 + s0:c0 + s0 + LANES, :] = piece.T.astype(o_ref.dtype)
            elif kind == "fox_k":
                norms = with_norms(norms, y, 1, c0 // HEAD_DIM)
                o_ref[:, c0:c0 + cw] = y.astype(o_ref.dtype)
            else:
                o_ref[:, c0:c0 + cw] = y.astype(o_ref.dtype)
        col += width
    if track_norms:
        out_refs[-1][...] = norms


def _norm_proj(x, g, weights, out_widths, out_dtypes, out_kinds=None, *, w_transposed=False, tables=None,
               name):
    t_rows, d = x.shape
    out_kinds = tuple(out_kinds or ["plain"] * len(out_widths))
    w_arrays = [w[0] if isinstance(w, tuple) else w for w in weights]
    w_blocks = [(w[1], d) if isinstance(w, tuple) else w.shape for w in weights]
    out_axis = 0 if w_transposed else 1
    assert sum(out_widths) == sum(blk[out_axis] for blk in w_blocks) and t_rows % PROJ_TM == 0
    tiles_per_seq = SEQ // PROJ_TM
    in_specs = [
        pl.BlockSpec((PROJ_TM, d), lambda i: (i, 0)),
        pl.BlockSpec((1, d), lambda i: (0, 0)),
    ]
    in_specs += [pl.BlockSpec(blk, lambda i: (0, 0), pipeline_mode=pl.Buffered(1)) for blk in w_blocks]
    args = [x, g.reshape(1, d), *w_arrays]
    if "rot" in out_kinds:
        for tab in tables:
            in_specs.append(pl.BlockSpec((PROJ_TM, LANES), lambda i: (i % tiles_per_seq, 0)))
            args.append(tab)
    out_specs, out_shape = [], []
    for wd, dt, kind in zip(out_widths, out_dtypes, out_kinds):
        if kind in ("kt", "fox_q"):
            kt_per_tile = PROJ_TM // FOX_T
            out_specs.append(pl.BlockSpec((None, kt_per_tile, wd, FOX_T),
                                          lambda i: (i // tiles_per_seq, i % tiles_per_seq, 0, 0)))
            out_shape.append(jax.ShapeDtypeStruct((t_rows // SEQ, SEQ // FOX_T, wd, FOX_T), dt))
        else:
            out_specs.append(pl.BlockSpec((PROJ_TM, wd), lambda i: (i, 0)))
            out_shape.append(jax.ShapeDtypeStruct((t_rows, wd), dt))
    if "fox_q" in out_kinds:
        out_specs.append(pl.BlockSpec((None, 8, LANES), lambda i: (i, 0, 0)))
        out_shape.append(jax.ShapeDtypeStruct((t_rows // PROJ_TM, 8, LANES), f32))
    return pl.pallas_call(
        functools.partial(_norm_proj_kernel, n_weights=len(weights), w_transposed=w_transposed,
                          out_widths=tuple(out_widths), out_kinds=out_kinds),
        grid=(t_rows // PROJ_TM,),
        in_specs=in_specs,
        out_specs=out_specs,
        out_shape=out_shape,
        compiler_params=pltpu.CompilerParams(dimension_semantics=("arbitrary",),
                                             vmem_limit_bytes=VMEM_LIMIT),
        name=name,
    )(*args)


def _fox_gate_kernel(f_ref, bf_ref, o_ref, ot_ref):
    blk = LANES
    row = lax.broadcasted_iota(jnp.int32, (blk, blk), 0)
    coli = lax.broadcasted_iota(jnp.int32, (blk, blk), 1)
    tri = jnp.where(row >= coli, 1.0, 0.0).astype(bf16)
    spread = [jnp.where((coli == BIAS_TERMS * row + t) & (row < N_MAIN_HEADS), 1.0, 0.0).astype(bf16)
              for t in range(BIAS_TERMS)]
    carry = jnp.zeros((1, F_PAD), f32)
    for b0 in range(0, SEQ, blk):
        xg = f_ref[b0:b0 + blk, :] + bf_ref[...]
        ls = jnp.minimum(xg, 0.0) - jnp.log1p(jnp.exp(-jnp.abs(xg)))
        cs = sum(jnp.dot(tri, term, preferred_element_type=f32) for term in _split_bf16(ls)) + carry
        carry = cs[blk - 1:blk, :]
        bias = cs * (-LOG2E)
        nb = sum(jnp.dot(term, sp, preferred_element_type=f32)
                 for term, sp in zip(_split_bf16(bias), spread))
        o_ref[b0:b0 + blk, :] = nb.astype(o_ref.dtype)
        ot_ref[:, b0:b0 + blk] = bias.T[:ot_ref.shape[0], :]


def _fox_gate(f, b_f):
    bf_pad = jnp.zeros((1, F_PAD), f32).at[0, :N_MAIN_HEADS].set(b_f.astype(f32))
    return pl.pallas_call(
        _fox_gate_kernel,
        grid=(BATCH,),
        in_specs=[pl.BlockSpec((SEQ, F_PAD), lambda b: (b, 0)),
                  pl.BlockSpec((1, F_PAD), lambda b: (0, 0))],
        out_specs=[pl.BlockSpec((SEQ, LANES), lambda b: (b, 0)),
                   pl.BlockSpec((None, GATE_HEAD_ROWS, SEQ), lambda b: (b, 0, 0))],
        out_shape=[jax.ShapeDtypeStruct((BATCH * SEQ, LANES), bf16),
                   jax.ShapeDtypeStruct((BATCH, GATE_HEAD_ROWS, SEQ), f32)],
        compiler_params=pltpu.CompilerParams(dimension_semantics=("arbitrary",)),
        name="fox_gate",
    )(f, bf_pad)


def _fox_attn_kernel(qt_ref, k_ref, vt_ref, nb_ref, z_ref, nbq_ref, norm_ref, o_ref,
                     m_ref, acc_ref, u_ref, p_ref, a_ref):
    t = FOX_T
    nt = SEQ // t
    head0 = pl.program_id(1) * FOX_HEADS_PER_STEP
    heads = range(FOX_HEADS_PER_STEP)
    col = [slice(hd * HEAD_DIM, (hd + 1) * HEAD_DIM) for hd in heads]
    sub = lax.broadcasted_iota(jnp.int32, (LANES, t), 0)
    sel = []
    for hd in heads:
        lo = BIAS_TERMS * (head0 + hd)
        sel.append(jnp.where((sub >= lo) & (sub < lo + BIAS_TERMS), 1.0, 0.0).astype(bf16))
    ones_rows = jnp.ones((FOX_SUM_ROWS, t), bf16)

    def tile_rows(i):
        return pl.ds(pl.multiple_of(i * t, t), t)

    def score(qt, kt, hd):
        k_aug = jnp.concatenate([k_ref[tile_rows(kt), col[hd]], nb_ref[tile_rows(kt), :]], axis=1)
        q_aug_t = jnp.concatenate([qt_ref[qt, col[hd], :], sel[hd]], axis=0)
        return jnp.dot(k_aug, q_aug_t, preferred_element_type=f32)

    def scores(qt, kt):
        return [score(qt, kt, hd) for hd in heads]

    def pv1(kt, p_hd, hd):
        return jnp.dot(jnp.concatenate([vt_ref[kt, col[hd], :], ones_rows], axis=0), p_hd,
                       preferred_element_type=f32)

    def pv(kt, p):
        return [pv1(kt, p[hd], hd) for hd in heads]

    def finalize(qt):
        for hd in heads:
            acc = acc_ref[qt, hd]
            out_t = acc[:HEAD_DIM, :] * (1.0 / acc[HEAD_DIM:HEAD_DIM + 1, :])
            gate = _silu(z_ref[tile_rows(qt), col[hd]].astype(f32))
            o_ref[tile_rows(qt), col[hd]] = (out_t.T * gate).astype(o_ref.dtype)

    def next_item(qt, kt):
        wrap = kt + 1 >= qt
        return jnp.minimum(jnp.where(wrap, qt + 1, qt), nt - 1), jnp.where(wrap, 0, kt + 1)

    def load_list(ref):
        return [ref[hd] for hd in heads]

    def store_list(ref, vals):
        for hd in heads:
            ref[hd] = vals[hd]

    def run(online):
        def numerators(qt, u, masked):
            if masked:
                causal = (lax.broadcasted_iota(jnp.int32, (t, t), 0)
                          <= lax.broadcasted_iota(jnp.int32, (t, t), 1))
            p, alpha = [], []
            for hd in heads:
                uh = jnp.where(causal, u[hd], NEG) if masked else u[hd]
                m_old = m_ref[qt, hd]
                if online:
                    m_new = jnp.maximum(m_old, jnp.max(uh, axis=0, keepdims=True))
                    alpha.append(jnp.exp2(m_old - m_new))
                    m_ref[qt, hd] = m_new
                else:
                    m_new = m_old
                p.append(jnp.exp2(uh - m_new).astype(bf16))
            return p, alpha

        def accumulate(qt, alpha, pv_vals):
            for hd in heads:
                prev = alpha[hd] * acc_ref[qt, hd] if online else acc_ref[qt, hd]
                acc_ref[qt, hd] = prev + pv_vals[hd]

        acc_ref[...] = jnp.zeros(acc_ref.shape, f32)
        p_ref[...] = jnp.zeros(p_ref.shape, bf16)
        if online:
            m_ref[...] = jnp.full(m_ref.shape, NEG, f32)
            a_ref[...] = jnp.ones(a_ref.shape, f32)

        n_items = nt * (nt - 1) // 2
        ni = FOX_ITEMS_PER_ITER_ONLINE if online else FOX_ITEMS_PER_ITER
        assert n_items % ni == 0
        store_list(u_ref, scores(1, 0))

        def pass1(_, carry):
            q0, k0, q_prev, k_prev = carry
            items = [(q0, k0)]
            for _k in range(ni):
                items.append(next_item(*items[-1]))
            u_cur = load_list(u_ref)
            pv_vals = pv(k_prev, load_list(p_ref))
            u_next = scores(*items[1])
            p_cur, al_cur = numerators(q0, u_cur, False)
            accumulate(q_prev, load_list(a_ref) if online else None, pv_vals)
            for k in range(1, ni):
                u_cur, u_next, pv_vals = u_next, [], []
                for hd in heads:
                    s = score(items[k + 1][0], items[k + 1][1], hd)
                    if k == ni - 1:
                        u_ref[hd] = s
                    else:
                        u_next.append(s)
                    pv_vals.append(pv1(items[k - 1][1], p_cur[hd], hd))
                p_new, al_new = numerators(items[k][0], u_cur, False)
                accumulate(items[k - 1][0], al_cur, pv_vals)
                p_cur, al_cur = p_new, al_new
            store_list(p_ref, p_cur)
            if online:
                store_list(a_ref, al_cur)
            return items[ni] + items[ni - 1]

        one, zero = jnp.int32(1), jnp.int32(0)
        _, _, q_prev, k_prev = lax.fori_loop(0, n_items // ni, pass1, (one, zero, one, zero))
        accumulate(q_prev, load_list(a_ref) if online else None, pv(k_prev, load_list(p_ref)))

        assert nt % 2 == 0
        store_list(u_ref, scores(0, 0))

        def pass2(i, carry):
            ta = 2 * i
            tb = ta + 1
            u_b = scores(tb, tb)
            p_a, al_a = numerators(ta, load_list(u_ref), True)
            pv_a = pv(ta, p_a)
            tn = jnp.minimum(ta + 2, nt - 1)
            u_n = scores(tn, tn)
            p_b, al_b = numerators(tb, u_b, True)
            accumulate(ta, al_a, pv_a)
            pv_b = pv(tb, p_b)
            finalize(ta)
            store_list(u_ref, u_n)
            accumulate(tb, al_b, pv_b)
            finalize(tb)
            return carry

        lax.fori_loop(0, nt // 2, pass2, 0)

    norms = jnp.max(norm_ref[...], axis=0)
    hlane = lax.broadcasted_iota(jnp.int32, (1, LANES), 1)
    for hd in heads:
        pick = hlane == head0 + hd
        q2 = jnp.max(jnp.where(pick, norms[0:1, :], 0.0), axis=1, keepdims=True)
        k2 = jnp.max(jnp.where(pick, norms[1:2, :], 0.0), axis=1, keepdims=True)
        qk_bound = jnp.sqrt(q2 * k2) * FOX_BOUND_MARGIN + 1.0
        for qt in range(nt):
            m_ref[qt, hd] = nbq_ref[hd, qt:qt + 1, :] + qk_bound

    run(online=False)

    denom_min = jnp.min(acc_ref[:, :, HEAD_DIM:HEAD_DIM + 1, :])

    @pl.when(jnp.logical_not(denom_min >= FOX_MIN_DENOM))
    def _():
        run(online=True)


def _fox_attention(q, k, vt, nb, z, nbq, norms):
    hw = FOX_HEADS_PER_STEP * HEAD_DIM
    return pl.pallas_call(
        _fox_attn_kernel,
        grid=(BATCH, N_MAIN_HEADS // FOX_HEADS_PER_STEP),
        in_specs=[
            pl.BlockSpec((None, SEQ // FOX_T, hw, FOX_T), lambda b, p: (b, 0, p, 0)),
            pl.BlockSpec((SEQ, hw), lambda b, p: (b, p)),
            pl.BlockSpec((None, SEQ // FOX_T, hw, FOX_T), lambda b, p: (b, 0, p, 0)),
            pl.BlockSpec((SEQ, LANES), lambda b, p: (b, 0)),
            pl.BlockSpec((SEQ, hw), lambda b, p: (b, p)),
            pl.BlockSpec((None, FOX_HEADS_PER_STEP, SEQ // FOX_T, FOX_T), lambda b, p: (b, p, 0, 0)),
            pl.BlockSpec((SEQ // PROJ_TM, 8, LANES), lambda b, p: (b, 0, 0)),
        ],
        out_specs=pl.BlockSpec((SEQ, hw), lambda b, p: (b, p)),
        out_shape=jax.ShapeDtypeStruct((BATCH * SEQ, MAIN_W), bf16),
        scratch_shapes=[
            pltpu.VMEM((SEQ // FOX_T, FOX_HEADS_PER_STEP, 1, FOX_T), f32),
            pltpu.VMEM((SEQ // FOX_T, FOX_HEADS_PER_STEP, HEAD_DIM + FOX_SUM_ROWS, FOX_T), f32),
            pltpu.VMEM((FOX_HEADS_PER_STEP, FOX_T, FOX_T), f32),
            pltpu.VMEM((FOX_HEADS_PER_STEP, FOX_T, FOX_T), bf16),
            pltpu.VMEM((FOX_HEADS_PER_STEP, 1, FOX_T), f32),
        ],
        compiler_params=pltpu.CompilerParams(dimension_semantics=("arbitrary", "arbitrary"),
                                             vmem_limit_bytes=VMEM_LIMIT),
        name="fox_attention",
    )(q, k, vt, nb, z, nbq, norms)


def _retention_kernel(q_ref, k_ref, v_ref, z_ref, d_ref, xi_ref, zeta_ref, g_ref, o_ref, r_ref):
    c_len = RET_CHUNK
    r_ref[...] = jnp.zeros(r_ref.shape, f32)
    lane = lax.broadcasted_iota(jnp.int32, (c_len, LANES), 1)
    head_mask = (lane < RET_KEY_DIM, lane >= RET_KEY_DIM)
    pairs = range(RET_PAIRS_PER_STEP)
    chunks = range(RET_CHUNKS_PER_ITER)

    def body(it, carry):
        rows = [pl.ds(pl.multiple_of((it * RET_CHUNKS_PER_ITER + ci) * c_len, c_len), c_len) for ci in chunks]
        qa, s, upd = {}, {}, {}
        for ci in chunks:
            for pr in pairs:
                q2 = q_ref[rows[ci], pr * LANES:(pr + 1) * LANES]
                k2 = k_ref[rows[ci], pr * LANES:(pr + 1) * LANES]
                v2 = v_ref[rows[ci], pr * 2 * HEAD_DIM:(pr + 1) * 2 * HEAD_DIM]
                qa[ci, pr] = jnp.concatenate([jnp.where(head_mask[hd], q2, jnp.zeros_like(q2))
                                              for hd in range(2)], axis=0)
                s[ci, pr] = lax.dot_general(qa[ci, pr], k2, (((1,), (1,)), ((), ())),
                                            preferred_element_type=f32)
                kz = (k2.astype(f32) * zeta_ref[pr]).astype(bf16)
                upd[ci, pr] = lax.dot_general(kz, v2, (((0,), (0,)), ((), ())), preferred_element_type=f32)
        r_b = {}
        for pr in pairs:
            r_state = r_ref[pr]
            for ci in chunks:
                r_b[ci, pr] = r_state.astype(bf16)
                r_state = r_state * g_ref[pr] + upd[ci, pr]
            r_ref[pr] = r_state
        o = {}
        for ci in chunks:
            for pr in pairs:
                for hd in range(2):
                    hrows = slice(hd * c_len, (hd + 1) * c_len)
                    cols = slice((2 * pr + hd) * HEAD_DIM, (2 * pr + hd + 1) * HEAD_DIM)
                    inner = (s[ci, pr][hrows, :] * d_ref[2 * pr + hd]).astype(bf16)
                    intra = jnp.dot(inner, v_ref[rows[ci], cols], preferred_element_type=f32)
                    cross = jnp.dot(qa[ci, pr][hrows, :], r_b[ci, pr][:, hd * HEAD_DIM:(hd + 1) * HEAD_DIM],
                                    preferred_element_type=f32)
                    o[ci, pr, hd] = intra + cross * xi_ref[2 * pr + hd]
        for ci in chunks:
            for pr in pairs:
                for hd in range(2):
                    cols = slice((2 * pr + hd) * HEAD_DIM, (2 * pr + hd + 1) * HEAD_DIM)
                    oh = o[ci, pr, hd]
                    ms = jnp.mean(oh * oh, axis=-1, keepdims=True)
                    on = oh * lax.rsqrt(ms + EPS)
                    zg = _silu(z_ref[rows[ci], cols].astype(f32))
                    o_ref[rows[ci], cols] = (on * zg).astype(o_ref.dtype)
        return carry

    lax.fori_loop(0, SEQ // (c_len * RET_CHUNKS_PER_ITER), body, 0)


def _retention_tables():
    h, c_len = N_MAIN_HEADS, RET_CHUNK
    lg = np.log1p(-np.exp2(-5.0 - np.arange(h, dtype=np.float64)))
    n = np.arange(c_len, dtype=np.float64)
    diff = n[:, None] - n[None, :]
    d_inner = np.where(diff[None] >= 0, np.exp(lg[:, None, None] * np.maximum(diff, 0.0)[None]), 0.0)
    xi = np.exp(lg[:, None] * (n[None, :] + 1.0))
    zeta = np.exp(lg[:, None] * (c_len - 1.0 - n[None, :]))
    g_chunk = np.exp(lg * c_len)
    xi_b = np.broadcast_to(xi[:, :, None], (h, c_len, HEAD_DIM))
    zeta2 = np.repeat(zeta.reshape(h // 2, 2, c_len).transpose(0, 2, 1), RET_KEY_DIM, axis=-1)
    g2 = np.broadcast_to(np.repeat(g_chunk.reshape(h // 2, 2), RET_KEY_DIM, axis=-1)[:, :, None],
                         (h // 2, 2 * RET_KEY_DIM, 2 * HEAD_DIM))
    return tuple(jnp.asarray(np.ascontiguousarray(t), dtype=f32) for t in (d_inner, xi_b, zeta2, g2))


def _retention(qk, v, z):
    n_pairs = N_MAIN_HEADS // 2
    pp = RET_PAIRS_PER_STEP
    steps = n_pairs // pp
    d_inner, xi_b, zeta2, g2 = _retention_tables()
    c_len = RET_CHUNK
    return pl.pallas_call(
        _retention_kernel,
        grid=(BATCH, steps),
        in_specs=[
            pl.BlockSpec((SEQ, pp * LANES), lambda b, p: (b, p)),
            pl.BlockSpec((SEQ, pp * LANES), lambda b, p: (b, steps + p)),
            pl.BlockSpec((SEQ, pp * 2 * HEAD_DIM), lambda b, p: (b, p)),
            pl.BlockSpec((SEQ, pp * 2 * HEAD_DIM), lambda b, p: (b, p)),
            pl.BlockSpec((2 * pp, c_len, c_len), lambda b, p: (p, 0, 0)),
            pl.BlockSpec((2 * pp, c_len, HEAD_DIM), lambda b, p: (p, 0, 0)),
            pl.BlockSpec((pp, c_len, LANES), lambda b, p: (p, 0, 0)),
            pl.BlockSpec((pp, LANES, 2 * HEAD_DIM), lambda b, p: (p, 0, 0)),
        ],
        out_specs=pl.BlockSpec((SEQ, pp * 2 * HEAD_DIM), lambda b, p: (b, p)),
        out_shape=jax.ShapeDtypeStruct((BATCH * SEQ, MAIN_W), bf16),
        scratch_shapes=[pltpu.VMEM((pp, LANES, 2 * HEAD_DIM), f32)],
        compiler_params=pltpu.CompilerParams(dimension_semantics=("arbitrary", "arbitrary"),
                                             vmem_limit_bytes=VMEM_LIMIT),
        name="retention",
    )(qk, qk, v, z, d_inner, xi_b, zeta2, g2)


def _out_proj_kernel(om_ref, qm_ref, zm_ref, kv_ref, w_ref, x_ref, g_ref, o_ref, *, final_norm):
    scale = 1.0 / math.sqrt(HEAD_DIM)
    heads = range(N_MEM_HEADS)
    col = [slice(hd * HEAD_DIM, (hd + 1) * HEAD_DIM) for hd in heads]
    half = D_MODEL // 2
    s = [lax.dot_general(qm_ref[:, col[hd]], kv_ref[:, col[hd]], (((1,), (1,)), ((), ())),
                         preferred_element_type=f32) * scale for hd in heads]
    y_lo = jnp.dot(om_ref[...], w_ref[:MAIN_W, :half], preferred_element_type=f32)
    p = []
    for hd in heads:
        e = jnp.exp(s[hd] - jnp.max(s[hd], axis=-1, keepdims=True))
        p.append((e * (1.0 / jnp.sum(e, axis=-1, keepdims=True))).astype(bf16))
    memo = [jnp.dot(p[hd], kv_ref[:, MEM_W + hd * HEAD_DIM:MEM_W + (hd + 1) * HEAD_DIM],
                    preferred_element_type=f32) for hd in heads]
    y_hi = jnp.dot(om_ref[...], w_ref[:MAIN_W, half:], preferred_element_type=f32)
    og = jnp.concatenate([(memo[hd] * _silu(zm_ref[:, col[hd]].astype(f32))).astype(bf16) for hd in heads],
                         axis=1)
    y = jnp.concatenate([y_lo, y_hi], axis=1) + jnp.dot(og, w_ref[MAIN_W:, :], preferred_element_type=f32)
    xn = x_ref[...] + y
    if final_norm:
        xn = _rmsnorm_rows(xn, g_ref[...])
    o_ref[...] = xn


def _out_proj(o_main, qm, z, kv, w_out, x, g_final, *, final_norm):
    t_rows = x.shape[0]
    tiles_per_seq = SEQ // OUT_TM
    z_blk = MAIN_W // MEM_W
    return pl.pallas_call(
        functools.partial(_out_proj_kernel, final_norm=final_norm),
        grid=(t_rows // OUT_TM,),
        in_specs=[
            pl.BlockSpec((OUT_TM, MAIN_W), lambda i: (i, 0)),
            pl.BlockSpec((OUT_TM, MEM_W), lambda i: (i, 0)),
            pl.BlockSpec((OUT_TM, MEM_W), lambda i: (i, z_blk)),
            pl.BlockSpec((N_MEM, 2 * MEM_W), lambda i: (i // tiles_per_seq, 0)),
            pl.BlockSpec((INNER, D_MODEL), lambda i: (0, 0), pipeline_mode=pl.Buffered(1)),
            pl.BlockSpec((OUT_TM, D_MODEL), lambda i: (i, 0)),
            pl.BlockSpec((1, D_MODEL), lambda i: (0, 0)),
        ],
        out_specs=pl.BlockSpec((OUT_TM, D_MODEL), lambda i: (i, 0)),
        out_shape=jax.ShapeDtypeStruct((t_rows, D_MODEL), f32),
        compiler_params=pltpu.CompilerParams(dimension_semantics=("arbitrary",),
                                             vmem_limit_bytes=VMEM_LIMIT),
        name="out_proj_final" if final_norm else "out_proj",
    )(o_main, qm, z, kv, w_out, x, g_final.reshape(1, D_MODEL))


def _rotary_tables():
    half = RET_KEY_DIM // 2
    pos = np.arange(SEQ, dtype=np.float64)
    inv = 1.0 / (ROPE_BASE ** (np.arange(half, dtype=np.float64) / half))
    ang = pos[:, None] * inv[None, :]
    cos, sin = np.cos(ang), np.sin(ang)
    reps = LANES // RET_KEY_DIM
    cos_t = np.tile(np.concatenate([cos, cos], axis=-1), (1, reps))
    sin_t = np.tile(np.concatenate([-sin, sin], axis=-1), (1, reps))
    k_scale = RET_KEY_DIM ** -0.5
    return tuple(jnp.asarray(t, dtype=f32) for t in (cos_t, sin_t, cos_t * k_scale, sin_t * k_scale))


def kernel(x, mem, norm_g, fox_w_in, fox_b_f, ret_w_in, mem_norm_g, w_mem_kv, w_out, final_norm_g):
    t_rows = BATCH * SEQ
    x2 = x.reshape(t_rows, D_MODEL)

    kv0, kv1 = _norm_proj(mem.reshape(BATCH * N_MEM, D_MODEL), mem_norm_g,
                          [w_mem_kv[0].astype(bf16), w_mem_kv[1].astype(bf16)],
                          [2 * MEM_W, 2 * MEM_W], [bf16, bf16], name="mem_kv_proj")

    o_f = 3 * MAIN_W
    o_qm = o_f + N_MAIN_HEADS
    w0t = jnp.swapaxes(fox_w_in[0], 0, 1).astype(bf16)
    w_qmz = w0t[o_qm:]
    w_f = jnp.pad(w0t[o_f:o_qm], ((0, F_PAD - N_MAIN_HEADS), (0, 0)))
    q, k, vt, qm, z, f, norms = _norm_proj(
        x2, norm_g[0], [(w0t, o_f), w_qmz, w_f], [MAIN_W, MAIN_W, MAIN_W, MEM_W, INNER, F_PAD],
        [bf16, bf16, bf16, bf16, bf16, f32], ["fox_q", "fox_k", "kt", "plain", "plain", "plain"],
        w_transposed=True, name="fox_in_proj")
    nb, nbq = _fox_gate(f, fox_b_f[0])
    nbq = nbq.reshape(BATCH, GATE_HEAD_ROWS, SEQ // FOX_T, FOX_T)
    o_main = _fox_attention(q, k, vt, nb, z, nbq, norms)
    x2 = _out_proj(o_main, qm, z, kv0, w_out[0].astype(bf16), x2, final_norm_g, final_norm=False)

    w1 = ret_w_in[0].astype(bf16)
    qk, v, qm, z = _norm_proj(
        x2, norm_g[1], [w1], [2 * RET_QK_W, MAIN_W, MEM_W, INNER], [bf16] * 4,
        ["rot", "plain", "plain", "plain"], tables=_rotary_tables(), name="ret_in_proj")
    o_main = _retention(qk, v, z)
    out = _out_proj(o_main, qm, z, kv1, w_out[1].astype(bf16), x2, final_norm_g, final_norm=True)
    return out.reshape(BATCH, SEQ, D_MODEL)
```

```python
import functools
import math

import jax
import jax.numpy as jnp
import numpy as np
from jax import lax
from jax.experimental import pallas as pl
from jax.experimental.pallas import tpu as pltpu

D_MODEL = 1024
BATCH = 8
SEQ = 2048
HEAD_DIM = 128
N_MAIN_HEADS = 12
N_MEM_HEADS = 4
N_MEM = 256
MAIN_W = N_MAIN_HEADS * HEAD_DIM
MEM_W = N_MEM_HEADS * HEAD_DIM
INNER = MAIN_W + MEM_W
RET_KEY_DIM = HEAD_DIM // 2
RET_QK_W = N_MAIN_HEADS * RET_KEY_DIM
RET_CHUNK = 128
ROPE_BASE = 10000.0
EPS = 1e-6
NEG = -1e30
LOG2E = 1.4426950408889634

LANES = 128
F_PAD = LANES
VMEM_LIMIT = 56 * 1024 * 1024

PROJ_TM = 512
PROJ_TN = 512
FOX_T = 256
FOX_HEADS_PER_STEP = 6
FOX_ITEMS_PER_ITER = 7
FOX_ITEMS_PER_ITER_ONLINE = 2
BIAS_TERMS = 3
FOX_SUM_ROWS = 16
FOX_Q_SCALE = LOG2E / math.sqrt(HEAD_DIM)
FOX_BOUND_MARGIN = 1.02
FOX_MIN_DENOM = 2.0 ** -90
GATE_HEAD_ROWS = 16
RET_PAIRS_PER_STEP = 3
RET_CHUNKS_PER_ITER = 2
OUT_TM = 512

f32 = jnp.float32
bf16 = jnp.bfloat16


def _silu(z):
    h = 0.5 * z
    return h + h * jnp.tanh(h)


def _rmsnorm_rows(x, g):
    ms = jnp.mean(x * x, axis=-1, keepdims=True)
    return (x * lax.rsqrt(ms + EPS)) * g


def _split_bf16(v):
    hi = v.astype(bf16)
    r1 = v - hi.astype(f32)
    mid = r1.astype(bf16)
    lo = (r1 - mid.astype(f32)).astype(bf16)
    return hi, mid, lo


def _norm_proj_kernel(*refs, n_weights, w_transposed, out_widths, out_kinds):
    track_norms = "fox_q" in out_kinds
    n_out = len(out_widths) + (1 if track_norms else 0)
    x_ref, g_ref = refs[:2]
    w_refs = refs[2:2 + n_weights]
    tab_refs = refs[2 + n_weights:len(refs) - n_out]
    out_refs = refs[len(refs) - n_out:]
    h = _rmsnorm_rows(x_ref[...], g_ref[...]).astype(bf16)
    tm = h.shape[0]
    if "rot" in out_kinds:
        lane = lax.broadcasted_iota(jnp.int32, (tm, LANES), 1)
        first_half = (lane % RET_KEY_DIM) < (RET_KEY_DIM // 2)
    if track_norms:
        norm_sub = lax.broadcasted_iota(jnp.int32, (8, LANES), 0)
        norm_lane = lax.broadcasted_iota(jnp.int32, (8, LANES), 1)
        norms = jnp.zeros((8, LANES), f32)

        def with_norms(norms, y, row, head0):
            for s0 in range(0, y.shape[1], HEAD_DIM):
                ys = y[:, s0:s0 + HEAD_DIM]
                worst = jnp.max(jnp.sum(ys * ys, axis=1, keepdims=True), axis=0, keepdims=True)
                norms = jnp.where((norm_sub == row) & (norm_lane == head0 + s0 // HEAD_DIM), worst, norms)
            return norms
    out_axis = 0 if w_transposed else 1
    wi, col = 0, 0
    for o_ref, width, kind in zip(out_refs, out_widths, out_kinds):
        if col == w_refs[wi].shape[out_axis]:
            wi, col = wi + 1, 0
        w_ref = w_refs[wi]
        for c0 in range(0, width, PROJ_TN):
            cw = min(PROJ_TN, width - c0)
            if w_transposed:
                y = lax.dot_general(h, w_ref[col + c0:col + c0 + cw, :], (((1,), (1,)), ((), ())),
                                    preferred_element_type=f32)
            else:
                y = jnp.dot(h, w_ref[:, col + c0:col + c0 + cw], preferred_element_type=f32)
            if kind == "rot":
                for s0 in range(0, cw, LANES):
                    ys = y[:, s0:s0 + LANES]
                    sw = jnp.where(first_half, pltpu.roll(ys, LANES - RET_KEY_DIM // 2, 1),
                                   pltpu.roll(ys, RET_KEY_DIM // 2, 1))
                    cos_ref, sin_ref = tab_refs[:2] if c0 + s0 < RET_QK_W else tab_refs[2:]
                    r = ys * cos_ref[...] + sw * sin_ref[...]
                    o_ref[:, c0 + s0:c0 + s0 + LANES] = r.astype(o_ref.dtype)
            elif kind in ("kt", "fox_q"):
                if kind == "fox_q":
                    y = y * FOX_Q_SCALE
                    norms = with_norms(norms, y, 0, c0 // HEAD_DIM)
                for r0 in range(tm // FOX_T):
                    for s0 in range(0, cw, LANES):
                        piece = y[r0 * FOX_T:(r0 + 1) * FOX_T, s0:s0 + LANES]
                        o_ref[r0, c0 + s0:c0 + s0 + LANES, :] = piece.T.astype(o_ref.dtype)
            elif kind == "fox_k":
                norms = with_norms(norms, y, 1, c0 // HEAD_DIM)
                o_ref[:, c0:c0 + cw] = y.astype(o_ref.dtype)
            else:
                o_ref[:, c0:c0 + cw] = y.astype(o_ref.dtype)
        col += width
    if track_norms:
        out_refs[-1][...] = norms


def _norm_proj(x, g, weights, out_widths, out_dtypes, out_kinds=None, *, w_transposed=False, tables=None,
               name):
    t_rows, d = x.shape
    out_kinds = tuple(out_kinds or ["plain"] * len(out_widths))
    w_arrays = [w[0] if isinstance(w, tuple) else w for w in weights]
    w_blocks = [(w[1], d) if isinstance(w, tuple) else w.shape for w in weights]
    out_axis = 0 if w_transposed else 1
    assert sum(out_widths) == sum(blk[out_axis] for blk in w_blocks) and t_rows % PROJ_TM == 0
    tiles_per_seq = SEQ // PROJ_TM
    in_specs = [
        pl.BlockSpec((PROJ_TM, d), lambda i: (i, 0)),
        pl.BlockSpec((1, d), lambda i: (0, 0)),
    ]
    in_specs += [pl.BlockSpec(blk, lambda i: (0, 0), pipeline_mode=pl.Buffered(1)) for blk in w_blocks]
    args = [x, g.reshape(1, d), *w_arrays]
    if "rot" in out_kinds:
        for tab in tables:
            in_specs.append(pl.BlockSpec((PROJ_TM, LANES), lambda i: (i % tiles_per_seq, 0)))
            args.append(tab)
    out_specs, out_shape = [], []
    for wd, dt, kind in zip(out_widths, out_dtypes, out_kinds):
        if kind in ("kt", "fox_q"):
            kt_per_tile = PROJ_TM // FOX_T
            out_specs.append(pl.BlockSpec((None, kt_per_tile, wd, FOX_T),
                                          lambda i: (i // tiles_per_seq, i % tiles_per_seq, 0, 0)))
            out_shape.append(jax.ShapeDtypeStruct((t_rows // SEQ, SEQ // FOX_T, wd, FOX_T), dt))
        else:
            out_specs.append(pl.BlockSpec((PROJ_TM, wd), lambda i: (i, 0)))
            out_shape.append(jax.ShapeDtypeStruct((t_rows, wd), dt))
    if "fox_q" in out_kinds:
        out_specs.append(pl.BlockSpec((None, 8, LANES), lambda i: (i, 0, 0)))
        out_shape.append(jax.ShapeDtypeStruct((t_rows // PROJ_TM, 8, LANES), f32))
    return pl.pallas_call(
        functools.partial(_norm_proj_kernel, n_weights=len(weights), w_transposed=w_transposed,
                          out_widths=tuple(out_widths), out_kinds=out_kinds),
        grid=(t_rows // PROJ_TM,),
        in_specs=in_specs,
        out_specs=out_specs,
        out_shape=out_shape,
        compiler_params=pltpu.CompilerParams(dimension_semantics=("arbitrary",),
                                             vmem_limit_bytes=VMEM_LIMIT),
        name=name,
    )(*args)


def _fox_gate_kernel(f_ref, bf_ref, o_ref, ot_ref):
    blk = LANES
    row = lax.broadcasted_iota(jnp.int32, (blk, blk), 0)
    coli = lax.broadcasted_iota(jnp.int32, (blk, blk), 1)
    tri = jnp.where(row >= coli, 1.0, 0.0).astype(bf16)
    spread = [jnp.where((coli == BIAS_TERMS * row + t) & (row < N_MAIN_HEADS), 1.0, 0.0).astype(bf16)
              for t in range(BIAS_TERMS)]
    carry = jnp.zeros((1, F_PAD), f32)
    for b0 in range(0, SEQ, blk):
        xg = f_ref[b0:b0 + blk, :] + bf_ref[...]
        ls = jnp.minimum(xg, 0.0) - jnp.log1p(jnp.exp(-jnp.abs(xg)))
        cs = sum(jnp.dot(tri, term, preferred_element_type=f32) for term in _split_bf16(ls)) + carry
        carry = cs[blk - 1:blk, :]
        bias = cs * (-LOG2E)
        nb = sum(jnp.dot(term, sp, preferred_element_type=f32)
                 for term, sp in zip(_split_bf16(bias), spread))
        o_ref[b0:b0 + blk, :] = nb.astype(o_ref.dtype)
        ot_ref[:, b0:b0 + blk] = bias.T[:ot_ref.shape[0], :]


def _fox_gate(f, b_f):
    bf_pad = jnp.zeros((1, F_PAD), f32).at[0, :N_MAIN_HEADS].set(b_f.astype(f32))
    return pl.pallas_call(
        _fox_gate_kernel,
        grid=(BATCH,),
        in_specs=[pl.BlockSpec((SEQ, F_PAD), lambda b: (b, 0)),
                  pl.BlockSpec((1, F_PAD), lambda b: (0, 0))],
        out_specs=[pl.BlockSpec((SEQ, LANES), lambda b: (b, 0)),
                   pl.BlockSpec((None, GATE_HEAD_ROWS, SEQ), lambda b: (b, 0, 0))],
        out_shape=[jax.ShapeDtypeStruct((BATCH * SEQ, LANES), bf16),
                   jax.ShapeDtypeStruct((BATCH, GATE_HEAD_ROWS, SEQ), f32)],
        compiler_params=pltpu.CompilerParams(dimension_semantics=("arbitrary",)),
        name="fox_gate",
    )(f, bf_pad)


def _fox_attn_kernel(qt_ref, k_ref, vt_ref, nb_ref, z_ref, nbq_ref, norm_ref, o_ref,
                     m_ref, acc_ref, u_ref, p_ref, a_ref):
    t = FOX_T
    nt = SEQ // t
    head0 = pl.program_id(1) * FOX_HEADS_PER_STEP
    heads = range(FOX_HEADS_PER_STEP)
    col = [slice(hd * HEAD_DIM, (hd + 1) * HEAD_DIM) for hd in heads]
    sub = lax.broadcasted_iota(jnp.int32, (LANES, t), 0)
    sel = []
    for hd in heads:
        lo = BIAS_TERMS * (head0 + hd)
        sel.append(jnp.where((sub >= lo) & (sub < lo + BIAS_TERMS), 1.0, 0.0).astype(bf16))
    ones_rows = jnp.ones((FOX_SUM_ROWS, t), bf16)

    def tile_rows(i):
        return pl.ds(pl.multiple_of(i * t, t), t)

    def score(qt, kt, hd):
        k_aug = jnp.concatenate([k_ref[tile_rows(kt), col[hd]], nb_ref[tile_rows(kt), :]], axis=1)
        q_aug_t = jnp.concatenate([qt_ref[qt, col[hd], :], sel[hd]], axis=0)
        return jnp.dot(k_aug, q_aug_t, preferred_element_type=f32)

    def scores(qt, kt):
        return [score(qt, kt, hd) for hd in heads]

    def pv1(kt, p_hd, hd):
        return jnp.dot(jnp.concatenate([vt_ref[kt, col[hd], :], ones_rows], axis=0), p_hd,
                       preferred_element_type=f32)

    def pv(kt, p):
        return [pv1(kt, p[hd], hd) for hd in heads]

    def finalize(qt):
        for hd in heads:
            acc = acc_ref[qt, hd]
            out_t = acc[:HEAD_DIM, :] * (1.0 / acc[HEAD_DIM:HEAD_DIM + 1, :])
            gate = _silu(z_ref[tile_rows(qt), col[hd]].astype(f32))
            o_ref[tile_rows(qt), col[hd]] = (out_t.T * gate).astype(o_ref.dtype)

    def next_item(qt, kt):
        wrap = kt + 1 >= qt
        return jnp.minimum(jnp.where(wrap, qt + 1, qt), nt - 1), jnp.where(wrap, 0, kt + 1)

    def load_list(ref):
        return [ref[hd] for hd in heads]

    def store_list(ref, vals):
        for hd in heads:
            ref[hd] = vals[hd]

    def run(online):
        def numerators(qt, u, masked):
            if masked:
                causal = (lax.broadcasted_iota(jnp.int32, (t, t), 0)
                          <= lax.broadcasted_iota(jnp.int32, (t, t), 1))
            p, alpha = [], []
            for hd in heads:
                uh = jnp.where(causal, u[hd], NEG) if masked else u[hd]
                m_old = m_ref[qt, hd]
                if online:
                    m_new = jnp.maximum(m_old, jnp.max(uh, axis=0, keepdims=True))
                    alpha.append(jnp.exp2(m_old - m_new))
                    m_ref[qt, hd] = m_new
                else:
                    m_new = m_old
                p.append(jnp.exp2(uh - m_new).astype(bf16))
            return p, alpha

        def accumulate(qt, alpha, pv_vals):
            for hd in heads:
                prev = alpha[hd] * acc_ref[qt, hd] if online else acc_ref[qt, hd]
                acc_ref[qt, hd] = prev + pv_vals[hd]

        acc_ref[...] = jnp.zeros(acc_ref.shape, f32)
        p_ref[...] = jnp.zeros(p_ref.shape, bf16)
        if online:
            m_ref[...] = jnp.full(m_ref.shape, NEG, f32)
            a_ref[...] = jnp.ones(a_ref.shape, f32)

        n_items = nt * (nt - 1) // 2
        ni = FOX_ITEMS_PER_ITER_ONLINE if online else FOX_ITEMS_PER_ITER
        assert n_items % ni == 0
        store_list(u_ref, scores(1, 0))

        def pass1(_, carry):
            q0, k0, q_prev, k_prev = carry
            items = [(q0, k0)]
            for _k in range(ni):
                items.append(next_item(*items[-1]))
            u_cur = load_list(u_ref)
            pv_vals = pv(k_prev, load_list(p_ref))
            u_next = scores(*items[1])
            p_cur, al_cur = numerators(q0, u_cur, False)
            accumulate(q_prev, load_list(a_ref) if online else None, pv_vals)
            for k in range(1, ni):
                u_cur, u_next, pv_vals = u_next, [], []
                for hd in heads:
                    s = score(items[k + 1][0], items[k + 1][1], hd)
                    if k == ni - 1:
                        u_ref[hd] = s
                    else:
                        u_next.append(s)
                    pv_vals.append(pv1(items[k - 1][1], p_cur[hd], hd))
                p_new, al_new = numerators(items[k][0], u_cur, False)
                accumulate(items[k - 1][0], al_cur, pv_vals)
                p_cur, al_cur = p_new, al_new
            store_list(p_ref, p_cur)
            if online:
                store_list(a_ref, al_cur)
            return items[ni] + items[ni - 1]

        one, zero = jnp.int32(1), jnp.int32(0)
        _, _, q_prev, k_prev = lax.fori_loop(0, n_items // ni, pass1, (one, zero, one, zero))
        accumulate(q_prev, load_list(a_ref) if online else None, pv(k_prev, load_list(p_ref)))

        assert nt % 2 == 0
        store_list(u_ref, scores(0, 0))

        def pass2(i, carry):
            ta = 2 * i
            tb = ta + 1
            u_b = scores(tb, tb)
            p_a, al_a = numerators(ta, load_list(u_ref), True)
            pv_a = pv(ta, p_a)
            tn = jnp.minimum(ta + 2, nt - 1)
            u_n = scores(tn, tn)
            p_b, al_b = numerators(tb, u_b, True)
            accumulate(ta, al_a, pv_a)
            pv_b = pv(tb, p_b)
            finalize(ta)
            store_list(u_ref, u_n)
            accumulate(tb, al_b, pv_b)
            finalize(tb)
            return carry

        lax.fori_loop(0, nt // 2, pass2, 0)

    norms = jnp.max(norm_ref[...], axis=0)
    hlane = lax.broadcasted_iota(jnp.int32, (1, LANES), 1)
    for hd in heads:
        pick = hlane == head0 + hd
        q2 = jnp.max(jnp.where(pick, norms[0:1, :], 0.0), axis=1, keepdims=True)
        k2 = jnp.max(jnp.where(pick, norms[1:2, :], 0.0), axis=1, keepdims=True)
        qk_bound = jnp.sqrt(q2 * k2) * FOX_BOUND_MARGIN + 1.0
        for qt in range(nt):
            m_ref[qt, hd] = nbq_ref[hd, qt:qt + 1, :] + qk_bound

    run(online=False)

    denom_min = jnp.min(acc_ref[:, :, HEAD_DIM:HEAD_DIM + 1, :])

    @pl.when(jnp.logical_not(denom_min >= FOX_MIN_DENOM))
    def _():
        run(online=True)


def _fox_attention(q, k, vt, nb, z, nbq, norms):
    hw = FOX_HEADS_PER_STEP * HEAD_DIM
    return pl.pallas_call(
        _fox_attn_kernel,
        grid=(BATCH, N_MAIN_HEADS // FOX_HEADS_PER_STEP),
        in_specs=[
            pl.BlockSpec((None, SEQ // FOX_T, hw, FOX_T), lambda b, p: (b, 0, p, 0)),
            pl.BlockSpec((SEQ, hw), lambda b, p: (b, p)),
            pl.BlockSpec((None, SEQ // FOX_T, hw, FOX_T), lambda b, p: (b, 0, p, 0)),
            pl.BlockSpec((SEQ, LANES), lambda b, p: (b, 0)),
            pl.BlockSpec((SEQ, hw), lambda b, p: (b, p)),
            pl.BlockSpec((None, FOX_HEADS_PER_STEP, SEQ // FOX_T, FOX_T), lambda b, p: (b, p, 0, 0)),
            pl.BlockSpec((SEQ // PROJ_TM, 8, LANES), lambda b, p: (b, 0, 0)),
        ],
        out_specs=pl.BlockSpec((SEQ, hw), lambda b, p: (b, p)),
        out_shape=jax.ShapeDtypeStruct((BATCH * SEQ, MAIN_W), bf16),
        scratch_shapes=[
            pltpu.VMEM((SEQ // FOX_T, FOX_HEADS_PER_STEP, 1, FOX_T), f32),
            pltpu.VMEM((SEQ // FOX_T, FOX_HEADS_PER_STEP, HEAD_DIM + FOX_SUM_ROWS, FOX_T), f32),
            pltpu.VMEM((FOX_HEADS_PER_STEP, FOX_T, FOX_T), f32),
            pltpu.VMEM((FOX_HEADS_PER_STEP, FOX_T, FOX_T), bf16),
            pltpu.VMEM((FOX_HEADS_PER_STEP, 1, FOX_T), f32),
        ],
        compiler_params=pltpu.CompilerParams(dimension_semantics=("arbitrary", "arbitrary"),
                                             vmem_limit_bytes=VMEM_LIMIT),
        name="fox_attention",
    )(q, k, vt, nb, z, nbq, norms)


def _retention_kernel(q_ref, k_ref, v_ref, z_ref, d_ref, xi_ref, zeta_ref, g_ref, o_ref, r_ref):
    c_len = RET_CHUNK
    r_ref[...] = jnp.zeros(r_ref.shape, f32)
    lane = lax.broadcasted_iota(jnp.int32, (c_len, LANES), 1)
    head_mask = (lane < RET_KEY_DIM, lane >= RET_KEY_DIM)
    pairs = range(RET_PAIRS_PER_STEP)
    chunks = range(RET_CHUNKS_PER_ITER)

    def body(it, carry):
        rows = [pl.ds(pl.multiple_of((it * RET_CHUNKS_PER_ITER + ci) * c_len, c_len), c_len) for ci in chunks]
        qa, s, upd = {}, {}, {}
        for ci in chunks:
            for pr in pairs:
                q2 = q_ref[rows[ci], pr * LANES:(pr + 1) * LANES]
                k2 = k_ref[rows[ci], pr * LANES:(pr + 1) * LANES]
                v2 = v_ref[rows[ci], pr * 2 * HEAD_DIM:(pr + 1) * 2 * HEAD_DIM]
                qa[ci, pr] = jnp.concatenate([jnp.where(head_mask[hd], q2, jnp.zeros_like(q2))
                                              for hd in range(2)], axis=0)
                s[ci, pr] = lax.dot_general(qa[ci, pr], k2, (((1,), (1,)), ((), ())),
                                            preferred_element_type=f32)
                kz = (k2.astype(f32) * zeta_ref[pr]).astype(bf16)
                upd[ci, pr] = lax.dot_general(kz, v2, (((0,), (0,)), ((), ())), preferred_element_type=f32)
        r_b = {}
        for pr in pairs:
            r_state = r_ref[pr]
            for ci in chunks:
                r_b[ci, pr] = r_state.astype(bf16)
                r_state = r_state * g_ref[pr] + upd[ci, pr]
            r_ref[pr] = r_state
        o = {}
        for ci in chunks:
            for pr in pairs:
                for hd in range(2):
                    hrows = slice(hd * c_len, (hd + 1) * c_len)
                    cols = slice((2 * pr + hd) * HEAD_DIM, (2 * pr + hd + 1) * HEAD_DIM)
                    inner = (s[ci, pr][hrows, :] * d_ref[2 * pr + hd]).astype(bf16)
                    intra = jnp.dot(inner, v_ref[rows[ci], cols], preferred_element_type=f32)
                    cross = jnp.dot(qa[ci, pr][hrows, :], r_b[ci, pr][:, hd * HEAD_DIM:(hd + 1) * HEAD_DIM],
                                    preferred_element_type=f32)
                    o[ci, pr, hd] = intra + cross * xi_ref[2 * pr + hd]
        for ci in chunks:
            for pr in pairs:
                for hd in range(2):
                    cols = slice((2 * pr + hd) * HEAD_DIM, (2 * pr + hd + 1) * HEAD_DIM)
                    oh = o[ci, pr, hd]
                    ms = jnp.mean(oh * oh, axis=-1, keepdims=True)
                    on = oh * lax.rsqrt(ms + EPS)
                    zg = _silu(z_ref[rows[ci], cols].astype(f32))
                    o_ref[rows[ci], cols] = (on * zg).astype(o_ref.dtype)
        return carry

    lax.fori_loop(0, SEQ // (c_len * RET_CHUNKS_PER_ITER), body, 0)


def _retention_tables():
    h, c_len = N_MAIN_HEADS, RET_CHUNK
    lg = np.log1p(-np.exp2(-5.0 - np.arange(h, dtype=np.float64)))
    n = np.arange(c_len, dtype=np.float64)
    diff = n[:, None] - n[None, :]
    d_inner = np.where(diff[None] >= 0, np.exp(lg[:, None, None] * np.maximum(diff, 0.0)[None]), 0.0)
    xi = np.exp(lg[:, None] * (n[None, :] + 1.0))
    zeta = np.exp(lg[:, None] * (c_len - 1.0 - n[None, :]))
    g_chunk = np.exp(lg * c_len)
    xi_b = np.broadcast_to(xi[:, :, None], (h, c_len, HEAD_DIM))
    zeta2 = np.repeat(zeta.reshape(h // 2, 2, c_len).transpose(0, 2, 1), RET_KEY_DIM, axis=-1)
    g2 = np.broadcast_to(np.repeat(g_chunk.reshape(h // 2, 2), RET_KEY_DIM, axis=-1)[:, :, None],
                         (h // 2, 2 * RET_KEY_DIM, 2 * HEAD_DIM))
    return tuple(jnp.asarray(np.ascontiguousarray(t), dtype=f32) for t in (d_inner, xi_b, zeta2, g2))


def _retention(qk, v, z):
    n_pairs = N_MAIN_HEADS // 2
    pp = RET_PAIRS_PER_STEP
    steps = n_pairs // pp
    d_inner, xi_b, zeta2, g2 = _retention_tables()
    c_len = RET_CHUNK
    return pl.pallas_call(
        _retention_kernel,
        grid=(BATCH, steps),
        in_specs=[
            pl.BlockSpec((SEQ, pp * LANES), lambda b, p: (b, p)),
            pl.BlockSpec((SEQ, pp * LANES), lambda b, p: (b, steps + p)),
            pl.BlockSpec((SEQ, pp * 2 * HEAD_DIM), lambda b, p: (b, p)),
            pl.BlockSpec((SEQ, pp * 2 * HEAD_DIM), lambda b, p: (b, p)),
            pl.BlockSpec((2 * pp, c_len, c_len), lambda b, p: (p, 0, 0)),
            pl.BlockSpec((2 * pp, c_len, HEAD_DIM), lambda b, p: (p, 0, 0)),
            pl.BlockSpec((pp, c_len, LANES), lambda b, p: (p, 0, 0)),
            pl.BlockSpec((pp, LANES, 2 * HEAD_DIM), lambda b, p: (p, 0, 0)),
        ],
        out_specs=pl.BlockSpec((SEQ, pp * 2 * HEAD_DIM), lambda b, p: (b, p)),
        out_shape=jax.ShapeDtypeStruct((BATCH * SEQ, MAIN_W), bf16),
        scratch_shapes=[pltpu.VMEM((pp, LANES, 2 * HEAD_DIM), f32)],
        compiler_params=pltpu.CompilerParams(dimension_semantics=("arbitrary", "arbitrary"),
                                             vmem_limit_bytes=VMEM_LIMIT),
        name="retention",
    )(qk, qk, v, z, d_inner, xi_b, zeta2, g2)


def _out_proj_kernel(om_ref, qm_ref, zm_ref, kv_ref, w_ref, x_ref, g_ref, o_ref, *, final_norm):
    scale = 1.0 / math.sqrt(HEAD_DIM)
    heads = range(N_MEM_HEADS)
    col = [slice(hd * HEAD_DIM, (hd + 1) * HEAD_DIM) for hd in heads]
    half = D_MODEL // 2
    s = [lax.dot_general(qm_ref[:, col[hd]], kv_ref[:, col[hd]], (((1,), (1,)), ((), ())),
                         preferred_element_type=f32) * scale for hd in heads]
    y_lo = jnp.dot(om_ref[...], w_ref[:MAIN_W, :half], preferred_element_type=f32)
    p = []
    for hd in heads:
        e = jnp.exp(s[hd] - jnp.max(s[hd], axis=-1, keepdims=True))
        p.append((e * (1.0 / jnp.sum(e, axis=-1, keepdims=True))).astype(bf16))
    memo = [jnp.dot(p[hd], kv_ref[:, MEM_W + hd * HEAD_DIM:MEM_W + (hd + 1) * HEAD_DIM],
                    preferred_element_type=f32) for hd in heads]
    y_hi = jnp.dot(om_ref[...], w_ref[:MAIN_W, half:], preferred_element_type=f32)
    og = jnp.concatenate([(memo[hd] * _silu(zm_ref[:, col[hd]].astype(f32))).astype(bf16) for hd in heads],
                         axis=1)
    y = jnp.concatenate([y_lo, y_hi], axis=1) + jnp.dot(og, w_ref[MAIN_W:, :], preferred_element_type=f32)
    xn = x_ref[...] + y
    if final_norm:
        xn = _rmsnorm_rows(xn, g_ref[...])
    o_ref[...] = xn


def _out_proj(o_main, qm, z, kv, w_out, x, g_final, *, final_norm):
    t_rows = x.shape[0]
    tiles_per_seq = SEQ // OUT_TM
    z_blk = MAIN_W // MEM_W
    return pl.pallas_call(
        functools.partial(_out_proj_kernel, final_norm=final_norm),
        grid=(t_rows // OUT_TM,),
        in_specs=[
            pl.BlockSpec((OUT_TM, MAIN_W), lambda i: (i, 0)),
            pl.BlockSpec((OUT_TM, MEM_W), lambda i: (i, 0)),
            pl.BlockSpec((OUT_TM, MEM_W), lambda i: (i, z_blk)),
            pl.BlockSpec((N_MEM, 2 * MEM_W), lambda i: (i // tiles_per_seq, 0)),
            pl.BlockSpec((INNER, D_MODEL), lambda i: (0, 0), pipeline_mode=pl.Buffered(1)),
            pl.BlockSpec((OUT_TM, D_MODEL), lambda i: (i, 0)),
            pl.BlockSpec((1, D_MODEL), lambda i: (0, 0)),
        ],
        out_specs=pl.BlockSpec((OUT_TM, D_MODEL), lambda i: (i, 0)),
        out_shape=jax.ShapeDtypeStruct((t_rows, D_MODEL), f32),
        compiler_params=pltpu.CompilerParams(dimension_semantics=("arbitrary",),
                                             vmem_limit_bytes=VMEM_LIMIT),
        name="out_proj_final" if final_norm else "out_proj",
    )(o_main, qm, z, kv, w_out, x, g_final.reshape(1, D_MODEL))


def _rotary_tables():
    half = RET_KEY_DIM // 2
    pos = np.arange(SEQ, dtype=np.float64)
    inv = 1.0 / (ROPE_BASE ** (np.arange(half, dtype=np.float64) / half))
    ang = pos[:, None] * inv[None, :]
    cos, sin = np.cos(ang), np.sin(ang)
    reps = LANES // RET_KEY_DIM
    cos_t = np.tile(np.concatenate([cos, cos], axis=-1), (1, reps))
    sin_t = np.tile(np.concatenate([-sin, sin], axis=-1), (1, reps))
    k_scale = RET_KEY_DIM ** -0.5
    return tuple(jnp.asarray(t, dtype=f32) for t in (cos_t, sin_t, cos_t * k_scale, sin_t * k_scale))


def kernel(x, mem, norm_g, fox_w_in, fox_b_f, ret_w_in, mem_norm_g, w_mem_kv, w_out, final_norm_g):
    t_rows = BATCH * SEQ
    x2 = x.reshape(t_rows, D_MODEL)

    kv0, kv1 = _norm_proj(mem.reshape(BATCH * N_MEM, D_MODEL), mem_norm_g,
                          [w_mem_kv[0].astype(bf16), w_mem_kv[1].astype(bf16)],
                          [2 * MEM_W, 2 * MEM_W], [bf16, bf16], name="mem_kv_proj")

    o_f = 3 * MAIN_W
    o_qm = o_f + N_MAIN_HEADS
    w0t = jnp.swapaxes(fox_w_in[0], 0, 1).astype(bf16)
    w_qmz = w0t[o_qm:]
    w_f = jnp.pad(w0t[o_f:o_qm], ((0, F_PAD - N_MAIN_HEADS), (0, 0)))
    q, k, vt, qm, z, f, norms = _norm_proj(
        x2, norm_g[0], [(w0t, o_f), w_qmz, w_f], [MAIN_W, MAIN_W, MAIN_W, MEM_W, INNER, F_PAD],
        [bf16, bf16, bf16, bf16, bf16, f32], ["fox_q", "fox_k", "kt", "plain", "plain", "plain"],
        w_transposed=True, name="fox_in_proj")
    nb, nbq = _fox_gate(f, fox_b_f[0])
    nbq = nbq.reshape(BATCH, GATE_HEAD_ROWS, SEQ // FOX_T, FOX_T)
    o_main = _fox_attention(q, k, vt, nb, z, nbq, norms)
    x2 = _out_proj(o_main, qm, z, kv0, w_out[0].astype(bf16), x2, final_norm_g, final_norm=False)

    w1 = ret_w_in[0].astype(bf16)
    qk, v, qm, z = _norm_proj(
        x2, norm_g[1], [w1], [2 * RET_QK_W, MAIN_W, MEM_W, INNER], [bf16] * 4,
        ["rot", "plain", "plain", "plain"], tables=_rotary_tables(), name="ret_in_proj")
    o_main = _retention(qk, v, z)
    out = _out_proj(o_main, qm, z, kv1, w_out[1].astype(bf16), x2, final_norm_g, final_norm=True)
    return out.reshape(BATCH, SEQ, D_MODEL)
```

```python
import functools
import math

import jax
import jax.numpy as jnp
import numpy as np
from jax import lax
from jax.experimental import pallas as pl
from jax.experimental.pallas import tpu as pltpu

D_MODEL = 1024
BATCH = 8
SEQ = 2048
HEAD_DIM = 128
N_MAIN_HEADS = 12
N_MEM_HEADS = 4
N_MEM = 256
MAIN_W = N_MAIN_HEADS * HEAD_DIM
MEM_W = N_MEM_HEADS * HEAD_DIM
INNER = MAIN_W + MEM_W
RET_KEY_DIM = HEAD_DIM // 2
RET_QK_W = N_MAIN_HEADS * RET_KEY_DIM
RET_CHUNK = 128
ROPE_BASE = 10000.0
EPS = 1e-6
NEG = -1e30
LOG2E = 1.4426950408889634

LANES = 128
F_PAD = LANES
VMEM_LIMIT = 56 * 1024 * 1024

PROJ_TM = 512
PROJ_TN = 512
FOX_T = 256
FOX_HEADS_PER_STEP = 6
FOX_ITEMS_PER_ITER = 7
FOX_ITEMS_PER_ITER_ONLINE = 2
FOX_DIAG_TILES_PER_ITER = 4
BIAS_TERMS = 3
FOX_SUM_ROWS = 16
FOX_Q_SCALE = LOG2E / math.sqrt(HEAD_DIM)
FOX_BOUND_MARGIN = 1.02
FOX_MIN_DENOM = 2.0 ** -90
GATE_HEAD_ROWS = 16
RET_PAIRS_PER_STEP = 3
RET_CHUNKS_PER_ITER = 2
OUT_TM = 512

f32 = jnp.float32
bf16 = jnp.bfloat16


def _silu(z):
    h = 0.5 * z
    return h + h * jnp.tanh(h)


def _rmsnorm_rows(x, g):
    ms = jnp.mean(x * x, axis=-1, keepdims=True)
    return (x * lax.rsqrt(ms + EPS)) * g


def _split_bf16(v):
    hi = v.astype(bf16)
    r1 = v - hi.astype(f32)
    mid = r1.astype(bf16)
    lo = (r1 - mid.astype(f32)).astype(bf16)
    return hi, mid, lo


def _norm_proj_kernel(*refs, n_weights, w_transposed, out_widths, out_kinds):
    track_norms = "fox_q" in out_kinds
    n_out = len(out_widths) + (1 if track_norms else 0)
    x_ref, g_ref = refs[:2]
    w_refs = refs[2:2 + n_weights]
    tab_refs = refs[2 + n_weights:len(refs) - n_out]
    out_refs = refs[len(refs) - n_out:]
    h = _rmsnorm_rows(x_ref[...], g_ref[...]).astype(bf16)
    tm = h.shape[0]
    if "rot" in out_kinds:
        lane = lax.broadcasted_iota(jnp.int32, (tm, LANES), 1)
        first_half = (lane % RET_KEY_DIM) < (RET_KEY_DIM // 2)
    if track_norms:
        norm_sub = lax.broadcasted_iota(jnp.int32, (8, LANES), 0)
        norm_lane = lax.broadcasted_iota(jnp.int32, (8, LANES), 1)
        norms = jnp.zeros((8, LANES), f32)

        def with_norms(norms, y, row, head0):
            for s0 in range(0, y.shape[1], HEAD_DIM):
                ys = y[:, s0:s0 + HEAD_DIM]
                worst = jnp.max(jnp.sum(ys * ys, axis=1, keepdims=True), axis=0, keepdims=True)
                norms = jnp.where((norm_sub == row) & (norm_lane == head0 + s0 // HEAD_DIM), worst, norms)
            return norms
    out_axis = 0 if w_transposed else 1
    wi, col = 0, 0
    for o_ref, width, kind in zip(out_refs, out_widths, out_kinds):
        if col == w_refs[wi].shape[out_axis]:
            wi, col = wi + 1, 0
        w_ref = w_refs[wi]
        for c0 in range(0, width, PROJ_TN):
            cw = min(PROJ_TN, width - c0)
            if w_transposed:
                y = lax.dot_general(h, w_ref[col + c0:col + c0 + cw, :], (((1,), (1,)), ((), ())),
                                    preferred_element_type=f32)
            else:
                y = jnp.dot(h, w_ref[:, col + c0:col + c0 + cw], preferred_element_type=f32)
            if kind == "rot":
                for s0 in range(0, cw, LANES):
                    ys = y[:, s0:s0 + LANES]
                    sw = jnp.where(first_half, pltpu.roll(ys, LANES - RET_KEY_DIM // 2, 1),
                                   pltpu.roll(ys, RET_KEY_DIM // 2, 1))
                    cos_ref, sin_ref = tab_refs[:2] if c0 + s0 < RET_QK_W else tab_refs[2:]
                    r = ys * cos_ref[...] + sw * sin_ref[...]
                    o_ref[:, c0 + s0:c0 + s0 + LANES] = r.astype(o_ref.dtype)
            elif kind in ("kt", "fox_q"):
                if kind == "fox_q":
                    y = y * FOX_Q_SCALE
                    norms = with_norms(norms, y, 0, c0 // HEAD_DIM)
                for r0 in range(tm // FOX_T):
                    for s0 in range(0, cw, LANES):
                        piece = y[r0 * FOX_T:(r0 + 1) * FOX_T, s0:s0 + LANES]
                        o_ref[r0, c0 + s0:c0 + s0 + LANES, :] = piece.T.astype(o_ref.dtype)
            elif kind == "fox_k":
                norms = with_norms(norms, y, 1, c0 // HEAD_DIM)
                o_ref[:, c0:c0 + cw] = y.astype(o_ref.dtype)
            else:
                o_ref[:, c0:c0 + cw] = y.astype(o_ref.dtype)
        col += width
    if track_norms:
        out_refs[-1][...] = norms


def _norm_proj(x, g, weights, out_widths, out_dtypes, out_kinds=None, *, w_transposed=False, tables=None,
               name):
    t_rows, d = x.shape
    out_kinds = tuple(out_kinds or ["plain"] * len(out_widths))
    w_arrays = [w[0] if isinstance(w, tuple) else w for w in weights]
    w_blocks = [(w[1], d) if isinstance(w, tuple) else w.shape for w in weights]
    out_axis = 0 if w_transposed else 1
    assert sum(out_widths) == sum(blk[out_axis] for blk in w_blocks) and t_rows % PROJ_TM == 0
    tiles_per_seq = SEQ // PROJ_TM
    in_specs = [
        pl.BlockSpec((PROJ_TM, d), lambda i: (i, 0)),
        pl.BlockSpec((1, d), lambda i: (0, 0)),
    ]
    in_specs += [pl.BlockSpec(blk, lambda i: (0, 0), pipeline_mode=pl.Buffered(1)) for blk in w_blocks]
    args = [x, g.reshape(1, d), *w_arrays]
    if "rot" in out_kinds:
        for tab in tables:
            in_specs.append(pl.BlockSpec((PROJ_TM, LANES), lambda i: (i % tiles_per_seq, 0)))
            args.append(tab)
    out_specs, out_shape = [], []
    for wd, dt, kind in zip(out_widths, out_dtypes, out_kinds):
        if kind in ("kt", "fox_q"):
            kt_per_tile = PROJ_TM // FOX_T
            out_specs.append(pl.BlockSpec((None, kt_per_tile, wd, FOX_T),
                                          lambda i: (i // tiles_per_seq, i % tiles_per_seq, 0, 0)))
            out_shape.append(jax.ShapeDtypeStruct((t_rows // SEQ, SEQ // FOX_T, wd, FOX_T), dt))
        else:
            out_specs.append(pl.BlockSpec((PROJ_TM, wd), lambda i: (i, 0)))
            out_shape.append(jax.ShapeDtypeStruct((t_rows, wd), dt))
    if "fox_q" in out_kinds:
        out_specs.append(pl.BlockSpec((None, 8, LANES), lambda i: (i, 0, 0)))
        out_shape.append(jax.ShapeDtypeStruct((t_rows // PROJ_TM, 8, LANES), f32))
    return pl.pallas_call(
        functools.partial(_norm_proj_kernel, n_weights=len(weights), w_transposed=w_transposed,
                          out_widths=tuple(out_widths), out_kinds=out_kinds),
        grid=(t_rows // PROJ_TM,),
        in_specs=in_specs,
        out_specs=out_specs,
        out_shape=out_shape,
        compiler_params=pltpu.CompilerParams(dimension_semantics=("arbitrary",),
                                             vmem_limit_bytes=VMEM_LIMIT),
        name=name,
    )(*args)


def _fox_gate_kernel(f_ref, bf_ref, o_ref, ot_ref):
    blk = LANES
    row = lax.broadcasted_iota(jnp.int32, (blk, blk), 0)
    coli = lax.broadcasted_iota(jnp.int32, (blk, blk), 1)
    tri = jnp.where(row >= coli, 1.0, 0.0).astype(bf16)
    spread = [jnp.where((coli == BIAS_TERMS * row + t) & (row < N_MAIN_HEADS), 1.0, 0.0).astype(bf16)
              for t in range(BIAS_TERMS)]
    carry = jnp.zeros((1, F_PAD), f32)
    for b0 in range(0, SEQ, blk):
        xg = f_ref[b0:b0 + blk, :] + bf_ref[...]
        ls = jnp.minimum(xg, 0.0) - jnp.log1p(jnp.exp(-jnp.abs(xg)))
        cs = sum(jnp.dot(tri, term, preferred_element_type=f32) for term in _split_bf16(ls)) + carry
        carry = cs[blk - 1:blk, :]
        bias = cs * (-LOG2E)
        nb = sum(jnp.dot(term, sp, preferred_element_type=f32)
                 for term, sp in zip(_split_bf16(bias), spread))
        o_ref[b0:b0 + blk, :] = nb.astype(o_ref.dtype)
        ot_ref[:, b0:b0 + blk] = bias.T[:ot_ref.shape[0], :]


def _fox_gate(f, b_f):
    bf_pad = jnp.zeros((1, F_PAD), f32).at[0, :N_MAIN_HEADS].set(b_f.astype(f32))
    return pl.pallas_call(
        _fox_gate_kernel,
        grid=(BATCH,),
        in_specs=[pl.BlockSpec((SEQ, F_PAD), lambda b: (b, 0)),
                  pl.BlockSpec((1, F_PAD), lambda b: (0, 0))],
        out_specs=[pl.BlockSpec((SEQ, LANES), lambda b: (b, 0)),
                   pl.BlockSpec((None, GATE_HEAD_ROWS, SEQ), lambda b: (b, 0, 0))],
        out_shape=[jax.ShapeDtypeStruct((BATCH * SEQ, LANES), bf16),
                   jax.ShapeDtypeStruct((BATCH, GATE_HEAD_ROWS, SEQ), f32)],
        compiler_params=pltpu.CompilerParams(dimension_semantics=("arbitrary",)),
        name="fox_gate",
    )(f, bf_pad)


def _fox_attn_kernel(qt_ref, k_ref, vt_ref, nb_ref, z_ref, nbq_ref, norm_ref, o_ref,
                     m_ref, acc_ref, u_ref, p_ref, a_ref):
    t = FOX_T
    nt = SEQ // t
    head0 = pl.program_id(1) * FOX_HEADS_PER_STEP
    heads = range(FOX_HEADS_PER_STEP)
    col = [slice(hd * HEAD_DIM, (hd + 1) * HEAD_DIM) for hd in heads]
    sub = lax.broadcasted_iota(jnp.int32, (LANES, t), 0)
    sel = []
    for hd in heads:
        lo = BIAS_TERMS * (head0 + hd)
        sel.append(jnp.where((sub >= lo) & (sub < lo + BIAS_TERMS), 1.0, 0.0).astype(bf16))
    ones_rows = jnp.ones((FOX_SUM_ROWS, t), bf16)

    def tile_rows(i):
        return pl.ds(pl.multiple_of(i * t, t), t)

    def score(qt, kt, hd):
        k_aug = jnp.concatenate([k_ref[tile_rows(kt), col[hd]], nb_ref[tile_rows(kt), :]], axis=1)
        q_aug_t = jnp.concatenate([qt_ref[qt, col[hd], :], sel[hd]], axis=0)
        return jnp.dot(k_aug, q_aug_t, preferred_element_type=f32)

    def scores(qt, kt):
        return [score(qt, kt, hd) for hd in heads]

    def pv1(kt, p_hd, hd):
        return jnp.dot(jnp.concatenate([vt_ref[kt, col[hd], :], ones_rows], axis=0), p_hd,
                       preferred_element_type=f32)

    def pv(kt, p):
        return [pv1(kt, p[hd], hd) for hd in heads]

    def finalize(qt):
        for hd in heads:
            acc = acc_ref[qt, hd]
            out_t = acc[:HEAD_DIM, :] * (1.0 / acc[HEAD_DIM:HEAD_DIM + 1, :])
            gate = _silu(z_ref[tile_rows(qt), col[hd]].astype(f32))
            o_ref[tile_rows(qt), col[hd]] = (out_t.T * gate).astype(o_ref.dtype)

    def next_item(qt, kt):
        wrap = kt + 1 >= qt
        return jnp.minimum(jnp.where(wrap, qt + 1, qt), nt - 1), jnp.where(wrap, 0, kt + 1)

    def load_list(ref):
        return [ref[hd] for hd in heads]

    def store_list(ref, vals):
        for hd in heads:
            ref[hd] = vals[hd]

    def run(online):
        def numerators(qt, u, masked):
            half = t // 2
            if masked:
                ksub = lax.broadcasted_iota(jnp.int32, (t if online else half,) * 2, 0)
                qlane = lax.broadcasted_iota(jnp.int32, (t if online else half,) * 2, 1)
                causal = ksub <= qlane
            p, alpha = [], []
            for hd in heads:
                m_old = m_ref[qt, hd]
                if online:
                    uh = jnp.where(causal, u[hd], NEG) if masked else u[hd]
                    m_new = jnp.maximum(m_old, jnp.max(uh, axis=0, keepdims=True))
                    alpha.append(jnp.exp2(m_old - m_new))
                    m_ref[qt, hd] = m_new
                    p.append(jnp.exp2(uh - m_new).astype(bf16))
                elif masked:
                    lo, hi = slice(0, half), slice(half, t)
                    p_ll = jnp.exp2(jnp.where(causal, u[hd][lo, lo], NEG) - m_old[:, lo])
                    p_lh = jnp.exp2(u[hd][lo, hi] - m_old[:, hi])
                    p_hh = jnp.exp2(jnp.where(causal, u[hd][hi, hi], NEG) - m_old[:, hi])
                    top = jnp.concatenate([p_ll, p_lh], axis=1)
                    bottom = jnp.concatenate([jnp.zeros_like(p_hh), p_hh], axis=1)
                    p.append(jnp.concatenate([top, bottom], axis=0).astype(bf16))
                else:
                    p.append(jnp.exp2(u[hd] - m_old).astype(bf16))
            return p, alpha

        def accumulate(qt, alpha, pv_vals):
            for hd in heads:
                prev = alpha[hd] * acc_ref[qt, hd] if online else acc_ref[qt, hd]
                acc_ref[qt, hd] = prev + pv_vals[hd]

        acc_ref[...] = jnp.zeros(acc_ref.shape, f32)
        p_ref[...] = jnp.zeros(p_ref.shape, bf16)
        if online:
            m_ref[...] = jnp.full(m_ref.shape, NEG, f32)
            a_ref[...] = jnp.ones(a_ref.shape, f32)

        n_items = nt * (nt - 1) // 2
        ni = FOX_ITEMS_PER_ITER_ONLINE if online else FOX_ITEMS_PER_ITER
        assert n_items % ni == 0
        store_list(u_ref, scores(1, 0))

        def pass1(_, carry):
            q0, k0, q_prev, k_prev = carry
            items = [(q0, k0)]
            for _k in range(ni):
                items.append(next_item(*items[-1]))
            u_cur = load_list(u_ref)
            pv_vals = pv(k_prev, load_list(p_ref))
            u_next = scores(*items[1])
            p_cur, al_cur = numerators(q0, u_cur, False)
            accumulate(q_prev, load_list(a_ref) if online else None, pv_vals)
            for k in range(1, ni):
                u_cur, u_next, pv_vals = u_next, [], []
                for hd in heads:
                    s = score(items[k + 1][0], items[k + 1][1], hd)
                    if k == ni - 1:
                        u_ref[hd] = s
                    else:
                        u_next.append(s)
                    pv_vals.append(pv1(items[k - 1][1], p_cur[hd], hd))
                p_new, al_new = numerators(items[k][0], u_cur, False)
                accumulate(items[k - 1][0], al_cur, pv_vals)
                p_cur, al_cur = p_new, al_new
            store_list(p_ref, p_cur)
            if online:
                store_list(a_ref, al_cur)
            return items[ni] + items[ni - 1]

        one, zero = jnp.int32(1), jnp.int32(0)
        _, _, q_prev, k_prev = lax.fori_loop(0, n_items // ni, pass1, (one, zero, one, zero))
        accumulate(q_prev, load_list(a_ref) if online else None, pv(k_prev, load_list(p_ref)))

        nd = FOX_ITEMS_PER_ITER_ONLINE if online else FOX_DIAG_TILES_PER_ITER
        assert nt % nd == 0
        store_list(u_ref, scores(0, 0))

        def pass2(i, carry):
            u_cur = load_list(u_ref)
            for k in range(nd):
                tile = nd * i + k
                nxt = jnp.minimum(tile + 1, nt - 1)
                u_next = scores(nxt, nxt)
                p_cur, al_cur = numerators(tile, u_cur, True)
                accumulate(tile, al_cur, pv(tile, p_cur))
                finalize(tile)
                u_cur = u_next
            store_list(u_ref, u_cur)
            return carry

        lax.fori_loop(0, nt // nd, pass2, 0)

    norms = jnp.max(norm_ref[...], axis=0)
    hlane = lax.broadcasted_iota(jnp.int32, (1, LANES), 1)
    for hd in heads:
        pick = hlane == head0 + hd
        q2 = jnp.max(jnp.where(pick, norms[0:1, :], 0.0), axis=1, keepdims=True)
        k2 = jnp.max(jnp.where(pick, norms[1:2, :], 0.0), axis=1, keepdims=True)
        qk_bound = jnp.sqrt(q2 * k2) * FOX_BOUND_MARGIN + 1.0
        for qt in range(nt):
            m_ref[qt, hd] = nbq_ref[hd, qt:qt + 1, :] + qk_bound

    run(online=False)

    denom_min = jnp.min(acc_ref[:, :, HEAD_DIM:HEAD_DIM + 1, :])

    @pl.when(jnp.logical_not(denom_min >= FOX_MIN_DENOM))
    def _():
        run(online=True)


def _fox_attention(q, k, vt, nb, z, nbq, norms):
    hw = FOX_HEADS_PER_STEP * HEAD_DIM
    return pl.pallas_call(
        _fox_attn_kernel,
        grid=(BATCH, N_MAIN_HEADS // FOX_HEADS_PER_STEP),
        in_specs=[
            pl.BlockSpec((None, SEQ // FOX_T, hw, FOX_T), lambda b, p: (b, 0, p, 0)),
            pl.BlockSpec((SEQ, hw), lambda b, p: (b, p)),
            pl.BlockSpec((None, SEQ // FOX_T, hw, FOX_T), lambda b, p: (b, 0, p, 0)),
            pl.BlockSpec((SEQ, LANES), lambda b, p: (b, 0)),
            pl.BlockSpec((SEQ, hw), lambda b, p: (b, p)),
            pl.BlockSpec((None, FOX_HEADS_PER_STEP, SEQ // FOX_T, FOX_T), lambda b, p: (b, p, 0, 0)),
            pl.BlockSpec((SEQ // PROJ_TM, 8, LANES), lambda b, p: (b, 0, 0)),
        ],
        out_specs=pl.BlockSpec((SEQ, hw), lambda b, p: (b, p)),
        out_shape=jax.ShapeDtypeStruct((BATCH * SEQ, MAIN_W), bf16),
        scratch_shapes=[
            pltpu.VMEM((SEQ // FOX_T, FOX_HEADS_PER_STEP, 1, FOX_T), f32),
            pltpu.VMEM((SEQ // FOX_T, FOX_HEADS_PER_STEP, HEAD_DIM + FOX_SUM_ROWS, FOX_T), f32),
            pltpu.VMEM((FOX_HEADS_PER_STEP, FOX_T, FOX_T), f32),
            pltpu.VMEM((FOX_HEADS_PER_STEP, FOX_T, FOX_T), bf16),
            pltpu.VMEM((FOX_HEADS_PER_STEP, 1, FOX_T), f32),
        ],
        compiler_params=pltpu.CompilerParams(dimension_semantics=("arbitrary", "arbitrary"),
                                             vmem_limit_bytes=VMEM_LIMIT),
        name="fox_attention",
    )(q, k, vt, nb, z, nbq, norms)


def _retention_kernel(q_ref, k_ref, v_ref, z_ref, d_ref, xi_ref, zeta_ref, g_ref, o_ref, r_ref):
    c_len = RET_CHUNK
    r_ref[...] = jnp.zeros(r_ref.shape, f32)
    lane = lax.broadcasted_iota(jnp.int32, (c_len, LANES), 1)
    head_mask = (lane < RET_KEY_DIM, lane >= RET_KEY_DIM)
    pairs = range(RET_PAIRS_PER_STEP)
    chunks = range(RET_CHUNKS_PER_ITER)

    def body(it, carry):
        rows = [pl.ds(pl.multiple_of((it * RET_CHUNKS_PER_ITER + ci) * c_len, c_len), c_len) for ci in chunks]
        qa, s, upd = {}, {}, {}
        for ci in chunks:
            for pr in pairs:
                q2 = q_ref[rows[ci], pr * LANES:(pr + 1) * LANES]
                k2 = k_ref[rows[ci], pr * LANES:(pr + 1) * LANES]
                v2 = v_ref[rows[ci], pr * 2 * HEAD_DIM:(pr + 1) * 2 * HEAD_DIM]
                qa[ci, pr] = jnp.concatenate([jnp.where(head_mask[hd], q2, jnp.zeros_like(q2))
                                              for hd in range(2)], axis=0)
                s[ci, pr] = lax.dot_general(qa[ci, pr], k2, (((1,), (1,)), ((), ())),
                                            preferred_element_type=f32)
                kz = (k2.astype(f32) * zeta_ref[pr]).astype(bf16)
                upd[ci, pr] = lax.dot_general(kz, v2, (((0,), (0,)), ((), ())), preferred_element_type=f32)
        r_b = {}
        for pr in pairs:
            r_state = r_ref[pr]
            for ci in chunks:
                r_b[ci, pr] = r_state.astype(bf16)
                r_state = r_state * g_ref[pr] + upd[ci, pr]
            r_ref[pr] = r_state
        o = {}
        for ci in chunks:
            for pr in pairs:
                for hd in range(2):
                    hrows = slice(hd * c_len, (hd + 1) * c_len)
                    cols = slice((2 * pr + hd) * HEAD_DIM, (2 * pr + hd + 1) * HEAD_DIM)
                    inner = (s[ci, pr][hrows, :] * d_ref[2 * pr + hd]).astype(bf16)
                    intra = jnp.dot(inner, v_ref[rows[ci], cols], preferred_element_type=f32)
                    cross = jnp.dot(qa[ci, pr][hrows, :], r_b[ci, pr][:, hd * HEAD_DIM:(hd + 1) * HEAD_DIM],
                                    preferred_element_type=f32)
                    o[ci, pr, hd] = intra + cross * xi_ref[2 * pr + hd]
        for ci in chunks:
            for pr in pairs:
                for hd in range(2):
                    cols = slice((2 * pr + hd) * HEAD_DIM, (2 * pr + hd + 1) * HEAD_DIM)
                    oh = o[ci, pr, hd]
                    ms = jnp.mean(oh * oh, axis=-1, keepdims=True)
                    on = oh * lax.rsqrt(ms + EPS)
                    zg = _silu(z_ref[rows[ci], cols].astype(f32))
                    o_ref[rows[ci], cols] = (on * zg).astype(o_ref.dtype)
        return carry

    lax.fori_loop(0, SEQ // (c_len * RET_CHUNKS_PER_ITER), body, 0)


def _retention_tables():
    h, c_len = N_MAIN_HEADS, RET_CHUNK
    lg = np.log1p(-np.exp2(-5.0 - np.arange(h, dtype=np.float64)))
    n = np.arange(c_len, dtype=np.float64)
    diff = n[:, None] - n[None, :]
    d_inner = np.where(diff[None] >= 0, np.exp(lg[:, None, None] * np.maximum(diff, 0.0)[None]), 0.0)
    xi = np.exp(lg[:, None] * (n[None, :] + 1.0))
    zeta = np.exp(lg[:, None] * (c_len - 1.0 - n[None, :]))
    g_chunk = np.exp(lg * c_len)
    xi_b = np.broadcast_to(xi[:, :, None], (h, c_len, HEAD_DIM))
    zeta2 = np.repeat(zeta.reshape(h // 2, 2, c_len).transpose(0, 2, 1), RET_KEY_DIM, axis=-1)
    g2 = np.broadcast_to(np.repeat(g_chunk.reshape(h // 2, 2), RET_KEY_DIM, axis=-1)[:, :, None],
                         (h // 2, 2 * RET_KEY_DIM, 2 * HEAD_DIM))
    return tuple(jnp.asarray(np.ascontiguousarray(t), dtype=f32) for t in (d_inner, xi_b, zeta2, g2))


def _retention(qk, v, z):
    n_pairs = N_MAIN_HEADS // 2
    pp = RET_PAIRS_PER_STEP
    steps = n_pairs // pp
    d_inner, xi_b, zeta2, g2 = _retention_tables()
    c_len = RET_CHUNK
    return pl.pallas_call(
        _retention_kernel,
        grid=(BATCH, steps),
        in_specs=[
            pl.BlockSpec((SEQ, pp * LANES), lambda b, p: (b, p)),
            pl.BlockSpec((SEQ, pp * LANES), lambda b, p: (b, steps + p)),
            pl.BlockSpec((SEQ, pp * 2 * HEAD_DIM), lambda b, p: (b, p)),
            pl.BlockSpec((SEQ, pp * 2 * HEAD_DIM), lambda b, p: (b, p)),
            pl.BlockSpec((2 * pp, c_len, c_len), lambda b, p: (p, 0, 0)),
            pl.BlockSpec((2 * pp, c_len, HEAD_DIM), lambda b, p: (p, 0, 0)),
            pl.BlockSpec((pp, c_len, LANES), lambda b, p: (p, 0, 0)),
            pl.BlockSpec((pp, LANES, 2 * HEAD_DIM), lambda b, p: (p, 0, 0)),
        ],
        out_specs=pl.BlockSpec((SEQ, pp * 2 * HEAD_DIM), lambda b, p: (b, p)),
        out_shape=jax.ShapeDtypeStruct((BATCH * SEQ, MAIN_W), bf16),
        scratch_shapes=[pltpu.VMEM((pp, LANES, 2 * HEAD_DIM), f32)],
        compiler_params=pltpu.CompilerParams(dimension_semantics=("arbitrary", "arbitrary"),
                                             vmem_limit_bytes=VMEM_LIMIT),
        name="retention",
    )(qk, qk, v, z, d_inner, xi_b, zeta2, g2)


def _out_proj_kernel(om_ref, qm_ref, zm_ref, kv_ref, w_ref, x_ref, g_ref, o_ref, *, final_norm):
    scale = 1.0 / math.sqrt(HEAD_DIM)
    heads = range(N_MEM_HEADS)
    col = [slice(hd * HEAD_DIM, (hd + 1) * HEAD_DIM) for hd in heads]
    half = D_MODEL // 2
    s = [lax.dot_general(qm_ref[:, col[hd]], kv_ref[:, col[hd]], (((1,), (1,)), ((), ())),
                         preferred_element_type=f32) * scale for hd in heads]
    y_lo = jnp.dot(om_ref[...], w_ref[:MAIN_W, :half], preferred_element_type=f32)
    p = []
    for hd in heads:
        e = jnp.exp(s[hd] - jnp.max(s[hd], axis=-1, keepdims=True))
        p.append((e * (1.0 / jnp.sum(e, axis=-1, keepdims=True))).astype(bf16))
    memo = [jnp.dot(p[hd], kv_ref[:, MEM_W + hd * HEAD_DIM:MEM_W + (hd + 1) * HEAD_DIM],
                    preferred_element_type=f32) for hd in heads]
    y_hi = jnp.dot(om_ref[...], w_ref[:MAIN_W, half:], preferred_element_type=f32)
    og = jnp.concatenate([(memo[hd] * _silu(zm_ref[:, col[hd]].astype(f32))).astype(bf16) for hd in heads],
                         axis=1)
    y = jnp.concatenate([y_lo, y_hi], axis=1) + jnp.dot(og, w_ref[MAIN_W:, :], preferred_element_type=f32)
    xn = x_ref[...] + y
    if final_norm:
        xn = _rmsnorm_rows(xn, g_ref[...])
    o_ref[...] = xn


def _out_proj(o_main, qm, z, kv, w_out, x, g_final, *, final_norm):
    t_rows = x.shape[0]
    tiles_per_seq = SEQ // OUT_TM
    z_blk = MAIN_W // MEM_W
    return pl.pallas_call(
        functools.partial(_out_proj_kernel, final_norm=final_norm),
        grid=(t_rows // OUT_TM,),
        in_specs=[
            pl.BlockSpec((OUT_TM, MAIN_W), lambda i: (i, 0)),
            pl.BlockSpec((OUT_TM, MEM_W), lambda i: (i, 0)),
            pl.BlockSpec((OUT_TM, MEM_W), lambda i: (i, z_blk)),
            pl.BlockSpec((N_MEM, 2 * MEM_W), lambda i: (i // tiles_per_seq, 0)),
            pl.BlockSpec((INNER, D_MODEL), lambda i: (0, 0), pipeline_mode=pl.Buffered(1)),
            pl.BlockSpec((OUT_TM, D_MODEL), lambda i: (i, 0)),
            pl.BlockSpec((1, D_MODEL), lambda i: (0, 0)),
        ],
        out_specs=pl.BlockSpec((OUT_TM, D_MODEL), lambda i: (i, 0)),
        out_shape=jax.ShapeDtypeStruct((t_rows, D_MODEL), f32),
        compiler_params=pltpu.CompilerParams(dimension_semantics=("arbitrary",),
                                             vmem_limit_bytes=VMEM_LIMIT),
        name="out_proj_final" if final_norm else "out_proj",
    )(o_main, qm, z, kv, w_out, x, g_final.reshape(1, D_MODEL))


def _rotary_tables():
    half = RET_KEY_DIM // 2
    pos = np.arange(SEQ, dtype=np.float64)
    inv = 1.0 / (ROPE_BASE ** (np.arange(half, dtype=np.float64) / half))
    ang = pos[:, None] * inv[None, :]
    cos, sin = np.cos(ang), np.sin(ang)
    reps = LANES // RET_KEY_DIM
    cos_t = np.tile(np.concatenate([cos, cos], axis=-1), (1, reps))
    sin_t = np.tile(np.concatenate([-sin, sin], axis=-1), (1, reps))
    k_scale = RET_KEY_DIM ** -0.5
    return tuple(jnp.asarray(t, dtype=f32) for t in (cos_t, sin_t, cos_t * k_scale, sin_t * k_scale))


def kernel(x, mem, norm_g, fox_w_in, fox_b_f, ret_w_in, mem_norm_g, w_mem_kv, w_out, final_norm_g):
    t_rows = BATCH * SEQ
    x2 = x.reshape(t_rows, D_MODEL)

    kv0, kv1 = _norm_proj(mem.reshape(BATCH * N_MEM, D_MODEL), mem_norm_g,
                          [w_mem_kv[0].astype(bf16), w_mem_kv[1].astype(bf16)],
                          [2 * MEM_W, 2 * MEM_W], [bf16, bf16], name="mem_kv_proj")

    o_f = 3 * MAIN_W
    o_qm = o_f + N_MAIN_HEADS
    w0t = jnp.swapaxes(fox_w_in[0], 0, 1).astype(bf16)
    w_qmz = w0t[o_qm:]
    w_f = jnp.pad(w0t[o_f:o_qm], ((0, F_PAD - N_MAIN_HEADS), (0, 0)))
    q, k, vt, qm, z, f, norms = _norm_proj(
        x2, norm_g[0], [(w0t, o_f), w_qmz, w_f], [MAIN_W, MAIN_W, MAIN_W, MEM_W, INNER, F_PAD],
        [bf16, bf16, bf16, bf16, bf16, f32], ["fox_q", "fox_k", "kt", "plain", "plain", "plain"],
        w_transposed=True, name="fox_in_proj")
    nb, nbq = _fox_gate(f, fox_b_f[0])
    nbq = nbq.reshape(BATCH, GATE_HEAD_ROWS, SEQ // FOX_T, FOX_T)
    o_main = _fox_attention(q, k, vt, nb, z, nbq, norms)
    x2 = _out_proj(o_main, qm, z, kv0, w_out[0].astype(bf16), x2, final_norm_g, final_norm=False)

    w1 = ret_w_in[0].astype(bf16)
    qk, v, qm, z = _norm_proj(
        x2, norm_g[1], [w1], [2 * RET_QK_W, MAIN_W, MEM_W, INNER], [bf16] * 4,
        ["rot", "plain", "plain", "plain"], tables=_rotary_tables(), name="ret_in_proj")
    o_main = _retention(qk, v, z)
    out = _out_proj(o_main, qm, z, kv1, w_out[1].astype(bf16), x2, final_norm_g, final_norm=True)
    return out.reshape(BATCH, SEQ, D_MODEL)
```

```python
import functools
import math

import jax
import jax.numpy as jnp
import numpy as np
from jax import lax
from jax.experimental import pallas as pl
from jax.experimental.pallas import tpu as pltpu

D_MODEL = 1024
BATCH = 8
SEQ = 2048
HEAD_DIM = 128
N_MAIN_HEADS = 12
N_MEM_HEADS = 4
N_MEM = 256
MAIN_W = N_MAIN_HEADS * HEAD_DIM
MEM_W = N_MEM_HEADS * HEAD_DIM
INNER = MAIN_W + MEM_W
RET_KEY_DIM = HEAD_DIM // 2
RET_QK_W = N_MAIN_HEADS * RET_KEY_DIM
RET_CHUNK = 128
ROPE_BASE = 10000.0
EPS = 1e-6
NEG = -1e30
LOG2E = 1.4426950408889634

LANES = 128
F_PAD = LANES
VMEM_LIMIT = 56 * 1024 * 1024

PROJ_TM = 512
PROJ_TN = 512
FOX_T = 256
FOX_HEADS_PER_STEP = 6
FOX_ITEMS_PER_ITER = 7
FOX_ITEMS_PER_ITER_ONLINE = 2
FOX_DIAG_TILES_PER_ITER = 4
BIAS_TERMS = 3
FOX_SUM_ROWS = 16
FOX_Q_SCALE = LOG2E / math.sqrt(HEAD_DIM)
FOX_BOUND_MARGIN = 1.02
FOX_MIN_DENOM = 2.0 ** -90
GATE_HEAD_ROWS = 16
RET_PAIRS_PER_STEP = 3
RET_CHUNKS_PER_ITER = 2
OUT_TM = 512

f32 = jnp.float32
bf16 = jnp.bfloat16


def _silu_of_half(h):
    return h + h * jnp.tanh(h)


def _rmsnorm_rows(x, g):
    ms = jnp.mean(x * x, axis=-1, keepdims=True)
    return (x * lax.rsqrt(ms + EPS)) * g


def _split_bf16(v):
    hi = v.astype(bf16)
    r1 = v - hi.astype(f32)
    mid = r1.astype(bf16)
    lo = (r1 - mid.astype(f32)).astype(bf16)
    return hi, mid, lo


def _norm_proj_kernel(*refs, n_weights, w_transposed, out_widths, out_kinds):
    track_norms = "fox_q" in out_kinds
    n_out = len(out_widths) + (1 if track_norms else 0)
    x_ref, g_ref = refs[:2]
    w_refs = refs[2:2 + n_weights]
    tab_refs = refs[2 + n_weights:len(refs) - n_out]
    out_refs = refs[len(refs) - n_out:]
    h = _rmsnorm_rows(x_ref[...], g_ref[...]).astype(bf16)
    tm = h.shape[0]
    if "rot_q" in out_kinds:
        lane = lax.broadcasted_iota(jnp.int32, (tm, LANES), 1)
        first_half = (lane % RET_KEY_DIM) < (RET_KEY_DIM // 2)
    if track_norms:
        norm_sub = lax.broadcasted_iota(jnp.int32, (8, LANES), 0)
        norm_lane = lax.broadcasted_iota(jnp.int32, (8, LANES), 1)
        norms = jnp.zeros((8, LANES), f32)

        def with_norms(norms, y, row, head0):
            for s0 in range(0, y.shape[1], HEAD_DIM):
                ys = y[:, s0:s0 + HEAD_DIM]
                worst = jnp.max(jnp.sum(ys * ys, axis=1, keepdims=True), axis=0, keepdims=True)
                norms = jnp.where((norm_sub == row) & (norm_lane == head0 + s0 // HEAD_DIM), worst, norms)
            return norms
    out_axis = 0 if w_transposed else 1
    wi, col = 0, 0
    for o_ref, width, kind in zip(out_refs, out_widths, out_kinds):
        if col == w_refs[wi].shape[out_axis]:
            wi, col = wi + 1, 0
        w_ref = w_refs[wi]
        for c0 in range(0, width, PROJ_TN):
            cw = min(PROJ_TN, width - c0)
            if w_transposed:
                y = lax.dot_general(h, w_ref[col + c0:col + c0 + cw, :], (((1,), (1,)), ((), ())),
                                    preferred_element_type=f32)
            else:
                y = jnp.dot(h, w_ref[:, col + c0:col + c0 + cw], preferred_element_type=f32)
            if kind in ("rot_q", "rot_kt"):
                cos_ref, sin_ref = tab_refs[:2] if kind == "rot_q" else tab_refs[2:]
                for s0 in range(0, cw, LANES):
                    ys = y[:, s0:s0 + LANES]
                    sw = jnp.where(first_half, pltpu.roll(ys, LANES - RET_KEY_DIM // 2, 1),
                                   pltpu.roll(ys, RET_KEY_DIM // 2, 1))
                    r = ys * cos_ref[...] + sw * sin_ref[...]
                    if kind == "rot_q":
                        o_ref[:, c0 + s0:c0 + s0 + LANES] = r.astype(o_ref.dtype)
                    else:
                        for r0 in range(tm // RET_CHUNK):
                            piece = r[r0 * RET_CHUNK:(r0 + 1) * RET_CHUNK, :]
                            o_ref[r0, c0 + s0:c0 + s0 + LANES, :] = piece.T.astype(o_ref.dtype)
            elif kind in ("kt", "fox_q"):
                if kind == "fox_q":
                    y = y * FOX_Q_SCALE
                    norms = with_norms(norms, y, 0, c0 // HEAD_DIM)
                for r0 in range(tm // FOX_T):
                    for s0 in range(0, cw, LANES):
                        piece = y[r0 * FOX_T:(r0 + 1) * FOX_T, s0:s0 + LANES]
                        o_ref[r0, c0 + s0:c0 + s0 + LANES, :] = piece.T.astype(o_ref.dtype)
            elif kind == "fox_k":
                norms = with_norms(norms, y, 1, c0 // HEAD_DIM)
                o_ref[:, c0:c0 + cw] = y.astype(o_ref.dtype)
            elif kind == "half":
                o_ref[:, c0:c0 + cw] = (y * 0.5).astype(o_ref.dtype)
            else:
                o_ref[:, c0:c0 + cw] = y.astype(o_ref.dtype)
        col += width
    if track_norms:
        out_refs[-1][...] = norms


def _norm_proj(x, g, weights, out_widths, out_dtypes, out_kinds=None, *, w_transposed=False, tables=None,
               name):
    t_rows, d = x.shape
    out_kinds = tuple(out_kinds or ["plain"] * len(out_widths))
    w_arrays = [w[0] if isinstance(w, tuple) else w for w in weights]
    w_blocks = [(w[1], d) if isinstance(w, tuple) else w.shape for w in weights]
    out_axis = 0 if w_transposed else 1
    assert sum(out_widths) == sum(blk[out_axis] for blk in w_blocks) and t_rows % PROJ_TM == 0
    tiles_per_seq = SEQ // PROJ_TM
    in_specs = [
        pl.BlockSpec((PROJ_TM, d), lambda i: (i, 0)),
        pl.BlockSpec((1, d), lambda i: (0, 0)),
    ]
    in_specs += [pl.BlockSpec(blk, lambda i: (0, 0), pipeline_mode=pl.Buffered(1)) for blk in w_blocks]
    args = [x, g.reshape(1, d), *w_arrays]
    if "rot_q" in out_kinds:
        for tab in tables:
            in_specs.append(pl.BlockSpec((PROJ_TM, LANES), lambda i: (i % tiles_per_seq, 0)))
            args.append(tab)
    out_specs, out_shape = [], []
    for wd, dt, kind in zip(out_widths, out_dtypes, out_kinds):
        if kind in ("kt", "fox_q", "rot_kt"):
            tile = RET_CHUNK if kind == "rot_kt" else FOX_T
            out_specs.append(pl.BlockSpec((None, PROJ_TM // tile, wd, tile),
                                          lambda i: (i // tiles_per_seq, i % tiles_per_seq, 0, 0)))
            out_shape.append(jax.ShapeDtypeStruct((t_rows // SEQ, SEQ // tile, wd, tile), dt))
        else:
            out_specs.append(pl.BlockSpec((PROJ_TM, wd), lambda i: (i, 0)))
            out_shape.append(jax.ShapeDtypeStruct((t_rows, wd), dt))
    if "fox_q" in out_kinds:
        out_specs.append(pl.BlockSpec((None, 8, LANES), lambda i: (i, 0, 0)))
        out_shape.append(jax.ShapeDtypeStruct((t_rows // PROJ_TM, 8, LANES), f32))
    return pl.pallas_call(
        functools.partial(_norm_proj_kernel, n_weights=len(weights), w_transposed=w_transposed,
                          out_widths=tuple(out_widths), out_kinds=out_kinds),
        grid=(t_rows // PROJ_TM,),
        in_specs=in_specs,
        out_specs=out_specs,
        out_shape=out_shape,
        compiler_params=pltpu.CompilerParams(dimension_semantics=("arbitrary",),
                                             vmem_limit_bytes=VMEM_LIMIT),
        name=name,
    )(*args)


def _fox_gate_kernel(f_ref, bf_ref, o_ref, ot_ref):
    blk = LANES
    row = lax.broadcasted_iota(jnp.int32, (blk, blk), 0)
    coli = lax.broadcasted_iota(jnp.int32, (blk, blk), 1)
    tri = jnp.where(row >= coli, 1.0, 0.0).astype(bf16)
    spread = [jnp.where((coli == BIAS_TERMS * row + t) & (row < N_MAIN_HEADS), 1.0, 0.0).astype(bf16)
              for t in range(BIAS_TERMS)]
    carry = jnp.zeros((1, F_PAD), f32)
    for b0 in range(0, SEQ, blk):
        xg = f_ref[b0:b0 + blk, :] + bf_ref[...]
        ls = jnp.minimum(xg, 0.0) - jnp.log1p(jnp.exp(-jnp.abs(xg)))
        cs = sum(jnp.dot(tri, term, preferred_element_type=f32) for term in _split_bf16(ls)) + carry
        carry = cs[blk - 1:blk, :]
        bias = cs * (-LOG2E)
        nb = sum(jnp.dot(term, sp, preferred_element_type=f32)
                 for term, sp in zip(_split_bf16(bias), spread))
        o_ref[b0:b0 + blk, :] = nb.astype(o_ref.dtype)
        ot_ref[:, b0:b0 + blk] = bias.T[:ot_ref.shape[0], :]


def _fox_gate(f, b_f):
    bf_pad = jnp.zeros((1, F_PAD), f32).at[0, :N_MAIN_HEADS].set(b_f.astype(f32))
    return pl.pallas_call(
        _fox_gate_kernel,
        grid=(BATCH,),
        in_specs=[pl.BlockSpec((SEQ, F_PAD), lambda b: (b, 0)),
                  pl.BlockSpec((1, F_PAD), lambda b: (0, 0))],
        out_specs=[pl.BlockSpec((SEQ, LANES), lambda b: (b, 0)),
                   pl.BlockSpec((None, GATE_HEAD_ROWS, SEQ), lambda b: (b, 0, 0))],
        out_shape=[jax.ShapeDtypeStruct((BATCH * SEQ, LANES), bf16),
                   jax.ShapeDtypeStruct((BATCH, GATE_HEAD_ROWS, SEQ), f32)],
        compiler_params=pltpu.CompilerParams(dimension_semantics=("arbitrary",)),
        name="fox_gate",
    )(f, bf_pad)


def _fox_attn_kernel(qt_ref, k_ref, vt_ref, nb_ref, z_ref, nbq_ref, norm_ref, o_ref,
                     m_ref, acc_ref, u_ref, p_ref, a_ref):
    t = FOX_T
    nt = SEQ // t
    head0 = pl.program_id(1) * FOX_HEADS_PER_STEP
    heads = range(FOX_HEADS_PER_STEP)
    col = [slice(hd * HEAD_DIM, (hd + 1) * HEAD_DIM) for hd in heads]
    sub = lax.broadcasted_iota(jnp.int32, (LANES, t), 0)
    sel = []
    for hd in heads:
        lo = BIAS_TERMS * (head0 + hd)
        sel.append(jnp.where((sub >= lo) & (sub < lo + BIAS_TERMS), 1.0, 0.0).astype(bf16))
    ones_rows = jnp.ones((FOX_SUM_ROWS, t), bf16)

    def tile_rows(i):
        return pl.ds(pl.multiple_of(i * t, t), t)

    def score(qt, kt, hd):
        k_aug = jnp.concatenate([k_ref[tile_rows(kt), col[hd]], nb_ref[tile_rows(kt), :]], axis=1)
        q_aug_t = jnp.concatenate([qt_ref[qt, col[hd], :], sel[hd]], axis=0)
        return jnp.dot(k_aug, q_aug_t, preferred_element_type=f32)

    def scores(qt, kt):
        return [score(qt, kt, hd) for hd in heads]

    def pv1(kt, p_hd, hd):
        return jnp.dot(jnp.concatenate([vt_ref[kt, col[hd], :], ones_rows], axis=0), p_hd,
                       preferred_element_type=f32)

    def pv(kt, p):
        return [pv1(kt, p[hd], hd) for hd in heads]

    def finalize(qt):
        for hd in heads:
            acc = acc_ref[qt, hd]
            out_t = acc[:HEAD_DIM, :] * (1.0 / acc[HEAD_DIM:HEAD_DIM + 1, :])
            gate = _silu_of_half(z_ref[tile_rows(qt), col[hd]].astype(f32))
            o_ref[tile_rows(qt), col[hd]] = (out_t.T * gate).astype(o_ref.dtype)

    def next_item(qt, kt):
        wrap = kt + 1 >= qt
        return jnp.minimum(jnp.where(wrap, qt + 1, qt), nt - 1), jnp.where(wrap, 0, kt + 1)

    def load_list(ref):
        return [ref[hd] for hd in heads]

    def store_list(ref, vals):
        for hd in heads:
            ref[hd] = vals[hd]

    def run(online):
        def numerators(qt, u, masked):
            half = t // 2
            if masked:
                ksub = lax.broadcasted_iota(jnp.int32, (t if online else half,) * 2, 0)
                qlane = lax.broadcasted_iota(jnp.int32, (t if online else half,) * 2, 1)
                causal = ksub <= qlane
            p, alpha = [], []
            for hd in heads:
                m_old = m_ref[qt, hd]
                if online:
                    uh = jnp.where(causal, u[hd], NEG) if masked else u[hd]
                    m_new = jnp.maximum(m_old, jnp.max(uh, axis=0, keepdims=True))
                    alpha.append(jnp.exp2(m_old - m_new))
                    m_ref[qt, hd] = m_new
                    p.append(jnp.exp2(uh - m_new).astype(bf16))
                elif masked:
                    lo, hi = slice(0, half), slice(half, t)
                    p_ll = jnp.exp2(jnp.where(causal, u[hd][lo, lo], NEG) - m_old[:, lo])
                    p_lh = jnp.exp2(u[hd][lo, hi] - m_old[:, hi])
                    p_hh = jnp.exp2(jnp.where(causal, u[hd][hi, hi], NEG) - m_old[:, hi])
                    top = jnp.concatenate([p_ll, p_lh], axis=1)
                    bottom = jnp.concatenate([jnp.zeros_like(p_hh), p_hh], axis=1)
                    p.append(jnp.concatenate([top, bottom], axis=0).astype(bf16))
                else:
                    p.append(jnp.exp2(u[hd] - m_old).astype(bf16))
            return p, alpha

        def accumulate(qt, alpha, pv_vals):
            for hd in heads:
                prev = alpha[hd] * acc_ref[qt, hd] if online else acc_ref[qt, hd]
                acc_ref[qt, hd] = prev + pv_vals[hd]

        acc_ref[...] = jnp.zeros(acc_ref.shape, f32)
        p_ref[...] = jnp.zeros(p_ref.shape, bf16)
        if online:
            m_ref[...] = jnp.full(m_ref.shape, NEG, f32)
            a_ref[...] = jnp.ones(a_ref.shape, f32)

        n_items = nt * (nt - 1) // 2
        ni = FOX_ITEMS_PER_ITER_ONLINE if online else FOX_ITEMS_PER_ITER
        assert n_items % ni == 0
        store_list(u_ref, scores(1, 0))

        def pass1(_, carry):
            q0, k0, q_prev, k_prev = carry
            items = [(q0, k0)]
            for _k in range(ni):
                items.append(next_item(*items[-1]))
            u_cur = load_list(u_ref)
            pv_vals = pv(k_prev, load_list(p_ref))
            u_next = scores(*items[1])
            p_cur, al_cur = numerators(q0, u_cur, False)
            accumulate(q_prev, load_list(a_ref) if online else None, pv_vals)
            for k in range(1, ni):
                u_cur, u_next, pv_vals = u_next, [], []
                for hd in heads:
                    s = score(items[k + 1][0], items[k + 1][1], hd)
                    if k == ni - 1:
                        u_ref[hd] = s
                    else:
                        u_next.append(s)
                    pv_vals.append(pv1(items[k - 1][1], p_cur[hd], hd))
                p_new, al_new = numerators(items[k][0], u_cur, False)
                accumulate(items[k - 1][0], al_cur, pv_vals)
                p_cur, al_cur = p_new, al_new
            store_list(p_ref, p_cur)
            if online:
                store_list(a_ref, al_cur)
            return items[ni] + items[ni - 1]

        one, zero = jnp.int32(1), jnp.int32(0)
        _, _, q_prev, k_prev = lax.fori_loop(0, n_items // ni, pass1, (one, zero, one, zero))
        accumulate(q_prev, load_list(a_ref) if online else None, pv(k_prev, load_list(p_ref)))

        nd = FOX_ITEMS_PER_ITER_ONLINE if online else FOX_DIAG_TILES_PER_ITER
        assert nt % nd == 0
        store_list(u_ref, scores(0, 0))

        def pass2(i, carry):
            u_cur = load_list(u_ref)
            for k in range(nd):
                tile = nd * i + k
                nxt = jnp.minimum(tile + 1, nt - 1)
                u_next = scores(nxt, nxt)
                p_cur, al_cur = numerators(tile, u_cur, True)
                accumulate(tile, al_cur, pv(tile, p_cur))
                finalize(tile)
                u_cur = u_next
            store_list(u_ref, u_cur)
            return carry

        lax.fori_loop(0, nt // nd, pass2, 0)

    norms = jnp.max(norm_ref[...], axis=0)
    hlane = lax.broadcasted_iota(jnp.int32, (1, LANES), 1)
    for hd in heads:
        pick = hlane == head0 + hd
        q2 = jnp.max(jnp.where(pick, norms[0:1, :], 0.0), axis=1, keepdims=True)
        k2 = jnp.max(jnp.where(pick, norms[1:2, :], 0.0), axis=1, keepdims=True)
        qk_bound = jnp.sqrt(q2 * k2) * FOX_BOUND_MARGIN + 1.0
        for qt in range(nt):
            m_ref[qt, hd] = nbq_ref[hd, qt:qt + 1, :] + qk_bound

    run(online=False)

    denom_min = jnp.min(acc_ref[:, :, HEAD_DIM:HEAD_DIM + 1, :])

    @pl.when(jnp.logical_not(denom_min >= FOX_MIN_DENOM))
    def _():
        run(online=True)


def _fox_attention(q, k, vt, nb, z, nbq, norms):
    hw = FOX_HEADS_PER_STEP * HEAD_DIM
    return pl.pallas_call(
        _fox_attn_kernel,
        grid=(BATCH, N_MAIN_HEADS // FOX_HEADS_PER_STEP),
        in_specs=[
            pl.BlockSpec((None, SEQ // FOX_T, hw, FOX_T), lambda b, p: (b, 0, p, 0)),
            pl.BlockSpec((SEQ, hw), lambda b, p: (b, p)),
            pl.BlockSpec((None, SEQ // FOX_T, hw, FOX_T), lambda b, p: (b, 0, p, 0)),
            pl.BlockSpec((SEQ, LANES), lambda b, p: (b, 0)),
            pl.BlockSpec((SEQ, hw), lambda b, p: (b, p)),
            pl.BlockSpec((None, FOX_HEADS_PER_STEP, SEQ // FOX_T, FOX_T), lambda b, p: (b, p, 0, 0)),
            pl.BlockSpec((SEQ // PROJ_TM, 8, LANES), lambda b, p: (b, 0, 0)),
        ],
        out_specs=pl.BlockSpec((SEQ, hw), lambda b, p: (b, p)),
        out_shape=jax.ShapeDtypeStruct((BATCH * SEQ, MAIN_W), bf16),
        scratch_shapes=[
            pltpu.VMEM((SEQ // FOX_T, FOX_HEADS_PER_STEP, 1, FOX_T), f32),
            pltpu.VMEM((SEQ // FOX_T, FOX_HEADS_PER_STEP, HEAD_DIM + FOX_SUM_ROWS, FOX_T), f32),
            pltpu.VMEM((FOX_HEADS_PER_STEP, FOX_T, FOX_T), f32),
            pltpu.VMEM((FOX_HEADS_PER_STEP, FOX_T, FOX_T), bf16),
            pltpu.VMEM((FOX_HEADS_PER_STEP, 1, FOX_T), f32),
        ],
        compiler_params=pltpu.CompilerParams(dimension_semantics=("arbitrary", "arbitrary"),
                                             vmem_limit_bytes=VMEM_LIMIT),
        name="fox_attention",
    )(q, k, vt, nb, z, nbq, norms)


def _retention_kernel(q_ref, kt_ref, v_ref, z_ref, d_ref, eps_ref, zeta_ref, g_ref, o_ref, r_ref):
    c_len = RET_CHUNK
    r_ref[...] = jnp.zeros(r_ref.shape, f32)
    lane = lax.broadcasted_iota(jnp.int32, (c_len, LANES), 1)
    sub = lax.broadcasted_iota(jnp.int32, (LANES, c_len), 0)
    q_mask = (lane < RET_KEY_DIM, lane >= RET_KEY_DIM)
    k_mask = (sub < RET_KEY_DIM, sub >= RET_KEY_DIM)
    pairs = range(RET_PAIRS_PER_STEP)
    chunks = range(RET_CHUNKS_PER_ITER)

    def body(it, carry):
        chunk = [it * RET_CHUNKS_PER_ITER + ci for ci in chunks]
        rows = [pl.ds(pl.multiple_of(chunk[ci] * c_len, c_len), c_len) for ci in chunks]
        q2, s, upd = {}, {}, {}
        for ci in chunks:
            for pr in pairs:
                q2[ci, pr] = q_ref[rows[ci], pr * LANES:(pr + 1) * LANES]
                kt2 = kt_ref[chunk[ci], pr * LANES:(pr + 1) * LANES, :]
                v2 = v_ref[rows[ci], pr * 2 * HEAD_DIM:(pr + 1) * 2 * HEAD_DIM]
                k_heads = jnp.concatenate([jnp.where(k_mask[hd], kt2, jnp.zeros_like(kt2))
                                           for hd in range(2)], axis=1)
                s[ci, pr] = jnp.dot(q2[ci, pr], k_heads, preferred_element_type=f32)
                kz = (kt2.astype(f32) * zeta_ref[pr]).astype(bf16)
                upd[ci, pr] = jnp.dot(kz, v2, preferred_element_type=f32)
        r_b = {}
        for pr in pairs:
            r_state = r_ref[pr]
            for ci in chunks:
                r_b[ci, pr] = r_state.astype(bf16)
                r_state = r_state * g_ref[pr] + upd[ci, pr]
            r_ref[pr] = r_state
        o = {}
        for ci in chunks:
            for pr in pairs:
                inner = (s[ci, pr] * d_ref[pr]).astype(bf16)
                for hd in range(2):
                    cols = slice((2 * pr + hd) * HEAD_DIM, (2 * pr + hd + 1) * HEAD_DIM)
                    q_hd = jnp.where(q_mask[hd], q2[ci, pr], jnp.zeros_like(q2[ci, pr]))
                    lhs = jnp.concatenate([inner[:, hd * c_len:(hd + 1) * c_len], q_hd], axis=1)
                    rhs = jnp.concatenate([v_ref[rows[ci], cols],
                                           r_b[ci, pr][:, hd * HEAD_DIM:(hd + 1) * HEAD_DIM]], axis=0)
                    o[ci, pr, hd] = jnp.dot(lhs, rhs, preferred_element_type=f32)
        for ci in chunks:
            for pr in pairs:
                for hd in range(2):
                    cols = slice((2 * pr + hd) * HEAD_DIM, (2 * pr + hd + 1) * HEAD_DIM)
                    oh = o[ci, pr, hd]
                    ms = jnp.mean(oh * oh, axis=-1, keepdims=True)
                    on = oh * lax.rsqrt(ms + eps_ref[2 * pr + hd])
                    zg = _silu_of_half(z_ref[rows[ci], cols].astype(f32))
                    o_ref[rows[ci], cols] = (on * zg).astype(o_ref.dtype)
        return carry

    lax.fori_loop(0, SEQ // (c_len * RET_CHUNKS_PER_ITER), body, 0)


def _retention_tables():
    h, c_len = N_MAIN_HEADS, RET_CHUNK
    lg = np.log1p(-np.exp2(-5.0 - np.arange(h, dtype=np.float64)))
    n = np.arange(c_len, dtype=np.float64)
    causal = n[:, None] >= n[None, :]
    d_col = np.where(causal[None], np.exp(-lg[:, None, None] * (n[None, None, :] + 1.0)), 0.0)
    d_pair = d_col.reshape(h // 2, 2, c_len, c_len).transpose(0, 2, 1, 3).reshape(h // 2, c_len, 2 * c_len)
    xi = np.exp(lg[:, None] * (n[None, :] + 1.0))
    zeta = np.exp(lg[:, None] * (c_len - 1.0 - n[None, :]))
    g_chunk = np.exp(lg * c_len)
    eps_rows = np.broadcast_to((EPS / (xi * xi))[:, :, None], (h, c_len, HEAD_DIM))
    zeta_t = np.repeat(zeta.reshape(h // 2, 2, c_len), RET_KEY_DIM, axis=1)
    g2 = np.broadcast_to(np.repeat(g_chunk.reshape(h // 2, 2), RET_KEY_DIM, axis=-1)[:, :, None],
                         (h // 2, 2 * RET_KEY_DIM, 2 * HEAD_DIM))
    return tuple(jnp.asarray(np.ascontiguousarray(t), dtype=f32) for t in (d_pair, eps_rows, zeta_t, g2))


def _retention(q, kt, v, z):
    n_pairs = N_MAIN_HEADS // 2
    pp = RET_PAIRS_PER_STEP
    steps = n_pairs // pp
    d_pair, eps_rows, zeta_t, g2 = _retention_tables()
    c_len = RET_CHUNK
    return pl.pallas_call(
        _retention_kernel,
        grid=(BATCH, steps),
        in_specs=[
            pl.BlockSpec((SEQ, pp * LANES), lambda b, p: (b, p)),
            pl.BlockSpec((None, SEQ // c_len, pp * LANES, c_len), lambda b, p: (b, 0, p, 0)),
            pl.BlockSpec((SEQ, pp * 2 * HEAD_DIM), lambda b, p: (b, p)),
            pl.BlockSpec((SEQ, pp * 2 * HEAD_DIM), lambda b, p: (b, p)),
            pl.BlockSpec((pp, c_len, 2 * c_len), lambda b, p: (p, 0, 0)),
            pl.BlockSpec((2 * pp, c_len, HEAD_DIM), lambda b, p: (p, 0, 0)),
            pl.BlockSpec((pp, LANES, c_len), lambda b, p: (p, 0, 0)),
            pl.BlockSpec((pp, LANES, 2 * HEAD_DIM), lambda b, p: (p, 0, 0)),
        ],
        out_specs=pl.BlockSpec((SEQ, pp * 2 * HEAD_DIM), lambda b, p: (b, p)),
        out_shape=jax.ShapeDtypeStruct((BATCH * SEQ, MAIN_W), bf16),
        scratch_shapes=[pltpu.VMEM((pp, LANES, 2 * HEAD_DIM), f32)],
        compiler_params=pltpu.CompilerParams(dimension_semantics=("arbitrary", "arbitrary"),
                                             vmem_limit_bytes=VMEM_LIMIT),
        name="retention",
    )(q, kt, v, z, d_pair, eps_rows, zeta_t, g2)


def _out_proj_kernel(om_ref, qm_ref, zm_ref, kv_ref, w_ref, x_ref, g_ref, o_ref, *, final_norm):
    scale = 1.0 / math.sqrt(HEAD_DIM)
    heads = range(N_MEM_HEADS)
    col = [slice(hd * HEAD_DIM, (hd + 1) * HEAD_DIM) for hd in heads]
    half = D_MODEL // 2
    s = [lax.dot_general(qm_ref[:, col[hd]], kv_ref[:, col[hd]], (((1,), (1,)), ((), ())),
                         preferred_element_type=f32) * scale for hd in heads]
    y_lo = jnp.dot(om_ref[...], w_ref[:MAIN_W, :half], preferred_element_type=f32)
    p = []
    for hd in heads:
        e = jnp.exp(s[hd] - jnp.max(s[hd], axis=-1, keepdims=True))
        p.append((e * (1.0 / jnp.sum(e, axis=-1, keepdims=True))).astype(bf16))
    memo = [jnp.dot(p[hd], kv_ref[:, MEM_W + hd * HEAD_DIM:MEM_W + (hd + 1) * HEAD_DIM],
                    preferred_element_type=f32) for hd in heads]
    y_hi = jnp.dot(om_ref[...], w_ref[:MAIN_W, half:], preferred_element_type=f32)
    og = jnp.concatenate([(memo[hd] * _silu_of_half(zm_ref[:, col[hd]].astype(f32))).astype(bf16) for hd in heads],
                         axis=1)
    y = jnp.concatenate([y_lo, y_hi], axis=1) + jnp.dot(og, w_ref[MAIN_W:, :], preferred_element_type=f32)
    xn = x_ref[...] + y
    if final_norm:
        xn = _rmsnorm_rows(xn, g_ref[...])
    o_ref[...] = xn


def _out_proj(o_main, qm, z, kv, w_out, x, g_final, *, final_norm):
    t_rows = x.shape[0]
    tiles_per_seq = SEQ // OUT_TM
    z_blk = MAIN_W // MEM_W
    return pl.pallas_call(
        functools.partial(_out_proj_kernel, final_norm=final_norm),
        grid=(t_rows // OUT_TM,),
        in_specs=[
            pl.BlockSpec((OUT_TM, MAIN_W), lambda i: (i, 0)),
            pl.BlockSpec((OUT_TM, MEM_W), lambda i: (i, 0)),
            pl.BlockSpec((OUT_TM, MEM_W), lambda i: (i, z_blk)),
            pl.BlockSpec((N_MEM, 2 * MEM_W), lambda i: (i // tiles_per_seq, 0)),
            pl.BlockSpec((INNER, D_MODEL), lambda i: (0, 0), pipeline_mode=pl.Buffered(1)),
            pl.BlockSpec((OUT_TM, D_MODEL), lambda i: (i, 0)),
            pl.BlockSpec((1, D_MODEL), lambda i: (0, 0)),
        ],
        out_specs=pl.BlockSpec((OUT_TM, D_MODEL), lambda i: (i, 0)),
        out_shape=jax.ShapeDtypeStruct((t_rows, D_MODEL), f32),
        compiler_params=pltpu.CompilerParams(dimension_semantics=("arbitrary",),
                                             vmem_limit_bytes=VMEM_LIMIT),
        name="out_proj_final" if final_norm else "out_proj",
    )(o_main, qm, z, kv, w_out, x, g_final.reshape(1, D_MODEL))


def _rotary_tables():
    half = RET_KEY_DIM // 2
    pos = np.arange(SEQ, dtype=np.float64)
    inv = 1.0 / (ROPE_BASE ** (np.arange(half, dtype=np.float64) / half))
    ang = pos[:, None] * inv[None, :]
    cos, sin = np.cos(ang), np.sin(ang)
    reps = LANES // RET_KEY_DIM
    cos_t = np.tile(np.concatenate([cos, cos], axis=-1), (1, reps))
    sin_t = np.tile(np.concatenate([-sin, sin], axis=-1), (1, reps))
    k_scale = RET_KEY_DIM ** -0.5
    return tuple(jnp.asarray(t, dtype=f32) for t in (cos_t, sin_t, cos_t * k_scale, sin_t * k_scale))


def kernel(x, mem, norm_g, fox_w_in, fox_b_f, ret_w_in, mem_norm_g, w_mem_kv, w_out, final_norm_g):
    t_rows = BATCH * SEQ
    x2 = x.reshape(t_rows, D_MODEL)

    kv0, kv1 = _norm_proj(mem.reshape(BATCH * N_MEM, D_MODEL), mem_norm_g,
                          [w_mem_kv[0].astype(bf16), w_mem_kv[1].astype(bf16)],
                          [2 * MEM_W, 2 * MEM_W], [bf16, bf16], name="mem_kv_proj")

    o_f = 3 * MAIN_W
    o_qm = o_f + N_MAIN_HEADS
    w0t = jnp.swapaxes(fox_w_in[0], 0, 1).astype(bf16)
    w_qmz = w0t[o_qm:]
    w_f = jnp.pad(w0t[o_f:o_qm], ((0, F_PAD - N_MAIN_HEADS), (0, 0)))
    q, k, vt, qm, z, f, norms = _norm_proj(
        x2, norm_g[0], [(w0t, o_f), w_qmz, w_f], [MAIN_W, MAIN_W, MAIN_W, MEM_W, INNER, F_PAD],
        [bf16, bf16, bf16, bf16, bf16, f32], ["fox_q", "fox_k", "kt", "plain", "half", "plain"],
        w_transposed=True, name="fox_in_proj")
    nb, nbq = _fox_gate(f, fox_b_f[0])
    nbq = nbq.reshape(BATCH, GATE_HEAD_ROWS, SEQ // FOX_T, FOX_T)
    o_main = _fox_attention(q, k, vt, nb, z, nbq, norms)
    x2 = _out_proj(o_main, qm, z, kv0, w_out[0].astype(bf16), x2, final_norm_g, final_norm=False)

    w1 = ret_w_in[0].astype(bf16)
    q, kt, v, qm, z = _norm_proj(
        x2, norm_g[1], [w1], [RET_QK_W, RET_QK_W, MAIN_W, MEM_W, INNER], [bf16] * 5,
        ["rot_q", "rot_kt", "plain", "plain", "half"], tables=_rotary_tables(), name="ret_in_proj")
    o_main = _retention(q, kt, v, z)
    out = _out_proj(o_main, qm, z, kv1, w_out[1].astype(bf16), x2, final_norm_g, final_norm=True)
    return out.reshape(BATCH, SEQ, D_MODEL)
```

```python
import functools
import math

import jax
import jax.numpy as jnp
import numpy as np
from jax import lax
from jax.experimental import pallas as pl
from jax.experimental.pallas import tpu as pltpu

D_MODEL = 1024
BATCH = 8
SEQ = 2048
HEAD_DIM = 128
N_MAIN_HEADS = 12
N_MEM_HEADS = 4
N_MEM = 256
MAIN_W = N_MAIN_HEADS * HEAD_DIM
MEM_W = N_MEM_HEADS * HEAD_DIM
INNER = MAIN_W + MEM_W
RET_KEY_DIM = HEAD_DIM // 2
RET_QK_W = N_MAIN_HEADS * RET_KEY_DIM
RET_CHUNK = 128
ROPE_BASE = 10000.0
EPS = 1e-6
NEG = -1e30
LOG2E = 1.4426950408889634

LANES = 128
F_PAD = LANES
VMEM_LIMIT = 56 * 1024 * 1024

PROJ_TM = 512
PROJ_TN = 512
FOX_T = 256
FOX_HEADS_PER_STEP = 6
FOX_ITEMS_PER_ITER = 7
FOX_ITEMS_PER_ITER_ONLINE = 2
FOX_DIAG_TILES_PER_ITER = 4
BIAS_TERMS = 3
FOX_SUM_ROWS = 16
FOX_Q_SCALE = LOG2E / math.sqrt(HEAD_DIM)
FOX_BOUND_MARGIN = 1.02
FOX_MIN_DENOM = 2.0 ** -90
GATE_HEAD_ROWS = 16
RET_PAIRS_PER_STEP = 3
RET_CHUNKS_PER_ITER = 2
OUT_TM = 512

f32 = jnp.float32
bf16 = jnp.bfloat16


def _silu_of_half(h):
    return h + h * jnp.tanh(h)


def _rmsnorm_rows(x, g):
    ms = jnp.mean(x * x, axis=-1, keepdims=True)
    return (x * lax.rsqrt(ms + EPS)) * g


def _split_bf16(v):
    hi = v.astype(bf16)
    r1 = v - hi.astype(f32)
    mid = r1.astype(bf16)
    lo = (r1 - mid.astype(f32)).astype(bf16)
    return hi, mid, lo


def _norm_proj_kernel(*refs, n_weights, w_transposed, out_widths, out_kinds):
    track_norms = "fox_q" in out_kinds
    n_out = len(out_widths) + (1 if track_norms else 0)
    x_ref, g_ref = refs[:2]
    w_refs = refs[2:2 + n_weights]
    tab_refs = refs[2 + n_weights:len(refs) - n_out]
    out_refs = refs[len(refs) - n_out:]
    h = _rmsnorm_rows(x_ref[...], g_ref[...]).astype(bf16)
    tm = h.shape[0]
    if "rot_q" in out_kinds:
        lane = lax.broadcasted_iota(jnp.int32, (tm, LANES), 1)
        first_half = (lane % RET_KEY_DIM) < (RET_KEY_DIM // 2)
    if track_norms:
        norm_sub = lax.broadcasted_iota(jnp.int32, (8, LANES), 0)
        norm_lane = lax.broadcasted_iota(jnp.int32, (8, LANES), 1)
        norms = jnp.zeros((8, LANES), f32)

        def with_norms(norms, y, row, head0):
            for s0 in range(0, y.shape[1], HEAD_DIM):
                ys = y[:, s0:s0 + HEAD_DIM]
                worst = jnp.max(jnp.sum(ys * ys, axis=1, keepdims=True), axis=0, keepdims=True)
                norms = jnp.where((norm_sub == row) & (norm_lane == head0 + s0 // HEAD_DIM), worst, norms)
            return norms
    out_axis = 0 if w_transposed else 1
    wi, col = 0, 0
    for o_ref, width, kind in zip(out_refs, out_widths, out_kinds):
        if col == w_refs[wi].shape[out_axis]:
            wi, col = wi + 1, 0
        w_ref = w_refs[wi]
        for c0 in range(0, width, PROJ_TN):
            cw = min(PROJ_TN, width - c0)
            if w_transposed:
                y = lax.dot_general(h, w_ref[col + c0:col + c0 + cw, :], (((1,), (1,)), ((), ())),
                                    preferred_element_type=f32)
            else:
                y = jnp.dot(h, w_ref[:, col + c0:col + c0 + cw], preferred_element_type=f32)
            if kind in ("rot_q", "rot_kt"):
                cos_ref, sin_ref = tab_refs[:2] if kind == "rot_q" else tab_refs[2:]
                for s0 in range(0, cw, LANES):
                    ys = y[:, s0:s0 + LANES]
                    sw = jnp.where(first_half, pltpu.roll(ys, LANES - RET_KEY_DIM // 2, 1),
                                   pltpu.roll(ys, RET_KEY_DIM // 2, 1))
                    r = ys * cos_ref[...] + sw * sin_ref[...]
                    if kind == "rot_q":
                        o_ref[:, c0 + s0:c0 + s0 + LANES] = r.astype(o_ref.dtype)
                    else:
                        for r0 in range(tm // RET_CHUNK):
                            piece = r[r0 * RET_CHUNK:(r0 + 1) * RET_CHUNK, :]
                            o_ref[r0, c0 + s0:c0 + s0 + LANES, :] = piece.T.astype(o_ref.dtype)
            elif kind in ("kt", "fox_q"):
                if kind == "fox_q":
                    y = y * FOX_Q_SCALE
                    norms = with_norms(norms, y, 0, c0 // HEAD_DIM)
                for r0 in range(tm // FOX_T):
                    for s0 in range(0, cw, LANES):
                        piece = y[r0 * FOX_T:(r0 + 1) * FOX_T, s0:s0 + LANES]
                        o_ref[r0, c0 + s0:c0 + s0 + LANES, :] = piece.T.astype(o_ref.dtype)
            elif kind == "fox_k":
                norms = with_norms(norms, y, 1, c0 // HEAD_DIM)
                o_ref[:, c0:c0 + cw] = y.astype(o_ref.dtype)
            elif kind == "half":
                o_ref[:, c0:c0 + cw] = (y * 0.5).astype(o_ref.dtype)
            else:
                o_ref[:, c0:c0 + cw] = y.astype(o_ref.dtype)
        col += width
    if track_norms:
        out_refs[-1][...] = norms


def _norm_proj(x, g, weights, out_widths, out_dtypes, out_kinds=None, *, w_transposed=False, tables=None,
               name):
    t_rows, d = x.shape
    out_kinds = tuple(out_kinds or ["plain"] * len(out_widths))
    w_arrays = [w[0] if isinstance(w, tuple) else w for w in weights]
    w_blocks = [(w[1], d) if isinstance(w, tuple) else w.shape for w in weights]
    out_axis = 0 if w_transposed else 1
    assert sum(out_widths) == sum(blk[out_axis] for blk in w_blocks) and t_rows % PROJ_TM == 0
    tiles_per_seq = SEQ // PROJ_TM
    in_specs = [
        pl.BlockSpec((PROJ_TM, d), lambda i: (i, 0)),
        pl.BlockSpec((1, d), lambda i: (0, 0)),
    ]
    in_specs += [pl.BlockSpec(blk, lambda i: (0, 0), pipeline_mode=pl.Buffered(1)) for blk in w_blocks]
    args = [x, g.reshape(1, d), *w_arrays]
    if "rot_q" in out_kinds:
        for tab in tables:
            in_specs.append(pl.BlockSpec((PROJ_TM, LANES), lambda i: (i % tiles_per_seq, 0)))
            args.append(tab)
    out_specs, out_shape = [], []
    for wd, dt, kind in zip(out_widths, out_dtypes, out_kinds):
        if kind in ("kt", "fox_q", "rot_kt"):
            tile = RET_CHUNK if kind == "rot_kt" else FOX_T
            out_specs.append(pl.BlockSpec((None, PROJ_TM // tile, wd, tile),
                                          lambda i: (i // tiles_per_seq, i % tiles_per_seq, 0, 0)))
            out_shape.append(jax.ShapeDtypeStruct((t_rows // SEQ, SEQ // tile, wd, tile), dt))
        else:
            out_specs.append(pl.BlockSpec((PROJ_TM, wd), lambda i: (i, 0)))
            out_shape.append(jax.ShapeDtypeStruct((t_rows, wd), dt))
    if "fox_q" in out_kinds:
        out_specs.append(pl.BlockSpec((None, 8, LANES), lambda i: (i, 0, 0)))
        out_shape.append(jax.ShapeDtypeStruct((t_rows // PROJ_TM, 8, LANES), f32))
    return pl.pallas_call(
        functools.partial(_norm_proj_kernel, n_weights=len(weights), w_transposed=w_transposed,
                          out_widths=tuple(out_widths), out_kinds=out_kinds),
        grid=(t_rows // PROJ_TM,),
        in_specs=in_specs,
        out_specs=out_specs,
        out_shape=out_shape,
        compiler_params=pltpu.CompilerParams(dimension_semantics=("arbitrary",),
                                             vmem_limit_bytes=VMEM_LIMIT),
        name=name,
    )(*args)


def _fox_gate_kernel(f_ref, bf_ref, o_ref, ot_ref):
    blk = LANES
    row = lax.broadcasted_iota(jnp.int32, (blk, blk), 0)
    coli = lax.broadcasted_iota(jnp.int32, (blk, blk), 1)
    tri = jnp.where(row >= coli, 1.0, 0.0).astype(bf16)
    spread = [jnp.where((coli == BIAS_TERMS * row + t) & (row < N_MAIN_HEADS), 1.0, 0.0).astype(bf16)
              for t in range(BIAS_TERMS)]
    carry = jnp.zeros((1, F_PAD), f32)
    for b0 in range(0, SEQ, blk):
        xg = f_ref[b0:b0 + blk, :] + bf_ref[...]
        ls = jnp.minimum(xg, 0.0) - jnp.log1p(jnp.exp(-jnp.abs(xg)))
        cs = sum(jnp.dot(tri, term, preferred_element_type=f32) for term in _split_bf16(ls)) + carry
        carry = cs[blk - 1:blk, :]
        bias = cs * (-LOG2E)
        nb = sum(jnp.dot(term, sp, preferred_element_type=f32)
                 for term, sp in zip(_split_bf16(bias), spread))
        o_ref[b0:b0 + blk, :] = nb.astype(o_ref.dtype)
        ot_ref[:, b0:b0 + blk] = bias.T[:ot_ref.shape[0], :]


def _fox_gate(f, b_f):
    bf_pad = jnp.zeros((1, F_PAD), f32).at[0, :N_MAIN_HEADS].set(b_f.astype(f32))
    return pl.pallas_call(
        _fox_gate_kernel,
        grid=(BATCH,),
        in_specs=[pl.BlockSpec((SEQ, F_PAD), lambda b: (b, 0)),
                  pl.BlockSpec((1, F_PAD), lambda b: (0, 0))],
        out_specs=[pl.BlockSpec((SEQ, LANES), lambda b: (b, 0)),
                   pl.BlockSpec((None, GATE_HEAD_ROWS, SEQ), lambda b: (b, 0, 0))],
        out_shape=[jax.ShapeDtypeStruct((BATCH * SEQ, LANES), bf16),
                   jax.ShapeDtypeStruct((BATCH, GATE_HEAD_ROWS, SEQ), f32)],
        compiler_params=pltpu.CompilerParams(dimension_semantics=("arbitrary",)),
        name="fox_gate",
    )(f, bf_pad)


def _fox_attn_kernel(qt_ref, k_ref, vt_ref, nb_ref, z_ref, nbq_ref, norm_ref, o_ref,
                     m_ref, acc_ref, u_ref, p_ref, a_ref):
    t = FOX_T
    nt = SEQ // t
    head0 = pl.program_id(1) * FOX_HEADS_PER_STEP
    heads = range(FOX_HEADS_PER_STEP)
    col = [slice(hd * HEAD_DIM, (hd + 1) * HEAD_DIM) for hd in heads]
    sub = lax.broadcasted_iota(jnp.int32, (LANES, t), 0)
    sel = []
    for hd in heads:
        lo = BIAS_TERMS * (head0 + hd)
        sel.append(jnp.where((sub >= lo) & (sub < lo + BIAS_TERMS), 1.0, 0.0).astype(bf16))
    ones_rows = jnp.ones((FOX_SUM_ROWS, t), bf16)

    def tile_rows(i):
        if isinstance(i, int):
            return slice(i * t, (i + 1) * t)
        return pl.ds(pl.multiple_of(i * t, t), t)

    def score(qt, kt, hd):
        k_aug = jnp.concatenate([k_ref[tile_rows(kt), col[hd]], nb_ref[tile_rows(kt), :]], axis=1)
        q_aug_t = jnp.concatenate([qt_ref[qt, col[hd], :], sel[hd]], axis=0)
        return jnp.dot(k_aug, q_aug_t, preferred_element_type=f32)

    def scores(qt, kt):
        return [score(qt, kt, hd) for hd in heads]

    def pv1(kt, p_hd, hd):
        return jnp.dot(jnp.concatenate([vt_ref[kt, col[hd], :], ones_rows], axis=0), p_hd,
                       preferred_element_type=f32)

    def pv(kt, p):
        return [pv1(kt, p[hd], hd) for hd in heads]

    def finalize(qt):
        for hd in heads:
            acc = acc_ref[qt, hd]
            out_t = acc[:HEAD_DIM, :] * (1.0 / acc[HEAD_DIM:HEAD_DIM + 1, :])
            gate = _silu_of_half(z_ref[tile_rows(qt), col[hd]].astype(f32))
            o_ref[tile_rows(qt), col[hd]] = (out_t.T * gate).astype(o_ref.dtype)

    def next_item(qt, kt):
        wrap = kt + 1 >= qt
        return jnp.minimum(jnp.where(wrap, qt + 1, qt), nt - 1), jnp.where(wrap, 0, kt + 1)

    def load_list(ref):
        return [ref[hd] for hd in heads]

    def store_list(ref, vals):
        for hd in heads:
            ref[hd] = vals[hd]

    def run(online):
        def numerators(qt, u, masked):
            half = t // 2
            if masked:
                ksub = lax.broadcasted_iota(jnp.int32, (t if online else half,) * 2, 0)
                qlane = lax.broadcasted_iota(jnp.int32, (t if online else half,) * 2, 1)
                causal = ksub <= qlane
            p, alpha = [], []
            for hd in heads:
                m_old = m_ref[qt, hd]
                if online:
                    uh = jnp.where(causal, u[hd], NEG) if masked else u[hd]
                    m_new = jnp.maximum(m_old, jnp.max(uh, axis=0, keepdims=True))
                    alpha.append(jnp.exp2(m_old - m_new))
                    m_ref[qt, hd] = m_new
                    p.append(jnp.exp2(uh - m_new).astype(bf16))
                elif masked:
                    lo, hi = slice(0, half), slice(half, t)
                    p_ll = jnp.exp2(jnp.where(causal, u[hd][lo, lo], NEG) - m_old[:, lo])
                    p_lh = jnp.exp2(u[hd][lo, hi] - m_old[:, hi])
                    p_hh = jnp.exp2(jnp.where(causal, u[hd][hi, hi], NEG) - m_old[:, hi])
                    top = jnp.concatenate([p_ll, p_lh], axis=1)
                    bottom = jnp.concatenate([jnp.zeros_like(p_hh), p_hh], axis=1)
                    p.append(jnp.concatenate([top, bottom], axis=0).astype(bf16))
                else:
                    p.append(jnp.exp2(u[hd] - m_old).astype(bf16))
            return p, alpha

        def accumulate(qt, alpha, pv_vals):
            for hd in heads:
                prev = alpha[hd] * acc_ref[qt, hd] if online else acc_ref[qt, hd]
                acc_ref[qt, hd] = prev + pv_vals[hd]

        acc_ref[...] = jnp.zeros(acc_ref.shape, f32)
        p_ref[...] = jnp.zeros(p_ref.shape, bf16)
        if online:
            m_ref[...] = jnp.full(m_ref.shape, NEG, f32)
            a_ref[...] = jnp.ones(a_ref.shape, f32)

        n_items = nt * (nt - 1) // 2
        ni = FOX_ITEMS_PER_ITER_ONLINE if online else FOX_ITEMS_PER_ITER
        assert n_items % ni == 0
        store_list(u_ref, scores(1, 0))

        def pass1(_, carry):
            q0, k0, q_prev, k_prev = carry
            items = [(q0, k0)]
            for _k in range(ni):
                items.append(next_item(*items[-1]))
            u_cur = load_list(u_ref)
            pv_vals = pv(k_prev, load_list(p_ref))
            u_next = scores(*items[1])
            p_cur, al_cur = numerators(q0, u_cur, False)
            accumulate(q_prev, load_list(a_ref) if online else None, pv_vals)
            for k in range(1, ni):
                u_cur, u_next, pv_vals = u_next, [], []
                for hd in heads:
                    s = score(items[k + 1][0], items[k + 1][1], hd)
                    if k == ni - 1:
                        u_ref[hd] = s
                    else:
                        u_next.append(s)
                    pv_vals.append(pv1(items[k - 1][1], p_cur[hd], hd))
                p_new, al_new = numerators(items[k][0], u_cur, False)
                accumulate(items[k - 1][0], al_cur, pv_vals)
                p_cur, al_cur = p_new, al_new
            store_list(p_ref, p_cur)
            if online:
                store_list(a_ref, al_cur)
            return items[ni] + items[ni - 1]

        one, zero = jnp.int32(1), jnp.int32(0)
        _, _, q_prev, k_prev = lax.fori_loop(0, n_items // ni, pass1, (one, zero, one, zero))
        accumulate(q_prev, load_list(a_ref) if online else None, pv(k_prev, load_list(p_ref)))

        nd = FOX_ITEMS_PER_ITER_ONLINE if online else FOX_DIAG_TILES_PER_ITER
        assert nt % nd == 0
        store_list(u_ref, scores(0, 0))

        def pass2(i, carry):
            u_cur = load_list(u_ref)
            for k in range(nd):
                tile = nd * i + k
                nxt = jnp.minimum(tile + 1, nt - 1)
                u_next = scores(nxt, nxt)
                p_cur, al_cur = numerators(tile, u_cur, True)
                accumulate(tile, al_cur, pv(tile, p_cur))
                finalize(tile)
                u_cur = u_next
            store_list(u_ref, u_cur)
            return carry

        lax.fori_loop(0, nt // nd, pass2, 0)

    norms = jnp.max(norm_ref[...], axis=0)
    hlane = lax.broadcasted_iota(jnp.int32, (1, LANES), 1)
    for hd in heads:
        pick = hlane == head0 + hd
        q2 = jnp.max(jnp.where(pick, norms[0:1, :], 0.0), axis=1, keepdims=True)
        k2 = jnp.max(jnp.where(pick, norms[1:2, :], 0.0), axis=1, keepdims=True)
        qk_bound = jnp.sqrt(q2 * k2) * FOX_BOUND_MARGIN + 1.0
        for qt in range(nt):
            m_ref[qt, hd] = nbq_ref[hd, qt:qt + 1, :] + qk_bound

    def run_fixed_shift():
        half = t // 2
        ksub = lax.broadcasted_iota(jnp.int32, (half, half), 0)
        qlane = lax.broadcasted_iota(jnp.int32, (half, half), 1)
        causal = ksub <= qlane
        lo, hi = slice(0, half), slice(half, t)
        units = [(qt, hd) for qt in range(nt) for hd in heads]

        def unit_scores(qt, hd):
            return [score(qt, kt, hd) for kt in range(qt + 1)]

        def unit_numerators(qt, hd, u):
            m = m_ref[qt, hd]
            p = [jnp.exp2(u[kt] - m).astype(bf16) for kt in range(qt)]
            ud = u[qt]
            p_ll = jnp.exp2(jnp.where(causal, ud[lo, lo], NEG) - m[:, lo])
            p_lh = jnp.exp2(ud[lo, hi] - m[:, hi])
            p_hh = jnp.exp2(jnp.where(causal, ud[hi, hi], NEG) - m[:, hi])
            top = jnp.concatenate([p_ll, p_lh], axis=1)
            bottom = jnp.concatenate([jnp.zeros_like(p_hh), p_hh], axis=1)
            p.append(jnp.concatenate([top, bottom], axis=0).astype(bf16))
            return p

        def unit_output(qt, hd, p):
            vt_all = jnp.concatenate([vt_ref[kt, col[hd], :] for kt in range(qt + 1)], axis=1)
            vt_all = jnp.concatenate([vt_all, jnp.ones((FOX_SUM_ROWS, vt_all.shape[1]), bf16)], axis=0)
            acc = jnp.dot(vt_all, jnp.concatenate(p, axis=0), preferred_element_type=f32)
            denom = acc[HEAD_DIM:HEAD_DIM + 1, :]
            out_t = acc[:HEAD_DIM, :] * (1.0 / denom)
            gate = _silu_of_half(z_ref[qt * t:(qt + 1) * t, col[hd]].astype(f32))
            o_ref[qt * t:(qt + 1) * t, col[hd]] = (out_t.T * gate).astype(o_ref.dtype)
            return denom

        denom_min = None
        u_next = unit_scores(*units[0])
        for w, (qt, hd) in enumerate(units):
            u_cur = u_next
            if w + 1 < len(units):
                u_next = unit_scores(*units[w + 1])
            denom = unit_output(qt, hd, unit_numerators(qt, hd, u_cur))
            denom_min = denom if denom_min is None else jnp.minimum(denom_min, denom)
        return jnp.min(denom_min)

    denom_min = run_fixed_shift()

    @pl.when(jnp.logical_not(denom_min >= FOX_MIN_DENOM))
    def _():
        run(online=True)


def _fox_attention(q, k, vt, nb, z, nbq, norms):
    hw = FOX_HEADS_PER_STEP * HEAD_DIM
    return pl.pallas_call(
        _fox_attn_kernel,
        grid=(BATCH, N_MAIN_HEADS // FOX_HEADS_PER_STEP),
        in_specs=[
            pl.BlockSpec((None, SEQ // FOX_T, hw, FOX_T), lambda b, p: (b, 0, p, 0)),
            pl.BlockSpec((SEQ, hw), lambda b, p: (b, p)),
            pl.BlockSpec((None, SEQ // FOX_T, hw, FOX_T), lambda b, p: (b, 0, p, 0)),
            pl.BlockSpec((SEQ, LANES), lambda b, p: (b, 0)),
            pl.BlockSpec((SEQ, hw), lambda b, p: (b, p)),
            pl.BlockSpec((None, FOX_HEADS_PER_STEP, SEQ // FOX_T, FOX_T), lambda b, p: (b, p, 0, 0)),
            pl.BlockSpec((SEQ // PROJ_TM, 8, LANES), lambda b, p: (b, 0, 0)),
        ],
        out_specs=pl.BlockSpec((SEQ, hw), lambda b, p: (b, p)),
        out_shape=jax.ShapeDtypeStruct((BATCH * SEQ, MAIN_W), bf16),
        scratch_shapes=[
            pltpu.VMEM((SEQ // FOX_T, FOX_HEADS_PER_STEP, 1, FOX_T), f32),
            pltpu.VMEM((SEQ // FOX_T, FOX_HEADS_PER_STEP, HEAD_DIM + FOX_SUM_ROWS, FOX_T), f32),
            pltpu.VMEM((FOX_HEADS_PER_STEP, FOX_T, FOX_T), f32),
            pltpu.VMEM((FOX_HEADS_PER_STEP, FOX_T, FOX_T), bf16),
            pltpu.VMEM((FOX_HEADS_PER_STEP, 1, FOX_T), f32),
        ],
        compiler_params=pltpu.CompilerParams(dimension_semantics=("arbitrary", "arbitrary"),
                                             vmem_limit_bytes=VMEM_LIMIT),
        name="fox_attention",
    )(q, k, vt, nb, z, nbq, norms)


def _retention_kernel(q_ref, kt_ref, v_ref, z_ref, d_ref, eps_ref, zeta_ref, g_ref, o_ref, r_ref):
    c_len = RET_CHUNK
    r_ref[...] = jnp.zeros(r_ref.shape, f32)
    lane = lax.broadcasted_iota(jnp.int32, (c_len, LANES), 1)
    sub = lax.broadcasted_iota(jnp.int32, (LANES, c_len), 0)
    q_mask = (lane < RET_KEY_DIM, lane >= RET_KEY_DIM)
    k_mask = (sub < RET_KEY_DIM, sub >= RET_KEY_DIM)
    pairs = range(RET_PAIRS_PER_STEP)
    chunks = range(RET_CHUNKS_PER_ITER)

    def body(it, carry):
        chunk = [it * RET_CHUNKS_PER_ITER + ci for ci in chunks]
        rows = [pl.ds(pl.multiple_of(chunk[ci] * c_len, c_len), c_len) for ci in chunks]
        q2, s, upd = {}, {}, {}
        for ci in chunks:
            for pr in pairs:
                q2[ci, pr] = q_ref[rows[ci], pr * LANES:(pr + 1) * LANES]
                kt2 = kt_ref[chunk[ci], pr * LANES:(pr + 1) * LANES, :]
                v2 = v_ref[rows[ci], pr * 2 * HEAD_DIM:(pr + 1) * 2 * HEAD_DIM]
                k_heads = jnp.concatenate([jnp.where(k_mask[hd], kt2, jnp.zeros_like(kt2))
                                           for hd in range(2)], axis=1)
                s[ci, pr] = jnp.dot(q2[ci, pr], k_heads, preferred_element_type=f32)
                kz = (kt2.astype(f32) * zeta_ref[pr]).astype(bf16)
                upd[ci, pr] = jnp.dot(kz, v2, preferred_element_type=f32)
        r_b = {}
        for pr in pairs:
            r_state = r_ref[pr]
            for ci in chunks:
                r_b[ci, pr] = r_state.astype(bf16)
                r_state = r_state * g_ref[pr] + upd[ci, pr]
            r_ref[pr] = r_state
        o = {}
        for ci in chunks:
            for pr in pairs:
                inner = (s[ci, pr] * d_ref[pr]).astype(bf16)
                for hd in range(2):
                    cols = slice((2 * pr + hd) * HEAD_DIM, (2 * pr + hd + 1) * HEAD_DIM)
                    q_hd = jnp.where(q_mask[hd], q2[ci, pr], jnp.zeros_like(q2[ci, pr]))
                    lhs = jnp.concatenate([inner[:, hd * c_len:(hd + 1) * c_len], q_hd], axis=1)
                    rhs = jnp.concatenate([v_ref[rows[ci], cols],
                                           r_b[ci, pr][:, hd * HEAD_DIM:(hd + 1) * HEAD_DIM]], axis=0)
                    o[ci, pr, hd] = jnp.dot(lhs, rhs, preferred_element_type=f32)
        for ci in chunks:
            for pr in pairs:
                for hd in range(2):
                    cols = slice((2 * pr + hd) * HEAD_DIM, (2 * pr + hd + 1) * HEAD_DIM)
                    oh = o[ci, pr, hd]
                    ms = jnp.mean(oh * oh, axis=-1, keepdims=True)
                    on = oh * lax.rsqrt(ms + eps_ref[2 * pr + hd])
                    zg = _silu_of_half(z_ref[rows[ci], cols].astype(f32))
                    o_ref[rows[ci], cols] = (on * zg).astype(o_ref.dtype)
        return carry

    lax.fori_loop(0, SEQ // (c_len * RET_CHUNKS_PER_ITER), body, 0)


def _retention_tables():
    h, c_len = N_MAIN_HEADS, RET_CHUNK
    lg = np.log1p(-np.exp2(-5.0 - np.arange(h, dtype=np.float64)))
    n = np.arange(c_len, dtype=np.float64)
    causal = n[:, None] >= n[None, :]
    d_col = np.where(causal[None], np.exp(-lg[:, None, None] * (n[None, None, :] + 1.0)), 0.0)
    d_pair = d_col.reshape(h // 2, 2, c_len, c_len).transpose(0, 2, 1, 3).reshape(h // 2, c_len, 2 * c_len)
    xi = np.exp(lg[:, None] * (n[None, :] + 1.0))
    zeta = np.exp(lg[:, None] * (c_len - 1.0 - n[None, :]))
    g_chunk = np.exp(lg * c_len)
    eps_rows = np.broadcast_to((EPS / (xi * xi))[:, :, None], (h, c_len, HEAD_DIM))
    zeta_t = np.repeat(zeta.reshape(h // 2, 2, c_len), RET_KEY_DIM, axis=1)
    g2 = np.broadcast_to(np.repeat(g_chunk.reshape(h // 2, 2), RET_KEY_DIM, axis=-1)[:, :, None],
                         (h // 2, 2 * RET_KEY_DIM, 2 * HEAD_DIM))
    return tuple(jnp.asarray(np.ascontiguousarray(t), dtype=f32) for t in (d_pair, eps_rows, zeta_t, g2))


def _retention(q, kt, v, z):
    n_pairs = N_MAIN_HEADS // 2
    pp = RET_PAIRS_PER_STEP
    steps = n_pairs // pp
    d_pair, eps_rows, zeta_t, g2 = _retention_tables()
    c_len = RET_CHUNK
    return pl.pallas_call(
        _retention_kernel,
        grid=(BATCH, steps),
        in_specs=[
            pl.BlockSpec((SEQ, pp * LANES), lambda b, p: (b, p)),
            pl.BlockSpec((None, SEQ // c_len, pp * LANES, c_len), lambda b, p: (b, 0, p, 0)),
            pl.BlockSpec((SEQ, pp * 2 * HEAD_DIM), lambda b, p: (b, p)),
            pl.BlockSpec((SEQ, pp * 2 * HEAD_DIM), lambda b, p: (b, p)),
            pl.BlockSpec((pp, c_len, 2 * c_len), lambda b, p: (p, 0, 0)),
            pl.BlockSpec((2 * pp, c_len, HEAD_DIM), lambda b, p: (p, 0, 0)),
            pl.BlockSpec((pp, LANES, c_len), lambda b, p: (p, 0, 0)),
            pl.BlockSpec((pp, LANES, 2 * HEAD_DIM), lambda b, p: (p, 0, 0)),
        ],
        out_specs=pl.BlockSpec((SEQ, pp * 2 * HEAD_DIM), lambda b, p: (b, p)),
        out_shape=jax.ShapeDtypeStruct((BATCH * SEQ, MAIN_W), bf16),
        scratch_shapes=[pltpu.VMEM((pp, LANES, 2 * HEAD_DIM), f32)],
        compiler_params=pltpu.CompilerParams(dimension_semantics=("arbitrary", "arbitrary"),
                                             vmem_limit_bytes=VMEM_LIMIT),
        name="retention",
    )(q, kt, v, z, d_pair, eps_rows, zeta_t, g2)


def _out_proj_kernel(om_ref, qm_ref, zm_ref, kv_ref, w_ref, x_ref, g_ref, o_ref, *, final_norm):
    scale = 1.0 / math.sqrt(HEAD_DIM)
    heads = range(N_MEM_HEADS)
    col = [slice(hd * HEAD_DIM, (hd + 1) * HEAD_DIM) for hd in heads]
    half = D_MODEL // 2
    s = [lax.dot_general(qm_ref[:, col[hd]], kv_ref[:, col[hd]], (((1,), (1,)), ((), ())),
                         preferred_element_type=f32) * scale for hd in heads]
    y_lo = jnp.dot(om_ref[...], w_ref[:MAIN_W, :half], preferred_element_type=f32)
    p = []
    for hd in heads:
        e = jnp.exp(s[hd] - jnp.max(s[hd], axis=-1, keepdims=True))
        p.append((e * (1.0 / jnp.sum(e, axis=-1, keepdims=True))).astype(bf16))
    memo = [jnp.dot(p[hd], kv_ref[:, MEM_W + hd * HEAD_DIM:MEM_W + (hd + 1) * HEAD_DIM],
                    preferred_element_type=f32) for hd in heads]
    y_hi = jnp.dot(om_ref[...], w_ref[:MAIN_W, half:], preferred_element_type=f32)
    og = jnp.concatenate([(memo[hd] * _silu_of_half(zm_ref[:, col[hd]].astype(f32))).astype(bf16) for hd in heads],
                         axis=1)
    y = jnp.concatenate([y_lo, y_hi], axis=1) + jnp.dot(og, w_ref[MAIN_W:, :], preferred_element_type=f32)
    xn = x_ref[...] + y
    if final_norm:
        xn = _rmsnorm_rows(xn, g_ref[...])
    o_ref[...] = xn


def _out_proj(o_main, qm, z, kv, w_out, x, g_final, *, final_norm):
    t_rows = x.shape[0]
    tiles_per_seq = SEQ // OUT_TM
    z_blk = MAIN_W // MEM_W
    return pl.pallas_call(
        functools.partial(_out_proj_kernel, final_norm=final_norm),
        grid=(t_rows // OUT_TM,),
        in_specs=[
            pl.BlockSpec((OUT_TM, MAIN_W), lambda i: (i, 0)),
            pl.BlockSpec((OUT_TM, MEM_W), lambda i: (i, 0)),
            pl.BlockSpec((OUT_TM, MEM_W), lambda i: (i, z_blk)),
            pl.BlockSpec((N_MEM, 2 * MEM_W), lambda i: (i // tiles_per_seq, 0)),
            pl.BlockSpec((INNER, D_MODEL), lambda i: (0, 0), pipeline_mode=pl.Buffered(1)),
            pl.BlockSpec((OUT_TM, D_MODEL), lambda i: (i, 0)),
            pl.BlockSpec((1, D_MODEL), lambda i: (0, 0)),
        ],
        out_specs=pl.BlockSpec((OUT_TM, D_MODEL), lambda i: (i, 0)),
        out_shape=jax.ShapeDtypeStruct((t_rows, D_MODEL), f32),
        compiler_params=pltpu.CompilerParams(dimension_semantics=("arbitrary",),
                                             vmem_limit_bytes=VMEM_LIMIT),
        name="out_proj_final" if final_norm else "out_proj",
    )(o_main, qm, z, kv, w_out, x, g_final.reshape(1, D_MODEL))


def _rotary_tables():
    half = RET_KEY_DIM // 2
    pos = np.arange(SEQ, dtype=np.float64)
    inv = 1.0 / (ROPE_BASE ** (np.arange(half, dtype=np.float64) / half))
    ang = pos[:, None] * inv[None, :]
    cos, sin = np.cos(ang), np.sin(ang)
    reps = LANES // RET_KEY_DIM
    cos_t = np.tile(np.concatenate([cos, cos], axis=-1), (1, reps))
    sin_t = np.tile(np.concatenate([-sin, sin], axis=-1), (1, reps))
    k_scale = RET_KEY_DIM ** -0.5
    return tuple(jnp.asarray(t, dtype=f32) for t in (cos_t, sin_t, cos_t * k_scale, sin_t * k_scale))


def kernel(x, mem, norm_g, fox_w_in, fox_b_f, ret_w_in, mem_norm_g, w_mem_kv, w_out, final_norm_g):
    t_rows = BATCH * SEQ
    x2 = x.reshape(t_rows, D_MODEL)

    kv0, kv1 = _norm_proj(mem.reshape(BATCH * N_MEM, D_MODEL), mem_norm_g,
                          [w_mem_kv[0].astype(bf16), w_mem_kv[1].astype(bf16)],
                          [2 * MEM_W, 2 * MEM_W], [bf16, bf16], name="mem_kv_proj")

    o_f = 3 * MAIN_W
    o_qm = o_f + N_MAIN_HEADS
    w0t = jnp.swapaxes(fox_w_in[0], 0, 1).astype(bf16)
    w_qmz = w0t[o_qm:]
    w_f = jnp.pad(w0t[o_f:o_qm], ((0, F_PAD - N_MAIN_HEADS), (0, 0)))
    q, k, vt, qm, z, f, norms = _norm_proj(
        x2, norm_g[0], [(w0t, o_f), w_qmz, w_f], [MAIN_W, MAIN_W, MAIN_W, MEM_W, INNER, F_PAD],
        [bf16, bf16, bf16, bf16, bf16, f32], ["fox_q", "fox_k", "kt", "plain", "half", "plain"],
        w_transposed=True, name="fox_in_proj")
    nb, nbq = _fox_gate(f, fox_b_f[0])
    nbq = nbq.reshape(BATCH, GATE_HEAD_ROWS, SEQ // FOX_T, FOX_T)
    o_main = _fox_attention(q, k, vt, nb, z, nbq, norms)
    x2 = _out_proj(o_main, qm, z, kv0, w_out[0].astype(bf16), x2, final_norm_g, final_norm=False)

    w1 = ret_w_in[0].astype(bf16)
    q, kt, v, qm, z = _norm_proj(
        x2, norm_g[1], [w1], [RET_QK_W, RET_QK_W, MAIN_W, MEM_W, INNER], [bf16] * 5,
        ["rot_q", "rot_kt", "plain", "plain", "half"], tables=_rotary_tables(), name="ret_in_proj")
    o_main = _retention(q, kt, v, z)
    out = _out_proj(o_main, qm, z, kv1, w_out[1].astype(bf16), x2, final_norm_g, final_norm=True)
    return out.reshape(BATCH, SEQ, D_MODEL)
```

```python
import functools
import math

import jax
import jax.numpy as jnp
import numpy as np
from jax import lax
from jax.experimental import pallas as pl
from jax.experimental.pallas import tpu as pltpu

D_MODEL = 1024
BATCH = 8
SEQ = 2048
HEAD_DIM = 128
N_MAIN_HEADS = 12
N_MEM_HEADS = 4
N_MEM = 256
MAIN_W = N_MAIN_HEADS * HEAD_DIM
MEM_W = N_MEM_HEADS * HEAD_DIM
INNER = MAIN_W + MEM_W
RET_KEY_DIM = HEAD_DIM // 2
RET_QK_W = N_MAIN_HEADS * RET_KEY_DIM
RET_CHUNK = 128
ROPE_BASE = 10000.0
EPS = 1e-6
NEG = -1e30
LOG2E = 1.4426950408889634

LANES = 128
F_PAD = LANES
VMEM_LIMIT = 56 * 1024 * 1024

PROJ_TM = 512
PROJ_TN = 512
FOX_T = 256
FOX_HEADS_PER_STEP = 6
FOX_ITEMS_PER_ITER = 7
FOX_ITEMS_PER_ITER_ONLINE = 2
FOX_DIAG_TILES_PER_ITER = 4
BIAS_TERMS = 3
FOX_SUM_ROWS = 16
FOX_Q_SCALE = LOG2E / math.sqrt(HEAD_DIM)
FOX_BOUND_MARGIN = 1.02
FOX_MIN_DENOM = 2.0 ** -90
GATE_HEAD_ROWS = 16
RET_PAIRS_PER_STEP = 3
RET_CHUNKS_PER_ITER = 2
OUT_TM = 1024

f32 = jnp.float32
bf16 = jnp.bfloat16


def _silu_of_half(h):
    return h + h * jnp.tanh(h)


def _rmsnorm_rows(x, g):
    ms = jnp.mean(x * x, axis=-1, keepdims=True)
    return (x * lax.rsqrt(ms + EPS)) * g


def _split_bf16(v):
    hi = v.astype(bf16)
    r1 = v - hi.astype(f32)
    mid = r1.astype(bf16)
    lo = (r1 - mid.astype(f32)).astype(bf16)
    return hi, mid, lo


def _norm_proj_kernel(*refs, n_weights, w_transposed, out_widths, out_kinds):
    track_norms = "fox_q" in out_kinds
    n_out = len(out_widths) + (1 if track_norms else 0)
    x_ref, g_ref = refs[:2]
    w_refs = refs[2:2 + n_weights]
    tab_refs = refs[2 + n_weights:len(refs) - n_out]
    out_refs = refs[len(refs) - n_out:]
    h = _rmsnorm_rows(x_ref[...], g_ref[...]).astype(bf16)
    tm = h.shape[0]
    if "rot_q" in out_kinds:
        lane = lax.broadcasted_iota(jnp.int32, (tm, LANES), 1)
        first_half = (lane % RET_KEY_DIM) < (RET_KEY_DIM // 2)
    if track_norms:
        norm_sub = lax.broadcasted_iota(jnp.int32, (8, LANES), 0)
        norm_lane = lax.broadcasted_iota(jnp.int32, (8, LANES), 1)
        norms = jnp.zeros((8, LANES), f32)

        def with_norms(norms, y, row, head0):
            for s0 in range(0, y.shape[1], HEAD_DIM):
                ys = y[:, s0:s0 + HEAD_DIM]
                worst = jnp.max(jnp.sum(ys * ys, axis=1, keepdims=True), axis=0, keepdims=True)
                norms = jnp.where((norm_sub == row) & (norm_lane == head0 + s0 // HEAD_DIM), worst, norms)
            return norms
    out_axis = 0 if w_transposed else 1
    wi, col = 0, 0
    for o_ref, width, kind in zip(out_refs, out_widths, out_kinds):
        if col == w_refs[wi].shape[out_axis]:
            wi, col = wi + 1, 0
        w_ref = w_refs[wi]
        for c0 in range(0, width, PROJ_TN):
            cw = min(PROJ_TN, width - c0)
            if w_transposed:
                y = lax.dot_general(h, w_ref[col + c0:col + c0 + cw, :], (((1,), (1,)), ((), ())),
                                    preferred_element_type=f32)
            else:
                y = jnp.dot(h, w_ref[:, col + c0:col + c0 + cw], preferred_element_type=f32)
            if kind in ("rot_q", "rot_kt"):
                cos_ref, sin_ref = tab_refs[:2] if kind == "rot_q" else tab_refs[2:]
                for s0 in range(0, cw, LANES):
                    ys = y[:, s0:s0 + LANES]
                    sw = jnp.where(first_half, pltpu.roll(ys, LANES - RET_KEY_DIM // 2, 1),
                                   pltpu.roll(ys, RET_KEY_DIM // 2, 1))
                    r = ys * cos_ref[...] + sw * sin_ref[...]
                    if kind == "rot_q":
                        o_ref[:, c0 + s0:c0 + s0 + LANES] = r.astype(o_ref.dtype)
                    else:
                        for r0 in range(tm // RET_CHUNK):
                            piece = r[r0 * RET_CHUNK:(r0 + 1) * RET_CHUNK, :]
                            o_ref[r0, c0 + s0:c0 + s0 + LANES, :] = piece.T.astype(o_ref.dtype)
            elif kind in ("kt", "fox_q"):
                if kind == "fox_q":
                    y = y * FOX_Q_SCALE
                    norms = with_norms(norms, y, 0, c0 // HEAD_DIM)
                for r0 in range(tm // FOX_T):
                    for s0 in range(0, cw, LANES):
                        piece = y[r0 * FOX_T:(r0 + 1) * FOX_T, s0:s0 + LANES]
                        o_ref[r0, c0 + s0:c0 + s0 + LANES, :] = piece.T.astype(o_ref.dtype)
            elif kind == "fox_k":
                norms = with_norms(norms, y, 1, c0 // HEAD_DIM)
                o_ref[:, c0:c0 + cw] = y.astype(o_ref.dtype)
            elif kind == "half":
                o_ref[:, c0:c0 + cw] = (y * 0.5).astype(o_ref.dtype)
            else:
                o_ref[:, c0:c0 + cw] = y.astype(o_ref.dtype)
        col += width
    if track_norms:
        out_refs[-1][...] = norms


def _norm_proj(x, g, weights, out_widths, out_dtypes, out_kinds=None, *, w_transposed=False, tables=None,
               name):
    t_rows, d = x.shape
    out_kinds = tuple(out_kinds or ["plain"] * len(out_widths))
    w_arrays = [w[0] if isinstance(w, tuple) else w for w in weights]
    w_blocks = [(w[1], d) if isinstance(w, tuple) else w.shape for w in weights]
    out_axis = 0 if w_transposed else 1
    assert sum(out_widths) == sum(blk[out_axis] for blk in w_blocks) and t_rows % PROJ_TM == 0
    tiles_per_seq = SEQ // PROJ_TM
    in_specs = [
        pl.BlockSpec((PROJ_TM, d), lambda i: (i, 0)),
        pl.BlockSpec((1, d), lambda i: (0, 0)),
    ]
    in_specs += [pl.BlockSpec(blk, lambda i: (0, 0), pipeline_mode=pl.Buffered(1)) for blk in w_blocks]
    args = [x, g.reshape(1, d), *w_arrays]
    if "rot_q" in out_kinds:
        for tab in tables:
            in_specs.append(pl.BlockSpec((PROJ_TM, LANES), lambda i: (i % tiles_per_seq, 0)))
            args.append(tab)
    out_specs, out_shape = [], []
    for wd, dt, kind in zip(out_widths, out_dtypes, out_kinds):
        if kind in ("kt", "fox_q", "rot_kt"):
            tile = RET_CHUNK if kind == "rot_kt" else FOX_T
            out_specs.append(pl.BlockSpec((None, PROJ_TM // tile, wd, tile),
                                          lambda i: (i // tiles_per_seq, i % tiles_per_seq, 0, 0)))
            out_shape.append(jax.ShapeDtypeStruct((t_rows // SEQ, SEQ // tile, wd, tile), dt))
        else:
            out_specs.append(pl.BlockSpec((PROJ_TM, wd), lambda i: (i, 0)))
            out_shape.append(jax.ShapeDtypeStruct((t_rows, wd), dt))
    if "fox_q" in out_kinds:
        out_specs.append(pl.BlockSpec((None, 8, LANES), lambda i: (i, 0, 0)))
        out_shape.append(jax.ShapeDtypeStruct((t_rows // PROJ_TM, 8, LANES), f32))
    return pl.pallas_call(
        functools.partial(_norm_proj_kernel, n_weights=len(weights), w_transposed=w_transposed,
                          out_widths=tuple(out_widths), out_kinds=out_kinds),
        grid=(t_rows // PROJ_TM,),
        in_specs=in_specs,
        out_specs=out_specs,
        out_shape=out_shape,
        compiler_params=pltpu.CompilerParams(dimension_semantics=("arbitrary",),
                                             vmem_limit_bytes=VMEM_LIMIT),
        name=name,
    )(*args)


def _fox_gate_kernel(f_ref, bf_ref, o_ref, ot_ref):
    blk = LANES
    row = lax.broadcasted_iota(jnp.int32, (blk, blk), 0)
    coli = lax.broadcasted_iota(jnp.int32, (blk, blk), 1)
    tri = jnp.where(row >= coli, 1.0, 0.0).astype(bf16)
    spread = [jnp.where((coli == BIAS_TERMS * row + t) & (row < N_MAIN_HEADS), 1.0, 0.0).astype(bf16)
              for t in range(BIAS_TERMS)]
    carry = jnp.zeros((1, F_PAD), f32)
    for b0 in range(0, SEQ, blk):
        xg = f_ref[b0:b0 + blk, :] + bf_ref[...]
        ls = jnp.minimum(xg, 0.0) - jnp.log1p(jnp.exp(-jnp.abs(xg)))
        cs = sum(jnp.dot(tri, term, preferred_element_type=f32) for term in _split_bf16(ls)) + carry
        carry = cs[blk - 1:blk, :]
        bias = cs * (-LOG2E)
        nb = sum(jnp.dot(term, sp, preferred_element_type=f32)
                 for term, sp in zip(_split_bf16(bias), spread))
        o_ref[b0:b0 + blk, :] = nb.astype(o_ref.dtype)
        ot_ref[:, b0:b0 + blk] = bias.T[:ot_ref.shape[0], :]


def _fox_gate(f, b_f):
    bf_pad = jnp.zeros((1, F_PAD), f32).at[0, :N_MAIN_HEADS].set(b_f.astype(f32))
    return pl.pallas_call(
        _fox_gate_kernel,
        grid=(BATCH,),
        in_specs=[pl.BlockSpec((SEQ, F_PAD), lambda b: (b, 0)),
                  pl.BlockSpec((1, F_PAD), lambda b: (0, 0))],
        out_specs=[pl.BlockSpec((SEQ, LANES), lambda b: (b, 0)),
                   pl.BlockSpec((None, GATE_HEAD_ROWS, SEQ), lambda b: (b, 0, 0))],
        out_shape=[jax.ShapeDtypeStruct((BATCH * SEQ, LANES), bf16),
                   jax.ShapeDtypeStruct((BATCH, GATE_HEAD_ROWS, SEQ), f32)],
        compiler_params=pltpu.CompilerParams(dimension_semantics=("arbitrary",)),
        name="fox_gate",
    )(f, bf_pad)


def _fox_attn_kernel(qt_ref, k_ref, vt_ref, nb_ref, z_ref, nbq_ref, norm_ref, o_ref,
                     m_ref, acc_ref, u_ref, p_ref, a_ref):
    t = FOX_T
    nt = SEQ // t
    head0 = pl.program_id(1) * FOX_HEADS_PER_STEP
    heads = range(FOX_HEADS_PER_STEP)
    col = [slice(hd * HEAD_DIM, (hd + 1) * HEAD_DIM) for hd in heads]
    sub = lax.broadcasted_iota(jnp.int32, (LANES, t), 0)
    sel = []
    for hd in heads:
        lo = BIAS_TERMS * (head0 + hd)
        sel.append(jnp.where((sub >= lo) & (sub < lo + BIAS_TERMS), 1.0, 0.0).astype(bf16))
    ones_rows = jnp.ones((FOX_SUM_ROWS, t), bf16)

    def tile_rows(i):
        if isinstance(i, int):
            return slice(i * t, (i + 1) * t)
        return pl.ds(pl.multiple_of(i * t, t), t)

    def score(qt, kt, hd):
        k_aug = jnp.concatenate([k_ref[tile_rows(kt), col[hd]], nb_ref[tile_rows(kt), :]], axis=1)
        q_aug_t = jnp.concatenate([qt_ref[qt, col[hd], :], sel[hd]], axis=0)
        return jnp.dot(k_aug, q_aug_t, preferred_element_type=f32)

    def scores(qt, kt):
        return [score(qt, kt, hd) for hd in heads]

    def pv1(kt, p_hd, hd):
        return jnp.dot(jnp.concatenate([vt_ref[kt, col[hd], :], ones_rows], axis=0), p_hd,
                       preferred_element_type=f32)

    def pv(kt, p):
        return [pv1(kt, p[hd], hd) for hd in heads]

    def finalize(qt):
        for hd in heads:
            acc = acc_ref[qt, hd]
            out_t = acc[:HEAD_DIM, :] * (1.0 / acc[HEAD_DIM:HEAD_DIM + 1, :])
            gate = _silu_of_half(z_ref[tile_rows(qt), col[hd]].astype(f32))
            o_ref[tile_rows(qt), col[hd]] = (out_t.T * gate).astype(o_ref.dtype)

    def next_item(qt, kt):
        wrap = kt + 1 >= qt
        return jnp.minimum(jnp.where(wrap, qt + 1, qt), nt - 1), jnp.where(wrap, 0, kt + 1)

    def load_list(ref):
        return [ref[hd] for hd in heads]

    def store_list(ref, vals):
        for hd in heads:
            ref[hd] = vals[hd]

    def run(online):
        def numerators(qt, u, masked):
            half = t // 2
            if masked:
                ksub = lax.broadcasted_iota(jnp.int32, (t if online else half,) * 2, 0)
                qlane = lax.broadcasted_iota(jnp.int32, (t if online else half,) * 2, 1)
                causal = ksub <= qlane
            p, alpha = [], []
            for hd in heads:
                m_old = m_ref[qt, hd]
                if online:
                    uh = jnp.where(causal, u[hd], NEG) if masked else u[hd]
                    m_new = jnp.maximum(m_old, jnp.max(uh, axis=0, keepdims=True))
                    alpha.append(jnp.exp2(m_old - m_new))
                    m_ref[qt, hd] = m_new
                    p.append(jnp.exp2(uh - m_new).astype(bf16))
                elif masked:
                    lo, hi = slice(0, half), slice(half, t)
                    p_ll = jnp.exp2(jnp.where(causal, u[hd][lo, lo], NEG) - m_old[:, lo])
                    p_lh = jnp.exp2(u[hd][lo, hi] - m_old[:, hi])
                    p_hh = jnp.exp2(jnp.where(causal, u[hd][hi, hi], NEG) - m_old[:, hi])
                    top = jnp.concatenate([p_ll, p_lh], axis=1)
                    bottom = jnp.concatenate([jnp.zeros_like(p_hh), p_hh], axis=1)
                    p.append(jnp.concatenate([top, bottom], axis=0).astype(bf16))
                else:
                    p.append(jnp.exp2(u[hd] - m_old).astype(bf16))
            return p, alpha

        def accumulate(qt, alpha, pv_vals):
            for hd in heads:
                prev = alpha[hd] * acc_ref[qt, hd] if online else acc_ref[qt, hd]
                acc_ref[qt, hd] = prev + pv_vals[hd]

        acc_ref[...] = jnp.zeros(acc_ref.shape, f32)
        p_ref[...] = jnp.zeros(p_ref.shape, bf16)
        if online:
            m_ref[...] = jnp.full(m_ref.shape, NEG, f32)
            a_ref[...] = jnp.ones(a_ref.shape, f32)

        n_items = nt * (nt - 1) // 2
        ni = FOX_ITEMS_PER_ITER_ONLINE if online else FOX_ITEMS_PER_ITER
        assert n_items % ni == 0
        store_list(u_ref, scores(1, 0))

        def pass1(_, carry):
            q0, k0, q_prev, k_prev = carry
            items = [(q0, k0)]
            for _k in range(ni):
                items.append(next_item(*items[-1]))
            u_cur = load_list(u_ref)
            pv_vals = pv(k_prev, load_list(p_ref))
            u_next = scores(*items[1])
            p_cur, al_cur = numerators(q0, u_cur, False)
            accumulate(q_prev, load_list(a_ref) if online else None, pv_vals)
            for k in range(1, ni):
                u_cur, u_next, pv_vals = u_next, [], []
                for hd in heads:
                    s = score(items[k + 1][0], items[k + 1][1], hd)
                    if k == ni - 1:
                        u_ref[hd] = s
                    else:
                        u_next.append(s)
                    pv_vals.append(pv1(items[k - 1][1], p_cur[hd], hd))
                p_new, al_new = numerators(items[k][0], u_cur, False)
                accumulate(items[k - 1][0], al_cur, pv_vals)
                p_cur, al_cur = p_new, al_new
            store_list(p_ref, p_cur)
            if online:
                store_list(a_ref, al_cur)
            return items[ni] + items[ni - 1]

        one, zero = jnp.int32(1), jnp.int32(0)
        _, _, q_prev, k_prev = lax.fori_loop(0, n_items // ni, pass1, (one, zero, one, zero))
        accumulate(q_prev, load_list(a_ref) if online else None, pv(k_prev, load_list(p_ref)))

        nd = FOX_ITEMS_PER_ITER_ONLINE if online else FOX_DIAG_TILES_PER_ITER
        assert nt % nd == 0
        store_list(u_ref, scores(0, 0))

        def pass2(i, carry):
            u_cur = load_list(u_ref)
            for k in range(nd):
                tile = nd * i + k
                nxt = jnp.minimum(tile + 1, nt - 1)
                u_next = scores(nxt, nxt)
                p_cur, al_cur = numerators(tile, u_cur, True)
                accumulate(tile, al_cur, pv(tile, p_cur))
                finalize(tile)
                u_cur = u_next
            store_list(u_ref, u_cur)
            return carry

        lax.fori_loop(0, nt // nd, pass2, 0)

    norms = jnp.max(norm_ref[...], axis=0)
    hlane = lax.broadcasted_iota(jnp.int32, (1, LANES), 1)
    for hd in heads:
        pick = hlane == head0 + hd
        q2 = jnp.max(jnp.where(pick, norms[0:1, :], 0.0), axis=1, keepdims=True)
        k2 = jnp.max(jnp.where(pick, norms[1:2, :], 0.0), axis=1, keepdims=True)
        qk_bound = jnp.sqrt(q2 * k2) * FOX_BOUND_MARGIN + 1.0
        for qt in range(nt):
            m_ref[qt, hd] = nbq_ref[hd, qt:qt + 1, :] + qk_bound

    def run_fixed_shift():
        half = t // 2
        ksub = lax.broadcasted_iota(jnp.int32, (half, half), 0)
        qlane = lax.broadcasted_iota(jnp.int32, (half, half), 1)
        causal = ksub <= qlane
        lo, hi = slice(0, half), slice(half, t)
        tile_order = list(range(1, nt, 2)) + list(range(nt - 2 + nt % 2, -1, -2))
        units = [(qt, hd) for qt in tile_order for hd in heads]

        def unit_scores(qt, hd):
            return [score(qt, kt, hd) for kt in range(qt + 1)]

        def unit_numerators(qt, hd, u):
            m = m_ref[qt, hd]
            p = [jnp.exp2(u[kt] - m).astype(bf16) for kt in range(qt)]
            ud = u[qt]
            p_ll = jnp.exp2(jnp.where(causal, ud[lo, lo], NEG) - m[:, lo])
            p_lh = jnp.exp2(ud[lo, hi] - m[:, hi])
            p_hh = jnp.exp2(jnp.where(causal, ud[hi, hi], NEG) - m[:, hi])
            top = jnp.concatenate([p_ll, p_lh], axis=1)
            bottom = jnp.concatenate([jnp.zeros_like(p_hh), p_hh], axis=1)
            p.append(jnp.concatenate([top, bottom], axis=0).astype(bf16))
            return p

        def unit_output(qt, hd, p):
            vt_all = jnp.concatenate([vt_ref[kt, col[hd], :] for kt in range(qt + 1)], axis=1)
            vt_all = jnp.concatenate([vt_all, jnp.ones((FOX_SUM_ROWS, vt_all.shape[1]), bf16)], axis=0)
            acc = jnp.dot(vt_all, jnp.concatenate(p, axis=0), preferred_element_type=f32)
            denom = acc[HEAD_DIM:HEAD_DIM + 1, :]
            out_t = acc[:HEAD_DIM, :] * (1.0 / denom)
            gate = _silu_of_half(z_ref[qt * t:(qt + 1) * t, col[hd]].astype(f32))
            o_ref[qt * t:(qt + 1) * t, col[hd]] = (out_t.T * gate).astype(o_ref.dtype)
            return denom

        denom_min = None
        u_next = unit_scores(*units[0])
        for w, (qt, hd) in enumerate(units):
            u_cur = u_next
            if w + 1 < len(units):
                u_next = unit_scores(*units[w + 1])
            denom = unit_output(qt, hd, unit_numerators(qt, hd, u_cur))
            denom_min = denom if denom_min is None else jnp.minimum(denom_min, denom)
        return jnp.min(denom_min)

    denom_min = run_fixed_shift()

    @pl.when(jnp.logical_not(denom_min >= FOX_MIN_DENOM))
    def _():
        run(online=True)


def _fox_attention(q, k, vt, nb, z, nbq, norms):
    hw = FOX_HEADS_PER_STEP * HEAD_DIM
    return pl.pallas_call(
        _fox_attn_kernel,
        grid=(BATCH, N_MAIN_HEADS // FOX_HEADS_PER_STEP),
        in_specs=[
            pl.BlockSpec((None, SEQ // FOX_T, hw, FOX_T), lambda b, p: (b, 0, p, 0)),
            pl.BlockSpec((SEQ, hw), lambda b, p: (b, p)),
            pl.BlockSpec((None, SEQ // FOX_T, hw, FOX_T), lambda b, p: (b, 0, p, 0)),
            pl.BlockSpec((SEQ, LANES), lambda b, p: (b, 0)),
            pl.BlockSpec((SEQ, hw), lambda b, p: (b, p)),
            pl.BlockSpec((None, FOX_HEADS_PER_STEP, SEQ // FOX_T, FOX_T), lambda b, p: (b, p, 0, 0)),
            pl.BlockSpec((SEQ // PROJ_TM, 8, LANES), lambda b, p: (b, 0, 0)),
        ],
        out_specs=pl.BlockSpec((SEQ, hw), lambda b, p: (b, p)),
        out_shape=jax.ShapeDtypeStruct((BATCH * SEQ, MAIN_W), bf16),
        scratch_shapes=[
            pltpu.VMEM((SEQ // FOX_T, FOX_HEADS_PER_STEP, 1, FOX_T), f32),
            pltpu.VMEM((SEQ // FOX_T, FOX_HEADS_PER_STEP, HEAD_DIM + FOX_SUM_ROWS, FOX_T), f32),
            pltpu.VMEM((FOX_HEADS_PER_STEP, FOX_T, FOX_T), f32),
            pltpu.VMEM((FOX_HEADS_PER_STEP, FOX_T, FOX_T), bf16),
            pltpu.VMEM((FOX_HEADS_PER_STEP, 1, FOX_T), f32),
        ],
        compiler_params=pltpu.CompilerParams(dimension_semantics=("arbitrary", "arbitrary"),
                                             vmem_limit_bytes=VMEM_LIMIT),
        name="fox_attention",
    )(q, k, vt, nb, z, nbq, norms)


def _retention_kernel(q_ref, kt_ref, v_ref, z_ref, d_ref, eps_ref, zeta_ref, g_ref, o_ref, r_ref):
    c_len = RET_CHUNK
    r_ref[...] = jnp.zeros(r_ref.shape, f32)
    lane = lax.broadcasted_iota(jnp.int32, (c_len, LANES), 1)
    sub = lax.broadcasted_iota(jnp.int32, (LANES, c_len), 0)
    q_mask = (lane < RET_KEY_DIM, lane >= RET_KEY_DIM)
    k_mask = (sub < RET_KEY_DIM, sub >= RET_KEY_DIM)
    pairs = range(RET_PAIRS_PER_STEP)
    chunks = range(RET_CHUNKS_PER_ITER)

    def body(it, carry):
        chunk = [it * RET_CHUNKS_PER_ITER + ci for ci in chunks]
        rows = [pl.ds(pl.multiple_of(chunk[ci] * c_len, c_len), c_len) for ci in chunks]
        q2, s, upd = {}, {}, {}
        for ci in chunks:
            for pr in pairs:
                q2[ci, pr] = q_ref[rows[ci], pr * LANES:(pr + 1) * LANES]
                kt2 = kt_ref[chunk[ci], pr * LANES:(pr + 1) * LANES, :]
                v2 = v_ref[rows[ci], pr * 2 * HEAD_DIM:(pr + 1) * 2 * HEAD_DIM]
                k_heads = jnp.concatenate([jnp.where(k_mask[hd], kt2, jnp.zeros_like(kt2))
                                           for hd in range(2)], axis=1)
                s[ci, pr] = jnp.dot(q2[ci, pr], k_heads, preferred_element_type=f32)
                kz = (kt2.astype(f32) * zeta_ref[pr]).astype(bf16)
                upd[ci, pr] = jnp.dot(kz, v2, preferred_element_type=f32)
        r_b = {}
        for pr in pairs:
            r_state = r_ref[pr]
            for ci in chunks:
                r_b[ci, pr] = r_state.astype(bf16)
                r_state = r_state * g_ref[pr] + upd[ci, pr]
            r_ref[pr] = r_state
        o = {}
        for ci in chunks:
            for pr in pairs:
                inner = (s[ci, pr] * d_ref[pr]).astype(bf16)
                for hd in range(2):
                    cols = slice((2 * pr + hd) * HEAD_DIM, (2 * pr + hd + 1) * HEAD_DIM)
                    q_hd = jnp.where(q_mask[hd], q2[ci, pr], jnp.zeros_like(q2[ci, pr]))
                    lhs = jnp.concatenate([inner[:, hd * c_len:(hd + 1) * c_len], q_hd], axis=1)
                    rhs = jnp.concatenate([v_ref[rows[ci], cols],
                                           r_b[ci, pr][:, hd * HEAD_DIM:(hd + 1) * HEAD_DIM]], axis=0)
                    o[ci, pr, hd] = jnp.dot(lhs, rhs, preferred_element_type=f32)
        for ci in chunks:
            for pr in pairs:
                for hd in range(2):
                    cols = slice((2 * pr + hd) * HEAD_DIM, (2 * pr + hd + 1) * HEAD_DIM)
                    oh = o[ci, pr, hd]
                    ms = jnp.mean(oh * oh, axis=-1, keepdims=True)
                    on = oh * lax.rsqrt(ms + eps_ref[2 * pr + hd])
                    zg = _silu_of_half(z_ref[rows[ci], cols].astype(f32))
                    o_ref[rows[ci], cols] = (on * zg).astype(o_ref.dtype)
        return carry

    lax.fori_loop(0, SEQ // (c_len * RET_CHUNKS_PER_ITER), body, 0)


def _retention_tables():
    h, c_len = N_MAIN_HEADS, RET_CHUNK
    lg = np.log1p(-np.exp2(-5.0 - np.arange(h, dtype=np.float64)))
    n = np.arange(c_len, dtype=np.float64)
    causal = n[:, None] >= n[None, :]
    d_col = np.where(causal[None], np.exp(-lg[:, None, None] * (n[None, None, :] + 1.0)), 0.0)
    d_pair = d_col.reshape(h // 2, 2, c_len, c_len).transpose(0, 2, 1, 3).reshape(h // 2, c_len, 2 * c_len)
    xi = np.exp(lg[:, None] * (n[None, :] + 1.0))
    zeta = np.exp(lg[:, None] * (c_len - 1.0 - n[None, :]))
    g_chunk = np.exp(lg * c_len)
    eps_rows = np.broadcast_to((EPS / (xi * xi))[:, :, None], (h, c_len, HEAD_DIM))
    zeta_t = np.repeat(zeta.reshape(h // 2, 2, c_len), RET_KEY_DIM, axis=1)
    g2 = np.broadcast_to(np.repeat(g_chunk.reshape(h // 2, 2), RET_KEY_DIM, axis=-1)[:, :, None],
                         (h // 2, 2 * RET_KEY_DIM, 2 * HEAD_DIM))
    return tuple(jnp.asarray(np.ascontiguousarray(t), dtype=f32) for t in (d_pair, eps_rows, zeta_t, g2))


def _retention(q, kt, v, z):
    n_pairs = N_MAIN_HEADS // 2
    pp = RET_PAIRS_PER_STEP
    steps = n_pairs // pp
    d_pair, eps_rows, zeta_t, g2 = _retention_tables()
    c_len = RET_CHUNK
    return pl.pallas_call(
        _retention_kernel,
        grid=(BATCH, steps),
        in_specs=[
            pl.BlockSpec((SEQ, pp * LANES), lambda b, p: (b, p)),
            pl.BlockSpec((None, SEQ // c_len, pp * LANES, c_len), lambda b, p: (b, 0, p, 0)),
            pl.BlockSpec((SEQ, pp * 2 * HEAD_DIM), lambda b, p: (b, p)),
            pl.BlockSpec((SEQ, pp * 2 * HEAD_DIM), lambda b, p: (b, p)),
            pl.BlockSpec((pp, c_len, 2 * c_len), lambda b, p: (p, 0, 0)),
            pl.BlockSpec((2 * pp, c_len, HEAD_DIM), lambda b, p: (p, 0, 0)),
            pl.BlockSpec((pp, LANES, c_len), lambda b, p: (p, 0, 0)),
            pl.BlockSpec((pp, LANES, 2 * HEAD_DIM), lambda b, p: (p, 0, 0)),
        ],
        out_specs=pl.BlockSpec((SEQ, pp * 2 * HEAD_DIM), lambda b, p: (b, p)),
        out_shape=jax.ShapeDtypeStruct((BATCH * SEQ, MAIN_W), bf16),
        scratch_shapes=[pltpu.VMEM((pp, LANES, 2 * HEAD_DIM), f32)],
        compiler_params=pltpu.CompilerParams(dimension_semantics=("arbitrary", "arbitrary"),
                                             vmem_limit_bytes=VMEM_LIMIT),
        name="retention",
    )(q, kt, v, z, d_pair, eps_rows, zeta_t, g2)


def _out_proj_kernel(om_ref, qm_ref, zm_ref, kv_ref, w_ref, x_ref, g_ref, o_ref, *, final_norm):
    scale = 1.0 / math.sqrt(HEAD_DIM)
    heads = range(N_MEM_HEADS)
    col = [slice(hd * HEAD_DIM, (hd + 1) * HEAD_DIM) for hd in heads]
    half = D_MODEL // 2
    s = [lax.dot_general(qm_ref[:, col[hd]], kv_ref[:, col[hd]], (((1,), (1,)), ((), ())),
                         preferred_element_type=f32) * scale for hd in heads]
    y_lo = jnp.dot(om_ref[...], w_ref[:MAIN_W, :half], preferred_element_type=f32)
    p = []
    for hd in heads:
        e = jnp.exp(s[hd] - jnp.max(s[hd], axis=-1, keepdims=True))
        p.append((e * (1.0 / jnp.sum(e, axis=-1, keepdims=True))).astype(bf16))
    memo = [jnp.dot(p[hd], kv_ref[:, MEM_W + hd * HEAD_DIM:MEM_W + (hd + 1) * HEAD_DIM],
                    preferred_element_type=f32) for hd in heads]
    y_hi = jnp.dot(om_ref[...], w_ref[:MAIN_W, half:], preferred_element_type=f32)
    og = jnp.concatenate([(memo[hd] * _silu_of_half(zm_ref[:, col[hd]].astype(f32))).astype(bf16) for hd in heads],
                         axis=1)
    y = jnp.concatenate([y_lo, y_hi], axis=1) + jnp.dot(og, w_ref[MAIN_W:, :], preferred_element_type=f32)
    xn = x_ref[...] + y
    if final_norm:
        xn = _rmsnorm_rows(xn, g_ref[...])
    o_ref[...] = xn


def _out_proj(o_main, qm, z, kv, w_out, x, g_final, *, final_norm):
    t_rows = x.shape[0]
    tiles_per_seq = SEQ // OUT_TM
    z_blk = MAIN_W // MEM_W
    return pl.pallas_call(
        functools.partial(_out_proj_kernel, final_norm=final_norm),
        grid=(t_rows // OUT_TM,),
        in_specs=[
            pl.BlockSpec((OUT_TM, MAIN_W), lambda i: (i, 0)),
            pl.BlockSpec((OUT_TM, MEM_W), lambda i: (i, 0)),
            pl.BlockSpec((OUT_TM, MEM_W), lambda i: (i, z_blk)),
            pl.BlockSpec((N_MEM, 2 * MEM_W), lambda i: (i // tiles_per_seq, 0)),
            pl.BlockSpec((INNER, D_MODEL), lambda i: (0, 0), pipeline_mode=pl.Buffered(1)),
            pl.BlockSpec((OUT_TM, D_MODEL), lambda i: (i, 0)),
            pl.BlockSpec((1, D_MODEL), lambda i: (0, 0)),
        ],
        out_specs=pl.BlockSpec((OUT_TM, D_MODEL), lambda i: (i, 0)),
        out_shape=jax.ShapeDtypeStruct((t_rows, D_MODEL), f32),
        compiler_params=pltpu.CompilerParams(dimension_semantics=("arbitrary",),
                                             vmem_limit_bytes=VMEM_LIMIT),
        name="out_proj_final" if final_norm else "out_proj",
    )(o_main, qm, z, kv, w_out, x, g_final.reshape(1, D_MODEL))


def _rotary_tables():
    half = RET_KEY_DIM // 2
    pos = np.arange(SEQ, dtype=np.float64)
    inv = 1.0 / (ROPE_BASE ** (np.arange(half, dtype=np.float64) / half))
    ang = pos[:, None] * inv[None, :]
    cos, sin = np.cos(ang), np.sin(ang)
    reps = LANES // RET_KEY_DIM
    cos_t = np.tile(np.concatenate([cos, cos], axis=-1), (1, reps))
    sin_t = np.tile(np.concatenate([-sin, sin], axis=-1), (1, reps))
    k_scale = RET_KEY_DIM ** -0.5
    return tuple(jnp.asarray(t, dtype=f32) for t in (cos_t, sin_t, cos_t * k_scale, sin_t * k_scale))


def kernel(x, mem, norm_g, fox_w_in, fox_b_f, ret_w_in, mem_norm_g, w_mem_kv, w_out, final_norm_g):
    t_rows = BATCH * SEQ
    x2 = x.reshape(t_rows, D_MODEL)

    kv0, kv1 = _norm_proj(mem.reshape(BATCH * N_MEM, D_MODEL), mem_norm_g,
                          [w_mem_kv[0].astype(bf16), w_mem_kv[1].astype(bf16)],
                          [2 * MEM_W, 2 * MEM_W], [bf16, bf16], name="mem_kv_proj")

    o_f = 3 * MAIN_W
    o_qm = o_f + N_MAIN_HEADS
    w0t = jnp.swapaxes(fox_w_in[0], 0, 1).astype(bf16)
    w_qmz = w0t[o_qm:]
    w_f = jnp.pad(w0t[o_f:o_qm], ((0, F_PAD - N_MAIN_HEADS), (0, 0)))
    q, k, vt, qm, z, f, norms = _norm_proj(
        x2, norm_g[0], [(w0t, o_f), w_qmz, w_f], [MAIN_W, MAIN_W, MAIN_W, MEM_W, INNER, F_PAD],
        [bf16, bf16, bf16, bf16, bf16, f32], ["fox_q", "fox_k", "kt", "plain", "half", "plain"],
        w_transposed=True, name="fox_in_proj")
    nb, nbq = _fox_gate(f, fox_b_f[0])
    nbq = nbq.reshape(BATCH, GATE_HEAD_ROWS, SEQ // FOX_T, FOX_T)
    o_main = _fox_attention(q, k, vt, nb, z, nbq, norms)
    x2 = _out_proj(o_main, qm, z, kv0, w_out[0].astype(bf16), x2, final_norm_g, final_norm=False)

    w1 = ret_w_in[0].astype(bf16)
    q, kt, v, qm, z = _norm_proj(
        x2, norm_g[1], [w1], [RET_QK_W, RET_QK_W, MAIN_W, MEM_W, INNER], [bf16] * 5,
        ["rot_q", "rot_kt", "plain", "plain", "half"], tables=_rotary_tables(), name="ret_in_proj")
    o_main = _retention(q, kt, v, z)
    out = _out_proj(o_main, qm, z, kv1, w_out[1].astype(bf16), x2, final_norm_g, final_norm=True)
    return out.reshape(BATCH, SEQ, D_MODEL)
```

```python
import functools
import math

import jax
import jax.numpy as jnp
import numpy as np
from jax import lax
from jax.experimental import pallas as pl
from jax.experimental.pallas import tpu as pltpu

D_MODEL = 1024
BATCH = 8
SEQ = 2048
HEAD_DIM = 128
N_MAIN_HEADS = 12
N_MEM_HEADS = 4
N_MEM = 256
MAIN_W = N_MAIN_HEADS * HEAD_DIM
MEM_W = N_MEM_HEADS * HEAD_DIM
INNER = MAIN_W + MEM_W
RET_KEY_DIM = HEAD_DIM // 2
RET_QK_W = N_MAIN_HEADS * RET_KEY_DIM
RET_CHUNK = 128
ROPE_BASE = 10000.0
EPS = 1e-6
NEG = -1e30
LOG2E = 1.4426950408889634

LANES = 128
F_PAD = LANES
VMEM_LIMIT = 60 * 1024 * 1024

PROJ_TM = 512
PROJ_TN = 512
RET_PROJ_TM = 512
FOX_PROJ_TM = 1024
WEIGHT_CAST_ROWS = 128
FOX_T = 256
FOX_HEADS_PER_STEP = 6
FOX_ITEMS_PER_ITER = 7
FOX_ITEMS_PER_ITER_ONLINE = 2
FOX_DIAG_TILES_PER_ITER = 4
BIAS_TERMS = 3
FOX_SUM_ROWS = 16
FOX_Q_SCALE = LOG2E / math.sqrt(HEAD_DIM)
FOX_BOUND_MARGIN = 1.02
FOX_MIN_DENOM = 2.0 ** -90
GATE_HEAD_ROWS = 16
RET_PAIRS_PER_STEP = 3
RET_CHUNKS_PER_ITER = 2
OUT_TM = 1024

f32 = jnp.float32
bf16 = jnp.bfloat16


def _silu_of_half(h):
    return h + h * jnp.tanh(h)


def _rmsnorm_rows(x, g):
    ms = jnp.mean(x * x, axis=-1, keepdims=True)
    return (x * lax.rsqrt(ms + EPS)) * g


def _split_bf16(v):
    hi = v.astype(bf16)
    r1 = v - hi.astype(f32)
    mid = r1.astype(bf16)
    lo = (r1 - mid.astype(f32)).astype(bf16)
    return hi, mid, lo


def _norm_proj_kernel(*refs, n_weights, w_transposed, cast_weight, out_widths, out_kinds):
    track_norms = "fox_q" in out_kinds
    n_out = len(out_widths) + (1 if track_norms else 0)
    if cast_weight:
        w_copy_ref = refs[-1]
        refs = refs[:-1]
    x_ref, g_ref = refs[:2]
    w_refs = refs[2:2 + n_weights]
    tab_refs = refs[2 + n_weights:len(refs) - n_out]
    out_refs = refs[len(refs) - n_out:]
    if cast_weight:
        @pl.when(pl.program_id(0) == 0)
        def _():
            for r0 in range(0, w_copy_ref.shape[0], WEIGHT_CAST_ROWS):
                rows = slice(r0, r0 + WEIGHT_CAST_ROWS)
                w_copy_ref[rows, :] = w_refs[0][rows, :].astype(bf16)
        w_refs = [w_copy_ref]
    h = _rmsnorm_rows(x_ref[...], g_ref[...]).astype(bf16)
    tm = h.shape[0]
    if "rot_q" in out_kinds:
        lane = lax.broadcasted_iota(jnp.int32, (tm, LANES), 1)
        first_half = (lane % RET_KEY_DIM) < (RET_KEY_DIM // 2)
    if track_norms:
        norm_sub = lax.broadcasted_iota(jnp.int32, (8, LANES), 0)
        norm_lane = lax.broadcasted_iota(jnp.int32, (8, LANES), 1)
        norms = jnp.zeros((8, LANES), f32)

        def with_norms(norms, y, row, head0):
            for s0 in range(0, y.shape[1], HEAD_DIM):
                ys = y[:, s0:s0 + HEAD_DIM]
                worst = jnp.max(jnp.sum(ys * ys, axis=1, keepdims=True), axis=0, keepdims=True)
                norms = jnp.where((norm_sub == row) & (norm_lane == head0 + s0 // HEAD_DIM), worst, norms)
            return norms
    out_axis = 0 if w_transposed else 1
    wi, col = 0, 0
    for o_ref, width, kind in zip(out_refs, out_widths, out_kinds):
        if col == w_refs[wi].shape[out_axis]:
            wi, col = wi + 1, 0
        w_ref = w_refs[wi]
        for c0 in range(0, width, PROJ_TN):
            cw = min(PROJ_TN, width - c0)
            if w_transposed:
                y = lax.dot_general(h, w_ref[col + c0:col + c0 + cw, :], (((1,), (1,)), ((), ())),
                                    preferred_element_type=f32)
            else:
                y = jnp.dot(h, w_ref[:, col + c0:col + c0 + cw], preferred_element_type=f32)
            if kind in ("rot_q", "rot_kt"):
                cos_ref, sin_ref = tab_refs[:2] if kind == "rot_q" else tab_refs[2:]
                for s0 in range(0, cw, LANES):
                    ys = y[:, s0:s0 + LANES]
                    sw = jnp.where(first_half, pltpu.roll(ys, LANES - RET_KEY_DIM // 2, 1),
                                   pltpu.roll(ys, RET_KEY_DIM // 2, 1))
                    r = ys * cos_ref[...] + sw * sin_ref[...]
                    if kind == "rot_q":
                        o_ref[:, c0 + s0:c0 + s0 + LANES] = r.astype(o_ref.dtype)
                    else:
                        for r0 in range(tm // RET_CHUNK):
                            piece = r[r0 * RET_CHUNK:(r0 + 1) * RET_CHUNK, :]
                            o_ref[r0, c0 + s0:c0 + s0 + LANES, :] = piece.T.astype(o_ref.dtype)
            elif kind in ("kt", "fox_q"):
                if kind == "fox_q":
                    y = y * FOX_Q_SCALE
                    norms = with_norms(norms, y, 0, c0 // HEAD_DIM)
                for r0 in range(tm // FOX_T):
                    for s0 in range(0, cw, LANES):
                        piece = y[r0 * FOX_T:(r0 + 1) * FOX_T, s0:s0 + LANES]
                        o_ref[r0, c0 + s0:c0 + s0 + LANES, :] = piece.T.astype(o_ref.dtype)
            elif kind == "fox_k":
                norms = with_norms(norms, y, 1, c0 // HEAD_DIM)
                o_ref[:, c0:c0 + cw] = y.astype(o_ref.dtype)
            elif kind == "half":
                o_ref[:, c0:c0 + cw] = (y * 0.5).astype(o_ref.dtype)
            else:
                o_ref[:, c0:c0 + cw] = y.astype(o_ref.dtype)
        col += width
    if track_norms:
        out_refs[-1][...] = norms


def _norm_proj(x, g, weights, out_widths, out_dtypes, out_kinds=None, *, w_transposed=False, tables=None,
               tm=PROJ_TM, name):
    t_rows, d = x.shape
    out_kinds = tuple(out_kinds or ["plain"] * len(out_widths))
    cast_weight = not isinstance(weights[0], tuple) and weights[0].ndim == 3
    w_arrays = [w[0] if isinstance(w, tuple) else w for w in weights]
    w_blocks = [(w[1], d) if isinstance(w, tuple) else w.shape[-2:] for w in weights]
    out_axis = 0 if w_transposed else 1
    assert sum(out_widths) == sum(blk[out_axis] for blk in w_blocks) and t_rows % tm == 0
    tiles_per_seq = SEQ // tm
    in_specs = [
        pl.BlockSpec((tm, d), lambda i: (i, 0)),
        pl.BlockSpec((1, d), lambda i: (0, 0)),
    ]
    scratch = []
    if cast_weight:
        assert len(weights) == 1 and not w_transposed
        in_specs.append(pl.BlockSpec((None,) + w_blocks[0], lambda i: (0, 0, 0), pipeline_mode=pl.Buffered(1)))
        scratch.append(pltpu.VMEM(w_blocks[0], bf16))
    else:
        in_specs += [pl.BlockSpec(blk, lambda i: (0, 0), pipeline_mode=pl.Buffered(1)) for blk in w_blocks]
    args = [x, g.reshape(1, d), *w_arrays]
    if "rot_q" in out_kinds:
        for tab in tables:
            in_specs.append(pl.BlockSpec((tm, LANES), lambda i: (i % tiles_per_seq, 0)))
            args.append(tab)
    out_specs, out_shape = [], []
    for wd, dt, kind in zip(out_widths, out_dtypes, out_kinds):
        if kind in ("kt", "fox_q", "rot_kt"):
            tile = RET_CHUNK if kind == "rot_kt" else FOX_T
            out_specs.append(pl.BlockSpec((None, tm // tile, wd, tile),
                                          lambda i: (i // tiles_per_seq, i % tiles_per_seq, 0, 0)))
            out_shape.append(jax.ShapeDtypeStruct((t_rows // SEQ, SEQ // tile, wd, tile), dt))
        else:
            out_specs.append(pl.BlockSpec((tm, wd), lambda i: (i, 0)))
            out_shape.append(jax.ShapeDtypeStruct((t_rows, wd), dt))
    if "fox_q" in out_kinds:
        out_specs.append(pl.BlockSpec((None, 8, LANES), lambda i: (i, 0, 0)))
        out_shape.append(jax.ShapeDtypeStruct((t_rows // tm, 8, LANES), f32))
    return pl.pallas_call(
        functools.partial(_norm_proj_kernel, n_weights=len(weights), w_transposed=w_transposed,
                          cast_weight=cast_weight, out_widths=tuple(out_widths), out_kinds=out_kinds),
        grid=(t_rows // tm,),
        in_specs=in_specs,
        out_specs=out_specs,
        out_shape=out_shape,
        scratch_shapes=scratch,
        compiler_params=pltpu.CompilerParams(dimension_semantics=("arbitrary",),
                                             vmem_limit_bytes=VMEM_LIMIT),
        name=name,
    )(*args)


def _fox_gate_kernel(f_ref, bf_ref, o_ref, ot_ref):
    blk = LANES
    row = lax.broadcasted_iota(jnp.int32, (blk, blk), 0)
    coli = lax.broadcasted_iota(jnp.int32, (blk, blk), 1)
    tri = jnp.where(row >= coli, 1.0, 0.0).astype(bf16)
    spread = [jnp.where((coli == BIAS_TERMS * row + t) & (row < N_MAIN_HEADS), 1.0, 0.0).astype(bf16)
              for t in range(BIAS_TERMS)]
    carry = jnp.zeros((1, F_PAD), f32)
    for b0 in range(0, SEQ, blk):
        xg = f_ref[b0:b0 + blk, :] + bf_ref[...]
        ls = jnp.minimum(xg, 0.0) - jnp.log1p(jnp.exp(-jnp.abs(xg)))
        cs = sum(jnp.dot(tri, term, preferred_element_type=f32) for term in _split_bf16(ls)) + carry
        carry = cs[blk - 1:blk, :]
        bias = cs * (-LOG2E)
        nb = sum(jnp.dot(term, sp, preferred_element_type=f32)
                 for term, sp in zip(_split_bf16(bias), spread))
        o_ref[b0:b0 + blk, :] = nb.astype(o_ref.dtype)
        ot_ref[:, b0 // FOX_T, b0 % FOX_T:b0 % FOX_T + blk] = bias.T[:ot_ref.shape[0], :]


def _fox_gate(f, b_f):
    bf_pad = jnp.zeros((1, F_PAD), f32).at[0, :N_MAIN_HEADS].set(b_f.astype(f32))
    return pl.pallas_call(
        _fox_gate_kernel,
        grid=(BATCH,),
        in_specs=[pl.BlockSpec((SEQ, F_PAD), lambda b: (b, 0)),
                  pl.BlockSpec((1, F_PAD), lambda b: (0, 0))],
        out_specs=[pl.BlockSpec((SEQ, LANES), lambda b: (b, 0)),
                   pl.BlockSpec((None, GATE_HEAD_ROWS, SEQ // FOX_T, FOX_T), lambda b: (b, 0, 0, 0))],
        out_shape=[jax.ShapeDtypeStruct((BATCH * SEQ, LANES), bf16),
                   jax.ShapeDtypeStruct((BATCH, GATE_HEAD_ROWS, SEQ // FOX_T, FOX_T), f32)],
        compiler_params=pltpu.CompilerParams(dimension_semantics=("arbitrary",)),
        name="fox_gate",
    )(f, bf_pad)


def _fox_attn_kernel(qt_ref, k_ref, vt_ref, nb_ref, z_ref, nbq_ref, norm_ref, o_ref,
                     m_ref, acc_ref, u_ref, p_ref, a_ref):
    t = FOX_T
    nt = SEQ // t
    head0 = pl.program_id(1) * FOX_HEADS_PER_STEP
    heads = range(FOX_HEADS_PER_STEP)
    col = [slice(hd * HEAD_DIM, (hd + 1) * HEAD_DIM) for hd in heads]
    sub = lax.broadcasted_iota(jnp.int32, (LANES, t), 0)
    sel = []
    for hd in heads:
        lo = BIAS_TERMS * (head0 + hd)
        sel.append(jnp.where((sub >= lo) & (sub < lo + BIAS_TERMS), 1.0, 0.0).astype(bf16))
    ones_rows = jnp.ones((FOX_SUM_ROWS, t), bf16)

    def tile_rows(i):
        if isinstance(i, int):
            return slice(i * t, (i + 1) * t)
        return pl.ds(pl.multiple_of(i * t, t), t)

    def score(qt, kt, hd):
        k_aug = jnp.concatenate([k_ref[tile_rows(kt), col[hd]], nb_ref[tile_rows(kt), :]], axis=1)
        q_aug_t = jnp.concatenate([qt_ref[qt, col[hd], :], sel[hd]], axis=0)
        return jnp.dot(k_aug, q_aug_t, preferred_element_type=f32)

    def scores(qt, kt):
        return [score(qt, kt, hd) for hd in heads]

    def pv1(kt, p_hd, hd):
        return jnp.dot(jnp.concatenate([vt_ref[kt, col[hd], :], ones_rows], axis=0), p_hd,
                       preferred_element_type=f32)

    def pv(kt, p):
        return [pv1(kt, p[hd], hd) for hd in heads]

    def finalize(qt):
        for hd in heads:
            acc = acc_ref[qt, hd]
            out_t = acc[:HEAD_DIM, :] * (1.0 / acc[HEAD_DIM:HEAD_DIM + 1, :])
            gate = _silu_of_half(z_ref[tile_rows(qt), col[hd]].astype(f32))
            o_ref[tile_rows(qt), col[hd]] = (out_t.T * gate).astype(o_ref.dtype)

    def next_item(qt, kt):
        wrap = kt + 1 >= qt
        return jnp.minimum(jnp.where(wrap, qt + 1, qt), nt - 1), jnp.where(wrap, 0, kt + 1)

    def load_list(ref):
        return [ref[hd] for hd in heads]

    def store_list(ref, vals):
        for hd in heads:
            ref[hd] = vals[hd]

    def run(online):
        def numerators(qt, u, masked):
            half = t // 2
            if masked:
                ksub = lax.broadcasted_iota(jnp.int32, (t if online else half,) * 2, 0)
                qlane = lax.broadcasted_iota(jnp.int32, (t if online else half,) * 2, 1)
                causal = ksub <= qlane
            p, alpha = [], []
            for hd in heads:
                m_old = m_ref[qt, hd]
                if online:
                    uh = jnp.where(causal, u[hd], NEG) if masked else u[hd]
                    m_new = jnp.maximum(m_old, jnp.max(uh, axis=0, keepdims=True))
                    alpha.append(jnp.exp2(m_old - m_new))
                    m_ref[qt, hd] = m_new
                    p.append(jnp.exp2(uh - m_new).astype(bf16))
                elif masked:
                    lo, hi = slice(0, half), slice(half, t)
                    p_ll = jnp.exp2(jnp.where(causal, u[hd][lo, lo], NEG) - m_old[:, lo])
                    p_lh = jnp.exp2(u[hd][lo, hi] - m_old[:, hi])
                    p_hh = jnp.exp2(jnp.where(causal, u[hd][hi, hi], NEG) - m_old[:, hi])
                    top = jnp.concatenate([p_ll, p_lh], axis=1)
                    bottom = jnp.concatenate([jnp.zeros_like(p_hh), p_hh], axis=1)
                    p.append(jnp.concatenate([top, bottom], axis=0).astype(bf16))
                else:
                    p.append(jnp.exp2(u[hd] - m_old).astype(bf16))
            return p, alpha

        def accumulate(qt, alpha, pv_vals):
            for hd in heads:
                prev = alpha[hd] * acc_ref[qt, hd] if online else acc_ref[qt, hd]
                acc_ref[qt, hd] = prev + pv_vals[hd]

        acc_ref[...] = jnp.zeros(acc_ref.shape, f32)
        p_ref[...] = jnp.zeros(p_ref.shape, bf16)
        if online:
            m_ref[...] = jnp.full(m_ref.shape, NEG, f32)
            a_ref[...] = jnp.ones(a_ref.shape, f32)

        n_items = nt * (nt - 1) // 2
        ni = FOX_ITEMS_PER_ITER_ONLINE if online else FOX_ITEMS_PER_ITER
        assert n_items % ni == 0
        store_list(u_ref, scores(1, 0))

        def pass1(_, carry):
            q0, k0, q_prev, k_prev = carry
            items = [(q0, k0)]
            for _k in range(ni):
                items.append(next_item(*items[-1]))
            u_cur = load_list(u_ref)
            pv_vals = pv(k_prev, load_list(p_ref))
            u_next = scores(*items[1])
            p_cur, al_cur = numerators(q0, u_cur, False)
            accumulate(q_prev, load_list(a_ref) if online else None, pv_vals)
            for k in range(1, ni):
                u_cur, u_next, pv_vals = u_next, [], []
                for hd in heads:
                    s = score(items[k + 1][0], items[k + 1][1], hd)
                    if k == ni - 1:
                        u_ref[hd] = s
                    else:
                        u_next.append(s)
                    pv_vals.append(pv1(items[k - 1][1], p_cur[hd], hd))
                p_new, al_new = numerators(items[k][0], u_cur, False)
                accumulate(items[k - 1][0], al_cur, pv_vals)
                p_cur, al_cur = p_new, al_new
            store_list(p_ref, p_cur)
            if online:
                store_list(a_ref, al_cur)
            return items[ni] + items[ni - 1]

        one, zero = jnp.int32(1), jnp.int32(0)
        _, _, q_prev, k_prev = lax.fori_loop(0, n_items // ni, pass1, (one, zero, one, zero))
        accumulate(q_prev, load_list(a_ref) if online else None, pv(k_prev, load_list(p_ref)))

        nd = FOX_ITEMS_PER_ITER_ONLINE if online else FOX_DIAG_TILES_PER_ITER
        assert nt % nd == 0
        store_list(u_ref, scores(0, 0))

        def pass2(i, carry):
            u_cur = load_list(u_ref)
            for k in range(nd):
                tile = nd * i + k
                nxt = jnp.minimum(tile + 1, nt - 1)
                u_next = scores(nxt, nxt)
                p_cur, al_cur = numerators(tile, u_cur, True)
                accumulate(tile, al_cur, pv(tile, p_cur))
                finalize(tile)
                u_cur = u_next
            store_list(u_ref, u_cur)
            return carry

        lax.fori_loop(0, nt // nd, pass2, 0)

    norms = jnp.max(norm_ref[...], axis=0)
    hlane = lax.broadcasted_iota(jnp.int32, (1, LANES), 1)
    for hd in heads:
        pick = hlane == head0 + hd
        q2 = jnp.max(jnp.where(pick, norms[0:1, :], 0.0), axis=1, keepdims=True)
        k2 = jnp.max(jnp.where(pick, norms[1:2, :], 0.0), axis=1, keepdims=True)
        qk_bound = jnp.sqrt(q2 * k2) * FOX_BOUND_MARGIN + 1.0
        for qt in range(nt):
            m_ref[qt, hd] = nbq_ref[hd, qt:qt + 1, :] + qk_bound

    def run_fixed_shift():
        half = t // 2
        ksub = lax.broadcasted_iota(jnp.int32, (half, half), 0)
        qlane = lax.broadcasted_iota(jnp.int32, (half, half), 1)
        causal = ksub <= qlane
        lo, hi = slice(0, half), slice(half, t)
        units = [(qt, hd) for qt in range(nt) for hd in heads]

        def unit_scores(qt, hd):
            return [score(qt, kt, hd) for kt in range(qt + 1)]

        def unit_numerators(qt, hd, u):
            m = m_ref[qt, hd]
            p = [jnp.exp2(u[kt] - m).astype(bf16) for kt in range(qt)]
            ud = u[qt]
            p_ll = jnp.exp2(jnp.where(causal, ud[lo, lo], NEG) - m[:, lo])
            p_lh = jnp.exp2(ud[lo, hi] - m[:, hi])
            p_hh = jnp.exp2(jnp.where(causal, ud[hi, hi], NEG) - m[:, hi])
            top = jnp.concatenate([p_ll, p_lh], axis=1)
            bottom = jnp.concatenate([jnp.zeros_like(p_hh), p_hh], axis=1)
            p.append(jnp.concatenate([top, bottom], axis=0).astype(bf16))
            return p

        def unit_output(qt, hd, p):
            vt_all = jnp.concatenate([vt_ref[kt, col[hd], :] for kt in range(qt + 1)], axis=1)
            vt_all = jnp.concatenate([vt_all, jnp.ones((FOX_SUM_ROWS, vt_all.shape[1]), bf16)], axis=0)
            acc = jnp.dot(vt_all, jnp.concatenate(p, axis=0), preferred_element_type=f32)
            denom = acc[HEAD_DIM:HEAD_DIM + 1, :]
            out_t = acc[:HEAD_DIM, :] * (1.0 / denom)
            gate = _silu_of_half(z_ref[qt * t:(qt + 1) * t, col[hd]].astype(f32))
            o_ref[qt * t:(qt + 1) * t, col[hd]] = (out_t.T * gate).astype(o_ref.dtype)
            return denom

        denom_min = None
        u_next = unit_scores(*units[0])
        for w, (qt, hd) in enumerate(units):
            u_cur = u_next
            if w + 1 < len(units):
                u_next = unit_scores(*units[w + 1])
            denom = unit_output(qt, hd, unit_numerators(qt, hd, u_cur))
            denom_min = denom if denom_min is None else jnp.minimum(denom_min, denom)
        return jnp.min(denom_min)

    denom_min = run_fixed_shift()

    @pl.when(jnp.logical_not(denom_min >= FOX_MIN_DENOM))
    def _():
        run(online=True)


def _fox_attention(q, k, vt, nb, z, nbq, norms):
    hw = FOX_HEADS_PER_STEP * HEAD_DIM
    return pl.pallas_call(
        _fox_attn_kernel,
        grid=(BATCH, N_MAIN_HEADS // FOX_HEADS_PER_STEP),
        in_specs=[
            pl.BlockSpec((None, SEQ // FOX_T, hw, FOX_T), lambda b, p: (b, 0, p, 0)),
            pl.BlockSpec((SEQ, hw), lambda b, p: (b, p)),
            pl.BlockSpec((None, SEQ // FOX_T, hw, FOX_T), lambda b, p: (b, 0, p, 0)),
            pl.BlockSpec((SEQ, LANES), lambda b, p: (b, 0)),
            pl.BlockSpec((SEQ, hw), lambda b, p: (b, p)),
            pl.BlockSpec((None, FOX_HEADS_PER_STEP, SEQ // FOX_T, FOX_T), lambda b, p: (b, p, 0, 0)),
            pl.BlockSpec((SEQ // FOX_PROJ_TM, 8, LANES), lambda b, p: (b, 0, 0)),
        ],
        out_specs=pl.BlockSpec((SEQ, hw), lambda b, p: (b, p)),
        out_shape=jax.ShapeDtypeStruct((BATCH * SEQ, MAIN_W), bf16),
        scratch_shapes=[
            pltpu.VMEM((SEQ // FOX_T, FOX_HEADS_PER_STEP, 1, FOX_T), f32),
            pltpu.VMEM((SEQ // FOX_T, FOX_HEADS_PER_STEP, HEAD_DIM + FOX_SUM_ROWS, FOX_T), f32),
            pltpu.VMEM((FOX_HEADS_PER_STEP, FOX_T, FOX_T), f32),
            pltpu.VMEM((FOX_HEADS_PER_STEP, FOX_T, FOX_T), bf16),
            pltpu.VMEM((FOX_HEADS_PER_STEP, 1, FOX_T), f32),
        ],
        compiler_params=pltpu.CompilerParams(dimension_semantics=("arbitrary", "arbitrary"),
                                             vmem_limit_bytes=VMEM_LIMIT),
        name="fox_attention",
    )(q, k, vt, nb, z, nbq, norms)


def _retention_kernel(q_ref, kt_ref, v_ref, z_ref, d_ref, eps_ref, zeta_ref, g_ref, o_ref, r_ref):
    c_len = RET_CHUNK
    r_ref[...] = jnp.zeros(r_ref.shape, f32)
    lane = lax.broadcasted_iota(jnp.int32, (c_len, LANES), 1)
    sub = lax.broadcasted_iota(jnp.int32, (LANES, c_len), 0)
    q_mask = (lane < RET_KEY_DIM, lane >= RET_KEY_DIM)
    k_mask = (sub < RET_KEY_DIM, sub >= RET_KEY_DIM)
    pairs = range(RET_PAIRS_PER_STEP)
    chunks = range(RET_CHUNKS_PER_ITER)

    def body(it, carry):
        chunk = [it * RET_CHUNKS_PER_ITER + ci for ci in chunks]
        rows = [pl.ds(pl.multiple_of(chunk[ci] * c_len, c_len), c_len) for ci in chunks]
        q2, s, upd = {}, {}, {}
        for ci in chunks:
            for pr in pairs:
                q2[ci, pr] = q_ref[rows[ci], pr * LANES:(pr + 1) * LANES]
                kt2 = kt_ref[chunk[ci], pr * LANES:(pr + 1) * LANES, :]
                v2 = v_ref[rows[ci], pr * 2 * HEAD_DIM:(pr + 1) * 2 * HEAD_DIM]
                k_heads = jnp.concatenate([jnp.where(k_mask[hd], kt2, jnp.zeros_like(kt2))
                                           for hd in range(2)], axis=1)
                s[ci, pr] = jnp.dot(q2[ci, pr], k_heads, preferred_element_type=f32)
                kz = (kt2.astype(f32) * zeta_ref[pr]).astype(bf16)
                upd[ci, pr] = jnp.dot(kz, v2, preferred_element_type=f32)
        r_b = {}
        for pr in pairs:
            r_state = r_ref[pr]
            for ci in chunks:
                r_b[ci, pr] = r_state.astype(bf16)
                r_state = r_state * g_ref[pr] + upd[ci, pr]
            r_ref[pr] = r_state
        o = {}
        for ci in chunks:
            for pr in pairs:
                inner = (s[ci, pr] * d_ref[pr]).astype(bf16)
                for hd in range(2):
                    cols = slice((2 * pr + hd) * HEAD_DIM, (2 * pr + hd + 1) * HEAD_DIM)
                    q_hd = jnp.where(q_mask[hd], q2[ci, pr], jnp.zeros_like(q2[ci, pr]))
                    lhs = jnp.concatenate([inner[:, hd * c_len:(hd + 1) * c_len], q_hd], axis=1)
                    rhs = jnp.concatenate([v_ref[rows[ci], cols],
                                           r_b[ci, pr][:, hd * HEAD_DIM:(hd + 1) * HEAD_DIM]], axis=0)
                    o[ci, pr, hd] = jnp.dot(lhs, rhs, preferred_element_type=f32)
        for ci in chunks:
            for pr in pairs:
                for hd in range(2):
                    cols = slice((2 * pr + hd) * HEAD_DIM, (2 * pr + hd + 1) * HEAD_DIM)
                    oh = o[ci, pr, hd]
                    ms = jnp.mean(oh * oh, axis=-1, keepdims=True)
                    on = oh * lax.rsqrt(ms + eps_ref[2 * pr + hd])
                    zg = _silu_of_half(z_ref[rows[ci], cols].astype(f32))
                    o_ref[rows[ci], cols] = (on * zg).astype(o_ref.dtype)
        return carry

    lax.fori_loop(0, SEQ // (c_len * RET_CHUNKS_PER_ITER), body, 0)


def _retention_tables():
    h, c_len = N_MAIN_HEADS, RET_CHUNK
    lg = np.log1p(-np.exp2(-5.0 - np.arange(h, dtype=np.float64)))
    n = np.arange(c_len, dtype=np.float64)
    causal = n[:, None] >= n[None, :]
    d_col = np.where(causal[None], np.exp(-lg[:, None, None] * (n[None, None, :] + 1.0)), 0.0)
    d_pair = d_col.reshape(h // 2, 2, c_len, c_len).transpose(0, 2, 1, 3).reshape(h // 2, c_len, 2 * c_len)
    xi = np.exp(lg[:, None] * (n[None, :] + 1.0))
    zeta = np.exp(lg[:, None] * (c_len - 1.0 - n[None, :]))
    g_chunk = np.exp(lg * c_len)
    eps_rows = np.broadcast_to((EPS / (xi * xi))[:, :, None], (h, c_len, HEAD_DIM))
    zeta_t = np.repeat(zeta.reshape(h // 2, 2, c_len), RET_KEY_DIM, axis=1)
    g2 = np.broadcast_to(np.repeat(g_chunk.reshape(h // 2, 2), RET_KEY_DIM, axis=-1)[:, :, None],
                         (h // 2, 2 * RET_KEY_DIM, 2 * HEAD_DIM))
    return tuple(jnp.asarray(np.ascontiguousarray(t), dtype=f32) for t in (d_pair, eps_rows, zeta_t, g2))


def _retention(q, kt, v, z):
    n_pairs = N_MAIN_HEADS // 2
    pp = RET_PAIRS_PER_STEP
    steps = n_pairs // pp
    d_pair, eps_rows, zeta_t, g2 = _retention_tables()
    c_len = RET_CHUNK
    return pl.pallas_call(
        _retention_kernel,
        grid=(BATCH, steps),
        in_specs=[
            pl.BlockSpec((SEQ, pp * LANES), lambda b, p: (b, p)),
            pl.BlockSpec((None, SEQ // c_len, pp * LANES, c_len), lambda b, p: (b, 0, p, 0)),
            pl.BlockSpec((SEQ, pp * 2 * HEAD_DIM), lambda b, p: (b, p)),
            pl.BlockSpec((SEQ, pp * 2 * HEAD_DIM), lambda b, p: (b, p)),
            pl.BlockSpec((pp, c_len, 2 * c_len), lambda b, p: (p, 0, 0)),
            pl.BlockSpec((2 * pp, c_len, HEAD_DIM), lambda b, p: (p, 0, 0)),
            pl.BlockSpec((pp, LANES, c_len), lambda b, p: (p, 0, 0)),
            pl.BlockSpec((pp, LANES, 2 * HEAD_DIM), lambda b, p: (p, 0, 0)),
        ],
        out_specs=pl.BlockSpec((SEQ, pp * 2 * HEAD_DIM), lambda b, p: (b, p)),
        out_shape=jax.ShapeDtypeStruct((BATCH * SEQ, MAIN_W), bf16),
        scratch_shapes=[pltpu.VMEM((pp, LANES, 2 * HEAD_DIM), f32)],
        compiler_params=pltpu.CompilerParams(dimension_semantics=("arbitrary", "arbitrary"),
                                             vmem_limit_bytes=VMEM_LIMIT),
        name="retention",
    )(q, kt, v, z, d_pair, eps_rows, zeta_t, g2)


def _out_proj_kernel(om_ref, qm_ref, zm_ref, kv_ref, w32_ref, x_ref, g_ref, o_ref, w_ref, *, final_norm):
    @pl.when(pl.program_id(0) == 0)
    def _():
        for r0 in range(0, INNER, WEIGHT_CAST_ROWS):
            rows = slice(r0, r0 + WEIGHT_CAST_ROWS)
            w_ref[rows, :] = w32_ref[rows, :].astype(bf16)

    scale = 1.0 / math.sqrt(HEAD_DIM)
    heads = range(N_MEM_HEADS)
    col = [slice(hd * HEAD_DIM, (hd + 1) * HEAD_DIM) for hd in heads]
    half = D_MODEL // 2
    for r0 in range(0, OUT_TM, OUT_TM // 2):
        rows = slice(r0, r0 + OUT_TM // 2)
        s = [lax.dot_general(qm_ref[rows, col[hd]], kv_ref[:, col[hd]], (((1,), (1,)), ((), ())),
                             preferred_element_type=f32) * scale for hd in heads]
        y_lo = jnp.dot(om_ref[rows, :], w_ref[:MAIN_W, :half], preferred_element_type=f32)
        p = []
        for hd in heads:
            e = jnp.exp(s[hd] - jnp.max(s[hd], axis=-1, keepdims=True))
            p.append((e * (1.0 / jnp.sum(e, axis=-1, keepdims=True))).astype(bf16))
        memo = [jnp.dot(p[hd], kv_ref[:, MEM_W + hd * HEAD_DIM:MEM_W + (hd + 1) * HEAD_DIM],
                        preferred_element_type=f32) for hd in heads]
        y_hi = jnp.dot(om_ref[rows, :], w_ref[:MAIN_W, half:], preferred_element_type=f32)
        og = jnp.concatenate([(memo[hd] * _silu_of_half(zm_ref[rows, col[hd]].astype(f32))).astype(bf16)
                              for hd in heads], axis=1)
        y = jnp.concatenate([y_lo, y_hi], axis=1) + jnp.dot(og, w_ref[MAIN_W:, :], preferred_element_type=f32)
        xn = x_ref[rows, :] + y
        if final_norm:
            xn = _rmsnorm_rows(xn, g_ref[...])
        o_ref[rows, :] = xn


def _out_proj(o_main, qm, z, kv, w_out, layer, x, g_final, *, final_norm):
    t_rows = x.shape[0]
    tiles_per_seq = SEQ // OUT_TM
    z_blk = MAIN_W // MEM_W
    return pl.pallas_call(
        functools.partial(_out_proj_kernel, final_norm=final_norm),
        grid=(t_rows // OUT_TM,),
        in_specs=[
            pl.BlockSpec((OUT_TM, MAIN_W), lambda i: (i, 0)),
            pl.BlockSpec((OUT_TM, MEM_W), lambda i: (i, 0)),
            pl.BlockSpec((OUT_TM, MEM_W), lambda i: (i, z_blk)),
            pl.BlockSpec((N_MEM, 2 * MEM_W), lambda i: (i // tiles_per_seq, 0)),
            pl.BlockSpec((None, INNER, D_MODEL), lambda i: (layer, 0, 0), pipeline_mode=pl.Buffered(1)),
            pl.BlockSpec((OUT_TM, D_MODEL), lambda i: (i, 0)),
            pl.BlockSpec((1, D_MODEL), lambda i: (0, 0)),
        ],
        out_specs=pl.BlockSpec((OUT_TM, D_MODEL), lambda i: (i, 0)),
        out_shape=jax.ShapeDtypeStruct((t_rows, D_MODEL), f32),
        scratch_shapes=[pltpu.VMEM((INNER, D_MODEL), bf16)],
        compiler_params=pltpu.CompilerParams(dimension_semantics=("arbitrary",),
                                             vmem_limit_bytes=VMEM_LIMIT),
        name="out_proj_final" if final_norm else "out_proj",
    )(o_main, qm, z, kv, w_out, x, g_final.reshape(1, D_MODEL))


def _rotary_tables():
    half = RET_KEY_DIM // 2
    pos = np.arange(SEQ, dtype=np.float64)
    inv = 1.0 / (ROPE_BASE ** (np.arange(half, dtype=np.float64) / half))
    ang = pos[:, None] * inv[None, :]
    cos, sin = np.cos(ang), np.sin(ang)
    reps = LANES // RET_KEY_DIM
    cos_t = np.tile(np.concatenate([cos, cos], axis=-1), (1, reps))
    sin_t = np.tile(np.concatenate([-sin, sin], axis=-1), (1, reps))
    k_scale = RET_KEY_DIM ** -0.5
    return tuple(jnp.asarray(t, dtype=f32) for t in (cos_t, sin_t, cos_t * k_scale, sin_t * k_scale))


def kernel(x, mem, norm_g, fox_w_in, fox_b_f, ret_w_in, mem_norm_g, w_mem_kv, w_out, final_norm_g):
    t_rows = BATCH * SEQ
    x2 = x.reshape(t_rows, D_MODEL)

    kv0, kv1 = _norm_proj(mem.reshape(BATCH * N_MEM, D_MODEL), mem_norm_g,
                          [w_mem_kv[0].astype(bf16), w_mem_kv[1].astype(bf16)],
                          [2 * MEM_W, 2 * MEM_W], [bf16, bf16], name="mem_kv_proj")

    o_f = 3 * MAIN_W
    o_qm = o_f + N_MAIN_HEADS
    w0t = jnp.swapaxes(fox_w_in[0], 0, 1).astype(bf16)
    w_qmz = w0t[o_qm:]
    w_f = jnp.pad(w0t[o_f:o_qm], ((0, F_PAD - N_MAIN_HEADS), (0, 0)))
    q, k, vt, qm, z, f, norms = _norm_proj(
        x2, norm_g[0], [(w0t, o_f), w_qmz, w_f], [MAIN_W, MAIN_W, MAIN_W, MEM_W, INNER, F_PAD],
        [bf16, bf16, bf16, bf16, bf16, f32], ["fox_q", "fox_k", "kt", "plain", "half", "plain"],
        w_transposed=True, tm=FOX_PROJ_TM, name="fox_in_proj")
    nb, nbq = _fox_gate(f, fox_b_f[0])
    o_main = _fox_attention(q, k, vt, nb, z, nbq, norms)
    x2 = _out_proj(o_main, qm, z, kv0, w_out, 0, x2, final_norm_g, final_norm=False)

    q, kt, v, qm, z = _norm_proj(
        x2, norm_g[1], [ret_w_in], [RET_QK_W, RET_QK_W, MAIN_W, MEM_W, INNER], [bf16] * 5,
        ["rot_q", "rot_kt", "plain", "plain", "half"], tables=_rotary_tables(), tm=RET_PROJ_TM,
        name="ret_in_proj")
    o_main = _retention(q, kt, v, z)
    out = _out_proj(o_main, qm, z, kv1, w_out, 1, x2, final_norm_g, final_norm=True)
    return out.reshape(BATCH, SEQ, D_MODEL)
```

```python
import functools
import math

import jax
import jax.numpy as jnp
import numpy as np
from jax import lax
from jax.experimental import pallas as pl
from jax.experimental.pallas import tpu as pltpu

D_MODEL = 1024
BATCH = 8
SEQ = 2048
HEAD_DIM = 128
N_MAIN_HEADS = 12
N_MEM_HEADS = 4
N_MEM = 256
MAIN_W = N_MAIN_HEADS * HEAD_DIM
MEM_W = N_MEM_HEADS * HEAD_DIM
INNER = MAIN_W + MEM_W
RET_KEY_DIM = HEAD_DIM // 2
RET_QK_W = N_MAIN_HEADS * RET_KEY_DIM
RET_CHUNK = 128
ROPE_BASE = 10000.0
EPS = 1e-6
NEG = -1e30
LOG2E = 1.4426950408889634

LANES = 128
F_PAD = LANES
VMEM_LIMIT = 56 * 1024 * 1024
PROJ_VMEM_LIMIT = 60 * 1024 * 1024

PROJ_TM = 512
PROJ_TN = 512
RET_PROJ_TM = 512
FOX_PROJ_TM = 1024
WEIGHT_CAST_ROWS = 128
FOX_T = 256
FOX_HEADS_PER_STEP = 6
FOX_ITEMS_PER_ITER = 7
FOX_ITEMS_PER_ITER_ONLINE = 2
FOX_DIAG_TILES_PER_ITER = 4
BIAS_TERMS = 3
FOX_SUM_ROWS = 16
FOX_Q_SCALE = LOG2E / math.sqrt(HEAD_DIM)
FOX_BOUND_MARGIN = 1.02
FOX_MIN_DENOM = 2.0 ** -90
GATE_HEAD_ROWS = 16
RET_PAIRS_PER_STEP = 3
RET_CHUNKS_PER_ITER = 2
OUT_TM = 1024

f32 = jnp.float32
bf16 = jnp.bfloat16


def _silu_of_half(h):
    return h + h * jnp.tanh(h)


def _rmsnorm_rows(x, g):
    ms = jnp.mean(x * x, axis=-1, keepdims=True)
    return (x * lax.rsqrt(ms + EPS)) * g


def _split_bf16(v):
    hi = v.astype(bf16)
    r1 = v - hi.astype(f32)
    mid = r1.astype(bf16)
    lo = (r1 - mid.astype(f32)).astype(bf16)
    return hi, mid, lo


def _norm_proj_kernel(*refs, n_weights, w_transposed, cast_weight, out_widths, out_kinds):
    track_norms = "fox_q" in out_kinds
    n_out = len(out_widths) + (1 if track_norms else 0)
    if cast_weight:
        w_copy_ref = refs[-1]
        refs = refs[:-1]
    x_ref, g_ref = refs[:2]
    w_refs = refs[2:2 + n_weights]
    tab_refs = refs[2 + n_weights:len(refs) - n_out]
    out_refs = refs[len(refs) - n_out:]
    if cast_weight:
        @pl.when(pl.program_id(0) == 0)
        def _():
            for r0 in range(0, w_copy_ref.shape[0], WEIGHT_CAST_ROWS):
                rows = slice(r0, r0 + WEIGHT_CAST_ROWS)
                w_copy_ref[rows, :] = w_refs[0][rows, :].astype(bf16)
        w_refs = [w_copy_ref]
    h = _rmsnorm_rows(x_ref[...], g_ref[...]).astype(bf16)
    tm = h.shape[0]
    if "rot_q" in out_kinds:
        lane = lax.broadcasted_iota(jnp.int32, (tm, LANES), 1)
        first_half = (lane % RET_KEY_DIM) < (RET_KEY_DIM // 2)
    if track_norms:
        norm_sub = lax.broadcasted_iota(jnp.int32, (8, LANES), 0)
        norm_lane = lax.broadcasted_iota(jnp.int32, (8, LANES), 1)
        norms = jnp.zeros((8, LANES), f32)

        def with_norms(norms, y, row, head0):
            for s0 in range(0, y.shape[1], HEAD_DIM):
                ys = y[:, s0:s0 + HEAD_DIM]
                worst = jnp.max(jnp.sum(ys * ys, axis=1, keepdims=True), axis=0, keepdims=True)
                norms = jnp.where((norm_sub == row) & (norm_lane == head0 + s0 // HEAD_DIM), worst, norms)
            return norms
    out_axis = 0 if w_transposed else 1
    wi, col = 0, 0
    for o_ref, width, kind in zip(out_refs, out_widths, out_kinds):
        if col == w_refs[wi].shape[out_axis]:
            wi, col = wi + 1, 0
        w_ref = w_refs[wi]
        for c0 in range(0, width, PROJ_TN):
            cw = min(PROJ_TN, width - c0)
            if w_transposed:
                y = lax.dot_general(h, w_ref[col + c0:col + c0 + cw, :], (((1,), (1,)), ((), ())),
                                    preferred_element_type=f32)
            else:
                y = jnp.dot(h, w_ref[:, col + c0:col + c0 + cw], preferred_element_type=f32)
            if kind in ("rot_q", "rot_kt"):
                cos_ref, sin_ref = tab_refs[:2] if kind == "rot_q" else tab_refs[2:]
                for s0 in range(0, cw, LANES):
                    ys = y[:, s0:s0 + LANES]
                    sw = jnp.where(first_half, pltpu.roll(ys, LANES - RET_KEY_DIM // 2, 1),
                                   pltpu.roll(ys, RET_KEY_DIM // 2, 1))
                    r = ys * cos_ref[...] + sw * sin_ref[...]
                    if kind == "rot_q":
                        o_ref[:, c0 + s0:c0 + s0 + LANES] = r.astype(o_ref.dtype)
                    else:
                        for r0 in range(tm // RET_CHUNK):
                            piece = r[r0 * RET_CHUNK:(r0 + 1) * RET_CHUNK, :]
                            o_ref[r0, c0 + s0:c0 + s0 + LANES, :] = piece.T.astype(o_ref.dtype)
            elif kind in ("kt", "fox_q"):
                if kind == "fox_q":
                    y = y * FOX_Q_SCALE
                    norms = with_norms(norms, y, 0, c0 // HEAD_DIM)
                for r0 in range(tm // FOX_T):
                    for s0 in range(0, cw, LANES):
                        piece = y[r0 * FOX_T:(r0 + 1) * FOX_T, s0:s0 + LANES]
                        o_ref[r0, c0 + s0:c0 + s0 + LANES, :] = piece.T.astype(o_ref.dtype)
            elif kind == "fox_k":
                norms = with_norms(norms, y, 1, c0 // HEAD_DIM)
                o_ref[:, c0:c0 + cw] = y.astype(o_ref.dtype)
            elif kind == "half":
                o_ref[:, c0:c0 + cw] = (y * 0.5).astype(o_ref.dtype)
            else:
                o_ref[:, c0:c0 + cw] = y.astype(o_ref.dtype)
        col += width
    if track_norms:
        out_refs[-1][...] = norms


def _norm_proj(x, g, weights, out_widths, out_dtypes, out_kinds=None, *, w_transposed=False, tables=None,
               tm=PROJ_TM, name):
    t_rows, d = x.shape
    out_kinds = tuple(out_kinds or ["plain"] * len(out_widths))
    cast_weight = not isinstance(weights[0], tuple) and weights[0].ndim == 3
    w_arrays = [w[0] if isinstance(w, tuple) else w for w in weights]
    w_blocks = [(w[1], d) if isinstance(w, tuple) else w.shape[-2:] for w in weights]
    out_axis = 0 if w_transposed else 1
    assert sum(out_widths) == sum(blk[out_axis] for blk in w_blocks) and t_rows % tm == 0
    tiles_per_seq = SEQ // tm
    in_specs = [
        pl.BlockSpec((tm, d), lambda i: (i, 0)),
        pl.BlockSpec((1, d), lambda i: (0, 0)),
    ]
    scratch = []
    if cast_weight:
        assert len(weights) == 1 and not w_transposed
        in_specs.append(pl.BlockSpec((None,) + w_blocks[0], lambda i: (0, 0, 0), pipeline_mode=pl.Buffered(1)))
        scratch.append(pltpu.VMEM(w_blocks[0], bf16))
    else:
        in_specs += [pl.BlockSpec(blk, lambda i: (0, 0), pipeline_mode=pl.Buffered(1)) for blk in w_blocks]
    args = [x, g.reshape(1, d), *w_arrays]
    if "rot_q" in out_kinds:
        for tab in tables:
            in_specs.append(pl.BlockSpec((tm, LANES), lambda i: (i % tiles_per_seq, 0)))
            args.append(tab)
    out_specs, out_shape = [], []
    for wd, dt, kind in zip(out_widths, out_dtypes, out_kinds):
        if kind in ("kt", "fox_q", "rot_kt"):
            tile = RET_CHUNK if kind == "rot_kt" else FOX_T
            out_specs.append(pl.BlockSpec((None, tm // tile, wd, tile),
                                          lambda i: (i // tiles_per_seq, i % tiles_per_seq, 0, 0)))
            out_shape.append(jax.ShapeDtypeStruct((t_rows // SEQ, SEQ // tile, wd, tile), dt))
        else:
            out_specs.append(pl.BlockSpec((tm, wd), lambda i: (i, 0)))
            out_shape.append(jax.ShapeDtypeStruct((t_rows, wd), dt))
    if "fox_q" in out_kinds:
        out_specs.append(pl.BlockSpec((None, 8, LANES), lambda i: (i, 0, 0)))
        out_shape.append(jax.ShapeDtypeStruct((t_rows // tm, 8, LANES), f32))
    return pl.pallas_call(
        functools.partial(_norm_proj_kernel, n_weights=len(weights), w_transposed=w_transposed,
                          cast_weight=cast_weight, out_widths=tuple(out_widths), out_kinds=out_kinds),
        grid=(t_rows // tm,),
        in_specs=in_specs,
        out_specs=out_specs,
        out_shape=out_shape,
        scratch_shapes=scratch,
        compiler_params=pltpu.CompilerParams(dimension_semantics=("arbitrary",),
                                             vmem_limit_bytes=PROJ_VMEM_LIMIT),
        name=name,
    )(*args)


def _fox_gate_kernel(f_ref, bf_ref, o_ref, ot_ref):
    blk = LANES
    row = lax.broadcasted_iota(jnp.int32, (blk, blk), 0)
    coli = lax.broadcasted_iota(jnp.int32, (blk, blk), 1)
    tri = jnp.where(row >= coli, 1.0, 0.0).astype(bf16)
    spread = [jnp.where((coli == BIAS_TERMS * row + t) & (row < N_MAIN_HEADS), 1.0, 0.0).astype(bf16)
              for t in range(BIAS_TERMS)]
    carry = jnp.zeros((1, F_PAD), f32)
    for b0 in range(0, SEQ, blk):
        xg = f_ref[b0:b0 + blk, :] + bf_ref[...]
        ls = jnp.minimum(xg, 0.0) - jnp.log1p(jnp.exp(-jnp.abs(xg)))
        cs = sum(jnp.dot(tri, term, preferred_element_type=f32) for term in _split_bf16(ls)) + carry
        carry = cs[blk - 1:blk, :]
        bias = cs * (-LOG2E)
        nb = sum(jnp.dot(term, sp, preferred_element_type=f32)
                 for term, sp in zip(_split_bf16(bias), spread))
        o_ref[b0:b0 + blk, :] = nb.astype(o_ref.dtype)
        ot_ref[:, b0 // FOX_T, b0 % FOX_T:b0 % FOX_T + blk] = bias.T[:ot_ref.shape[0], :]


def _fox_gate(f, b_f):
    bf_pad = jnp.zeros((1, F_PAD), f32).at[0, :N_MAIN_HEADS].set(b_f.astype(f32))
    return pl.pallas_call(
        _fox_gate_kernel,
        grid=(BATCH,),
        in_specs=[pl.BlockSpec((SEQ, F_PAD), lambda b: (b, 0)),
                  pl.BlockSpec((1, F_PAD), lambda b: (0, 0))],
        out_specs=[pl.BlockSpec((SEQ, LANES), lambda b: (b, 0)),
                   pl.BlockSpec((None, GATE_HEAD_ROWS, SEQ // FOX_T, FOX_T), lambda b: (b, 0, 0, 0))],
        out_shape=[jax.ShapeDtypeStruct((BATCH * SEQ, LANES), bf16),
                   jax.ShapeDtypeStruct((BATCH, GATE_HEAD_ROWS, SEQ // FOX_T, FOX_T), f32)],
        compiler_params=pltpu.CompilerParams(dimension_semantics=("arbitrary",)),
        name="fox_gate",
    )(f, bf_pad)


def _fox_attn_kernel(qt_ref, k_ref, vt_ref, nb_ref, z_ref, nbq_ref, norm_ref, o_ref,
                     m_ref, acc_ref, u_ref, p_ref, a_ref):
    t = FOX_T
    nt = SEQ // t
    head0 = pl.program_id(1) * FOX_HEADS_PER_STEP
    heads = range(FOX_HEADS_PER_STEP)
    col = [slice(hd * HEAD_DIM, (hd + 1) * HEAD_DIM) for hd in heads]
    sub = lax.broadcasted_iota(jnp.int32, (LANES, t), 0)
    sel = []
    for hd in heads:
        lo = BIAS_TERMS * (head0 + hd)
        sel.append(jnp.where((sub >= lo) & (sub < lo + BIAS_TERMS), 1.0, 0.0).astype(bf16))
    ones_rows = jnp.ones((FOX_SUM_ROWS, t), bf16)

    def tile_rows(i):
        if isinstance(i, int):
            return slice(i * t, (i + 1) * t)
        return pl.ds(pl.multiple_of(i * t, t), t)

    def score(qt, kt, hd):
        k_aug = jnp.concatenate([k_ref[tile_rows(kt), col[hd]], nb_ref[tile_rows(kt), :]], axis=1)
        q_aug_t = jnp.concatenate([qt_ref[qt, col[hd], :], sel[hd]], axis=0)
        return jnp.dot(k_aug, q_aug_t, preferred_element_type=f32)

    def scores(qt, kt):
        return [score(qt, kt, hd) for hd in heads]

    def pv1(kt, p_hd, hd):
        return jnp.dot(jnp.concatenate([vt_ref[kt, col[hd], :], ones_rows], axis=0), p_hd,
                       preferred_element_type=f32)

    def pv(kt, p):
        return [pv1(kt, p[hd], hd) for hd in heads]

    def finalize(qt):
        for hd in heads:
            acc = acc_ref[qt, hd]
            out_t = acc[:HEAD_DIM, :] * (1.0 / acc[HEAD_DIM:HEAD_DIM + 1, :])
            gate = _silu_of_half(z_ref[tile_rows(qt), col[hd]].astype(f32))
            o_ref[tile_rows(qt), col[hd]] = (out_t.T * gate).astype(o_ref.dtype)

    def next_item(qt, kt):
        wrap = kt + 1 >= qt
        return jnp.minimum(jnp.where(wrap, qt + 1, qt), nt - 1), jnp.where(wrap, 0, kt + 1)

    def load_list(ref):
        return [ref[hd] for hd in heads]

    def store_list(ref, vals):
        for hd in heads:
            ref[hd] = vals[hd]

    def run(online):
        def numerators(qt, u, masked):
            half = t // 2
            if masked:
                ksub = lax.broadcasted_iota(jnp.int32, (t if online else half,) * 2, 0)
                qlane = lax.broadcasted_iota(jnp.int32, (t if online else half,) * 2, 1)
                causal = ksub <= qlane
            p, alpha = [], []
            for hd in heads:
                m_old = m_ref[qt, hd]
                if online:
                    uh = jnp.where(causal, u[hd], NEG) if masked else u[hd]
                    m_new = jnp.maximum(m_old, jnp.max(uh, axis=0, keepdims=True))
                    alpha.append(jnp.exp2(m_old - m_new))
                    m_ref[qt, hd] = m_new
                    p.append(jnp.exp2(uh - m_new).astype(bf16))
                elif masked:
                    lo, hi = slice(0, half), slice(half, t)
                    p_ll = jnp.exp2(jnp.where(causal, u[hd][lo, lo], NEG) - m_old[:, lo])
                    p_lh = jnp.exp2(u[hd][lo, hi] - m_old[:, hi])
                    p_hh = jnp.exp2(jnp.where(causal, u[hd][hi, hi], NEG) - m_old[:, hi])
                    top = jnp.concatenate([p_ll, p_lh], axis=1)
                    bottom = jnp.concatenate([jnp.zeros_like(p_hh), p_hh], axis=1)
                    p.append(jnp.concatenate([top, bottom], axis=0).astype(bf16))
                else:
                    p.append(jnp.exp2(u[hd] - m_old).astype(bf16))
            return p, alpha

        def accumulate(qt, alpha, pv_vals):
            for hd in heads:
                prev = alpha[hd] * acc_ref[qt, hd] if online else acc_ref[qt, hd]
                acc_ref[qt, hd] = prev + pv_vals[hd]

        acc_ref[...] = jnp.zeros(acc_ref.shape, f32)
        p_ref[...] = jnp.zeros(p_ref.shape, bf16)
        if online:
            m_ref[...] = jnp.full(m_ref.shape, NEG, f32)
            a_ref[...] = jnp.ones(a_ref.shape, f32)

        n_items = nt * (nt - 1) // 2
        ni = FOX_ITEMS_PER_ITER_ONLINE if online else FOX_ITEMS_PER_ITER
        assert n_items % ni == 0
        store_list(u_ref, scores(1, 0))

        def pass1(_, carry):
            q0, k0, q_prev, k_prev = carry
            items = [(q0, k0)]
            for _k in range(ni):
                items.append(next_item(*items[-1]))
            u_cur = load_list(u_ref)
            pv_vals = pv(k_prev, load_list(p_ref))
            u_next = scores(*items[1])
            p_cur, al_cur = numerators(q0, u_cur, False)
            accumulate(q_prev, load_list(a_ref) if online else None, pv_vals)
            for k in range(1, ni):
                u_cur, u_next, pv_vals = u_next, [], []
                for hd in heads:
                    s = score(items[k + 1][0], items[k + 1][1], hd)
                    if k == ni - 1:
                        u_ref[hd] = s
                    else:
                        u_next.append(s)
                    pv_vals.append(pv1(items[k - 1][1], p_cur[hd], hd))
                p_new, al_new = numerators(items[k][0], u_cur, False)
                accumulate(items[k - 1][0], al_cur, pv_vals)
                p_cur, al_cur = p_new, al_new
            store_list(p_ref, p_cur)
            if online:
                store_list(a_ref, al_cur)
            return items[ni] + items[ni - 1]

        one, zero = jnp.int32(1), jnp.int32(0)
        _, _, q_prev, k_prev = lax.fori_loop(0, n_items // ni, pass1, (one, zero, one, zero))
        accumulate(q_prev, load_list(a_ref) if online else None, pv(k_prev, load_list(p_ref)))

        nd = FOX_ITEMS_PER_ITER_ONLINE if online else FOX_DIAG_TILES_PER_ITER
        assert nt % nd == 0
        store_list(u_ref, scores(0, 0))

        def pass2(i, carry):
            u_cur = load_list(u_ref)
            for k in range(nd):
                tile = nd * i + k
                nxt = jnp.minimum(tile + 1, nt - 1)
                u_next = scores(nxt, nxt)
                p_cur, al_cur = numerators(tile, u_cur, True)
                accumulate(tile, al_cur, pv(tile, p_cur))
                finalize(tile)
                u_cur = u_next
            store_list(u_ref, u_cur)
            return carry

        lax.fori_loop(0, nt // nd, pass2, 0)

    norms = jnp.max(norm_ref[...], axis=0)
    hlane = lax.broadcasted_iota(jnp.int32, (1, LANES), 1)
    for hd in heads:
        pick = hlane == head0 + hd
        q2 = jnp.max(jnp.where(pick, norms[0:1, :], 0.0), axis=1, keepdims=True)
        k2 = jnp.max(jnp.where(pick, norms[1:2, :], 0.0), axis=1, keepdims=True)
        qk_bound = jnp.sqrt(q2 * k2) * FOX_BOUND_MARGIN + 1.0
        for qt in range(nt):
            m_ref[qt, hd] = nbq_ref[hd, qt:qt + 1, :] + qk_bound

    def run_fixed_shift():
        half = t // 2
        ksub = lax.broadcasted_iota(jnp.int32, (half, half), 0)
        qlane = lax.broadcasted_iota(jnp.int32, (half, half), 1)
        causal = ksub <= qlane
        lo, hi = slice(0, half), slice(half, t)
        units = [(qt, hd) for qt in range(nt) for hd in heads]

        def unit_scores(qt, hd):
            return [score(qt, kt, hd) for kt in range(qt + 1)]

        def unit_numerators(qt, hd, u):
            m = m_ref[qt, hd]
            pf = [jnp.exp2(u[kt] - m) for kt in range(qt)]
            ud = u[qt]
            p_ll = jnp.exp2(jnp.where(causal, ud[lo, lo], NEG) - m[:, lo])
            p_lh = jnp.exp2(ud[lo, hi] - m[:, hi])
            p_hh = jnp.exp2(jnp.where(causal, ud[hi, hi], NEG) - m[:, hi])
            top = jnp.concatenate([p_ll, p_lh], axis=1)
            bottom = jnp.concatenate([jnp.zeros_like(p_hh), p_hh], axis=1)
            pf.append(jnp.concatenate([top, bottom], axis=0))
            partial = sum(blk.reshape(t // 8, 8, t).sum(axis=0) for blk in pf)
            return [blk.astype(bf16) for blk in pf], jnp.sum(partial, axis=0, keepdims=True)

        def unit_output(qt, hd, p, denom):
            vt_all = jnp.concatenate([vt_ref[kt, col[hd], :] for kt in range(qt + 1)], axis=1)
            acc = jnp.dot(vt_all, jnp.concatenate(p, axis=0), preferred_element_type=f32)
            out_t = acc * (1.0 / denom)
            gate = _silu_of_half(z_ref[qt * t:(qt + 1) * t, col[hd]].astype(f32))
            o_ref[qt * t:(qt + 1) * t, col[hd]] = (out_t.T * gate).astype(o_ref.dtype)

        denom_min = None
        u_next = unit_scores(*units[0])
        for w, (qt, hd) in enumerate(units):
            u_cur = u_next
            if w + 1 < len(units):
                u_next = unit_scores(*units[w + 1])
            p, denom = unit_numerators(qt, hd, u_cur)
            unit_output(qt, hd, p, denom)
            denom_min = denom if denom_min is None else jnp.minimum(denom_min, denom)
        return jnp.min(denom_min)

    denom_min = run_fixed_shift()

    @pl.when(jnp.logical_not(denom_min >= FOX_MIN_DENOM))
    def _():
        run(online=True)


def _fox_attention(q, k, vt, nb, z, nbq, norms):
    hw = FOX_HEADS_PER_STEP * HEAD_DIM
    return pl.pallas_call(
        _fox_attn_kernel,
        grid=(BATCH, N_MAIN_HEADS // FOX_HEADS_PER_STEP),
        in_specs=[
            pl.BlockSpec((None, SEQ // FOX_T, hw, FOX_T), lambda b, p: (b, 0, p, 0)),
            pl.BlockSpec((SEQ, hw), lambda b, p: (b, p)),
            pl.BlockSpec((None, SEQ // FOX_T, hw, FOX_T), lambda b, p: (b, 0, p, 0)),
            pl.BlockSpec((SEQ, LANES), lambda b, p: (b, 0)),
            pl.BlockSpec((SEQ, hw), lambda b, p: (b, p)),
            pl.BlockSpec((None, FOX_HEADS_PER_STEP, SEQ // FOX_T, FOX_T), lambda b, p: (b, p, 0, 0)),
            pl.BlockSpec((SEQ // FOX_PROJ_TM, 8, LANES), lambda b, p: (b, 0, 0)),
        ],
        out_specs=pl.BlockSpec((SEQ, hw), lambda b, p: (b, p)),
        out_shape=jax.ShapeDtypeStruct((BATCH * SEQ, MAIN_W), bf16),
        scratch_shapes=[
            pltpu.VMEM((SEQ // FOX_T, FOX_HEADS_PER_STEP, 1, FOX_T), f32),
            pltpu.VMEM((SEQ // FOX_T, FOX_HEADS_PER_STEP, HEAD_DIM + FOX_SUM_ROWS, FOX_T), f32),
            pltpu.VMEM((FOX_HEADS_PER_STEP, FOX_T, FOX_T), f32),
            pltpu.VMEM((FOX_HEADS_PER_STEP, FOX_T, FOX_T), bf16),
            pltpu.VMEM((FOX_HEADS_PER_STEP, 1, FOX_T), f32),
        ],
        compiler_params=pltpu.CompilerParams(dimension_semantics=("arbitrary", "arbitrary"),
                                             vmem_limit_bytes=VMEM_LIMIT),
        name="fox_attention",
    )(q, k, vt, nb, z, nbq, norms)


def _retention_kernel(q_ref, kt_ref, v_ref, z_ref, d_ref, eps_ref, zeta_ref, g_ref, o_ref, r_ref):
    c_len = RET_CHUNK
    r_ref[...] = jnp.zeros(r_ref.shape, f32)
    lane = lax.broadcasted_iota(jnp.int32, (c_len, LANES), 1)
    sub = lax.broadcasted_iota(jnp.int32, (LANES, c_len), 0)
    q_mask = (lane < RET_KEY_DIM, lane >= RET_KEY_DIM)
    k_mask = (sub < RET_KEY_DIM, sub >= RET_KEY_DIM)
    pairs = range(RET_PAIRS_PER_STEP)
    chunks = range(RET_CHUNKS_PER_ITER)

    def body(it, carry):
        chunk = [it * RET_CHUNKS_PER_ITER + ci for ci in chunks]
        rows = [pl.ds(pl.multiple_of(chunk[ci] * c_len, c_len), c_len) for ci in chunks]
        q2, s, upd = {}, {}, {}
        for ci in chunks:
            for pr in pairs:
                q2[ci, pr] = q_ref[rows[ci], pr * LANES:(pr + 1) * LANES]
                kt2 = kt_ref[chunk[ci], pr * LANES:(pr + 1) * LANES, :]
                v2 = v_ref[rows[ci], pr * 2 * HEAD_DIM:(pr + 1) * 2 * HEAD_DIM]
                k_heads = jnp.concatenate([jnp.where(k_mask[hd], kt2, jnp.zeros_like(kt2))
                                           for hd in range(2)], axis=1)
                s[ci, pr] = jnp.dot(q2[ci, pr], k_heads, preferred_element_type=f32)
                kz = (kt2.astype(f32) * zeta_ref[pr]).astype(bf16)
                upd[ci, pr] = jnp.dot(kz, v2, preferred_element_type=f32)
        r_b = {}
        for pr in pairs:
            r_state = r_ref[pr]
            for ci in chunks:
                r_b[ci, pr] = r_state.astype(bf16)
                r_state = r_state * g_ref[pr] + upd[ci, pr]
            r_ref[pr] = r_state
        o = {}
        for ci in chunks:
            for pr in pairs:
                inner = (s[ci, pr] * d_ref[pr]).astype(bf16)
                for hd in range(2):
                    cols = slice((2 * pr + hd) * HEAD_DIM, (2 * pr + hd + 1) * HEAD_DIM)
                    q_hd = jnp.where(q_mask[hd], q2[ci, pr], jnp.zeros_like(q2[ci, pr]))
                    lhs = jnp.concatenate([inner[:, hd * c_len:(hd + 1) * c_len], q_hd], axis=1)
                    rhs = jnp.concatenate([v_ref[rows[ci], cols],
                                           r_b[ci, pr][:, hd * HEAD_DIM:(hd + 1) * HEAD_DIM]], axis=0)
                    o[ci, pr, hd] = jnp.dot(lhs, rhs, preferred_element_type=f32)
        for ci in chunks:
            for pr in pairs:
                for hd in range(2):
                    cols = slice((2 * pr + hd) * HEAD_DIM, (2 * pr + hd + 1) * HEAD_DIM)
                    oh = o[ci, pr, hd]
                    ms = jnp.mean(oh * oh, axis=-1, keepdims=True)
                    on = oh * lax.rsqrt(ms + eps_ref[2 * pr + hd])
                    zg = _silu_of_half(z_ref[rows[ci], cols].astype(f32))
                    o_ref[rows[ci], cols] = (on * zg).astype(o_ref.dtype)
        return carry

    lax.fori_loop(0, SEQ // (c_len * RET_CHUNKS_PER_ITER), body, 0)


def _retention_tables():
    h, c_len = N_MAIN_HEADS, RET_CHUNK
    lg = np.log1p(-np.exp2(-5.0 - np.arange(h, dtype=np.float64)))
    n = np.arange(c_len, dtype=np.float64)
    causal = n[:, None] >= n[None, :]
    d_col = np.where(causal[None], np.exp(-lg[:, None, None] * (n[None, None, :] + 1.0)), 0.0)
    d_pair = d_col.reshape(h // 2, 2, c_len, c_len).transpose(0, 2, 1, 3).reshape(h // 2, c_len, 2 * c_len)
    xi = np.exp(lg[:, None] * (n[None, :] + 1.0))
    zeta = np.exp(lg[:, None] * (c_len - 1.0 - n[None, :]))
    g_chunk = np.exp(lg * c_len)
    eps_rows = np.broadcast_to((EPS / (xi * xi))[:, :, None], (h, c_len, HEAD_DIM))
    zeta_t = np.repeat(zeta.reshape(h // 2, 2, c_len), RET_KEY_DIM, axis=1)
    g2 = np.broadcast_to(np.repeat(g_chunk.reshape(h // 2, 2), RET_KEY_DIM, axis=-1)[:, :, None],
                         (h // 2, 2 * RET_KEY_DIM, 2 * HEAD_DIM))
    return tuple(jnp.asarray(np.ascontiguousarray(t), dtype=f32) for t in (d_pair, eps_rows, zeta_t, g2))


def _retention(q, kt, v, z):
    n_pairs = N_MAIN_HEADS // 2
    pp = RET_PAIRS_PER_STEP
    steps = n_pairs // pp
    d_pair, eps_rows, zeta_t, g2 = _retention_tables()
    c_len = RET_CHUNK
    return pl.pallas_call(
        _retention_kernel,
        grid=(BATCH, steps),
        in_specs=[
            pl.BlockSpec((SEQ, pp * LANES), lambda b, p: (b, p)),
            pl.BlockSpec((None, SEQ // c_len, pp * LANES, c_len), lambda b, p: (b, 0, p, 0)),
            pl.BlockSpec((SEQ, pp * 2 * HEAD_DIM), lambda b, p: (b, p)),
            pl.BlockSpec((SEQ, pp * 2 * HEAD_DIM), lambda b, p: (b, p)),
            pl.BlockSpec((pp, c_len, 2 * c_len), lambda b, p: (p, 0, 0)),
            pl.BlockSpec((2 * pp, c_len, HEAD_DIM), lambda b, p: (p, 0, 0)),
            pl.BlockSpec((pp, LANES, c_len), lambda b, p: (p, 0, 0)),
            pl.BlockSpec((pp, LANES, 2 * HEAD_DIM), lambda b, p: (p, 0, 0)),
        ],
        out_specs=pl.BlockSpec((SEQ, pp * 2 * HEAD_DIM), lambda b, p: (b, p)),
        out_shape=jax.ShapeDtypeStruct((BATCH * SEQ, MAIN_W), bf16),
        scratch_shapes=[pltpu.VMEM((pp, LANES, 2 * HEAD_DIM), f32)],
        compiler_params=pltpu.CompilerParams(dimension_semantics=("arbitrary", "arbitrary"),
                                             vmem_limit_bytes=VMEM_LIMIT),
        name="retention",
    )(q, kt, v, z, d_pair, eps_rows, zeta_t, g2)


def _out_proj_kernel(om_ref, qm_ref, zm_ref, kv_ref, w32_ref, x_ref, g_ref, o_ref, w_ref, *, final_norm):
    @pl.when(pl.program_id(0) == 0)
    def _():
        for r0 in range(0, INNER, WEIGHT_CAST_ROWS):
            rows = slice(r0, r0 + WEIGHT_CAST_ROWS)
            w_ref[rows, :] = w32_ref[rows, :].astype(bf16)

    scale = 1.0 / math.sqrt(HEAD_DIM)
    heads = range(N_MEM_HEADS)
    col = [slice(hd * HEAD_DIM, (hd + 1) * HEAD_DIM) for hd in heads]
    half = D_MODEL // 2
    for r0 in range(0, OUT_TM, OUT_TM // 2):
        rows = slice(r0, r0 + OUT_TM // 2)
        s = [lax.dot_general(qm_ref[rows, col[hd]], kv_ref[:, col[hd]], (((1,), (1,)), ((), ())),
                             preferred_element_type=f32) * scale for hd in heads]
        y_lo = jnp.dot(om_ref[rows, :], w_ref[:MAIN_W, :half], preferred_element_type=f32)
        p = []
        for hd in heads:
            e = jnp.exp(s[hd] - jnp.max(s[hd], axis=-1, keepdims=True))
            p.append((e * (1.0 / jnp.sum(e, axis=-1, keepdims=True))).astype(bf16))
        memo = [jnp.dot(p[hd], kv_ref[:, MEM_W + hd * HEAD_DIM:MEM_W + (hd + 1) * HEAD_DIM],
                        preferred_element_type=f32) for hd in heads]
        y_hi = jnp.dot(om_ref[rows, :], w_ref[:MAIN_W, half:], preferred_element_type=f32)
        og = jnp.concatenate([(memo[hd] * _silu_of_half(zm_ref[rows, col[hd]].astype(f32))).astype(bf16)
                              for hd in heads], axis=1)
        y = jnp.concatenate([y_lo, y_hi], axis=1) + jnp.dot(og, w_ref[MAIN_W:, :], preferred_element_type=f32)
        xn = x_ref[rows, :] + y
        if final_norm:
            xn = _rmsnorm_rows(xn, g_ref[...])
        o_ref[rows, :] = xn


def _out_proj(o_main, qm, z, kv, w_out, layer, x, g_final, *, final_norm):
    t_rows = x.shape[0]
    tiles_per_seq = SEQ // OUT_TM
    z_blk = MAIN_W // MEM_W
    return pl.pallas_call(
        functools.partial(_out_proj_kernel, final_norm=final_norm),
        grid=(t_rows // OUT_TM,),
        in_specs=[
            pl.BlockSpec((OUT_TM, MAIN_W), lambda i: (i, 0)),
            pl.BlockSpec((OUT_TM, MEM_W), lambda i: (i, 0)),
            pl.BlockSpec((OUT_TM, MEM_W), lambda i: (i, z_blk)),
            pl.BlockSpec((N_MEM, 2 * MEM_W), lambda i: (i // tiles_per_seq, 0)),
            pl.BlockSpec((None, INNER, D_MODEL), lambda i: (layer, 0, 0), pipeline_mode=pl.Buffered(1)),
            pl.BlockSpec((OUT_TM, D_MODEL), lambda i: (i, 0)),
            pl.BlockSpec((1, D_MODEL), lambda i: (0, 0)),
        ],
        out_specs=pl.BlockSpec((OUT_TM, D_MODEL), lambda i: (i, 0)),
        out_shape=jax.ShapeDtypeStruct((t_rows, D_MODEL), f32),
        scratch_shapes=[pltpu.VMEM((INNER, D_MODEL), bf16)],
        compiler_params=pltpu.CompilerParams(dimension_semantics=("arbitrary",),
                                             vmem_limit_bytes=VMEM_LIMIT),
        name="out_proj_final" if final_norm else "out_proj",
    )(o_main, qm, z, kv, w_out, x, g_final.reshape(1, D_MODEL))


def _rotary_tables():
    half = RET_KEY_DIM // 2
    pos = np.arange(SEQ, dtype=np.float64)
    inv = 1.0 / (ROPE_BASE ** (np.arange(half, dtype=np.float64) / half))
    ang = pos[:, None] * inv[None, :]
    cos, sin = np.cos(ang), np.sin(ang)
    reps = LANES // RET_KEY_DIM
    cos_t = np.tile(np.concatenate([cos, cos], axis=-1), (1, reps))
    sin_t = np.tile(np.concatenate([-sin, sin], axis=-1), (1, reps))
    k_scale = RET_KEY_DIM ** -0.5
    return tuple(jnp.asarray(t, dtype=f32) for t in (cos_t, sin_t, cos_t * k_scale, sin_t * k_scale))


def kernel(x, mem, norm_g, fox_w_in, fox_b_f, ret_w_in, mem_norm_g, w_mem_kv, w_out, final_norm_g):
    t_rows = BATCH * SEQ
    x2 = x.reshape(t_rows, D_MODEL)

    kv0, kv1 = _norm_proj(mem.reshape(BATCH * N_MEM, D_MODEL), mem_norm_g,
                          [w_mem_kv[0].astype(bf16), w_mem_kv[1].astype(bf16)],
                          [2 * MEM_W, 2 * MEM_W], [bf16, bf16], name="mem_kv_proj")

    o_f = 3 * MAIN_W
    o_qm = o_f + N_MAIN_HEADS
    w0t = jnp.swapaxes(fox_w_in[0], 0, 1).astype(bf16)
    w_qmz = w0t[o_qm:]
    w_f = jnp.pad(w0t[o_f:o_qm], ((0, F_PAD - N_MAIN_HEADS), (0, 0)))
    q, k, vt, qm, z, f, norms = _norm_proj(
        x2, norm_g[0], [(w0t, o_f), w_qmz, w_f], [MAIN_W, MAIN_W, MAIN_W, MEM_W, INNER, F_PAD],
        [bf16, bf16, bf16, bf16, bf16, f32], ["fox_q", "fox_k", "kt", "plain", "half", "plain"],
        w_transposed=True, tm=FOX_PROJ_TM, name="fox_in_proj")
    nb, nbq = _fox_gate(f, fox_b_f[0])
    o_main = _fox_attention(q, k, vt, nb, z, nbq, norms)
    x2 = _out_proj(o_main, qm, z, kv0, w_out, 0, x2, final_norm_g, final_norm=False)

    q, kt, v, qm, z = _norm_proj(
        x2, norm_g[1], [ret_w_in], [RET_QK_W, RET_QK_W, MAIN_W, MEM_W, INNER], [bf16] * 5,
        ["rot_q", "rot_kt", "plain", "plain", "half"], tables=_rotary_tables(), tm=RET_PROJ_TM,
        name="ret_in_proj")
    o_main = _retention(q, kt, v, z)
    out = _out_proj(o_main, qm, z, kv1, w_out, 1, x2, final_norm_g, final_norm=True)
    return out.reshape(BATCH, SEQ, D_MODEL)
```

```python
import functools
import math

import jax
import jax.numpy as jnp
import numpy as np
from jax import lax
from jax.experimental import pallas as pl
from jax.experimental.pallas import tpu as pltpu

D_MODEL = 1024
BATCH = 8
SEQ = 2048
HEAD_DIM = 128
N_MAIN_HEADS = 12
N_MEM_HEADS = 4
N_MEM = 256
MAIN_W = N_MAIN_HEADS * HEAD_DIM
MEM_W = N_MEM_HEADS * HEAD_DIM
INNER = MAIN_W + MEM_W
RET_KEY_DIM = HEAD_DIM // 2
RET_QK_W = N_MAIN_HEADS * RET_KEY_DIM
RET_CHUNK = 128
ROPE_BASE = 10000.0
EPS = 1e-6
NEG = -1e30
LOG2E = 1.4426950408889634

LANES = 128
F_PAD = LANES
VMEM_LIMIT = 56 * 1024 * 1024
PROJ_VMEM_LIMIT = 60 * 1024 * 1024

PROJ_TM = 512
PROJ_TN = 512
RET_PROJ_TM = 512
FOX_PROJ_TM = 1024
WEIGHT_CAST_ROWS = 128
FOX_T = 256
FOX_HEADS_PER_STEP = 6
FOX_ITEMS_PER_ITER_ONLINE = 2
BIAS_TERMS = 3
FOX_SUM_ROWS = 16
FOX_Q_SCALE = LOG2E / math.sqrt(HEAD_DIM)
FOX_BOUND_MARGIN = 1.02
FOX_MIN_DENOM = 2.0 ** -90
GATE_HEAD_ROWS = 16
RET_PAIRS_PER_STEP = 3
RET_CHUNKS_PER_ITER = 2
OUT_TM = 1024

f32 = jnp.float32
bf16 = jnp.bfloat16


def _silu_of_half(h):
    return h + h * jnp.tanh(h)


def _rmsnorm_rows(x, g):
    ms = jnp.mean(x * x, axis=-1, keepdims=True)
    return (x * lax.rsqrt(ms + EPS)) * g


def _split_bf16(v):
    hi = v.astype(bf16)
    r1 = v - hi.astype(f32)
    mid = r1.astype(bf16)
    lo = (r1 - mid.astype(f32)).astype(bf16)
    return hi, mid, lo


def _norm_proj_kernel(*refs, n_weights, w_transposed, cast_weight, out_widths, out_kinds):
    track_norms = "fox_q" in out_kinds
    n_out = len(out_widths) + (1 if track_norms else 0)
    if cast_weight:
        w_copy_refs = refs[-n_weights:]
        refs = refs[:-n_weights]
    x_ref, g_ref = refs[:2]
    w_refs = refs[2:2 + n_weights]
    tab_refs = refs[2 + n_weights:len(refs) - n_out]
    out_refs = refs[len(refs) - n_out:]
    if cast_weight:
        @pl.when(pl.program_id(0) == 0)
        def _():
            for w32_ref, w_copy_ref in zip(w_refs, w_copy_refs):
                for r0 in range(0, w_copy_ref.shape[0], WEIGHT_CAST_ROWS):
                    rows = slice(r0, r0 + WEIGHT_CAST_ROWS)
                    w_copy_ref[rows, :] = w32_ref[rows, :].astype(bf16)
        w_refs = list(w_copy_refs)
    h = _rmsnorm_rows(x_ref[...], g_ref[...]).astype(bf16)
    tm = h.shape[0]
    if "rot_q" in out_kinds:
        lane = lax.broadcasted_iota(jnp.int32, (tm, LANES), 1)
        first_half = (lane % RET_KEY_DIM) < (RET_KEY_DIM // 2)
    if track_norms:
        norm_sub = lax.broadcasted_iota(jnp.int32, (8, LANES), 0)
        norm_lane = lax.broadcasted_iota(jnp.int32, (8, LANES), 1)
        norms = jnp.zeros((8, LANES), f32)

        def with_norms(norms, y, row, head0):
            for s0 in range(0, y.shape[1], HEAD_DIM):
                ys = y[:, s0:s0 + HEAD_DIM]
                worst = jnp.max(jnp.sum(ys * ys, axis=1, keepdims=True), axis=0, keepdims=True)
                norms = jnp.where((norm_sub == row) & (norm_lane == head0 + s0 // HEAD_DIM), worst, norms)
            return norms
    out_axis = 0 if w_transposed else 1
    wi, col = 0, 0
    for o_ref, width, kind in zip(out_refs, out_widths, out_kinds):
        if col == w_refs[wi].shape[out_axis]:
            wi, col = wi + 1, 0
        w_ref = w_refs[wi]
        for c0 in range(0, width, PROJ_TN):
            cw = min(PROJ_TN, width - c0)
            if w_transposed:
                y = lax.dot_general(h, w_ref[col + c0:col + c0 + cw, :], (((1,), (1,)), ((), ())),
                                    preferred_element_type=f32)
            else:
                y = jnp.dot(h, w_ref[:, col + c0:col + c0 + cw], preferred_element_type=f32)
            if kind in ("rot_q", "rot_kt"):
                cos_ref, sin_ref = tab_refs[:2] if kind == "rot_q" else tab_refs[2:]
                for s0 in range(0, cw, LANES):
                    ys = y[:, s0:s0 + LANES]
                    sw = jnp.where(first_half, pltpu.roll(ys, LANES - RET_KEY_DIM // 2, 1),
                                   pltpu.roll(ys, RET_KEY_DIM // 2, 1))
                    r = ys * cos_ref[...] + sw * sin_ref[...]
                    if kind == "rot_q":
                        o_ref[:, c0 + s0:c0 + s0 + LANES] = r.astype(o_ref.dtype)
                    else:
                        for r0 in range(tm // RET_CHUNK):
                            piece = r[r0 * RET_CHUNK:(r0 + 1) * RET_CHUNK, :]
                            o_ref[r0, c0 + s0:c0 + s0 + LANES, :] = piece.T.astype(o_ref.dtype)
            elif kind in ("kt", "fox_q"):
                if kind == "fox_q":
                    y = y * FOX_Q_SCALE
                    norms = with_norms(norms, y, 0, c0 // HEAD_DIM)
                for r0 in range(tm // FOX_T):
                    for s0 in range(0, cw, LANES):
                        piece = y[r0 * FOX_T:(r0 + 1) * FOX_T, s0:s0 + LANES]
                        o_ref[r0, c0 + s0:c0 + s0 + LANES, :] = piece.T.astype(o_ref.dtype)
            elif kind == "fox_k":
                norms = with_norms(norms, y, 1, c0 // HEAD_DIM)
                o_ref[:, c0:c0 + cw] = y.astype(o_ref.dtype)
            elif kind == "half":
                o_ref[:, c0:c0 + cw] = (y * 0.5).astype(o_ref.dtype)
            else:
                o_ref[:, c0:c0 + cw] = y.astype(o_ref.dtype)
        col += width
    if track_norms:
        out_refs[-1][...] = norms


def _norm_proj(x, g, weights, out_widths, out_dtypes, out_kinds=None, *, w_transposed=False, tables=None,
               tm=PROJ_TM, name):
    t_rows, d = x.shape
    out_kinds = tuple(out_kinds or ["plain"] * len(out_widths))
    cast_weight = isinstance(weights[0], tuple) and weights[0][0].ndim == 3
    w_arrays = [w[0] if isinstance(w, tuple) else w for w in weights]
    if cast_weight:
        w_blocks = [w[0].shape[-2:] for w in weights]
    else:
        w_blocks = [(w[1], d) if isinstance(w, tuple) else w.shape for w in weights]
    out_axis = 0 if w_transposed else 1
    assert sum(out_widths) == sum(blk[out_axis] for blk in w_blocks) and t_rows % tm == 0
    tiles_per_seq = SEQ // tm
    in_specs = [
        pl.BlockSpec((tm, d), lambda i: (i, 0)),
        pl.BlockSpec((1, d), lambda i: (0, 0)),
    ]
    scratch = []
    if cast_weight:
        assert not w_transposed
        for (_, layer), blk in zip(weights, w_blocks):
            in_specs.append(pl.BlockSpec((None,) + blk, lambda i, layer=layer: (layer, 0, 0),
                                         pipeline_mode=pl.Buffered(1)))
            scratch.append(pltpu.VMEM(blk, bf16))
    else:
        in_specs += [pl.BlockSpec(blk, lambda i: (0, 0), pipeline_mode=pl.Buffered(1)) for blk in w_blocks]
    args = [x, g.reshape(1, d), *w_arrays]
    if "rot_q" in out_kinds:
        for tab in tables:
            in_specs.append(pl.BlockSpec((tm, LANES), lambda i: (i % tiles_per_seq, 0)))
            args.append(tab)
    out_specs, out_shape = [], []
    for wd, dt, kind in zip(out_widths, out_dtypes, out_kinds):
        if kind in ("kt", "fox_q", "rot_kt"):
            tile = RET_CHUNK if kind == "rot_kt" else FOX_T
            out_specs.append(pl.BlockSpec((None, tm // tile, wd, tile),
                                          lambda i: (i // tiles_per_seq, i % tiles_per_seq, 0, 0)))
            out_shape.append(jax.ShapeDtypeStruct((t_rows // SEQ, SEQ // tile, wd, tile), dt))
        else:
            out_specs.append(pl.BlockSpec((tm, wd), lambda i: (i, 0)))
            out_shape.append(jax.ShapeDtypeStruct((t_rows, wd), dt))
    if "fox_q" in out_kinds:
        out_specs.append(pl.BlockSpec((None, 8, LANES), lambda i: (i, 0, 0)))
        out_shape.append(jax.ShapeDtypeStruct((t_rows // tm, 8, LANES), f32))
    return pl.pallas_call(
        functools.partial(_norm_proj_kernel, n_weights=len(weights), w_transposed=w_transposed,
                          cast_weight=cast_weight, out_widths=tuple(out_widths), out_kinds=out_kinds),
        grid=(t_rows // tm,),
        in_specs=in_specs,
        out_specs=out_specs,
        out_shape=out_shape,
        scratch_shapes=scratch,
        compiler_params=pltpu.CompilerParams(dimension_semantics=("arbitrary",),
                                             vmem_limit_bytes=PROJ_VMEM_LIMIT),
        name=name,
    )(*args)


def _fox_gate_kernel(f_ref, bf_ref, o_ref, ot_ref):
    blk = LANES
    row = lax.broadcasted_iota(jnp.int32, (blk, blk), 0)
    coli = lax.broadcasted_iota(jnp.int32, (blk, blk), 1)
    tri = jnp.where(row >= coli, 1.0, 0.0).astype(bf16)
    spread = [jnp.where((coli == BIAS_TERMS * row + t) & (row < N_MAIN_HEADS), 1.0, 0.0).astype(bf16)
              for t in range(BIAS_TERMS)]
    carry = jnp.zeros((1, F_PAD), f32)
    for b0 in range(0, SEQ, blk):
        xg = f_ref[b0:b0 + blk, :] + bf_ref[...]
        ls = jnp.minimum(xg, 0.0) - jnp.log1p(jnp.exp(-jnp.abs(xg)))
        cs = sum(jnp.dot(tri, term, preferred_element_type=f32) for term in _split_bf16(ls)) + carry
        carry = cs[blk - 1:blk, :]
        bias = cs * (-LOG2E)
        nb = sum(jnp.dot(term, sp, preferred_element_type=f32)
                 for term, sp in zip(_split_bf16(bias), spread))
        o_ref[b0:b0 + blk, :] = nb.astype(o_ref.dtype)
        ot_ref[:, b0 // FOX_T, b0 % FOX_T:b0 % FOX_T + blk] = bias.T[:ot_ref.shape[0], :]


def _fox_gate(f, b_f):
    bf_pad = jnp.zeros((1, F_PAD), f32).at[0, :N_MAIN_HEADS].set(b_f.astype(f32))
    return pl.pallas_call(
        _fox_gate_kernel,
        grid=(BATCH,),
        in_specs=[pl.BlockSpec((SEQ, F_PAD), lambda b: (b, 0)),
                  pl.BlockSpec((1, F_PAD), lambda b: (0, 0))],
        out_specs=[pl.BlockSpec((SEQ, LANES), lambda b: (b, 0)),
                   pl.BlockSpec((None, GATE_HEAD_ROWS, SEQ // FOX_T, FOX_T), lambda b: (b, 0, 0, 0))],
        out_shape=[jax.ShapeDtypeStruct((BATCH * SEQ, LANES), bf16),
                   jax.ShapeDtypeStruct((BATCH, GATE_HEAD_ROWS, SEQ // FOX_T, FOX_T), f32)],
        compiler_params=pltpu.CompilerParams(dimension_semantics=("arbitrary",)),
        name="fox_gate",
    )(f, bf_pad)


def _fox_attn_kernel(qt_ref, k_ref, vt_ref, nb_ref, z_ref, nbq_ref, norm_ref, o_ref,
                     m_ref, acc_ref, u_ref, p_ref, a_ref):
    t = FOX_T
    nt = SEQ // t
    head0 = pl.program_id(1) * FOX_HEADS_PER_STEP
    heads = range(FOX_HEADS_PER_STEP)
    col = [slice(hd * HEAD_DIM, (hd + 1) * HEAD_DIM) for hd in heads]
    sub = lax.broadcasted_iota(jnp.int32, (LANES, t), 0)
    sel = []
    for hd in heads:
        lo = BIAS_TERMS * (head0 + hd)
        sel.append(jnp.where((sub >= lo) & (sub < lo + BIAS_TERMS), 1.0, 0.0).astype(bf16))
    ones_rows = jnp.ones((FOX_SUM_ROWS, t), bf16)

    def tile_rows(i):
        if isinstance(i, int):
            return slice(i * t, (i + 1) * t)
        return pl.ds(pl.multiple_of(i * t, t), t)

    def score(qt, kt, hd):
        k_aug = jnp.concatenate([k_ref[tile_rows(kt), col[hd]], nb_ref[tile_rows(kt), :]], axis=1)
        q_aug_t = jnp.concatenate([qt_ref[qt, col[hd], :], sel[hd]], axis=0)
        return jnp.dot(k_aug, q_aug_t, preferred_element_type=f32)

    def scores(qt, kt):
        return [score(qt, kt, hd) for hd in heads]

    def pv1(kt, p_hd, hd):
        return jnp.dot(jnp.concatenate([vt_ref[kt, col[hd], :], ones_rows], axis=0), p_hd,
                       preferred_element_type=f32)

    def pv(kt, p):
        return [pv1(kt, p[hd], hd) for hd in heads]

    def finalize(qt):
        for hd in heads:
            acc = acc_ref[qt, hd]
            out_t = acc[:HEAD_DIM, :] * (1.0 / acc[HEAD_DIM:HEAD_DIM + 1, :])
            gate = _silu_of_half(z_ref[tile_rows(qt), col[hd]].astype(f32))
            o_ref[tile_rows(qt), col[hd]] = (out_t.T * gate).astype(o_ref.dtype)

    def next_item(qt, kt):
        wrap = kt + 1 >= qt
        return jnp.minimum(jnp.where(wrap, qt + 1, qt), nt - 1), jnp.where(wrap, 0, kt + 1)

    def load_list(ref):
        return [ref[hd] for hd in heads]

    def store_list(ref, vals):
        for hd in heads:
            ref[hd] = vals[hd]

    def run_online():
        def numerators(qt, u, masked):
            if masked:
                causal = (lax.broadcasted_iota(jnp.int32, (t, t), 0) <= lax.broadcasted_iota(jnp.int32, (t, t), 1))
            p, alpha = [], []
            for hd in heads:
                m_old = m_ref[qt, hd]
                uh = jnp.where(causal, u[hd], NEG) if masked else u[hd]
                m_new = jnp.maximum(m_old, jnp.max(uh, axis=0, keepdims=True))
                alpha.append(jnp.exp2(m_old - m_new))
                m_ref[qt, hd] = m_new
                p.append(jnp.exp2(uh - m_new).astype(bf16))
            return p, alpha

        def accumulate(qt, alpha, pv_vals):
            for hd in heads:
                acc_ref[qt, hd] = alpha[hd] * acc_ref[qt, hd] + pv_vals[hd]

        acc_ref[...] = jnp.zeros(acc_ref.shape, f32)
        m_ref[...] = jnp.full(m_ref.shape, NEG, f32)
        p_ref[...] = jnp.zeros(p_ref.shape, bf16)
        a_ref[...] = jnp.ones(a_ref.shape, f32)

        n_items = nt * (nt - 1) // 2
        ni = nd = FOX_ITEMS_PER_ITER_ONLINE
        assert n_items % ni == 0 and nt % nd == 0
        store_list(u_ref, scores(1, 0))

        def pass1(_, carry):
            q0, k0, q_prev, k_prev = carry
            items = [(q0, k0)]
            for _k in range(ni):
                items.append(next_item(*items[-1]))
            u_cur = load_list(u_ref)
            pv_vals = pv(k_prev, load_list(p_ref))
            u_next = scores(*items[1])
            p_cur, al_cur = numerators(q0, u_cur, False)
            accumulate(q_prev, load_list(a_ref), pv_vals)
            for k in range(1, ni):
                u_cur, u_next, pv_vals = u_next, [], []
                for hd in heads:
                    s = score(items[k + 1][0], items[k + 1][1], hd)
                    if k == ni - 1:
                        u_ref[hd] = s
                    else:
                        u_next.append(s)
                    pv_vals.append(pv1(items[k - 1][1], p_cur[hd], hd))
                p_new, al_new = numerators(items[k][0], u_cur, False)
                accumulate(items[k - 1][0], al_cur, pv_vals)
                p_cur, al_cur = p_new, al_new
            store_list(p_ref, p_cur)
            store_list(a_ref, al_cur)
            return items[ni] + items[ni - 1]

        one, zero = jnp.int32(1), jnp.int32(0)
        _, _, q_prev, k_prev = lax.fori_loop(0, n_items // ni, pass1, (one, zero, one, zero))
        accumulate(q_prev, load_list(a_ref), pv(k_prev, load_list(p_ref)))

        store_list(u_ref, scores(0, 0))

        def pass2(i, carry):
            u_cur = load_list(u_ref)
            for k in range(nd):
                tile = nd * i + k
                nxt = jnp.minimum(tile + 1, nt - 1)
                u_next = scores(nxt, nxt)
                p_cur, al_cur = numerators(tile, u_cur, True)
                accumulate(tile, al_cur, pv(tile, p_cur))
                finalize(tile)
                u_cur = u_next
            store_list(u_ref, u_cur)
            return carry

        lax.fori_loop(0, nt // nd, pass2, 0)

    norms = jnp.max(norm_ref[...], axis=0)
    hlane = lax.broadcasted_iota(jnp.int32, (1, LANES), 1)
    for hd in heads:
        pick = hlane == head0 + hd
        q2 = jnp.max(jnp.where(pick, norms[0:1, :], 0.0), axis=1, keepdims=True)
        k2 = jnp.max(jnp.where(pick, norms[1:2, :], 0.0), axis=1, keepdims=True)
        qk_bound = jnp.sqrt(q2 * k2) * FOX_BOUND_MARGIN + 1.0
        for qt in range(nt):
            m_ref[qt, hd] = nbq_ref[hd, qt:qt + 1, :] + qk_bound

    def run_fixed_shift():
        half = t // 2
        ksub = lax.broadcasted_iota(jnp.int32, (half, half), 0)
        qlane = lax.broadcasted_iota(jnp.int32, (half, half), 1)
        causal = ksub <= qlane
        lo, hi = slice(0, half), slice(half, t)
        units = [(qt, hd) for qt in range(nt) for hd in heads]

        def unit_scores(qt, hd):
            return [score(qt, kt, hd) for kt in range(qt + 1)]

        def unit_numerators(qt, hd, u):
            m = m_ref[qt, hd]
            pf = [jnp.exp2(u[kt] - m) for kt in range(qt)]
            ud = u[qt]
            p_ll = jnp.exp2(jnp.where(causal, ud[lo, lo], NEG) - m[:, lo])
            p_lh = jnp.exp2(ud[lo, hi] - m[:, hi])
            p_hh = jnp.exp2(jnp.where(causal, ud[hi, hi], NEG) - m[:, hi])
            top = jnp.concatenate([p_ll, p_lh], axis=1)
            bottom = jnp.concatenate([jnp.zeros_like(p_hh), p_hh], axis=1)
            pf.append(jnp.concatenate([top, bottom], axis=0))
            partial = sum(blk.reshape(t // 8, 8, t).sum(axis=0) for blk in pf)
            return [blk.astype(bf16) for blk in pf], jnp.sum(partial, axis=0, keepdims=True)

        def unit_output(qt, hd, p, denom):
            vt_all = jnp.concatenate([vt_ref[kt, col[hd], :] for kt in range(qt + 1)], axis=1)
            acc = jnp.dot(vt_all, jnp.concatenate(p, axis=0), preferred_element_type=f32)
            out_t = acc * (1.0 / denom)
            gate = _silu_of_half(z_ref[qt * t:(qt + 1) * t, col[hd]].astype(f32))
            o_ref[qt * t:(qt + 1) * t, col[hd]] = (out_t.T * gate).astype(o_ref.dtype)

        denom_min = None
        u_next = unit_scores(*units[0])
        for w, (qt, hd) in enumerate(units):
            u_cur = u_next
            if w + 1 < len(units):
                u_next = unit_scores(*units[w + 1])
            p, denom = unit_numerators(qt, hd, u_cur)
            unit_output(qt, hd, p, denom)
            denom_min = denom if denom_min is None else jnp.minimum(denom_min, denom)
        return jnp.min(denom_min)

    denom_min = run_fixed_shift()

    @pl.when(jnp.logical_not(denom_min >= FOX_MIN_DENOM))
    def _():
        run_online()


def _fox_attention(q, k, vt, nb, z, nbq, norms):
    hw = FOX_HEADS_PER_STEP * HEAD_DIM
    return pl.pallas_call(
        _fox_attn_kernel,
        grid=(BATCH, N_MAIN_HEADS // FOX_HEADS_PER_STEP),
        in_specs=[
            pl.BlockSpec((None, SEQ // FOX_T, hw, FOX_T), lambda b, p: (b, 0, p, 0)),
            pl.BlockSpec((SEQ, hw), lambda b, p: (b, p)),
            pl.BlockSpec((None, SEQ // FOX_T, hw, FOX_T), lambda b, p: (b, 0, p, 0)),
            pl.BlockSpec((SEQ, LANES), lambda b, p: (b, 0)),
            pl.BlockSpec((SEQ, hw), lambda b, p: (b, p)),
            pl.BlockSpec((None, FOX_HEADS_PER_STEP, SEQ // FOX_T, FOX_T), lambda b, p: (b, p, 0, 0)),
            pl.BlockSpec((SEQ // FOX_PROJ_TM, 8, LANES), lambda b, p: (b, 0, 0)),
        ],
        out_specs=pl.BlockSpec((SEQ, hw), lambda b, p: (b, p)),
        out_shape=jax.ShapeDtypeStruct((BATCH * SEQ, MAIN_W), bf16),
        scratch_shapes=[
            pltpu.VMEM((SEQ // FOX_T, FOX_HEADS_PER_STEP, 1, FOX_T), f32),
            pltpu.VMEM((SEQ // FOX_T, FOX_HEADS_PER_STEP, HEAD_DIM + FOX_SUM_ROWS, FOX_T), f32),
            pltpu.VMEM((FOX_HEADS_PER_STEP, FOX_T, FOX_T), f32),
            pltpu.VMEM((FOX_HEADS_PER_STEP, FOX_T, FOX_T), bf16),
            pltpu.VMEM((FOX_HEADS_PER_STEP, 1, FOX_T), f32),
        ],
        compiler_params=pltpu.CompilerParams(dimension_semantics=("arbitrary", "arbitrary"),
                                             vmem_limit_bytes=VMEM_LIMIT),
        name="fox_attention",
    )(q, k, vt, nb, z, nbq, norms)


def _retention_kernel(q_ref, kt_ref, v_ref, z_ref, d_ref, eps_ref, zeta_ref, g_ref, o_ref, r_ref):
    c_len = RET_CHUNK
    r_ref[...] = jnp.zeros(r_ref.shape, f32)
    lane = lax.broadcasted_iota(jnp.int32, (c_len, LANES), 1)
    sub = lax.broadcasted_iota(jnp.int32, (LANES, c_len), 0)
    q_mask = (lane < RET_KEY_DIM, lane >= RET_KEY_DIM)
    k_mask = (sub < RET_KEY_DIM, sub >= RET_KEY_DIM)
    pairs = range(RET_PAIRS_PER_STEP)
    chunks = range(RET_CHUNKS_PER_ITER)

    def body(it, carry):
        chunk = [it * RET_CHUNKS_PER_ITER + ci for ci in chunks]
        rows = [pl.ds(pl.multiple_of(chunk[ci] * c_len, c_len), c_len) for ci in chunks]
        q2, s, upd = {}, {}, {}
        for ci in chunks:
            for pr in pairs:
                q2[ci, pr] = q_ref[rows[ci], pr * LANES:(pr + 1) * LANES]
                kt2 = kt_ref[chunk[ci], pr * LANES:(pr + 1) * LANES, :]
                v2 = v_ref[rows[ci], pr * 2 * HEAD_DIM:(pr + 1) * 2 * HEAD_DIM]
                k_heads = jnp.concatenate([jnp.where(k_mask[hd], kt2, jnp.zeros_like(kt2))
                                           for hd in range(2)], axis=1)
                s[ci, pr] = jnp.dot(q2[ci, pr], k_heads, preferred_element_type=f32)
                kz = (kt2.astype(f32) * zeta_ref[pr]).astype(bf16)
                upd[ci, pr] = jnp.dot(kz, v2, preferred_element_type=f32)
        r_b = {}
        for pr in pairs:
            r_state = r_ref[pr]
            for ci in chunks:
                r_b[ci, pr] = r_state.astype(bf16)
                r_state = r_state * g_ref[pr] + upd[ci, pr]
            r_ref[pr] = r_state
        o = {}
        for ci in chunks:
            for pr in pairs:
                inner = (s[ci, pr] * d_ref[pr]).astype(bf16)
                for hd in range(2):
                    cols = slice((2 * pr + hd) * HEAD_DIM, (2 * pr + hd + 1) * HEAD_DIM)
                    q_hd = jnp.where(q_mask[hd], q2[ci, pr], jnp.zeros_like(q2[ci, pr]))
                    lhs = jnp.concatenate([inner[:, hd * c_len:(hd + 1) * c_len], q_hd], axis=1)
                    rhs = jnp.concatenate([v_ref[rows[ci], cols],
                                           r_b[ci, pr][:, hd * HEAD_DIM:(hd + 1) * HEAD_DIM]], axis=0)
                    o[ci, pr, hd] = jnp.dot(lhs, rhs, preferred_element_type=f32)
        for ci in chunks:
            for pr in pairs:
                for hd in range(2):
                    cols = slice((2 * pr + hd) * HEAD_DIM, (2 * pr + hd + 1) * HEAD_DIM)
                    oh = o[ci, pr, hd]
                    ms = jnp.mean(oh * oh, axis=-1, keepdims=True)
                    on = oh * lax.rsqrt(ms + eps_ref[2 * pr + hd])
                    zg = _silu_of_half(z_ref[rows[ci], cols].astype(f32))
                    o_ref[rows[ci], cols] = (on * zg).astype(o_ref.dtype)
        return carry

    lax.fori_loop(0, SEQ // (c_len * RET_CHUNKS_PER_ITER), body, 0)


def _retention_tables():
    h, c_len = N_MAIN_HEADS, RET_CHUNK
    lg = np.log1p(-np.exp2(-5.0 - np.arange(h, dtype=np.float64)))
    n = np.arange(c_len, dtype=np.float64)
    causal = n[:, None] >= n[None, :]
    d_col = np.where(causal[None], np.exp(-lg[:, None, None] * (n[None, None, :] + 1.0)), 0.0)
    d_pair = d_col.reshape(h // 2, 2, c_len, c_len).transpose(0, 2, 1, 3).reshape(h // 2, c_len, 2 * c_len)
    xi = np.exp(lg[:, None] * (n[None, :] + 1.0))
    zeta = np.exp(lg[:, None] * (c_len - 1.0 - n[None, :]))
    g_chunk = np.exp(lg * c_len)
    eps_rows = np.broadcast_to((EPS / (xi * xi))[:, :, None], (h, c_len, HEAD_DIM))
    zeta_t = np.repeat(zeta.reshape(h // 2, 2, c_len), RET_KEY_DIM, axis=1)
    g2 = np.broadcast_to(np.repeat(g_chunk.reshape(h // 2, 2), RET_KEY_DIM, axis=-1)[:, :, None],
                         (h // 2, 2 * RET_KEY_DIM, 2 * HEAD_DIM))
    return tuple(jnp.asarray(np.ascontiguousarray(t), dtype=f32) for t in (d_pair, eps_rows, zeta_t, g2))


def _retention(q, kt, v, z):
    n_pairs = N_MAIN_HEADS // 2
    pp = RET_PAIRS_PER_STEP
    steps = n_pairs // pp
    d_pair, eps_rows, zeta_t, g2 = _retention_tables()
    c_len = RET_CHUNK
    return pl.pallas_call(
        _retention_kernel,
        grid=(BATCH, steps),
        in_specs=[
            pl.BlockSpec((SEQ, pp * LANES), lambda b, p: (b, p)),
            pl.BlockSpec((None, SEQ // c_len, pp * LANES, c_len), lambda b, p: (b, 0, p, 0)),
            pl.BlockSpec((SEQ, pp * 2 * HEAD_DIM), lambda b, p: (b, p)),
            pl.BlockSpec((SEQ, pp * 2 * HEAD_DIM), lambda b, p: (b, p)),
            pl.BlockSpec((pp, c_len, 2 * c_len), lambda b, p: (p, 0, 0)),
            pl.BlockSpec((2 * pp, c_len, HEAD_DIM), lambda b, p: (p, 0, 0)),
            pl.BlockSpec((pp, LANES, c_len), lambda b, p: (p, 0, 0)),
            pl.BlockSpec((pp, LANES, 2 * HEAD_DIM), lambda b, p: (p, 0, 0)),
        ],
        out_specs=pl.BlockSpec((SEQ, pp * 2 * HEAD_DIM), lambda b, p: (b, p)),
        out_shape=jax.ShapeDtypeStruct((BATCH * SEQ, MAIN_W), bf16),
        scratch_shapes=[pltpu.VMEM((pp, LANES, 2 * HEAD_DIM), f32)],
        compiler_params=pltpu.CompilerParams(dimension_semantics=("arbitrary", "arbitrary"),
                                             vmem_limit_bytes=VMEM_LIMIT),
        name="retention",
    )(q, kt, v, z, d_pair, eps_rows, zeta_t, g2)


def _out_proj_kernel(om_ref, qm_ref, zm_ref, kv_ref, w32_ref, x_ref, g_ref, o_ref, w_ref, *, final_norm):
    @pl.when(pl.program_id(0) == 0)
    def _():
        for r0 in range(0, INNER, WEIGHT_CAST_ROWS):
            rows = slice(r0, r0 + WEIGHT_CAST_ROWS)
            w_ref[rows, :] = w32_ref[rows, :].astype(bf16)

    scale = 1.0 / math.sqrt(HEAD_DIM)
    heads = range(N_MEM_HEADS)
    col = [slice(hd * HEAD_DIM, (hd + 1) * HEAD_DIM) for hd in heads]
    half = D_MODEL // 2
    for r0 in range(0, OUT_TM, OUT_TM // 2):
        rows = slice(r0, r0 + OUT_TM // 2)
        s = [lax.dot_general(qm_ref[rows, col[hd]], kv_ref[:, col[hd]], (((1,), (1,)), ((), ())),
                             preferred_element_type=f32) * scale for hd in heads]
        y_lo = jnp.dot(om_ref[rows, :], w_ref[:MAIN_W, :half], preferred_element_type=f32)
        p = []
        for hd in heads:
            e = jnp.exp(s[hd] - jnp.max(s[hd], axis=-1, keepdims=True))
            p.append((e * (1.0 / jnp.sum(e, axis=-1, keepdims=True))).astype(bf16))
        memo = [jnp.dot(p[hd], kv_ref[:, MEM_W + hd * HEAD_DIM:MEM_W + (hd + 1) * HEAD_DIM],
                        preferred_element_type=f32) for hd in heads]
        y_hi = jnp.dot(om_ref[rows, :], w_ref[:MAIN_W, half:], preferred_element_type=f32)
        og = jnp.concatenate([(memo[hd] * _silu_of_half(zm_ref[rows, col[hd]].astype(f32))).astype(bf16)
                              for hd in heads], axis=1)
        y = jnp.concatenate([y_lo, y_hi], axis=1) + jnp.dot(og, w_ref[MAIN_W:, :], preferred_element_type=f32)
        xn = x_ref[rows, :] + y
        if final_norm:
            xn = _rmsnorm_rows(xn, g_ref[...])
        o_ref[rows, :] = xn


def _out_proj(o_main, qm, z, kv, w_out, layer, x, g_final, *, final_norm):
    t_rows = x.shape[0]
    tiles_per_seq = SEQ // OUT_TM
    z_blk = MAIN_W // MEM_W
    return pl.pallas_call(
        functools.partial(_out_proj_kernel, final_norm=final_norm),
        grid=(t_rows // OUT_TM,),
        in_specs=[
            pl.BlockSpec((OUT_TM, MAIN_W), lambda i: (i, 0)),
            pl.BlockSpec((OUT_TM, MEM_W), lambda i: (i, 0)),
            pl.BlockSpec((OUT_TM, MEM_W), lambda i: (i, z_blk)),
            pl.BlockSpec((N_MEM, 2 * MEM_W), lambda i: (i // tiles_per_seq, 0)),
            pl.BlockSpec((None, INNER, D_MODEL), lambda i: (layer, 0, 0), pipeline_mode=pl.Buffered(1)),
            pl.BlockSpec((OUT_TM, D_MODEL), lambda i: (i, 0)),
            pl.BlockSpec((1, D_MODEL), lambda i: (0, 0)),
        ],
        out_specs=pl.BlockSpec((OUT_TM, D_MODEL), lambda i: (i, 0)),
        out_shape=jax.ShapeDtypeStruct((t_rows, D_MODEL), f32),
        scratch_shapes=[pltpu.VMEM((INNER, D_MODEL), bf16)],
        compiler_params=pltpu.CompilerParams(dimension_semantics=("arbitrary",),
                                             vmem_limit_bytes=VMEM_LIMIT),
        name="out_proj_final" if final_norm else "out_proj",
    )(o_main, qm, z, kv, w_out, x, g_final.reshape(1, D_MODEL))


def _rotary_tables():
    half = RET_KEY_DIM // 2
    pos = np.arange(SEQ, dtype=np.float64)
    inv = 1.0 / (ROPE_BASE ** (np.arange(half, dtype=np.float64) / half))
    ang = pos[:, None] * inv[None, :]
    cos, sin = np.cos(ang), np.sin(ang)
    reps = LANES // RET_KEY_DIM
    cos_t = np.tile(np.concatenate([cos, cos], axis=-1), (1, reps))
    sin_t = np.tile(np.concatenate([-sin, sin], axis=-1), (1, reps))
    k_scale = RET_KEY_DIM ** -0.5
    return tuple(jnp.asarray(t, dtype=f32) for t in (cos_t, sin_t, cos_t * k_scale, sin_t * k_scale))


def kernel(x, mem, norm_g, fox_w_in, fox_b_f, ret_w_in, mem_norm_g, w_mem_kv, w_out, final_norm_g):
    t_rows = BATCH * SEQ
    x2 = x.reshape(t_rows, D_MODEL)

    kv0, kv1 = _norm_proj(mem.reshape(BATCH * N_MEM, D_MODEL), mem_norm_g,
                          [(w_mem_kv, 0), (w_mem_kv, 1)],
                          [2 * MEM_W, 2 * MEM_W], [bf16, bf16], name="mem_kv_proj")

    o_f = 3 * MAIN_W
    o_qm = o_f + N_MAIN_HEADS
    w0t = jnp.swapaxes(fox_w_in[0], 0, 1).astype(bf16)
    w_qmz = w0t[o_qm:]
    w_f = jnp.pad(w0t[o_f:o_qm], ((0, F_PAD - N_MAIN_HEADS), (0, 0)))
    q, k, vt, qm, z, f, norms = _norm_proj(
        x2, norm_g[0], [(w0t, o_f), w_qmz, w_f], [MAIN_W, MAIN_W, MAIN_W, MEM_W, INNER, F_PAD],
        [bf16, bf16, bf16, bf16, bf16, f32], ["fox_q", "fox_k", "kt", "plain", "half", "plain"],
        w_transposed=True, tm=FOX_PROJ_TM, name="fox_in_proj")
    nb, nbq = _fox_gate(f, fox_b_f[0])
    o_main = _fox_attention(q, k, vt, nb, z, nbq, norms)
    x2 = _out_proj(o_main, qm, z, kv0, w_out, 0, x2, final_norm_g, final_norm=False)

    q, kt, v, qm, z = _norm_proj(
        x2, norm_g[1], [(ret_w_in, 0)], [RET_QK_W, RET_QK_W, MAIN_W, MEM_W, INNER], [bf16] * 5,
        ["rot_q", "rot_kt", "plain", "plain", "half"], tables=_rotary_tables(), tm=RET_PROJ_TM,
        name="ret_in_proj")
    o_main = _retention(q, kt, v, z)
    out = _out_proj(o_main, qm, z, kv1, w_out, 1, x2, final_norm_g, final_norm=True)
    return out.reshape(BATCH, SEQ, D_MODEL)
```

```python
import functools
import math

import jax
import jax.numpy as jnp
import numpy as np
from jax import lax
from jax.experimental import pallas as pl
from jax.experimental.pallas import tpu as pltpu

D_MODEL = 1024
BATCH = 8
SEQ = 2048
HEAD_DIM = 128
N_MAIN_HEADS = 12
N_MEM_HEADS = 4
N_MEM = 256
MAIN_W = N_MAIN_HEADS * HEAD_DIM
MEM_W = N_MEM_HEADS * HEAD_DIM
INNER = MAIN_W + MEM_W
RET_KEY_DIM = HEAD_DIM // 2
RET_QK_W = N_MAIN_HEADS * RET_KEY_DIM
RET_CHUNK = 128
ROPE_BASE = 10000.0
EPS = 1e-6
NEG = -1e30
LOG2E = 1.4426950408889634

LANES = 128
F_PAD = LANES
VMEM_LIMIT = 56 * 1024 * 1024
PROJ_VMEM_LIMIT = 60 * 1024 * 1024

PROJ_TM = 512
PROJ_TN = 512
RET_PROJ_TM = 512
FOX_PROJ_TM = 1024
WEIGHT_CAST_ROWS = 128
FOX_T = 256
FOX_HEADS_PER_STEP = 6
FOX_ITEMS_PER_ITER_ONLINE = 2
BIAS_TERMS = 3
FOX_SUM_ROWS = 16
FOX_Q_SCALE = LOG2E / math.sqrt(HEAD_DIM)
FOX_BOUND_MARGIN = 1.02
FOX_MIN_DENOM = 2.0 ** -90
GATE_HEAD_ROWS = 16
RET_PAIRS_PER_STEP = 3
RET_CHUNKS_PER_ITER = 2
OUT_TM = 1024

f32 = jnp.float32
bf16 = jnp.bfloat16


def _silu_of_half(h):
    return h + h * jnp.tanh(h)


def _rmsnorm_rows(x, g):
    ms = jnp.mean(x * x, axis=-1, keepdims=True)
    return (x * lax.rsqrt(ms + EPS)) * g


def _split_bf16(v):
    hi = v.astype(bf16)
    r1 = v - hi.astype(f32)
    mid = r1.astype(bf16)
    lo = (r1 - mid.astype(f32)).astype(bf16)
    return hi, mid, lo


def _norm_proj_kernel(*refs, n_weights, w_transposed, cast_weight, out_widths, out_kinds):
    track_norms = "fox_q" in out_kinds
    n_out = len(out_widths) + (1 if track_norms else 0)
    if cast_weight:
        w_copy_refs = refs[-n_weights:]
        refs = refs[:-n_weights]
    x_ref, g_ref = refs[:2]
    w_refs = refs[2:2 + n_weights]
    tab_refs = refs[2 + n_weights:len(refs) - n_out]
    out_refs = refs[len(refs) - n_out:]
    if cast_weight:
        @pl.when(pl.program_id(0) == 0)
        def _():
            for w32_ref, w_copy_ref in zip(w_refs, w_copy_refs):
                for r0 in range(0, w_copy_ref.shape[0], WEIGHT_CAST_ROWS):
                    rows = slice(r0, r0 + WEIGHT_CAST_ROWS)
                    w_copy_ref[rows, :] = w32_ref[rows, :].astype(bf16)
        w_refs = list(w_copy_refs)
    h = _rmsnorm_rows(x_ref[...], g_ref[...]).astype(bf16)
    tm = h.shape[0]
    if "rot_q" in out_kinds:
        lane = lax.broadcasted_iota(jnp.int32, (tm, LANES), 1)
        first_half = (lane % RET_KEY_DIM) < (RET_KEY_DIM // 2)
    if track_norms:
        norm_sub = lax.broadcasted_iota(jnp.int32, (8, LANES), 0)
        norm_lane = lax.broadcasted_iota(jnp.int32, (8, LANES), 1)
        norms = jnp.zeros((8, LANES), f32)

        def with_norms(norms, y, row, head0):
            for s0 in range(0, y.shape[1], HEAD_DIM):
                ys = y[:, s0:s0 + HEAD_DIM]
                worst = jnp.max(jnp.sum(ys * ys, axis=1, keepdims=True), axis=0, keepdims=True)
                norms = jnp.where((norm_sub == row) & (norm_lane == head0 + s0 // HEAD_DIM), worst, norms)
            return norms
    out_axis = 0 if w_transposed else 1
    wi, col = 0, 0
    for o_ref, width, kind in zip(out_refs, out_widths, out_kinds):
        if col == w_refs[wi].shape[out_axis]:
            wi, col = wi + 1, 0
        w_ref = w_refs[wi]
        for c0 in range(0, width, PROJ_TN):
            cw = min(PROJ_TN, width - c0)
            if w_transposed:
                y = lax.dot_general(h, w_ref[col + c0:col + c0 + cw, :], (((1,), (1,)), ((), ())),
                                    preferred_element_type=f32)
            else:
                y = jnp.dot(h, w_ref[:, col + c0:col + c0 + cw], preferred_element_type=f32)
            if kind in ("rot_q", "rot_kt"):
                cos_ref, sin_ref = tab_refs[:2] if kind == "rot_q" else tab_refs[2:]
                for s0 in range(0, cw, LANES):
                    ys = y[:, s0:s0 + LANES]
                    sw = jnp.where(first_half, pltpu.roll(ys, LANES - RET_KEY_DIM // 2, 1),
                                   pltpu.roll(ys, RET_KEY_DIM // 2, 1))
                    r = ys * cos_ref[...] + sw * sin_ref[...]
                    if kind == "rot_q":
                        o_ref[:, c0 + s0:c0 + s0 + LANES] = r.astype(o_ref.dtype)
                    else:
                        for r0 in range(tm // RET_CHUNK):
                            piece = r[r0 * RET_CHUNK:(r0 + 1) * RET_CHUNK, :]
                            o_ref[r0, c0 + s0:c0 + s0 + LANES, :] = piece.T.astype(o_ref.dtype)
            elif kind in ("kt", "fox_q"):
                if kind == "fox_q":
                    y = y * FOX_Q_SCALE
                    norms = with_norms(norms, y, 0, c0 // HEAD_DIM)
                for r0 in range(tm // FOX_T):
                    for s0 in range(0, cw, LANES):
                        piece = y[r0 * FOX_T:(r0 + 1) * FOX_T, s0:s0 + LANES]
                        o_ref[r0, c0 + s0:c0 + s0 + LANES, :] = piece.T.astype(o_ref.dtype)
            elif kind == "fox_k":
                norms = with_norms(norms, y, 1, c0 // HEAD_DIM)
                o_ref[:, c0:c0 + cw] = y.astype(o_ref.dtype)
            elif kind == "half":
                o_ref[:, c0:c0 + cw] = (y * 0.5).astype(o_ref.dtype)
            else:
                o_ref[:, c0:c0 + cw] = y.astype(o_ref.dtype)
        col += width
    if track_norms:
        out_refs[-1][...] = norms


def _norm_proj(x, g, weights, out_widths, out_dtypes, out_kinds=None, *, w_transposed=False, tables=None,
               tm=PROJ_TM, name):
    t_rows, d = x.shape
    out_kinds = tuple(out_kinds or ["plain"] * len(out_widths))
    cast_weight = isinstance(weights[0], tuple) and weights[0][0].ndim == 3
    w_arrays = [w[0] if isinstance(w, tuple) else w for w in weights]
    if cast_weight:
        w_blocks = [w[0].shape[-2:] for w in weights]
    else:
        w_blocks = [(w[1], d) if isinstance(w, tuple) else w.shape for w in weights]
    out_axis = 0 if w_transposed else 1
    assert sum(out_widths) == sum(blk[out_axis] for blk in w_blocks) and t_rows % tm == 0
    tiles_per_seq = SEQ // tm
    in_specs = [
        pl.BlockSpec((tm, d), lambda i: (i, 0)),
        pl.BlockSpec((1, d), lambda i: (0, 0)),
    ]
    scratch = []
    if cast_weight:
        assert not w_transposed
        for (_, layer), blk in zip(weights, w_blocks):
            in_specs.append(pl.BlockSpec((None,) + blk, lambda i, layer=layer: (layer, 0, 0),
                                         pipeline_mode=pl.Buffered(1)))
            scratch.append(pltpu.VMEM(blk, bf16))
    else:
        in_specs += [pl.BlockSpec(blk, lambda i: (0, 0), pipeline_mode=pl.Buffered(1)) for blk in w_blocks]
    args = [x, g.reshape(1, d), *w_arrays]
    if "rot_q" in out_kinds:
        for tab in tables:
            in_specs.append(pl.BlockSpec((tm, LANES), lambda i: (i % tiles_per_seq, 0)))
            args.append(tab)
    out_specs, out_shape = [], []
    for wd, dt, kind in zip(out_widths, out_dtypes, out_kinds):
        if kind in ("kt", "fox_q", "rot_kt"):
            tile = RET_CHUNK if kind == "rot_kt" else FOX_T
            out_specs.append(pl.BlockSpec((None, tm // tile, wd, tile),
                                          lambda i: (i // tiles_per_seq, i % tiles_per_seq, 0, 0)))
            out_shape.append(jax.ShapeDtypeStruct((t_rows // SEQ, SEQ // tile, wd, tile), dt))
        else:
            out_specs.append(pl.BlockSpec((tm, wd), lambda i: (i, 0)))
            out_shape.append(jax.ShapeDtypeStruct((t_rows, wd), dt))
    if "fox_q" in out_kinds:
        out_specs.append(pl.BlockSpec((None, 8, LANES), lambda i: (i, 0, 0)))
        out_shape.append(jax.ShapeDtypeStruct((t_rows // tm, 8, LANES), f32))
    return pl.pallas_call(
        functools.partial(_norm_proj_kernel, n_weights=len(weights), w_transposed=w_transposed,
                          cast_weight=cast_weight, out_widths=tuple(out_widths), out_kinds=out_kinds),
        grid=(t_rows // tm,),
        in_specs=in_specs,
        out_specs=out_specs,
        out_shape=out_shape,
        scratch_shapes=scratch,
        compiler_params=pltpu.CompilerParams(dimension_semantics=("arbitrary",),
                                             vmem_limit_bytes=PROJ_VMEM_LIMIT),
        name=name,
    )(*args)


def _fox_gate_kernel(f_ref, bf_ref, o_ref, ot_ref):
    blk = LANES
    row = lax.broadcasted_iota(jnp.int32, (blk, blk), 0)
    coli = lax.broadcasted_iota(jnp.int32, (blk, blk), 1)
    tri = jnp.where(row >= coli, 1.0, 0.0).astype(bf16)
    spread = [jnp.where((coli == BIAS_TERMS * row + t) & (row < N_MAIN_HEADS), 1.0, 0.0).astype(bf16)
              for t in range(BIAS_TERMS)]
    carry = jnp.zeros((1, F_PAD), f32)
    for b0 in range(0, SEQ, blk):
        xg = f_ref[b0:b0 + blk, :] + bf_ref[...]
        ls = jnp.minimum(xg, 0.0) - jnp.log1p(jnp.exp(-jnp.abs(xg)))
        cs = sum(jnp.dot(tri, term, preferred_element_type=f32) for term in _split_bf16(ls)) + carry
        carry = cs[blk - 1:blk, :]
        bias = cs * (-LOG2E)
        nb = sum(jnp.dot(term, sp, preferred_element_type=f32)
                 for term, sp in zip(_split_bf16(bias), spread))
        o_ref[b0:b0 + blk, :] = nb.astype(o_ref.dtype)
        ot_ref[:, b0 // FOX_T, b0 % FOX_T:b0 % FOX_T + blk] = bias.T[:ot_ref.shape[0], :]


def _fox_gate(f, b_f):
    bf_pad = jnp.zeros((1, F_PAD), f32).at[0, :N_MAIN_HEADS].set(b_f.astype(f32))
    return pl.pallas_call(
        _fox_gate_kernel,
        grid=(BATCH,),
        in_specs=[pl.BlockSpec((SEQ, F_PAD), lambda b: (b, 0)),
                  pl.BlockSpec((1, F_PAD), lambda b: (0, 0))],
        out_specs=[pl.BlockSpec((SEQ, LANES), lambda b: (b, 0)),
                   pl.BlockSpec((None, GATE_HEAD_ROWS, SEQ // FOX_T, FOX_T), lambda b: (b, 0, 0, 0))],
        out_shape=[jax.ShapeDtypeStruct((BATCH * SEQ, LANES), bf16),
                   jax.ShapeDtypeStruct((BATCH, GATE_HEAD_ROWS, SEQ // FOX_T, FOX_T), f32)],
        compiler_params=pltpu.CompilerParams(dimension_semantics=("arbitrary",)),
        name="fox_gate",
    )(f, bf_pad)


def _fox_attn_kernel(qt_ref, k_ref, vt_ref, nb_ref, z_ref, nbq_ref, norm_ref, o_ref,
                     m_ref, acc_ref, u_ref, p_ref, a_ref):
    t = FOX_T
    nt = SEQ // t
    head0 = pl.program_id(1) * FOX_HEADS_PER_STEP
    heads = range(FOX_HEADS_PER_STEP)
    col = [slice(hd * HEAD_DIM, (hd + 1) * HEAD_DIM) for hd in heads]
    sub = lax.broadcasted_iota(jnp.int32, (LANES, t), 0)
    sel = []
    for hd in heads:
        lo = BIAS_TERMS * (head0 + hd)
        sel.append(jnp.where((sub >= lo) & (sub < lo + BIAS_TERMS), 1.0, 0.0).astype(bf16))
    ones_rows = jnp.ones((FOX_SUM_ROWS, t), bf16)

    def tile_rows(i):
        if isinstance(i, int):
            return slice(i * t, (i + 1) * t)
        return pl.ds(pl.multiple_of(i * t, t), t)

    def score(qt, kt, hd):
        k_aug = jnp.concatenate([k_ref[tile_rows(kt), col[hd]], nb_ref[tile_rows(kt), :]], axis=1)
        q_aug_t = jnp.concatenate([qt_ref[qt, col[hd], :], sel[hd]], axis=0)
        return jnp.dot(k_aug, q_aug_t, preferred_element_type=f32)

    def scores(qt, kt):
        return [score(qt, kt, hd) for hd in heads]

    def pv1(kt, p_hd, hd):
        return jnp.dot(jnp.concatenate([vt_ref[kt, col[hd], :], ones_rows], axis=0), p_hd,
                       preferred_element_type=f32)

    def pv(kt, p):
        return [pv1(kt, p[hd], hd) for hd in heads]

    def finalize(qt):
        for hd in heads:
            acc = acc_ref[qt, hd]
            out_t = acc[:HEAD_DIM, :] * (1.0 / acc[HEAD_DIM:HEAD_DIM + 1, :])
            gate = _silu_of_half(z_ref[tile_rows(qt), col[hd]].astype(f32))
            o_ref[tile_rows(qt), col[hd]] = (out_t.T * gate).astype(o_ref.dtype)

    def next_item(qt, kt):
        wrap = kt + 1 >= qt
        return jnp.minimum(jnp.where(wrap, qt + 1, qt), nt - 1), jnp.where(wrap, 0, kt + 1)

    def load_list(ref):
        return [ref[hd] for hd in heads]

    def store_list(ref, vals):
        for hd in heads:
            ref[hd] = vals[hd]

    def run_online():
        def numerators(qt, u, masked):
            if masked:
                causal = (lax.broadcasted_iota(jnp.int32, (t, t), 0) <= lax.broadcasted_iota(jnp.int32, (t, t), 1))
            p, alpha = [], []
            for hd in heads:
                m_old = m_ref[qt, hd]
                uh = jnp.where(causal, u[hd], NEG) if masked else u[hd]
                m_new = jnp.maximum(m_old, jnp.max(uh, axis=0, keepdims=True))
                alpha.append(jnp.exp2(m_old - m_new))
                m_ref[qt, hd] = m_new
                p.append(jnp.exp2(uh - m_new).astype(bf16))
            return p, alpha

        def accumulate(qt, alpha, pv_vals):
            for hd in heads:
                acc_ref[qt, hd] = alpha[hd] * acc_ref[qt, hd] + pv_vals[hd]

        acc_ref[...] = jnp.zeros(acc_ref.shape, f32)
        m_ref[...] = jnp.full(m_ref.shape, NEG, f32)
        p_ref[...] = jnp.zeros(p_ref.shape, bf16)
        a_ref[...] = jnp.ones(a_ref.shape, f32)

        n_items = nt * (nt - 1) // 2
        ni = nd = FOX_ITEMS_PER_ITER_ONLINE
        assert n_items % ni == 0 and nt % nd == 0
        store_list(u_ref, scores(1, 0))

        def pass1(_, carry):
            q0, k0, q_prev, k_prev = carry
            items = [(q0, k0)]
            for _k in range(ni):
                items.append(next_item(*items[-1]))
            u_cur = load_list(u_ref)
            pv_vals = pv(k_prev, load_list(p_ref))
            u_next = scores(*items[1])
            p_cur, al_cur = numerators(q0, u_cur, False)
            accumulate(q_prev, load_list(a_ref), pv_vals)
            for k in range(1, ni):
                u_cur, u_next, pv_vals = u_next, [], []
                for hd in heads:
                    s = score(items[k + 1][0], items[k + 1][1], hd)
                    if k == ni - 1:
                        u_ref[hd] = s
                    else:
                        u_next.append(s)
                    pv_vals.append(pv1(items[k - 1][1], p_cur[hd], hd))
                p_new, al_new = numerators(items[k][0], u_cur, False)
                accumulate(items[k - 1][0], al_cur, pv_vals)
                p_cur, al_cur = p_new, al_new
            store_list(p_ref, p_cur)
            store_list(a_ref, al_cur)
            return items[ni] + items[ni - 1]

        one, zero = jnp.int32(1), jnp.int32(0)
        _, _, q_prev, k_prev = lax.fori_loop(0, n_items // ni, pass1, (one, zero, one, zero))
        accumulate(q_prev, load_list(a_ref), pv(k_prev, load_list(p_ref)))

        store_list(u_ref, scores(0, 0))

        def pass2(i, carry):
            u_cur = load_list(u_ref)
            for k in range(nd):
                tile = nd * i + k
                nxt = jnp.minimum(tile + 1, nt - 1)
                u_next = scores(nxt, nxt)
                p_cur, al_cur = numerators(tile, u_cur, True)
                accumulate(tile, al_cur, pv(tile, p_cur))
                finalize(tile)
                u_cur = u_next
            store_list(u_ref, u_cur)
            return carry

        lax.fori_loop(0, nt // nd, pass2, 0)

    norms = jnp.max(norm_ref[...], axis=0)
    hlane = lax.broadcasted_iota(jnp.int32, (1, LANES), 1)
    for hd in heads:
        pick = hlane == head0 + hd
        q2 = jnp.max(jnp.where(pick, norms[0:1, :], 0.0), axis=1, keepdims=True)
        k2 = jnp.max(jnp.where(pick, norms[1:2, :], 0.0), axis=1, keepdims=True)
        qk_bound = jnp.sqrt(q2 * k2) * FOX_BOUND_MARGIN + 1.0
        for qt in range(nt):
            m_ref[qt, hd] = nbq_ref[hd, qt:qt + 1, :] + qk_bound

    def run_fixed_shift():
        half = t // 2
        ksub = lax.broadcasted_iota(jnp.int32, (half, half), 0)
        qlane = lax.broadcasted_iota(jnp.int32, (half, half), 1)
        causal = ksub <= qlane
        lo, hi = slice(0, half), slice(half, t)
        units = [(qt, hd) for qt in range(nt) for hd in heads]

        def unit_scores(qt, hd):
            n_keys = (qt + 1) * t
            k_aug = jnp.concatenate([k_ref[0:n_keys, col[hd]], nb_ref[0:n_keys, :]], axis=1)
            q_aug_t = jnp.concatenate([qt_ref[qt, col[hd], :], sel[hd]], axis=0)
            u_all = jnp.dot(k_aug, q_aug_t, preferred_element_type=f32)
            return [u_all[kt * t:(kt + 1) * t, :] for kt in range(qt + 1)]

        def unit_numerators(qt, hd, u):
            m = m_ref[qt, hd]
            pf = [jnp.exp2(u[kt] - m) for kt in range(qt)]
            ud = u[qt]
            p_ll = jnp.exp2(jnp.where(causal, ud[lo, lo], NEG) - m[:, lo])
            p_lh = jnp.exp2(ud[lo, hi] - m[:, hi])
            p_hh = jnp.exp2(jnp.where(causal, ud[hi, hi], NEG) - m[:, hi])
            top = jnp.concatenate([p_ll, p_lh], axis=1)
            bottom = jnp.concatenate([jnp.zeros_like(p_hh), p_hh], axis=1)
            pf.append(jnp.concatenate([top, bottom], axis=0))
            partial = sum(blk.reshape(t // 8, 8, t).sum(axis=0) for blk in pf)
            return [blk.astype(bf16) for blk in pf], jnp.sum(partial, axis=0, keepdims=True)

        def unit_output(qt, hd, p, denom):
            vt_all = jnp.concatenate([vt_ref[kt, col[hd], :] for kt in range(qt + 1)], axis=1)
            acc = jnp.dot(vt_all, jnp.concatenate(p, axis=0), preferred_element_type=f32)
            out_t = acc * (1.0 / denom)
            gate = _silu_of_half(z_ref[qt * t:(qt + 1) * t, col[hd]].astype(f32))
            o_ref[qt * t:(qt + 1) * t, col[hd]] = (out_t.T * gate).astype(o_ref.dtype)

        denom_min = None
        u_next = unit_scores(*units[0])
        for w, (qt, hd) in enumerate(units):
            u_cur = u_next
            if w + 1 < len(units):
                u_next = unit_scores(*units[w + 1])
            p, denom = unit_numerators(qt, hd, u_cur)
            unit_output(qt, hd, p, denom)
            denom_min = denom if denom_min is None else jnp.minimum(denom_min, denom)
        return jnp.min(denom_min)

    denom_min = run_fixed_shift()

    @pl.when(jnp.logical_not(denom_min >= FOX_MIN_DENOM))
    def _():
        run_online()


def _fox_attention(q, k, vt, nb, z, nbq, norms):
    hw = FOX_HEADS_PER_STEP * HEAD_DIM
    return pl.pallas_call(
        _fox_attn_kernel,
        grid=(BATCH, N_MAIN_HEADS // FOX_HEADS_PER_STEP),
        in_specs=[
            pl.BlockSpec((None, SEQ // FOX_T, hw, FOX_T), lambda b, p: (b, 0, p, 0)),
            pl.BlockSpec((SEQ, hw), lambda b, p: (b, p)),
            pl.BlockSpec((None, SEQ // FOX_T, hw, FOX_T), lambda b, p: (b, 0, p, 0)),
            pl.BlockSpec((SEQ, LANES), lambda b, p: (b, 0)),
            pl.BlockSpec((SEQ, hw), lambda b, p: (b, p)),
            pl.BlockSpec((None, FOX_HEADS_PER_STEP, SEQ // FOX_T, FOX_T), lambda b, p: (b, p, 0, 0)),
            pl.BlockSpec((SEQ // FOX_PROJ_TM, 8, LANES), lambda b, p: (b, 0, 0)),
        ],
        out_specs=pl.BlockSpec((SEQ, hw), lambda b, p: (b, p)),
        out_shape=jax.ShapeDtypeStruct((BATCH * SEQ, MAIN_W), bf16),
        scratch_shapes=[
            pltpu.VMEM((SEQ // FOX_T, FOX_HEADS_PER_STEP, 1, FOX_T), f32),
            pltpu.VMEM((SEQ // FOX_T, FOX_HEADS_PER_STEP, HEAD_DIM + FOX_SUM_ROWS, FOX_T), f32),
            pltpu.VMEM((FOX_HEADS_PER_STEP, FOX_T, FOX_T), f32),
            pltpu.VMEM((FOX_HEADS_PER_STEP, FOX_T, FOX_T), bf16),
            pltpu.VMEM((FOX_HEADS_PER_STEP, 1, FOX_T), f32),
        ],
        compiler_params=pltpu.CompilerParams(dimension_semantics=("arbitrary", "arbitrary"),
                                             vmem_limit_bytes=VMEM_LIMIT),
        name="fox_attention",
    )(q, k, vt, nb, z, nbq, norms)


def _retention_kernel(q_ref, kt_ref, v_ref, z_ref, d_ref, eps_ref, zeta_ref, g_ref, o_ref, r_ref):
    c_len = RET_CHUNK
    r_ref[...] = jnp.zeros(r_ref.shape, f32)
    lane = lax.broadcasted_iota(jnp.int32, (c_len, LANES), 1)
    sub = lax.broadcasted_iota(jnp.int32, (LANES, c_len), 0)
    q_mask = (lane < RET_KEY_DIM, lane >= RET_KEY_DIM)
    k_mask = (sub < RET_KEY_DIM, sub >= RET_KEY_DIM)
    pairs = range(RET_PAIRS_PER_STEP)
    chunks = range(RET_CHUNKS_PER_ITER)

    def body(it, carry):
        chunk = [it * RET_CHUNKS_PER_ITER + ci for ci in chunks]
        rows = [pl.ds(pl.multiple_of(chunk[ci] * c_len, c_len), c_len) for ci in chunks]
        q2, s, upd = {}, {}, {}
        for ci in chunks:
            for pr in pairs:
                q2[ci, pr] = q_ref[rows[ci], pr * LANES:(pr + 1) * LANES]
                kt2 = kt_ref[chunk[ci], pr * LANES:(pr + 1) * LANES, :]
                v2 = v_ref[rows[ci], pr * 2 * HEAD_DIM:(pr + 1) * 2 * HEAD_DIM]
                k_heads = jnp.concatenate([jnp.where(k_mask[hd], kt2, jnp.zeros_like(kt2))
                                           for hd in range(2)], axis=1)
                s[ci, pr] = jnp.dot(q2[ci, pr], k_heads, preferred_element_type=f32)
                kz = (kt2.astype(f32) * zeta_ref[pr]).astype(bf16)
                upd[ci, pr] = jnp.dot(kz, v2, preferred_element_type=f32)
        r_b = {}
        for pr in pairs:
            r_state = r_ref[pr]
            for ci in chunks:
                r_b[ci, pr] = r_state.astype(bf16)
                r_state = r_state * g_ref[pr] + upd[ci, pr]
            r_ref[pr] = r_state
        o = {}
        for ci in chunks:
            for pr in pairs:
                inner = (s[ci, pr] * d_ref[pr]).astype(bf16)
                for hd in range(2):
                    cols = slice((2 * pr + hd) * HEAD_DIM, (2 * pr + hd + 1) * HEAD_DIM)
                    q_hd = jnp.where(q_mask[hd], q2[ci, pr], jnp.zeros_like(q2[ci, pr]))
                    lhs = jnp.concatenate([inner[:, hd * c_len:(hd + 1) * c_len], q_hd], axis=1)
                    rhs = jnp.concatenate([v_ref[rows[ci], cols],
                                           r_b[ci, pr][:, hd * HEAD_DIM:(hd + 1) * HEAD_DIM]], axis=0)
                    o[ci, pr, hd] = jnp.dot(lhs, rhs, preferred_element_type=f32)
        for ci in chunks:
            for pr in pairs:
                for hd in range(2):
                    cols = slice((2 * pr + hd) * HEAD_DIM, (2 * pr + hd + 1) * HEAD_DIM)
                    oh = o[ci, pr, hd]
                    ms = jnp.mean(oh * oh, axis=-1, keepdims=True)
                    on = oh * lax.rsqrt(ms + eps_ref[2 * pr + hd])
                    zg = _silu_of_half(z_ref[rows[ci], cols].astype(f32))
                    o_ref[rows[ci], cols] = (on * zg).astype(o_ref.dtype)
        return carry

    lax.fori_loop(0, SEQ // (c_len * RET_CHUNKS_PER_ITER), body, 0)


def _retention_tables():
    h, c_len = N_MAIN_HEADS, RET_CHUNK
    lg = np.log1p(-np.exp2(-5.0 - np.arange(h, dtype=np.float64)))
    n = np.arange(c_len, dtype=np.float64)
    causal = n[:, None] >= n[None, :]
    d_col = np.where(causal[None], np.exp(-lg[:, None, None] * (n[None, None, :] + 1.0)), 0.0)
    d_pair = d_col.reshape(h // 2, 2, c_len, c_len).transpose(0, 2, 1, 3).reshape(h // 2, c_len, 2 * c_len)
    xi = np.exp(lg[:, None] * (n[None, :] + 1.0))
    zeta = np.exp(lg[:, None] * (c_len - 1.0 - n[None, :]))
    g_chunk = np.exp(lg * c_len)
    eps_rows = np.broadcast_to((EPS / (xi * xi))[:, :, None], (h, c_len, HEAD_DIM))
    zeta_t = np.repeat(zeta.reshape(h // 2, 2, c_len), RET_KEY_DIM, axis=1)
    g2 = np.broadcast_to(np.repeat(g_chunk.reshape(h // 2, 2), RET_KEY_DIM, axis=-1)[:, :, None],
                         (h // 2, 2 * RET_KEY_DIM, 2 * HEAD_DIM))
    return tuple(jnp.asarray(np.ascontiguousarray(t), dtype=f32) for t in (d_pair, eps_rows, zeta_t, g2))


def _retention(q, kt, v, z):
    n_pairs = N_MAIN_HEADS // 2
    pp = RET_PAIRS_PER_STEP
    steps = n_pairs // pp
    d_pair, eps_rows, zeta_t, g2 = _retention_tables()
    c_len = RET_CHUNK
    return pl.pallas_call(
        _retention_kernel,
        grid=(BATCH, steps),
        in_specs=[
            pl.BlockSpec((SEQ, pp * LANES), lambda b, p: (b, p)),
            pl.BlockSpec((None, SEQ // c_len, pp * LANES, c_len), lambda b, p: (b, 0, p, 0)),
            pl.BlockSpec((SEQ, pp * 2 * HEAD_DIM), lambda b, p: (b, p)),
            pl.BlockSpec((SEQ, pp * 2 * HEAD_DIM), lambda b, p: (b, p)),
            pl.BlockSpec((pp, c_len, 2 * c_len), lambda b, p: (p, 0, 0)),
            pl.BlockSpec((2 * pp, c_len, HEAD_DIM), lambda b, p: (p, 0, 0)),
            pl.BlockSpec((pp, LANES, c_len), lambda b, p: (p, 0, 0)),
            pl.BlockSpec((pp, LANES, 2 * HEAD_DIM), lambda b, p: (p, 0, 0)),
        ],
        out_specs=pl.BlockSpec((SEQ, pp * 2 * HEAD_DIM), lambda b, p: (b, p)),
        out_shape=jax.ShapeDtypeStruct((BATCH * SEQ, MAIN_W), bf16),
        scratch_shapes=[pltpu.VMEM((pp, LANES, 2 * HEAD_DIM), f32)],
        compiler_params=pltpu.CompilerParams(dimension_semantics=("arbitrary", "arbitrary"),
                                             vmem_limit_bytes=VMEM_LIMIT),
        name="retention",
    )(q, kt, v, z, d_pair, eps_rows, zeta_t, g2)


def _out_proj_kernel(om_ref, qm_ref, zm_ref, kv_ref, w32_ref, x_ref, g_ref, o_ref, w_ref, *, final_norm):
    @pl.when(pl.program_id(0) == 0)
    def _():
        for r0 in range(0, INNER, WEIGHT_CAST_ROWS):
            rows = slice(r0, r0 + WEIGHT_CAST_ROWS)
            w_ref[rows, :] = w32_ref[rows, :].astype(bf16)

    scale = 1.0 / math.sqrt(HEAD_DIM)
    heads = range(N_MEM_HEADS)
    col = [slice(hd * HEAD_DIM, (hd + 1) * HEAD_DIM) for hd in heads]
    half = D_MODEL // 2
    for r0 in range(0, OUT_TM, OUT_TM // 2):
        rows = slice(r0, r0 + OUT_TM // 2)
        s = [lax.dot_general(qm_ref[rows, col[hd]], kv_ref[:, col[hd]], (((1,), (1,)), ((), ())),
                             preferred_element_type=f32) * scale for hd in heads]
        y_lo = jnp.dot(om_ref[rows, :], w_ref[:MAIN_W, :half], preferred_element_type=f32)
        p = []
        for hd in heads:
            e = jnp.exp(s[hd] - jnp.max(s[hd], axis=-1, keepdims=True))
            p.append((e * (1.0 / jnp.sum(e, axis=-1, keepdims=True))).astype(bf16))
        memo = [jnp.dot(p[hd], kv_ref[:, MEM_W + hd * HEAD_DIM:MEM_W + (hd + 1) * HEAD_DIM],
                        preferred_element_type=f32) for hd in heads]
        y_hi = jnp.dot(om_ref[rows, :], w_ref[:MAIN_W, half:], preferred_element_type=f32)
        og = jnp.concatenate([(memo[hd] * _silu_of_half(zm_ref[rows, col[hd]].astype(f32))).astype(bf16)
                              for hd in heads], axis=1)
        y = jnp.concatenate([y_lo, y_hi], axis=1) + jnp.dot(og, w_ref[MAIN_W:, :], preferred_element_type=f32)
        xn = x_ref[rows, :] + y
        if final_norm:
            xn = _rmsnorm_rows(xn, g_ref[...])
        o_ref[rows, :] = xn


def _out_proj(o_main, qm, z, kv, w_out, layer, x, g_final, *, final_norm):
    t_rows = x.shape[0]
    tiles_per_seq = SEQ // OUT_TM
    z_blk = MAIN_W // MEM_W
    return pl.pallas_call(
        functools.partial(_out_proj_kernel, final_norm=final_norm),
        grid=(t_rows // OUT_TM,),
        in_specs=[
            pl.BlockSpec((OUT_TM, MAIN_W), lambda i: (i, 0)),
            pl.BlockSpec((OUT_TM, MEM_W), lambda i: (i, 0)),
            pl.BlockSpec((OUT_TM, MEM_W), lambda i: (i, z_blk)),
            pl.BlockSpec((N_MEM, 2 * MEM_W), lambda i: (i // tiles_per_seq, 0)),
            pl.BlockSpec((None, INNER, D_MODEL), lambda i: (layer, 0, 0), pipeline_mode=pl.Buffered(1)),
            pl.BlockSpec((OUT_TM, D_MODEL), lambda i: (i, 0)),
            pl.BlockSpec((1, D_MODEL), lambda i: (0, 0)),
        ],
        out_specs=pl.BlockSpec((OUT_TM, D_MODEL), lambda i: (i, 0)),
        out_shape=jax.ShapeDtypeStruct((t_rows, D_MODEL), f32),
        scratch_shapes=[pltpu.VMEM((INNER, D_MODEL), bf16)],
        compiler_params=pltpu.CompilerParams(dimension_semantics=("arbitrary",),
                                             vmem_limit_bytes=VMEM_LIMIT),
        name="out_proj_final" if final_norm else "out_proj",
    )(o_main, qm, z, kv, w_out, x, g_final.reshape(1, D_MODEL))


def _rotary_tables():
    half = RET_KEY_DIM // 2
    pos = np.arange(SEQ, dtype=np.float64)
    inv = 1.0 / (ROPE_BASE ** (np.arange(half, dtype=np.float64) / half))
    ang = pos[:, None] * inv[None, :]
    cos, sin = np.cos(ang), np.sin(ang)
    reps = LANES // RET_KEY_DIM
    cos_t = np.tile(np.concatenate([cos, cos], axis=-1), (1, reps))
    sin_t = np.tile(np.concatenate([-sin, sin], axis=-1), (1, reps))
    k_scale = RET_KEY_DIM ** -0.5
    return tuple(jnp.asarray(t, dtype=f32) for t in (cos_t, sin_t, cos_t * k_scale, sin_t * k_scale))


def kernel(x, mem, norm_g, fox_w_in, fox_b_f, ret_w_in, mem_norm_g, w_mem_kv, w_out, final_norm_g):
    t_rows = BATCH * SEQ
    x2 = x.reshape(t_rows, D_MODEL)

    kv0, kv1 = _norm_proj(mem.reshape(BATCH * N_MEM, D_MODEL), mem_norm_g,
                          [(w_mem_kv, 0), (w_mem_kv, 1)],
                          [2 * MEM_W, 2 * MEM_W], [bf16, bf16], name="mem_kv_proj")

    o_f = 3 * MAIN_W
    o_qm = o_f + N_MAIN_HEADS
    w0t = jnp.swapaxes(fox_w_in[0], 0, 1).astype(bf16)
    w_qmz = w0t[o_qm:]
    w_f = jnp.pad(w0t[o_f:o_qm], ((0, F_PAD - N_MAIN_HEADS), (0, 0)))
    q, k, vt, qm, z, f, norms = _norm_proj(
        x2, norm_g[0], [(w0t, o_f), w_qmz, w_f], [MAIN_W, MAIN_W, MAIN_W, MEM_W, INNER, F_PAD],
        [bf16, bf16, bf16, bf16, bf16, f32], ["fox_q", "fox_k", "kt", "plain", "half", "plain"],
        w_transposed=True, tm=FOX_PROJ_TM, name="fox_in_proj")
    nb, nbq = _fox_gate(f, fox_b_f[0])
    o_main = _fox_attention(q, k, vt, nb, z, nbq, norms)
    x2 = _out_proj(o_main, qm, z, kv0, w_out, 0, x2, final_norm_g, final_norm=False)

    q, kt, v, qm, z = _norm_proj(
        x2, norm_g[1], [(ret_w_in, 0)], [RET_QK_W, RET_QK_W, MAIN_W, MEM_W, INNER], [bf16] * 5,
        ["rot_q", "rot_kt", "plain", "plain", "half"], tables=_rotary_tables(), tm=RET_PROJ_TM,
        name="ret_in_proj")
    o_main = _retention(q, kt, v, z)
    out = _out_proj(o_main, qm, z, kv1, w_out, 1, x2, final_norm_g, final_norm=True)
    return out.reshape(BATCH, SEQ, D_MODEL)
```
